```python
import jax, jax.numpy as jnp
from jax import lax
import numpy as np

D_MODEL = 1024
BATCH = 2
SEQ = 8192
DEPTH = 2

N_HEADS = 8
HEAD_DIM = 64
ATTN_WIDTH = 512
ROPE_DIM = 16
ROPE_THETA = 500000.0
IDX_HEADS = 8
IDX_DIM = 32
IDX_ROPE_DIM = 8
TOPK_MAX = 256
Q_BLOCK = 128
POOL_GROUPS = 4
POOL_GROUP_DIM = 128
POOL_WIDTH = 512
POOL_WINDOWS = (2, 4, 8, 16)
D_FF = 2816
N_EXPERTS = 8
TOP_K_EXPERTS = 2
D_FF_EXPERT = 3584
N_DENSE = (DEPTH + 1) // 2
N_MOE = DEPTH // 2
EPS = 1e-6
IN_SPLIT_SIZES = (ATTN_WIDTH, ATTN_WIDTH, ATTN_WIDTH, IDX_HEADS * IDX_DIM, IDX_DIM, IDX_HEADS, POOL_WIDTH, D_MODEL, D_MODEL)
IN_WIDTH = 4392

kernel_name = 'hybrid_dsa_pool_moe_block'


def rmsnorm(x, g):
    xf = x.astype(jnp.float32)
    y = xf * lax.rsqrt(jnp.mean(xf * xf, axis=-1, keepdims=True) + EPS)
    return (y * g.astype(jnp.float32)).astype(x.dtype)


def rope_partial(x, pos, rot_dim):
    inv = jnp.power(ROPE_THETA, -jnp.arange(0, rot_dim, 2, dtype=jnp.float32) / rot_dim)
    ang = pos[:, None] * inv[None, :]
    cos = jnp.cos(ang)[None, :, None, :]
    sin = jnp.sin(ang)[None, :, None, :]
    xr = x[..., :rot_dim].astype(jnp.float32)
    x1, x2 = xr[..., : rot_dim // 2], xr[..., rot_dim // 2:]
    rot = jnp.concatenate([x1 * cos - x2 * sin, x2 * cos + x1 * sin], axis=-1)
    return jnp.concatenate([rot.astype(x.dtype), x[..., rot_dim:]], axis=-1)


def dsa_attention(q, k, v, qi, ki, wi):
    B, S, H, hd = q.shape
    topk = min(TOPK_MAX, S // 4)
    nb = S // Q_BLOCK
    kpos = jnp.arange(S, dtype=jnp.int32)
    ki32 = ki.astype(jnp.float32)

    def to_blocks(a):
        return a.reshape((B, nb, Q_BLOCK) + a.shape[2:]).swapaxes(0, 1)

    def block(args):
        qb, qib, wib, qpos = args
        dots = jnp.einsum('bqhd,bsd->bqhs', qib.astype(jnp.float32), ki32) * (IDX_DIM ** -0.5)
        score = jnp.einsum('bqh,bqhs->bqs', wib.astype(jnp.float32) * (IDX_HEADS ** -0.5), jax.nn.relu(dots))
        causal = kpos[None, :] <= qpos[:, None]
        score = jnp.where(causal[None], score, -jnp.inf)
        _, sel = lax.top_k(score, topk)
        valid = sel <= qpos[None, :, None]
        k_sel = jax.vmap(lambda kk, ii: kk[ii])(k, sel)
        v_sel = jax.vmap(lambda vv, ii: vv[ii])(v, sel)
        logits = jnp.einsum('bqhd,bqkhd->bqhk', qb.astype(jnp.float32), k_sel.astype(jnp.float32)) * (HEAD_DIM ** -0.5)
        logits = jnp.where(valid[:, :, None, :], logits, -jnp.inf)
        p = jax.nn.softmax(logits, axis=-1)
        return jnp.einsum('bqhk,bqkhd->bqhd', p.astype(v.dtype), v_sel)

    pos_b = kpos.reshape(nb, Q_BLOCK)
    out = lax.map(block, (to_blocks(q), to_blocks(qi), to_blocks(wi), pos_b))
    return out.swapaxes(0, 1).reshape(B, S, H * hd)


def pool_mixer(u, w_pool, pool_scale):
    B, S, C = u.shape
    cs = jnp.pad(jnp.cumsum(u.astype(jnp.float32), axis=1), ((0, 0), (1, 0), (0, 0)))
    t1 = jnp.arange(1, S + 1, dtype=jnp.float32)
    means = []
    for g, w in enumerate(POOL_WINDOWS):
        c = cs[:, :, g * POOL_GROUP_DIM:(g + 1) * POOL_GROUP_DIM]
        lagged = jnp.pad(c[:, :S + 1 - w], ((0, 0), (w - 1, 0), (0, 0)))
        means.append((c[:, 1:] - lagged) / jnp.minimum(t1, w)[None, :, None])
    pooled = jnp.stack(means, axis=2)
    diff = pooled - u.reshape(B, S, POOL_GROUPS, POOL_GROUP_DIM).astype(jnp.float32)
    mixed = jnp.einsum('bsgc,gcd->bsgd', diff.astype(u.dtype), w_pool)
    return mixed.reshape(B, S, C) * pool_scale


def swiglu(h, w1, w3, w2):
    return (jax.nn.silu(h @ w1) * (h @ w3)) @ w2


def moe_swiglu(h, w_router, w1, w3, w2):
    logits = jnp.einsum('bsd,de->bse', h.astype(jnp.float32), w_router.astype(jnp.float32))
    top_v, top_i = lax.top_k(logits, TOP_K_EXPERTS)
    top_w = jax.nn.softmax(top_v, axis=-1)
    gate = jnp.sum(jax.nn.one_hot(top_i, N_EXPERTS, dtype=jnp.float32) * top_w[..., None], axis=-2)
    gate = gate.astype(h.dtype)
    y = jnp.zeros_like(h)
    for e in range(N_EXPERTS):
        y = y + gate[..., e:e + 1] * swiglu(h, w1[e], w3[e], w2[e])
    return y


def setup_inputs(seed: int = 0) -> dict:
    key = jax.random.key(seed)
    ks = jax.random.split(key, 20)

    def nrm(k, shape, scale):
        return jax.random.normal(k, shape, jnp.float32) * scale

    return {
        'x': nrm(ks[0], (BATCH, SEQ, D_MODEL), 1.0),
        'mix_norm': 1.0 + nrm(ks[1], (DEPTH, D_MODEL), 0.02),
        'w_in': nrm(ks[2], (DEPTH, D_MODEL, IN_WIDTH), D_MODEL ** -0.5),
        'q_norm': 1.0 + nrm(ks[3], (DEPTH, HEAD_DIM), 0.02),
        'k_norm': 1.0 + nrm(ks[4], (DEPTH, HEAD_DIM), 0.02),
        'w_pool': nrm(ks[5], (DEPTH, POOL_GROUPS, POOL_GROUP_DIM, POOL_GROUP_DIM), POOL_GROUP_DIM ** -0.5),
        'pool_scale': 1.0 + nrm(ks[6], (DEPTH, POOL_WIDTH), 0.1),
        'w_attn_proj': nrm(ks[7], (DEPTH, ATTN_WIDTH, D_MODEL), ATTN_WIDTH ** -0.5),
        'w_pool_proj': nrm(ks[8], (DEPTH, POOL_WIDTH, D_MODEL), POOL_WIDTH ** -0.5),
        'w_out': nrm(ks[9], (DEPTH, D_MODEL, D_MODEL), D_MODEL ** -0.5),
        'ffn_norm': 1.0 + nrm(ks[10], (DEPTH, D_MODEL), 0.02),
        'dense_w1': nrm(ks[11], (N_DENSE, D_MODEL, D_FF), D_MODEL ** -0.5),
        'dense_w3': nrm(ks[12], (N_DENSE, D_MODEL, D_FF), D_MODEL ** -0.5),
        'dense_w2': nrm(ks[13], (N_DENSE, D_FF, D_MODEL), D_FF ** -0.5),
        'moe_router': nrm(ks[14], (N_MOE, D_MODEL, N_EXPERTS), D_MODEL ** -0.5),
        'moe_w1': nrm(ks[15], (N_MOE, N_EXPERTS, D_MODEL, D_FF_EXPERT), D_MODEL ** -0.5),
        'moe_w3': nrm(ks[16], (N_MOE, N_EXPERTS, D_MODEL, D_FF_EXPERT), D_MODEL ** -0.5),
        'moe_w2': nrm(ks[17], (N_MOE, N_EXPERTS, D_FF_EXPERT, D_MODEL), D_FF_EXPERT ** -0.5),
    }


def reference(x, mix_norm, w_in, q_norm, k_norm, w_pool, pool_scale, w_attn_proj, w_pool_proj, w_out, ffn_norm, dense_w1, dense_w3, dense_w2, moe_router, moe_w1, moe_w3, moe_w2):
    B, S, _ = x.shape
    pos = jnp.arange(S, dtype=jnp.float32)
    offsets = [int(o) for o in np.cumsum(IN_SPLIT_SIZES)[:-1]]
    for layer in range(DEPTH):
        h = rmsnorm(x, mix_norm[layer])
        z = h @ w_in[layer]
        q, k, v, qi, ki, wi, u, ga, gp = jnp.split(z, offsets, axis=-1)
        q = rope_partial(rmsnorm(q.reshape(B, S, N_HEADS, HEAD_DIM), q_norm[layer]), pos, ROPE_DIM)
        k = rope_partial(rmsnorm(k.reshape(B, S, N_HEADS, HEAD_DIM), k_norm[layer]), pos, ROPE_DIM)
        v = v.reshape(B, S, N_HEADS, HEAD_DIM)
        qi = rope_partial(qi.reshape(B, S, IDX_HEADS, IDX_DIM), pos, IDX_ROPE_DIM)
        ki = rope_partial(ki[:, :, None, :], pos, IDX_ROPE_DIM)[:, :, 0, :]
        a = dsa_attention(q, k, v, qi, ki, wi)
        p = pool_mixer(u, w_pool[layer], pool_scale[layer])
        merged = jax.nn.sigmoid(ga) * (a @ w_attn_proj[layer]) + jax.nn.sigmoid(gp) * (p @ w_pool_proj[layer])
        x = x + merged @ w_out[layer]
        h2 = rmsnorm(x, ffn_norm[layer])
        if layer % 2 == 0:
            i = layer // 2
            x = x + swiglu(h2, dense_w1[i], dense_w3[i], dense_w2[i])
        else:
            i = layer // 2
            x = x + moe_swiglu(h2, moe_router[i], moe_w1[i], moe_w3[i], moe_w2[i])
    return x
```

```python
import functools

import jax
import jax.numpy as jnp
from jax import lax
from jax.experimental import pallas as pl
from jax.experimental.pallas import tpu as pltpu

bf16 = jnp.bfloat16
f32 = jnp.float32
i32 = jnp.int32

D_MODEL = 1024
N_HEADS = 8
HEAD_DIM = 64
ATTN_WIDTH = 512
ROPE_DIM = 16
ROPE_THETA = 500000.0
IDX_HEADS = 8
IDX_DIM = 32
IDX_ROPE_DIM = 8
TOPK_MAX = 256
POOL_GROUPS = 4
POOL_GROUP_DIM = 128
POOL_WIDTH = 512
POOL_WINDOWS = (2, 4, 8, 16)
POOL_HALO = 16
N_EXPERTS = 8
EPS = 1e-6

LANES = 128
INT_MIN = -(2 ** 31)
VMEM_LIMIT = 56 * 1024 * 1024

OFF_Q = 0
OFF_K = 512
OFF_V = 1024
OFF_QCAT = 1536
OFF_KW = 2560
OFF_U = 2688
OFF_GA = 3200
OFF_GP = 4224
W_COLS = 5248

TM_PROJ = 512
TM_MERGE = 512
QB = 128
TK = 512


def _sigmoid(x):
    return 1.0 / (1.0 + jnp.exp(-x))


def _in_proj_kernel(x_ref, g_ref, w_ref, qn_ref, kn_ref, bd_ref, rqk_ref, rqi_ref, rki_ref,
                    q_ref, k_ref, v_ref, qc_ref, kw_ref, u_ref, ga_ref, gp_ref):
    x = x_ref[...]
    ms = jnp.mean(x * x, axis=-1, keepdims=True)
    h = (x * lax.rsqrt(ms + EPS) * g_ref[...]).astype(bf16)

    def proj(lo, n):
        return jnp.dot(h, w_ref[:, lo:lo + n], preferred_element_type=f32)

    def rope(xc, tab_ref, sh):
        return (xc * tab_ref[0] + pltpu.roll(xc, LANES - sh, 1) * tab_ref[1]
                + pltpu.roll(xc, sh, 1) * tab_ref[2])

    def headnorm(z, gain_ref):
        msh = jnp.dot((z * z).astype(bf16), bd_ref[...], preferred_element_type=f32)
        return z * lax.rsqrt(msh + EPS) * gain_ref[...]

    zq = headnorm(proj(OFF_Q, ATTN_WIDTH), qn_ref) * (HEAD_DIM ** -0.5)
    zk = headnorm(proj(OFF_K, ATTN_WIDTH), kn_ref)
    for c in range(ATTN_WIDTH // LANES):
        sl = slice(c * LANES, (c + 1) * LANES)
        q_ref[:, sl] = rope(zq[:, sl], rqk_ref, ROPE_DIM // 2).astype(bf16)
        k_ref[:, sl] = rope(zk[:, sl], rqk_ref, ROPE_DIM // 2).astype(bf16)
    v_ref[...] = proj(OFF_V, ATTN_WIDTH).astype(bf16)

    lane = lax.broadcasted_iota(i32, (1, LANES), 1)
    zc = proj(OFF_QCAT, IDX_HEADS * LANES)
    for c in range(IDX_HEADS):
        sl = slice(c * LANES, (c + 1) * LANES)
        r = rope(zc[:, sl], rqi_ref, IDX_ROPE_DIM // 2)
        lo = r - r.astype(bf16).astype(f32)
        qc_ref[:, sl] = jnp.where((lane >= 32) & (lane < 64), lo, r).astype(bf16)

    zkw = rope(proj(OFF_KW, LANES), rki_ref, IDX_ROPE_DIM // 2)
    lo = zkw - zkw.astype(bf16).astype(f32)
    zkw = jnp.where((lane >= 64) & (lane < 96), lo, zkw)
    kw_ref[...] = jnp.where(lane >= 96, zkw * ((IDX_HEADS * IDX_DIM) ** -0.5), zkw)

    u_ref[...] = proj(OFF_U, POOL_WIDTH)
    ga_ref[...] = proj(OFF_GA, D_MODEL)
    gp_ref[...] = proj(OFF_GP, D_MODEL)


def _in_proj(x2d, g, w, qn, kn, bd, rqk, rqi, rki, seq):
    t = x2d.shape[0]
    tm = TM_PROJ
    ns = seq // tm
    row = lambda i: (i, 0)
    const = lambda i: (0, 0)
    tab = lambda i: (0, i % ns, 0)
    out_shapes = (
        jax.ShapeDtypeStruct((t, ATTN_WIDTH), bf16),
        jax.ShapeDtypeStruct((t, ATTN_WIDTH), bf16),
        jax.ShapeDtypeStruct((t, ATTN_WIDTH), bf16),
        jax.ShapeDtypeStruct((t, IDX_HEADS * LANES), bf16),
        jax.ShapeDtypeStruct((t, LANES), f32),
        jax.ShapeDtypeStruct((t, POOL_WIDTH), f32),
        jax.ShapeDtypeStruct((t, D_MODEL), f32),
        jax.ShapeDtypeStruct((t, D_MODEL), f32),
    )
    return pl.pallas_call(
        _in_proj_kernel,
        grid=(t // tm,),
        in_specs=[
            pl.BlockSpec((tm, D_MODEL), row),
            pl.BlockSpec((1, D_MODEL), const),
            pl.BlockSpec((D_MODEL, W_COLS), const),
            pl.BlockSpec((1, ATTN_WIDTH), const),
            pl.BlockSpec((1, ATTN_WIDTH), const),
            pl.BlockSpec((ATTN_WIDTH, ATTN_WIDTH), const),
            pl.BlockSpec((3, tm, LANES), tab),
            pl.BlockSpec((3, tm, LANES), tab),
            pl.BlockSpec((3, tm, LANES), tab),
        ],
        out_specs=[pl.BlockSpec((tm, s.shape[1]), row) for s in out_shapes],
        out_shape=out_shapes,
        compiler_params=pltpu.CompilerParams(
            dimension_semantics=("arbitrary",), vmem_limit_bytes=VMEM_LIMIT),
        name="in_proj",
    )(x2d, g, w, qn, kn, bd, rqk, rqi, rki)


def _dsa_kernel(topk, seq, q_ref, qc_ref, kwq_ref, k_ref, v_ref, kw_ref, o_ref,
                key_sc, kcat_sc, qm_sc, m_sc, l_sc, acc_sc):
    i = pl.program_id(1)
    nk = (i * QB + QB + TK - 1) // TK
    kf = float(topk)

    @pl.when(i == 0)
    def _():
        kcat_sc[...] = kw_ref[...].astype(bf16)

    qpos = i * QB + lax.broadcasted_iota(i32, (QB, 1), 0)
    lane_tk = lax.broadcasted_iota(i32, (1, TK), 1)
    wq = kwq_ref[...]

    def score_chunk(c, carry):
        koff = pl.multiple_of(c * TK, TK)
        kc = kcat_sc[pl.ds(koff, TK), :]
        acc = jnp.zeros((QB, TK), f32)
        for h in range(IDX_HEADS):
            d = lax.dot_general(qc_ref[:, h * LANES:(h + 1) * LANES], kc,
                                (((1,), (1,)), ((), ())), preferred_element_type=f32)
            acc = acc + jnp.maximum(d, 0.0) * wq[:, 96 + h:97 + h]
        bits = pltpu.bitcast(acc, i32)
        key = bits ^ ((bits >> 31) & 0x7FFFFFFF)
        kpos = koff + lane_tk
        key_sc[c] = jnp.where(kpos <= qpos, key, INT_MIN)
        return carry

    lax.fori_loop(0, nk, score_chunk, 0)

    def lane_fold(x):
        out = x[:, 0:LANES]
        for j in range(1, TK // LANES):
            out = out + x[:, j * LANES:(j + 1) * LANES]
        return out

    def count_ge(cand):
        def body(c, part):
            return part + lane_fold(jnp.where(key_sc[c] >= cand, 1.0, 0.0))
        part = lax.fori_loop(0, nk, body, jnp.zeros((QB, LANES), f32))
        return jnp.sum(part, axis=1, keepdims=True)

    zero = jnp.zeros((QB, 1), i32)
    prefix = jnp.where(count_ge(zero) >= kf, zero, INT_MIN)

    def bit_body(b, prefix):
        cand = prefix | jnp.left_shift(1, 30 - b)
        return jnp.where(count_ge(cand) >= kf, cand, prefix)

    thr = lax.fori_loop(0, 31, bit_body, prefix)

    is_tie = (count_ge(thr) > kf) & (thr > INT_MIN)
    any_tie = jnp.max(jnp.where(is_tie, 1.0, 0.0)) > 0.0

    @pl.when(any_tie)
    def _():
        need = kf - count_ge(thr + 1)

        def count_eq_below(m):
            def body(c, part):
                idx = c * TK + lane_tk
                hit = (key_sc[c] == thr) & (idx < m)
                return part + lane_fold(jnp.where(hit, 1.0, 0.0))
            part = lax.fori_loop(0, nk, body, jnp.zeros((QB, LANES), f32))
            return jnp.sum(part, axis=1, keepdims=True)

        mprime = jnp.zeros((QB, 1), i32)
        bit = seq // 2
        while bit >= 1:
            cand = mprime | bit
            mprime = jnp.where(count_eq_below(cand) < need, cand, mprime)
            bit //= 2
        mstar = mprime + 1

        def kill(c, carry):
            idx = c * TK + lane_tk
            key = key_sc[c]
            drop = is_tie & (key == thr) & (idx >= mstar)
            key_sc[c] = jnp.where(drop, thr - 1, key)
            return carry

        lax.fori_loop(0, nk, kill, 0)

    thr_eff = jnp.maximum(thr, INT_MIN + 1)

    lane = lax.broadcasted_iota(i32, (1, LANES), 1)
    for h in range(N_HEADS):
        pr = (h // 2) * LANES
        in_head = (lane >= (h % 2) * HEAD_DIM) & (lane < (h % 2 + 1) * HEAD_DIM)
        qm_sc[h] = jnp.where(in_head, q_ref[:, pr:pr + LANES], jnp.zeros((), bf16))
    m_sc[...] = jnp.full(m_sc.shape, -jnp.inf, f32)
    l_sc[...] = jnp.zeros(l_sc.shape, f32)
    acc_sc[...] = jnp.zeros(acc_sc.shape, f32)

    def attn_chunk(c, carry):
        koff = pl.multiple_of(c * TK, TK)
        bias = jnp.where(key_sc[c] >= thr_eff, 0.0, -jnp.inf)
        for h in range(N_HEADS):
            pr = (h // 2) * LANES
            kc = k_ref[pl.ds(koff, TK), pr:pr + LANES]
            s = lax.dot_general(qm_sc[h], kc, (((1,), (1,)), ((), ())),
                                preferred_element_type=f32) + bias
            m_old = m_sc[h]
            m_new = jnp.maximum(m_old, jnp.max(s, axis=1, keepdims=True))
            m_safe = jnp.where(m_new == -jnp.inf, 0.0, m_new)
            alpha = jnp.exp(m_old - m_safe)
            p = jnp.exp(s - m_safe)
            l_sc[h] = alpha * l_sc[h] + jnp.sum(p, axis=1, keepdims=True)
            vc = v_ref[pl.ds(koff, TK), pr:pr + LANES]
            acc_sc[h] = alpha * acc_sc[h] + jnp.dot(p.astype(bf16), vc, preferred_element_type=f32)
            m_sc[h] = m_new
        return carry

    lax.fori_loop(0, nk, attn_chunk, 0)

    for pr in range(N_HEADS // 2):
        o0 = acc_sc[2 * pr] / l_sc[2 * pr]
        o1 = acc_sc[2 * pr + 1] / l_sc[2 * pr + 1]
        o_ref[:, pr * LANES:(pr + 1) * LANES] = jnp.where(lane < HEAD_DIM, o0, o1).astype(bf16)


def _dsa_attention(q, qc, kw, k, v, batch, seq):
    t = q.shape[0]
    nq = seq // QB
    topk = min(TOPK_MAX, seq // 4)
    qrow = lambda b, i: (b * nq + i, 0)
    per_batch = lambda b, i: (b, 0)
    return pl.pallas_call(
        functools.partial(_dsa_kernel, topk, seq),
        grid=(batch, nq),
        in_specs=[
            pl.BlockSpec((QB, ATTN_WIDTH), qrow),
            pl.BlockSpec((QB, IDX_HEADS * LANES), qrow),
            pl.BlockSpec((QB, LANES), qrow),
            pl.BlockSpec((seq, ATTN_WIDTH), per_batch),
            pl.BlockSpec((seq, ATTN_WIDTH), per_batch),
            pl.BlockSpec((seq, LANES), per_batch),
        ],
        out_specs=pl.BlockSpec((QB, ATTN_WIDTH), qrow),
        out_shape=jax.ShapeDtypeStruct((t, ATTN_WIDTH), bf16),
        scratch_shapes=[
            pltpu.VMEM((seq // TK, QB, TK), i32),
            pltpu.VMEM((seq, LANES), bf16),
            pltpu.VMEM((N_HEADS, QB, LANES), bf16),
            pltpu.VMEM((N_HEADS, QB, 1), f32),
            pltpu.VMEM((N_HEADS, QB, 1), f32),
            pltpu.VMEM((N_HEADS, QB, LANES), f32),
        ],
        compiler_params=pltpu.CompilerParams(
            dimension_semantics=("arbitrary", "arbitrary"), vmem_limit_bytes=VMEM_LIMIT),
        name="dsa_attention",
    )(q, qc, kw, k, v, kw)


def _merge_kernel(moe, seq, *refs):
    if moe:
        (x_ref, a_ref, u_ref, uh_ref, ga_ref, gp_ref, wp_ref, ps_ref, pa_ref, pb_ref, wo_ref, g2_ref,
         wrh_ref, wrl_ref, x2_ref, h2_ref, gate_ref, e_sc) = refs
    else:
        (x_ref, a_ref, u_ref, uh_ref, ga_ref, gp_ref, wp_ref, ps_ref, pa_ref, pb_ref, wo_ref, g2_ref,
         x2_ref, h2_ref, e_sc) = refs
    tm = TM_MERGE
    i = pl.program_id(0)
    ti = i % (seq // tm)
    e_sc[0:POOL_HALO, :] = jnp.where(ti == 0, 0.0, uh_ref[...])
    e_sc[POOL_HALO:POOL_HALO + tm, :] = u_ref[...]
    npos = (ti * tm + 1 + lax.broadcasted_iota(i32, (tm, 1), 0)).astype(f32)

    parts = []
    for g, w in enumerate(POOL_WINDOWS):
        sl = slice(g * POOL_GROUP_DIM, (g + 1) * POOL_GROUP_DIM)
        tot = e_sc[POOL_HALO:POOL_HALO + tm, sl]
        for j in range(1, w):
            tot = tot + e_sc[POOL_HALO - j:POOL_HALO - j + tm, sl]
        diff = tot / jnp.minimum(npos, float(w)) - u_ref[:, sl]
        parts.append(jnp.dot(diff.astype(bf16), wp_ref[g], preferred_element_type=f32))
    p = jnp.concatenate(parts, axis=1) * ps_ref[...]

    ab = jnp.dot(a_ref[...], pa_ref[...], preferred_element_type=f32)
    pb = jnp.dot(p.astype(bf16), pb_ref[...], preferred_element_type=f32)
    merged = _sigmoid(ga_ref[...]) * ab + _sigmoid(gp_ref[...]) * pb
    x2 = x_ref[...] + jnp.dot(merged.astype(bf16), wo_ref[...], preferred_element_type=f32)
    x2_ref[...] = x2
    ms = jnp.mean(x2 * x2, axis=-1, keepdims=True)
    hf = x2 * lax.rsqrt(ms + EPS) * g2_ref[...]
    h2_ref[...] = hf.astype(bf16)

    if moe:
        hi = hf.astype(bf16)
        lo = (hf - hi.astype(f32)).astype(bf16)
        logits = (jnp.dot(hi, wrh_ref[...], preferred_element_type=f32)
                  + jnp.dot(lo, wrh_ref[...], preferred_element_type=f32)
                  + jnp.dot(hi, wrl_ref[...], preferred_element_type=f32))
        lanef = lax.broadcasted_iota(i32, (1, LANES), 1).astype(f32)
        lg = jnp.where(lanef < N_EXPERTS, logits, -jnp.inf)
        v1 = jnp.max(lg, axis=1, keepdims=True)
        i1 = jnp.min(jnp.where(lg == v1, lanef, float(LANES)), axis=1, keepdims=True)
        lg2 = jnp.where(lanef == i1, -jnp.inf, lg)
        v2 = jnp.max(lg2, axis=1, keepdims=True)
        i2 = jnp.min(jnp.where(lg2 == v2, lanef, float(LANES)), axis=1, keepdims=True)
        tt = jnp.exp(v2 - v1)
        w1 = 1.0 / (1.0 + tt)
        w2 = tt / (1.0 + tt)
        gate_ref[...] = jnp.where(lanef == i1, w1, 0.0) + jnp.where(lanef == i2, w2, 0.0)


def _merge(moe, seq, x2d, a, u, ga, gp, wp, ps, pa, pb, wo, g2, wrh=None, wrl=None):
    t = x2d.shape[0]
    tm = TM_MERGE
    row = lambda i: (i, 0)
    const2 = lambda i: (0, 0)
    const3 = lambda i: (0, 0, 0)
    halo = lambda i: (jnp.maximum(i * (tm // POOL_HALO) - 1, 0), 0)
    in_specs = [
        pl.BlockSpec((tm, D_MODEL), row),
        pl.BlockSpec((tm, ATTN_WIDTH), row),
        pl.BlockSpec((tm, POOL_WIDTH), row),
        pl.BlockSpec((POOL_HALO, POOL_WIDTH), halo),
        pl.BlockSpec((tm, D_MODEL), row),
        pl.BlockSpec((tm, D_MODEL), row),
        pl.BlockSpec((POOL_GROUPS, POOL_GROUP_DIM, POOL_GROUP_DIM), const3),
        pl.BlockSpec((1, POOL_WIDTH), const2),
        pl.BlockSpec((ATTN_WIDTH, D_MODEL), const2),
        pl.BlockSpec((POOL_WIDTH, D_MODEL), const2),
        pl.BlockSpec((D_MODEL, D_MODEL), const2),
        pl.BlockSpec((1, D_MODEL), const2),
    ]
    args = [x2d, a, u, u, ga, gp, wp, ps, pa, pb, wo, g2]
    out_shapes = [jax.ShapeDtypeStruct((t, D_MODEL), f32), jax.ShapeDtypeStruct((t, D_MODEL), bf16)]
    out_specs = [pl.BlockSpec((tm, D_MODEL), row), pl.BlockSpec((tm, D_MODEL), row)]
    if moe:
        in_specs += [pl.BlockSpec((D_MODEL, LANES), const2), pl.BlockSpec((D_MODEL, LANES), const2)]
        args += [wrh, wrl]
        out_shapes.append(jax.ShapeDtypeStruct((t, LANES), f32))
        out_specs.append(pl.BlockSpec((tm, LANES), row))
    return pl.pallas_call(
        functools.partial(_merge_kernel, moe, seq),
        grid=(t // tm,),
        in_specs=in_specs,
        out_specs=out_specs,
        out_shape=out_shapes,
        scratch_shapes=[pltpu.VMEM((tm + POOL_HALO, POOL_WIDTH), f32)],
        compiler_params=pltpu.CompilerParams(
            dimension_semantics=("arbitrary",), vmem_limit_bytes=VMEM_LIMIT),
        name="merge_moe" if moe else "merge_dense",
    )(*args)


def _ffn_kernel(gated, h_ref, x_ref, gate_ref, w1_ref, w3_ref, w2_ref, o_ref):
    e = pl.program_id(1)
    j = pl.program_id(2)

    @pl.when((e == 0) & (j == 0))
    def _():
        o_ref[...] = x_ref[...]

    h = h_ref[...]
    a = jnp.dot(h, w1_ref[0], preferred_element_type=f32)
    b = jnp.dot(h, w3_ref[0], preferred_element_type=f32)
    act = (a * _sigmoid(a) * b).astype(bf16)
    y = jnp.dot(act, w2_ref[0], preferred_element_type=f32)
    if gated:
        lane = lax.broadcasted_iota(i32, (1, LANES), 1)
        g = jnp.sum(jnp.where(lane == e, gate_ref[...], 0.0), axis=1, keepdims=True)
        y = g * y
    o_ref[...] += y


def _ffn(gated, h2, x2, gate, w1, w3, w2, tm, tf):
    t = h2.shape[0]
    ne, _, ff = w1.shape
    row = lambda i, e, j: (i, 0)
    return pl.pallas_call(
        functools.partial(_ffn_kernel, gated),
        grid=(t // tm, ne, ff // tf),
        in_specs=[
            pl.BlockSpec((tm, D_MODEL), row),
            pl.BlockSpec((tm, D_MODEL), row),
            pl.BlockSpec((tm, LANES), row),
            pl.BlockSpec((1, D_MODEL, tf), lambda i, e, j: (e, 0, j)),
            pl.BlockSpec((1, D_MODEL, tf), lambda i, e, j: (e, 0, j)),
            pl.BlockSpec((1, tf, D_MODEL), lambda i, e, j: (e, j, 0)),
        ],
        out_specs=pl.BlockSpec((tm, D_MODEL), row),
        out_shape=jax.ShapeDtypeStruct((t, D_MODEL), f32),
        compiler_params=pltpu.CompilerParams(
            dimension_semantics=("arbitrary", "arbitrary", "arbitrary"), vmem_limit_bytes=VMEM_LIMIT),
        name="ffn_moe" if gated else "ffn_dense",
    )(h2, x2, gate, w1, w3, w2)


def _rope_tables(seq, rot_dim, period, active_lanes):
    half = rot_dim // 2
    inv = jnp.power(ROPE_THETA, -jnp.arange(0, rot_dim, 2, dtype=f32) / rot_dim)
    ang = jnp.arange(seq, dtype=f32)[:, None] * inv[None, :]
    cos, sin = jnp.cos(ang), jnp.sin(ang)
    ones = jnp.ones((seq, period - rot_dim), f32)
    zeros = lambda n: jnp.zeros((seq, n), f32)
    cos_p = jnp.concatenate([cos, cos, ones], axis=1)
    sin_a = jnp.concatenate([-sin, zeros(period - half)], axis=1)
    sin_b = jnp.concatenate([zeros(half), sin, zeros(period - rot_dim)], axis=1)
    reps = LANES // period
    tabs = [jnp.tile(tb, (1, reps)) for tb in (cos_p, sin_a, sin_b)]
    live = (jnp.arange(LANES) < active_lanes)[None, :]
    tabs = [jnp.where(live, tabs[0], 1.0), jnp.where(live, tabs[1], 0.0), jnp.where(live, tabs[2], 0.0)]
    return jnp.stack(tabs, axis=0)


def _relayout_w_in(w):
    d = w.shape[0]
    o = 0
    segs = {}
    for name, n in (("q", 512), ("k", 512), ("v", 512), ("qi", 256), ("ki", 32), ("wi", 8),
                    ("u", 512), ("ga", 1024), ("gp", 1024)):
        segs[name] = w[:, o:o + n]
        o += n
    qi = segs["qi"].reshape(d, IDX_HEADS, IDX_DIM)
    qcat = jnp.concatenate([qi, qi, qi, jnp.zeros_like(qi)], axis=-1).reshape(d, IDX_HEADS * LANES)
    kwc = jnp.concatenate([segs["ki"], segs["ki"], segs["ki"], segs["wi"],
                           jnp.zeros((d, LANES - 3 * IDX_DIM - IDX_HEADS), w.dtype)], axis=1)
    out = jnp.concatenate([segs["q"], segs["k"], segs["v"], qcat, kwc, segs["u"], segs["ga"], segs["gp"]],
                          axis=1)
    return out.astype(bf16)


def kernel(x, mix_norm, w_in, q_norm, k_norm, w_pool, pool_scale, w_attn_proj, w_pool_proj, w_out, ffn_norm,
           dense_w1, dense_w3, dense_w2, moe_router, moe_w1, moe_w3, moe_w2):
    batch, seq, d = x.shape
    depth = w_in.shape[0]
    t = batch * seq
    assert d == D_MODEL and seq % TK == 0 and seq % TM_PROJ == 0 and seq % TM_MERGE == 0

    rqk = _rope_tables(seq, ROPE_DIM, HEAD_DIM, LANES)
    rqi = _rope_tables(seq, IDX_ROPE_DIM, IDX_DIM, LANES)
    rki = _rope_tables(seq, IDX_ROPE_DIM, IDX_DIM, 3 * IDX_DIM)
    head_of = jnp.arange(ATTN_WIDTH) // HEAD_DIM
    bd = jnp.where(head_of[:, None] == head_of[None, :], 1.0 / HEAD_DIM, 0.0).astype(bf16)
    ones_gate = jnp.ones((t, LANES), f32)

    xc = x.reshape(t, d)
    for layer in range(depth):
        w = _relayout_w_in(w_in[layer])
        qn = jnp.tile(q_norm[layer], N_HEADS)[None, :]
        kn = jnp.tile(k_norm[layer], N_HEADS)[None, :]
        q, k, v, qc, kw, u, ga, gp = _in_proj(xc, mix_norm[layer][None, :], w, qn, kn, bd, rqk, rqi, rki, seq)
        a = _dsa_attention(q, qc, kw, k, v, batch, seq)
        moe = layer % 2 == 1
        common = (xc, a, u, ga, gp, w_pool[layer].astype(bf16), pool_scale[layer][None, :],
                  w_attn_proj[layer].astype(bf16), w_pool_proj[layer].astype(bf16), w_out[layer].astype(bf16),
                  ffn_norm[layer][None, :])
        idx = layer // 2
        if moe:
            wr = jnp.pad(moe_router[idx], ((0, 0), (0, LANES - N_EXPERTS)))
            wrh = wr.astype(bf16)
            wrl = (wr - wrh.astype(f32)).astype(bf16)
            x2, h2, gate = _merge(True, seq, *common, wrh, wrl)
            xc = _ffn(True, h2, x2, gate, moe_w1[idx].astype(bf16), moe_w3[idx].astype(bf16),
                      moe_w2[idx].astype(bf16), tm=1024, tf=896)
        else:
            x2, h2 = _merge(False, seq, *common)
            xc = _ffn(False, h2, x2, ones_gate, dense_w1[idx][None].astype(bf16),
                      dense_w3[idx][None].astype(bf16), dense_w2[idx][None].astype(bf16), tm=1024, tf=1408)
    return xc.reshape(batch, seq, d)
```

```python
import functools

import jax
import jax.numpy as jnp
from jax import lax
from jax.experimental import pallas as pl
from jax.experimental.pallas import tpu as pltpu

bf16 = jnp.bfloat16
f32 = jnp.float32
i32 = jnp.int32

D_MODEL = 1024
N_HEADS = 8
HEAD_DIM = 64
ATTN_WIDTH = 512
ROPE_DIM = 16
ROPE_THETA = 500000.0
IDX_HEADS = 8
IDX_DIM = 32
IDX_ROPE_DIM = 8
TOPK_MAX = 256
POOL_GROUPS = 4
POOL_GROUP_DIM = 128
POOL_WIDTH = 512
POOL_WINDOWS = (2, 4, 8, 16)
POOL_HALO = 16
N_EXPERTS = 8
EPS = 1e-6

LANES = 128
INT_MIN = -(2 ** 31)
VMEM_LIMIT = 56 * 1024 * 1024
LOG2E = 1.4426950408889634

OFF_Q = 0
OFF_K = 512
OFF_VX = 1024
OFF_QCAT = 2048
OFF_KW = 3072
OFF_U = 3200
OFF_GA = 3712
OFF_GP = 4736
W_COLS = 5760

TM_PROJ = 512
TM_MERGE = 512
QB = 128
TK = 512


def _sigmoid(x):
    return 1.0 / (1.0 + jnp.exp(-x))


def _in_proj_kernel(x_ref, g_ref, w_ref, qn_ref, kn_ref, bd_ref, one_ref, rqk_ref, rqi_ref, rki_ref,
                    q_ref, k_ref, vx_ref, qc_ref, kw_ref, u_ref, ga_ref, gp_ref):
    x = x_ref[...]
    ms = jnp.mean(x * x, axis=-1, keepdims=True)
    h = (x * lax.rsqrt(ms + EPS) * g_ref[...]).astype(bf16)

    def proj(lo, n):
        return jnp.dot(h, w_ref[:, lo:lo + n], preferred_element_type=f32)

    def rope(xc, tab_ref, sh):
        return (xc * tab_ref[0] + pltpu.roll(xc, LANES - sh, 1) * tab_ref[1]
                + pltpu.roll(xc, sh, 1) * tab_ref[2])

    def headnorm(z, gain_ref):
        msh = jnp.dot((z * z).astype(bf16), bd_ref[...], preferred_element_type=f32)
        return z * lax.rsqrt(msh + EPS) * gain_ref[...]

    zq = headnorm(proj(OFF_Q, ATTN_WIDTH), qn_ref) * (HEAD_DIM ** -0.5 * LOG2E)
    zk = headnorm(proj(OFF_K, ATTN_WIDTH), kn_ref)
    for c in range(ATTN_WIDTH // LANES):
        sl = slice(c * LANES, (c + 1) * LANES)
        q_ref[:, sl] = rope(zq[:, sl], rqk_ref, ROPE_DIM // 2).astype(bf16)
        k_ref[:, sl] = rope(zk[:, sl], rqk_ref, ROPE_DIM // 2).astype(bf16)
    vx_ref[...] = jnp.where(one_ref[...] > 0.0, 1.0, proj(OFF_VX, N_HEADS * LANES)).astype(bf16)

    lane = lax.broadcasted_iota(i32, (1, LANES), 1)
    zc = proj(OFF_QCAT, IDX_HEADS * LANES)
    for c in range(IDX_HEADS):
        sl = slice(c * LANES, (c + 1) * LANES)
        r = rope(zc[:, sl], rqi_ref, IDX_ROPE_DIM // 2)
        lo = r - r.astype(bf16).astype(f32)
        qc_ref[:, sl] = jnp.where((lane >= 32) & (lane < 64), lo, r).astype(bf16)

    zkw = rope(proj(OFF_KW, LANES), rki_ref, IDX_ROPE_DIM // 2)
    lo = zkw - zkw.astype(bf16).astype(f32)
    zkw = jnp.where((lane >= 64) & (lane < 96), lo, zkw)
    kw_ref[...] = jnp.where(lane >= 96, zkw * ((IDX_HEADS * IDX_DIM) ** -0.5), zkw)

    u_ref[...] = proj(OFF_U, POOL_WIDTH)
    ga_ref[...] = proj(OFF_GA, D_MODEL)
    gp_ref[...] = proj(OFF_GP, D_MODEL)


def _in_proj(x2d, g, w, qn, kn, bd, ones_lane, rqk, rqi, rki, seq):
    t = x2d.shape[0]
    tm = TM_PROJ
    ns = seq // tm
    row = lambda i: (i, 0)
    const = lambda i: (0, 0)
    tab = lambda i: (0, i % ns, 0)
    out_shapes = (
        jax.ShapeDtypeStruct((t, ATTN_WIDTH), bf16),
        jax.ShapeDtypeStruct((t, ATTN_WIDTH), bf16),
        jax.ShapeDtypeStruct((t, N_HEADS * LANES), bf16),
        jax.ShapeDtypeStruct((t, IDX_HEADS * LANES), bf16),
        jax.ShapeDtypeStruct((t, LANES), f32),
        jax.ShapeDtypeStruct((t, POOL_WIDTH), f32),
        jax.ShapeDtypeStruct((t, D_MODEL), f32),
        jax.ShapeDtypeStruct((t, D_MODEL), f32),
    )
    return pl.pallas_call(
        _in_proj_kernel,
        grid=(t // tm,),
        in_specs=[
            pl.BlockSpec((tm, D_MODEL), row),
            pl.BlockSpec((1, D_MODEL), const),
            pl.BlockSpec((D_MODEL, W_COLS), const, pipeline_mode=pl.Buffered(1)),
            pl.BlockSpec((1, ATTN_WIDTH), const),
            pl.BlockSpec((1, ATTN_WIDTH), const),
            pl.BlockSpec((ATTN_WIDTH, ATTN_WIDTH), const),
            pl.BlockSpec((1, N_HEADS * LANES), const),
            pl.BlockSpec((3, tm, LANES), tab),
            pl.BlockSpec((3, tm, LANES), tab),
            pl.BlockSpec((3, tm, LANES), tab),
        ],
        out_specs=[pl.BlockSpec((tm, s.shape[1]), row) for s in out_shapes],
        out_shape=out_shapes,
        compiler_params=pltpu.CompilerParams(
            dimension_semantics=("arbitrary",), vmem_limit_bytes=VMEM_LIMIT),
        name="in_proj",
    )(x2d, g, w, qn, kn, bd, ones_lane, rqk, rqi, rki)


def _lane_fold(x, op):
    out = x[:, 0:LANES]
    for j in range(1, x.shape[1] // LANES):
        out = op(out, x[:, j * LANES:(j + 1) * LANES])
    return out


def _dsa_kernel(topk, q_ref, qc_ref, kwq_ref, k_ref, vx_ref, kw_ref, o_ref,
                key_sc, kcat_sc, qm2_sc, mrun_sc, m_sc, acc_sc):
    i = pl.program_id(1)
    nk = (i * QB + QB + TK - 1) // TK
    kf = float(topk)

    @pl.when(i == 0)
    def _():
        kcat_sc[...] = kw_ref[...].astype(bf16)

    qpos = i * QB + lax.broadcasted_iota(i32, (QB, 1), 0)
    lane_tk = lax.broadcasted_iota(i32, (1, TK), 1)
    wq = kwq_ref[...]

    def to_key(score):
        bits = pltpu.bitcast(score, i32)
        return bits ^ ((bits >> 31) & 0x7FFFFFFF)

    def score_chunk(c, carry):
        koff = pl.multiple_of(c * TK, TK)
        kc = kcat_sc[pl.ds(koff, TK), :]
        acc = jnp.zeros((QB, TK), f32)
        for h in range(IDX_HEADS):
            d = lax.dot_general(qc_ref[:, h * LANES:(h + 1) * LANES], kc,
                                (((1,), (1,)), ((), ())), preferred_element_type=f32)
            acc = acc + jnp.maximum(d, 0.0) * wq[:, 96 + h:97 + h]
        causal = (koff + lane_tk) <= qpos
        key_sc[c] = jnp.where(causal, to_key(acc), INT_MIN)
        return carry

    lax.fori_loop(0, nk, score_chunk, 0)

    def count_ge(cand):
        def body(c, part):
            return part + _lane_fold(jnp.where(key_sc[c] >= cand, 1.0, 0.0), jnp.add)
        part = lax.fori_loop(0, nk, body, jnp.zeros((QB, LANES), f32))
        return jnp.sum(part, axis=1, keepdims=True)

    zero = jnp.zeros((QB, 1), i32)
    prefix = jnp.where(count_ge(zero) >= kf, zero, INT_MIN)

    def bit_body(b, prefix):
        cand = prefix | jnp.left_shift(1, 30 - b)
        return jnp.where(count_ge(cand) >= kf, cand, prefix)

    thr_raw = lax.fori_loop(0, 31, bit_body, prefix)
    thr = jnp.maximum(thr_raw, INT_MIN + 1)

    is_tie = (count_ge(thr) > kf) & (thr_raw > INT_MIN)

    @pl.when(jnp.max(jnp.where(is_tie, 1.0, 0.0)) > 0.0)
    def _():
        need = kf - count_ge(thr + 1)

        def count_eq_below(m):
            def body(c, part):
                hit = (key_sc[c] == thr) & ((c * TK + lane_tk) < m)
                return part + _lane_fold(jnp.where(hit, 1.0, 0.0), jnp.add)
            part = lax.fori_loop(0, nk, body, jnp.zeros((QB, LANES), f32))
            return jnp.sum(part, axis=1, keepdims=True)

        mprime = jnp.zeros((QB, 1), i32)
        bit = key_sc.shape[0] * TK // 2
        while bit >= 1:
            cand = mprime | bit
            mprime = jnp.where(count_eq_below(cand) < need, cand, mprime)
            bit //= 2

        def drop_ties(c, carry):
            key = key_sc[c]
            drop = is_tie & (key == thr) & ((c * TK + lane_tk) > mprime)
            key_sc[c] = jnp.where(drop, thr - 1, key)
            return carry

        lax.fori_loop(0, nk, drop_ties, 0)

    lane = lax.broadcasted_iota(i32, (1, LANES), 1)
    for pr in range(N_HEADS // 2):
        qp = q_ref[:, pr * LANES:(pr + 1) * LANES]
        qm2_sc[pr, 0:QB, :] = jnp.where(lane < HEAD_DIM, qp, jnp.zeros((), bf16))
        qm2_sc[pr, QB:2 * QB, :] = jnp.where(lane >= HEAD_DIM, qp, jnp.zeros((), bf16))
    mrun_sc[...] = jnp.full(mrun_sc.shape, -jnp.inf, f32)
    acc_sc[...] = jnp.zeros(acc_sc.shape, f32)

    def pair_logits(c, pr, bias):
        koff = pl.multiple_of(c * TK, TK)
        kc = k_ref[pl.ds(koff, TK), pr * LANES:(pr + 1) * LANES]
        s2 = lax.dot_general(qm2_sc[pr], kc, (((1,), (1,)), ((), ())), preferred_element_type=f32)
        return s2[0:QB] + bias, s2[QB:2 * QB] + bias

    def max_chunk(c, carry):
        bias = jnp.where(key_sc[c] >= thr, 0.0, -jnp.inf)
        for pr in range(N_HEADS // 2):
            sa, sb = pair_logits(c, pr, bias)
            mrun_sc[pr, 0:QB, :] = jnp.maximum(mrun_sc[pr, 0:QB, :], _lane_fold(sa, jnp.maximum))
            mrun_sc[pr, QB:2 * QB, :] = jnp.maximum(mrun_sc[pr, QB:2 * QB, :], _lane_fold(sb, jnp.maximum))
        return carry

    lax.fori_loop(0, nk, max_chunk, 0)

    for pr in range(N_HEADS // 2):
        m = jnp.max(mrun_sc[pr], axis=1, keepdims=True)
        m_sc[pr] = jnp.where(m == -jnp.inf, 0.0, m)

    def acc_chunk(c, carry):
        koff = pl.multiple_of(c * TK, TK)
        bias = jnp.where(key_sc[c] >= thr, 0.0, -jnp.inf)
        for pr in range(N_HEADS // 2):
            sa, sb = pair_logits(c, pr, bias)
            m = m_sc[pr]
            pa = jnp.exp2(sa - m[0:QB]).astype(bf16)
            pb = jnp.exp2(sb - m[QB:2 * QB]).astype(bf16)
            va = vx_ref[pl.ds(koff, TK), (2 * pr) * LANES:(2 * pr + 1) * LANES]
            vb = vx_ref[pl.ds(koff, TK), (2 * pr + 1) * LANES:(2 * pr + 2) * LANES]
            acc_sc[2 * pr] += jnp.dot(pa, va, preferred_element_type=f32)
            acc_sc[2 * pr + 1] += jnp.dot(pb, vb, preferred_element_type=f32)
        return carry

    lax.fori_loop(0, nk, acc_chunk, 0)

    for pr in range(N_HEADS // 2):
        a0 = acc_sc[2 * pr]
        a1 = acc_sc[2 * pr + 1]
        o0 = a0 / a0[:, HEAD_DIM:HEAD_DIM + 1]
        o1 = a1 / a1[:, 0:1]
        o_ref[:, pr * LANES:(pr + 1) * LANES] = jnp.where(lane < HEAD_DIM, o0, o1).astype(bf16)


def _dsa_attention(q, qc, kw, k, vx, batch, seq):
    t = q.shape[0]
    nq = seq // QB
    topk = min(TOPK_MAX, seq // 4)
    qrow = lambda b, i: (b * nq + i, 0)
    per_batch = lambda b, i: (b, 0)
    return pl.pallas_call(
        functools.partial(_dsa_kernel, topk),
        grid=(batch, nq),
        in_specs=[
            pl.BlockSpec((QB, ATTN_WIDTH), qrow),
            pl.BlockSpec((QB, IDX_HEADS * LANES), qrow),
            pl.BlockSpec((QB, LANES), qrow),
            pl.BlockSpec((seq, ATTN_WIDTH), per_batch, pipeline_mode=pl.Buffered(1)),
            pl.BlockSpec((seq, N_HEADS * LANES), per_batch, pipeline_mode=pl.Buffered(1)),
            pl.BlockSpec((seq, LANES), per_batch, pipeline_mode=pl.Buffered(1)),
        ],
        out_specs=pl.BlockSpec((QB, ATTN_WIDTH), qrow),
        out_shape=jax.ShapeDtypeStruct((t, ATTN_WIDTH), bf16),
        scratch_shapes=[
            pltpu.VMEM((seq // TK, QB, TK), i32),
            pltpu.VMEM((seq, LANES), bf16),
            pltpu.VMEM((N_HEADS // 2, 2 * QB, LANES), bf16),
            pltpu.VMEM((N_HEADS // 2, 2 * QB, LANES), f32),
            pltpu.VMEM((N_HEADS // 2, 2 * QB, 1), f32),
            pltpu.VMEM((N_HEADS, QB, LANES), f32),
        ],
        compiler_params=pltpu.CompilerParams(
            dimension_semantics=("arbitrary", "arbitrary"), vmem_limit_bytes=VMEM_LIMIT),
        name="dsa_attention",
    )(q, qc, kw, k, vx, kw)


def _merge_kernel(moe, seq, *refs):
    if moe:
        (x_ref, a_ref, u_ref, uh_ref, ga_ref, gp_ref, wp_ref, ps_ref, pa_ref, pb_ref, wo_ref, g2_ref,
         wrh_ref, wrl_ref, x2_ref, h2_ref, gate_ref, e_sc) = refs
    else:
        (x_ref, a_ref, u_ref, uh_ref, ga_ref, gp_ref, wp_ref, ps_ref, pa_ref, pb_ref, wo_ref, g2_ref,
         x2_ref, h2_ref, e_sc) = refs
    tm = TM_MERGE
    i = pl.program_id(0)
    ti = i % (seq // tm)
    e_sc[0:POOL_HALO, :] = jnp.where(ti == 0, 0.0, uh_ref[...])
    e_sc[POOL_HALO:POOL_HALO + tm, :] = u_ref[...]
    npos = (ti * tm + 1 + lax.broadcasted_iota(i32, (tm, 1), 0)).astype(f32)

    parts = []
    for g, w in enumerate(POOL_WINDOWS):
        sl = slice(g * POOL_GROUP_DIM, (g + 1) * POOL_GROUP_DIM)
        tot = e_sc[POOL_HALO:POOL_HALO + tm, sl]
        for j in range(1, w):
            tot = tot + e_sc[POOL_HALO - j:POOL_HALO - j + tm, sl]
        diff = tot / jnp.minimum(npos, float(w)) - u_ref[:, sl]
        parts.append(jnp.dot(diff.astype(bf16), wp_ref[g], preferred_element_type=f32))
    p = jnp.concatenate(parts, axis=1) * ps_ref[...]

    ab = jnp.dot(a_ref[...], pa_ref[...], preferred_element_type=f32)
    pb = jnp.dot(p.astype(bf16), pb_ref[...], preferred_element_type=f32)
    merged = _sigmoid(ga_ref[...]) * ab + _sigmoid(gp_ref[...]) * pb
    x2 = x_ref[...] + jnp.dot(merged.astype(bf16), wo_ref[...], preferred_element_type=f32)
    x2_ref[...] = x2
    ms = jnp.mean(x2 * x2, axis=-1, keepdims=True)
    hf = x2 * lax.rsqrt(ms + EPS) * g2_ref[...]
    h2_ref[...] = hf.astype(bf16)

    if moe:
        hi = hf.astype(bf16)
        lo = (hf - hi.astype(f32)).astype(bf16)
        logits = (jnp.dot(hi, wrh_ref[...], preferred_element_type=f32)
                  + jnp.dot(lo, wrh_ref[...], preferred_element_type=f32)
                  + jnp.dot(hi, wrl_ref[...], preferred_element_type=f32))
        lanef = lax.broadcasted_iota(i32, (1, LANES), 1).astype(f32)
        lg = jnp.where(lanef < N_EXPERTS, logits, -jnp.inf)
        v1 = jnp.max(lg, axis=1, keepdims=True)
        i1 = jnp.min(jnp.where(lg == v1, lanef, float(LANES)), axis=1, keepdims=True)
        lg2 = jnp.where(lanef == i1, -jnp.inf, lg)
        v2 = jnp.max(lg2, axis=1, keepdims=True)
        i2 = jnp.min(jnp.where(lg2 == v2, lanef, float(LANES)), axis=1, keepdims=True)
        tt = jnp.exp(v2 - v1)
        w1 = 1.0 / (1.0 + tt)
        w2 = tt / (1.0 + tt)
        gate_ref[...] = jnp.where(lanef == i1, w1, 0.0) + jnp.where(lanef == i2, w2, 0.0)


def _merge(moe, seq, x2d, a, u, ga, gp, wp, ps, pa, pb, wo, g2, wrh=None, wrl=None):
    t = x2d.shape[0]
    tm = TM_MERGE
    row = lambda i: (i, 0)
    const2 = lambda i: (0, 0)
    const3 = lambda i: (0, 0, 0)
    halo = lambda i: (jnp.maximum(i * (tm // POOL_HALO) - 1, 0), 0)
    in_specs = [
        pl.BlockSpec((tm, D_MODEL), row),
        pl.BlockSpec((tm, ATTN_WIDTH), row),
        pl.BlockSpec((tm, POOL_WIDTH), row),
        pl.BlockSpec((POOL_HALO, POOL_WIDTH), halo),
        pl.BlockSpec((tm, D_MODEL), row),
        pl.BlockSpec((tm, D_MODEL), row),
        pl.BlockSpec((POOL_GROUPS, POOL_GROUP_DIM, POOL_GROUP_DIM), const3),
        pl.BlockSpec((1, POOL_WIDTH), const2),
        pl.BlockSpec((ATTN_WIDTH, D_MODEL), const2),
        pl.BlockSpec((POOL_WIDTH, D_MODEL), const2),
        pl.BlockSpec((D_MODEL, D_MODEL), const2),
        pl.BlockSpec((1, D_MODEL), const2),
    ]
    args = [x2d, a, u, u, ga, gp, wp, ps, pa, pb, wo, g2]
    out_shapes = [jax.ShapeDtypeStruct((t, D_MODEL), f32), jax.ShapeDtypeStruct((t, D_MODEL), bf16)]
    out_specs = [pl.BlockSpec((tm, D_MODEL), row), pl.BlockSpec((tm, D_MODEL), row)]
    if moe:
        in_specs += [pl.BlockSpec((D_MODEL, LANES), const2), pl.BlockSpec((D_MODEL, LANES), const2)]
        args += [wrh, wrl]
        out_shapes.append(jax.ShapeDtypeStruct((t, LANES), f32))
        out_specs.append(pl.BlockSpec((tm, LANES), row))
    return pl.pallas_call(
        functools.partial(_merge_kernel, moe, seq),
        grid=(t // tm,),
        in_specs=in_specs,
        out_specs=out_specs,
        out_shape=out_shapes,
        scratch_shapes=[pltpu.VMEM((tm + POOL_HALO, POOL_WIDTH), f32)],
        compiler_params=pltpu.CompilerParams(
            dimension_semantics=("arbitrary",), vmem_limit_bytes=VMEM_LIMIT),
        name="merge_moe" if moe else "merge_dense",
    )(*args)


def _ffn_kernel(gated, h_ref, x_ref, gate_ref, w1_ref, w3_ref, w2_ref, o_ref):
    e = pl.program_id(1)
    j = pl.program_id(2)

    @pl.when((e == 0) & (j == 0))
    def _():
        o_ref[...] = x_ref[...]

    h = h_ref[...]
    a = jnp.dot(h, w1_ref[0], preferred_element_type=f32)
    b = jnp.dot(h, w3_ref[0], preferred_element_type=f32)
    act = (a * _sigmoid(a) * b).astype(bf16)
    y = jnp.dot(act, w2_ref[0], preferred_element_type=f32)
    if gated:
        lane = lax.broadcasted_iota(i32, (1, LANES), 1)
        g = jnp.sum(jnp.where(lane == e, gate_ref[...], 0.0), axis=1, keepdims=True)
        y = g * y
    o_ref[...] += y


def _ffn(gated, h2, x2, gate, w1, w3, w2, tm, tf):
    t = h2.shape[0]
    ne, _, ff = w1.shape
    row = lambda i, e, j: (i, 0)
    return pl.pallas_call(
        functools.partial(_ffn_kernel, gated),
        grid=(t // tm, ne, ff // tf),
        in_specs=[
            pl.BlockSpec((tm, D_MODEL), row),
            pl.BlockSpec((tm, D_MODEL), row),
            pl.BlockSpec((tm, LANES), row),
            pl.BlockSpec((1, D_MODEL, tf), lambda i, e, j: (e, 0, j)),
            pl.BlockSpec((1, D_MODEL, tf), lambda i, e, j: (e, 0, j)),
            pl.BlockSpec((1, tf, D_MODEL), lambda i, e, j: (e, j, 0)),
        ],
        out_specs=pl.BlockSpec((tm, D_MODEL), row),
        out_shape=jax.ShapeDtypeStruct((t, D_MODEL), f32),
        compiler_params=pltpu.CompilerParams(
            dimension_semantics=("arbitrary", "arbitrary", "arbitrary"), vmem_limit_bytes=VMEM_LIMIT),
        name="ffn_moe" if gated else "ffn_dense",
    )(h2, x2, gate, w1, w3, w2)


def _rope_tables(seq, rot_dim, period, active_lanes):
    half = rot_dim // 2
    inv = jnp.power(ROPE_THETA, -jnp.arange(0, rot_dim, 2, dtype=f32) / rot_dim)
    ang = jnp.arange(seq, dtype=f32)[:, None] * inv[None, :]
    cos, sin = jnp.cos(ang), jnp.sin(ang)
    ones = jnp.ones((seq, period - rot_dim), f32)
    zeros = lambda n: jnp.zeros((seq, n), f32)
    cos_p = jnp.concatenate([cos, cos, ones], axis=1)
    sin_a = jnp.concatenate([-sin, zeros(period - half)], axis=1)
    sin_b = jnp.concatenate([zeros(half), sin, zeros(period - rot_dim)], axis=1)
    reps = LANES // period
    tabs = [jnp.tile(tb, (1, reps)) for tb in (cos_p, sin_a, sin_b)]
    live = (jnp.arange(LANES) < active_lanes)[None, :]
    tabs = [jnp.where(live, tabs[0], 1.0), jnp.where(live, tabs[1], 0.0), jnp.where(live, tabs[2], 0.0)]
    return jnp.stack(tabs, axis=0)


def _relayout_w_in(w):
    d = w.shape[0]
    o = 0
    segs = {}
    for name, n in (("q", 512), ("k", 512), ("v", 512), ("qi", 256), ("ki", 32), ("wi", 8),
                    ("u", 512), ("ga", 1024), ("gp", 1024)):
        segs[name] = w[:, o:o + n]
        o += n
    qi = segs["qi"].reshape(d, IDX_HEADS, IDX_DIM)
    qcat = jnp.concatenate([qi, qi, qi, jnp.zeros_like(qi)], axis=-1).reshape(d, IDX_HEADS * LANES)
    kwc = jnp.concatenate([segs["ki"], segs["ki"], segs["ki"], segs["wi"],
                           jnp.zeros((d, LANES - 3 * IDX_DIM - IDX_HEADS), w.dtype)], axis=1)
    vh = segs["v"].reshape(d, N_HEADS // 2, 2, HEAD_DIM)
    zh = jnp.zeros_like(vh[:, :, 0])
    vx = jnp.stack([jnp.concatenate([vh[:, :, 0], zh], axis=-1), jnp.concatenate([zh, vh[:, :, 1]], axis=-1)],
                   axis=2).reshape(d, N_HEADS * LANES)
    out = jnp.concatenate([segs["q"], segs["k"], vx, qcat, kwc, segs["u"], segs["ga"], segs["gp"]], axis=1)
    return out.astype(bf16)


def kernel(x, mix_norm, w_in, q_norm, k_norm, w_pool, pool_scale, w_attn_proj, w_pool_proj, w_out, ffn_norm,
           dense_w1, dense_w3, dense_w2, moe_router, moe_w1, moe_w3, moe_w2):
    batch, seq, d = x.shape
    depth = w_in.shape[0]
    t = batch * seq
    assert d == D_MODEL and seq % TK == 0 and seq % TM_PROJ == 0 and seq % TM_MERGE == 0

    rqk = _rope_tables(seq, ROPE_DIM, HEAD_DIM, LANES)
    rqi = _rope_tables(seq, IDX_ROPE_DIM, IDX_DIM, LANES)
    rki = _rope_tables(seq, IDX_ROPE_DIM, IDX_DIM, 3 * IDX_DIM)
    head_of = jnp.arange(ATTN_WIDTH) // HEAD_DIM
    bd = jnp.where(head_of[:, None] == head_of[None, :], 1.0 / HEAD_DIM, 0.0).astype(bf16)
    ones_gate = jnp.ones((t, LANES), f32)
    lane_in_tile = jnp.arange(N_HEADS * LANES) % LANES
    odd_head = (jnp.arange(N_HEADS * LANES) // LANES) % 2 == 1
    ones_lane = (lane_in_tile == jnp.where(odd_head, 0, HEAD_DIM)).astype(f32)[None, :]

    xc = x.reshape(t, d)
    for layer in range(depth):
        w = _relayout_w_in(w_in[layer])
        qn = jnp.tile(q_norm[layer], N_HEADS)[None, :]
        kn = jnp.tile(k_norm[layer], N_HEADS)[None, :]
        q, k, vx, qc, kw, u, ga, gp = _in_proj(xc, mix_norm[layer][None, :], w, qn, kn, bd, ones_lane,
                                               rqk, rqi, rki, seq)
        a = _dsa_attention(q, qc, kw, k, vx, batch, seq)
        moe = layer % 2 == 1
        common = (xc, a, u, ga, gp, w_pool[layer].astype(bf16), pool_scale[layer][None, :],
                  w_attn_proj[layer].astype(bf16), w_pool_proj[layer].astype(bf16), w_out[layer].astype(bf16),
                  ffn_norm[layer][None, :])
        idx = layer // 2
        if moe:
            wr = jnp.pad(moe_router[idx], ((0, 0), (0, LANES - N_EXPERTS)))
            wrh = wr.astype(bf16)
            wrl = (wr - wrh.astype(f32)).astype(bf16)
            x2, h2, gate = _merge(True, seq, *common, wrh, wrl)
            xc = _ffn(True, h2, x2, gate, moe_w1[idx].astype(bf16), moe_w3[idx].astype(bf16),
                      moe_w2[idx].astype(bf16), tm=1024, tf=896)
        else:
            x2, h2 = _merge(False, seq, *common)
            xc = _ffn(False, h2, x2, ones_gate, dense_w1[idx][None].astype(bf16),
                      dense_w3[idx][None].astype(bf16), dense_w2[idx][None].astype(bf16), tm=1024, tf=1408)
    return xc.reshape(batch, seq, d)
```

```python
import functools

import jax
import jax.numpy as jnp
from jax import lax
from jax.experimental import pallas as pl
from jax.experimental.pallas import tpu as pltpu

bf16 = jnp.bfloat16
f32 = jnp.float32
i32 = jnp.int32

D_MODEL = 1024
N_HEADS = 8
HEAD_DIM = 64
ATTN_WIDTH = 512
ROPE_DIM = 16
ROPE_THETA = 500000.0
IDX_HEADS = 8
IDX_DIM = 32
IDX_ROPE_DIM = 8
TOPK_MAX = 256
POOL_GROUPS = 4
POOL_GROUP_DIM = 128
POOL_WIDTH = 512
POOL_WINDOWS = (2, 4, 8, 16)
POOL_HALO = 16
N_EXPERTS = 8
EPS = 1e-6

LANES = 128
INT_MIN = -(2 ** 31)
VMEM_LIMIT = 56 * 1024 * 1024
LOG2E = 1.4426950408889634

OFF_Q = 0
OFF_K = 512
OFF_VX = 1024
OFF_QCAT = 2048
OFF_KW = 3072
OFF_U = 3200
OFF_GA = 3712
OFF_GP = 4736
W_COLS = 5760

ROUTE_W1, ROUTE_W2, ROUTE_I1, ROUTE_I2 = 8, 9, 10, 11

TM_PROJ = 512
TM_MERGE = 512
TM_RANK = 512
TM_EXPERT = 512
TF_EXPERT = 896
TM_ROWS = 256
QB = 128
TK = 512


def _sigmoid(x):
    return 1.0 / (1.0 + jnp.exp(-x))


def _in_proj_kernel(x_ref, g_ref, w_ref, qn_ref, kn_ref, bd_ref, one_ref, rqk_ref, rqi_ref, rki_ref,
                    q_ref, k_ref, vx_ref, qc_ref, kw_ref, u_ref, ga_ref, gp_ref):
    x = x_ref[...]
    ms = jnp.mean(x * x, axis=-1, keepdims=True)
    h = (x * lax.rsqrt(ms + EPS) * g_ref[...]).astype(bf16)

    def proj(lo, n):
        return jnp.dot(h, w_ref[:, lo:lo + n], preferred_element_type=f32)

    def rope(xc, tab_ref, sh):
        return (xc * tab_ref[0] + pltpu.roll(xc, LANES - sh, 1) * tab_ref[1]
                + pltpu.roll(xc, sh, 1) * tab_ref[2])

    def headnorm(z, gain_ref):
        msh = jnp.dot((z * z).astype(bf16), bd_ref[...], preferred_element_type=f32)
        return z * lax.rsqrt(msh + EPS) * gain_ref[...]

    zq = headnorm(proj(OFF_Q, ATTN_WIDTH), qn_ref) * (HEAD_DIM ** -0.5 * LOG2E)
    zk = headnorm(proj(OFF_K, ATTN_WIDTH), kn_ref)
    for c in range(ATTN_WIDTH // LANES):
        sl = slice(c * LANES, (c + 1) * LANES)
        q_ref[:, sl] = rope(zq[:, sl], rqk_ref, ROPE_DIM // 2).astype(bf16)
        k_ref[:, sl] = rope(zk[:, sl], rqk_ref, ROPE_DIM // 2).astype(bf16)
    vx_ref[...] = jnp.where(one_ref[...] > 0.0, 1.0, proj(OFF_VX, N_HEADS * LANES)).astype(bf16)

    lane = lax.broadcasted_iota(i32, (1, LANES), 1)
    zc = proj(OFF_QCAT, IDX_HEADS * LANES)
    for c in range(IDX_HEADS):
        sl = slice(c * LANES, (c + 1) * LANES)
        r = rope(zc[:, sl], rqi_ref, IDX_ROPE_DIM // 2)
        lo = r - r.astype(bf16).astype(f32)
        qc_ref[:, sl] = jnp.where((lane >= 32) & (lane < 64), lo, r).astype(bf16)

    zkw = rope(proj(OFF_KW, LANES), rki_ref, IDX_ROPE_DIM // 2)
    lo = zkw - zkw.astype(bf16).astype(f32)
    zkw = jnp.where((lane >= 64) & (lane < 96), lo, zkw)
    kw_ref[...] = jnp.where(lane >= 96, zkw * ((IDX_HEADS * IDX_DIM) ** -0.5), zkw)

    u_ref[...] = proj(OFF_U, POOL_WIDTH)
    ga_ref[...] = proj(OFF_GA, D_MODEL)
    gp_ref[...] = proj(OFF_GP, D_MODEL)


def _in_proj(x2d, g, w, qn, kn, bd, ones_lane, rqk, rqi, rki, seq):
    t = x2d.shape[0]
    tm = TM_PROJ
    ns = seq // tm
    row = lambda i: (i, 0)
    const = lambda i: (0, 0)
    tab = lambda i: (0, i % ns, 0)
    out_shapes = (
        jax.ShapeDtypeStruct((t, ATTN_WIDTH), bf16),
        jax.ShapeDtypeStruct((t, ATTN_WIDTH), bf16),
        jax.ShapeDtypeStruct((t, N_HEADS * LANES), bf16),
        jax.ShapeDtypeStruct((t, IDX_HEADS * LANES), bf16),
        jax.ShapeDtypeStruct((t, LANES), f32),
        jax.ShapeDtypeStruct((t, POOL_WIDTH), f32),
        jax.ShapeDtypeStruct((t, D_MODEL), f32),
        jax.ShapeDtypeStruct((t, D_MODEL), f32),
    )
    return pl.pallas_call(
        _in_proj_kernel,
        grid=(t // tm,),
        in_specs=[
            pl.BlockSpec((tm, D_MODEL), row),
            pl.BlockSpec((1, D_MODEL), const),
            pl.BlockSpec((D_MODEL, W_COLS), const, pipeline_mode=pl.Buffered(1)),
            pl.BlockSpec((1, ATTN_WIDTH), const),
            pl.BlockSpec((1, ATTN_WIDTH), const),
            pl.BlockSpec((ATTN_WIDTH, ATTN_WIDTH), const),
            pl.BlockSpec((1, N_HEADS * LANES), const),
            pl.BlockSpec((3, tm, LANES), tab),
            pl.BlockSpec((3, tm, LANES), tab),
            pl.BlockSpec((3, tm, LANES), tab),
        ],
        out_specs=[pl.BlockSpec((tm, s.shape[1]), row) for s in out_shapes],
        out_shape=out_shapes,
        compiler_params=pltpu.CompilerParams(
            dimension_semantics=("arbitrary",), vmem_limit_bytes=VMEM_LIMIT),
        name="in_proj",
    )(x2d, g, w, qn, kn, bd, ones_lane, rqk, rqi, rki)


def _lane_fold(x, op):
    out = x[:, 0:LANES]
    for j in range(1, x.shape[1] // LANES):
        out = op(out, x[:, j * LANES:(j + 1) * LANES])
    return out


def _dsa_kernel(topk, q_ref, qc_ref, kwq_ref, k_ref, vx_ref, kw_ref, o_ref,
                key_sc, kcat_sc, qm2_sc, mrun_sc, m_sc, acc_sc):
    i = pl.program_id(1)
    nk = (i * QB + QB + TK - 1) // TK
    kf = float(topk)

    @pl.when(i == 0)
    def _():
        kcat_sc[...] = kw_ref[...].astype(bf16)

    qpos = i * QB + lax.broadcasted_iota(i32, (QB, 1), 0)
    lane_tk = lax.broadcasted_iota(i32, (1, TK), 1)
    wq = kwq_ref[...]

    def to_key(score):
        bits = pltpu.bitcast(score, i32)
        return bits ^ ((bits >> 31) & 0x7FFFFFFF)

    def score_chunk(c, carry):
        koff = pl.multiple_of(c * TK, TK)
        kc = kcat_sc[pl.ds(koff, TK), :]
        acc = jnp.zeros((QB, TK), f32)
        for h in range(IDX_HEADS):
            d = lax.dot_general(qc_ref[:, h * LANES:(h + 1) * LANES], kc,
                                (((1,), (1,)), ((), ())), preferred_element_type=f32)
            acc = acc + jnp.maximum(d, 0.0) * wq[:, 96 + h:97 + h]
        causal = (koff + lane_tk) <= qpos
        key_sc[c] = jnp.where(causal, to_key(acc), INT_MIN)
        return carry

    lax.fori_loop(0, nk, score_chunk, 0)

    def count_ge(cand):
        def body(c, part):
            return part + _lane_fold(jnp.where(key_sc[c] >= cand, 1.0, 0.0), jnp.add)
        part = lax.fori_loop(0, nk, body, jnp.zeros((QB, LANES), f32))
        return jnp.sum(part, axis=1, keepdims=True)

    zero = jnp.zeros((QB, 1), i32)
    prefix = jnp.where(count_ge(zero) >= kf, zero, INT_MIN)

    def bit_body(b, prefix):
        cand = prefix | jnp.left_shift(1, 30 - b)
        return jnp.where(count_ge(cand) >= kf, cand, prefix)

    thr_raw = lax.fori_loop(0, 31, bit_body, prefix)
    thr = jnp.maximum(thr_raw, INT_MIN + 1)

    is_tie = (count_ge(thr) > kf) & (thr_raw > INT_MIN)

    @pl.when(jnp.max(jnp.where(is_tie, 1.0, 0.0)) > 0.0)
    def _():
        need = kf - count_ge(thr + 1)

        def count_eq_below(m):
            def body(c, part):
                hit = (key_sc[c] == thr) & ((c * TK + lane_tk) < m)
                return part + _lane_fold(jnp.where(hit, 1.0, 0.0), jnp.add)
            part = lax.fori_loop(0, nk, body, jnp.zeros((QB, LANES), f32))
            return jnp.sum(part, axis=1, keepdims=True)

        mprime = jnp.zeros((QB, 1), i32)
        bit = key_sc.shape[0] * TK // 2
        while bit >= 1:
            cand = mprime | bit
            mprime = jnp.where(count_eq_below(cand) < need, cand, mprime)
            bit //= 2

        def drop_ties(c, carry):
            key = key_sc[c]
            drop = is_tie & (key == thr) & ((c * TK + lane_tk) > mprime)
            key_sc[c] = jnp.where(drop, thr - 1, key)
            return carry

        lax.fori_loop(0, nk, drop_ties, 0)

    lane = lax.broadcasted_iota(i32, (1, LANES), 1)
    for pr in range(N_HEADS // 2):
        qp = q_ref[:, pr * LANES:(pr + 1) * LANES]
        qm2_sc[pr, 0:QB, :] = jnp.where(lane < HEAD_DIM, qp, jnp.zeros((), bf16))
        qm2_sc[pr, QB:2 * QB, :] = jnp.where(lane >= HEAD_DIM, qp, jnp.zeros((), bf16))
    mrun_sc[...] = jnp.full(mrun_sc.shape, -jnp.inf, f32)
    acc_sc[...] = jnp.zeros(acc_sc.shape, f32)

    def pair_logits(c, pr, bias):
        koff = pl.multiple_of(c * TK, TK)
        kc = k_ref[pl.ds(koff, TK), pr * LANES:(pr + 1) * LANES]
        s2 = lax.dot_general(qm2_sc[pr], kc, (((1,), (1,)), ((), ())), preferred_element_type=f32)
        return s2[0:QB] + bias, s2[QB:2 * QB] + bias

    def max_chunk(c, carry):
        bias = jnp.where(key_sc[c] >= thr, 0.0, -jnp.inf)
        for pr in range(N_HEADS // 2):
            sa, sb = pair_logits(c, pr, bias)
            mrun_sc[pr, 0:QB, :] = jnp.maximum(mrun_sc[pr, 0:QB, :], _lane_fold(sa, jnp.maximum))
            mrun_sc[pr, QB:2 * QB, :] = jnp.maximum(mrun_sc[pr, QB:2 * QB, :], _lane_fold(sb, jnp.maximum))
        return carry

    lax.fori_loop(0, nk, max_chunk, 0)

    for pr in range(N_HEADS // 2):
        m = jnp.max(mrun_sc[pr], axis=1, keepdims=True)
        m_sc[pr] = jnp.where(m == -jnp.inf, 0.0, m)

    def acc_chunk(c, carry):
        koff = pl.multiple_of(c * TK, TK)
        bias = jnp.where(key_sc[c] >= thr, 0.0, -jnp.inf)
        for pr in range(N_HEADS // 2):
            sa, sb = pair_logits(c, pr, bias)
            m = m_sc[pr]
            pa = jnp.exp2(sa - m[0:QB]).astype(bf16)
            pb = jnp.exp2(sb - m[QB:2 * QB]).astype(bf16)
            va = vx_ref[pl.ds(koff, TK), (2 * pr) * LANES:(2 * pr + 1) * LANES]
            vb = vx_ref[pl.ds(koff, TK), (2 * pr + 1) * LANES:(2 * pr + 2) * LANES]
            acc_sc[2 * pr] += jnp.dot(pa, va, preferred_element_type=f32)
            acc_sc[2 * pr + 1] += jnp.dot(pb, vb, preferred_element_type=f32)
        return carry

    lax.fori_loop(0, nk, acc_chunk, 0)

    for pr in range(N_HEADS // 2):
        a0 = acc_sc[2 * pr]
        a1 = acc_sc[2 * pr + 1]
        o0 = a0 / a0[:, HEAD_DIM:HEAD_DIM + 1]
        o1 = a1 / a1[:, 0:1]
        o_ref[:, pr * LANES:(pr + 1) * LANES] = jnp.where(lane < HEAD_DIM, o0, o1).astype(bf16)


def _dsa_attention(q, qc, kw, k, vx, batch, seq):
    t = q.shape[0]
    nq = seq // QB
    topk = min(TOPK_MAX, seq // 4)
    qrow = lambda b, i: (b * nq + i, 0)
    per_batch = lambda b, i: (b, 0)
    return pl.pallas_call(
        functools.partial(_dsa_kernel, topk),
        grid=(batch, nq),
        in_specs=[
            pl.BlockSpec((QB, ATTN_WIDTH), qrow),
            pl.BlockSpec((QB, IDX_HEADS * LANES), qrow),
            pl.BlockSpec((QB, LANES), qrow),
            pl.BlockSpec((seq, ATTN_WIDTH), per_batch, pipeline_mode=pl.Buffered(1)),
            pl.BlockSpec((seq, N_HEADS * LANES), per_batch, pipeline_mode=pl.Buffered(1)),
            pl.BlockSpec((seq, LANES), per_batch, pipeline_mode=pl.Buffered(1)),
        ],
        out_specs=pl.BlockSpec((QB, ATTN_WIDTH), qrow),
        out_shape=jax.ShapeDtypeStruct((t, ATTN_WIDTH), bf16),
        scratch_shapes=[
            pltpu.VMEM((seq // TK, QB, TK), i32),
            pltpu.VMEM((seq, LANES), bf16),
            pltpu.VMEM((N_HEADS // 2, 2 * QB, LANES), bf16),
            pltpu.VMEM((N_HEADS // 2, 2 * QB, LANES), f32),
            pltpu.VMEM((N_HEADS // 2, 2 * QB, 1), f32),
            pltpu.VMEM((N_HEADS, QB, LANES), f32),
        ],
        compiler_params=pltpu.CompilerParams(
            dimension_semantics=("arbitrary", "arbitrary"), vmem_limit_bytes=VMEM_LIMIT),
        name="dsa_attention",
    )(q, qc, kw, k, vx, kw)


def _merge_kernel(moe, seq, *refs):
    if moe:
        (x_ref, a_ref, u_ref, uh_ref, ga_ref, gp_ref, wp_ref, ps_ref, pa_ref, pb_ref, wo_ref, g2_ref,
         wrh_ref, wrl_ref, x2_ref, h2_ref, gate_ref, e_sc) = refs
    else:
        (x_ref, a_ref, u_ref, uh_ref, ga_ref, gp_ref, wp_ref, ps_ref, pa_ref, pb_ref, wo_ref, g2_ref,
         x2_ref, h2_ref, e_sc) = refs
    tm = TM_MERGE
    i = pl.program_id(0)
    ti = i % (seq // tm)
    e_sc[0:POOL_HALO, :] = jnp.where(ti == 0, 0.0, uh_ref[...])
    e_sc[POOL_HALO:POOL_HALO + tm, :] = u_ref[...]
    npos = (ti * tm + 1 + lax.broadcasted_iota(i32, (tm, 1), 0)).astype(f32)

    parts = []
    for g, w in enumerate(POOL_WINDOWS):
        sl = slice(g * POOL_GROUP_DIM, (g + 1) * POOL_GROUP_DIM)
        tot = e_sc[POOL_HALO:POOL_HALO + tm, sl]
        for j in range(1, w):
            tot = tot + e_sc[POOL_HALO - j:POOL_HALO - j + tm, sl]
        diff = tot / jnp.minimum(npos, float(w)) - u_ref[:, sl]
        parts.append(jnp.dot(diff.astype(bf16), wp_ref[g], preferred_element_type=f32))
    p = jnp.concatenate(parts, axis=1) * ps_ref[...]

    ab = jnp.dot(a_ref[...], pa_ref[...], preferred_element_type=f32)
    pb = jnp.dot(p.astype(bf16), pb_ref[...], preferred_element_type=f32)
    merged = _sigmoid(ga_ref[...]) * ab + _sigmoid(gp_ref[...]) * pb
    x2 = x_ref[...] + jnp.dot(merged.astype(bf16), wo_ref[...], preferred_element_type=f32)
    x2_ref[...] = x2
    ms = jnp.mean(x2 * x2, axis=-1, keepdims=True)
    hf = x2 * lax.rsqrt(ms + EPS) * g2_ref[...]
    h2_ref[...] = hf.astype(h2_ref.dtype)

    if moe:
        hi = hf.astype(bf16)
        lo = (hf - hi.astype(f32)).astype(bf16)
        logits = (jnp.dot(hi, wrh_ref[...], preferred_element_type=f32)
                  + jnp.dot(lo, wrh_ref[...], preferred_element_type=f32)
                  + jnp.dot(hi, wrl_ref[...], preferred_element_type=f32))
        lanef = lax.broadcasted_iota(i32, (1, LANES), 1).astype(f32)
        lg = jnp.where(lanef < N_EXPERTS, logits, -jnp.inf)
        v1 = jnp.max(lg, axis=1, keepdims=True)
        i1 = jnp.min(jnp.where(lg == v1, lanef, float(LANES)), axis=1, keepdims=True)
        lg2 = jnp.where(lanef == i1, -jnp.inf, lg)
        v2 = jnp.max(lg2, axis=1, keepdims=True)
        i2 = jnp.min(jnp.where(lg2 == v2, lanef, float(LANES)), axis=1, keepdims=True)
        tt = jnp.exp(v2 - v1)
        w1 = 1.0 / (1.0 + tt)
        w2 = tt / (1.0 + tt)
        route = jnp.where((lanef == i1) | (lanef == i2), 1.0, 0.0)
        for ln, val in ((ROUTE_W1, w1), (ROUTE_W2, w2), (ROUTE_I1, i1), (ROUTE_I2, i2)):
            route = jnp.where(lanef == float(ln), val, route)
        gate_ref[...] = route


def _merge(moe, seq, x2d, a, u, ga, gp, wp, ps, pa, pb, wo, g2, wrh=None, wrl=None):
    t = x2d.shape[0]
    tm = TM_MERGE
    row = lambda i: (i, 0)
    const2 = lambda i: (0, 0)
    const3 = lambda i: (0, 0, 0)
    halo = lambda i: (jnp.maximum(i * (tm // POOL_HALO) - 1, 0), 0)
    in_specs = [
        pl.BlockSpec((tm, D_MODEL), row),
        pl.BlockSpec((tm, ATTN_WIDTH), row),
        pl.BlockSpec((tm, POOL_WIDTH), row),
        pl.BlockSpec((POOL_HALO, POOL_WIDTH), halo),
        pl.BlockSpec((tm, D_MODEL), row),
        pl.BlockSpec((tm, D_MODEL), row),
        pl.BlockSpec((POOL_GROUPS, POOL_GROUP_DIM, POOL_GROUP_DIM), const3),
        pl.BlockSpec((1, POOL_WIDTH), const2),
        pl.BlockSpec((ATTN_WIDTH, D_MODEL), const2),
        pl.BlockSpec((POOL_WIDTH, D_MODEL), const2),
        pl.BlockSpec((D_MODEL, D_MODEL), const2),
        pl.BlockSpec((1, D_MODEL), const2),
    ]
    args = [x2d, a, u, u, ga, gp, wp, ps, pa, pb, wo, g2]
    out_shapes = [jax.ShapeDtypeStruct((t, D_MODEL), f32), jax.ShapeDtypeStruct((t, D_MODEL), f32 if moe else bf16)]
    out_specs = [pl.BlockSpec((tm, D_MODEL), row), pl.BlockSpec((tm, D_MODEL), row)]
    if moe:
        in_specs += [pl.BlockSpec((D_MODEL, LANES), const2), pl.BlockSpec((D_MODEL, LANES), const2)]
        args += [wrh, wrl]
        out_shapes.append(jax.ShapeDtypeStruct((t, LANES), f32))
        out_specs.append(pl.BlockSpec((tm, LANES), row))
    return pl.pallas_call(
        functools.partial(_merge_kernel, moe, seq),
        grid=(t // tm,),
        in_specs=in_specs,
        out_specs=out_specs,
        out_shape=out_shapes,
        scratch_shapes=[pltpu.VMEM((tm + POOL_HALO, POOL_WIDTH), f32)],
        compiler_params=pltpu.CompilerParams(
            dimension_semantics=("arbitrary",), vmem_limit_bytes=VMEM_LIMIT),
        name="merge_moe" if moe else "merge_dense",
    )(*args)


def _swiglu_partial(h, w1, w3, w2):
    a = jnp.dot(h, w1, preferred_element_type=f32)
    b = jnp.dot(h, w3, preferred_element_type=f32)
    act = (a * _sigmoid(a) * b).astype(bf16)
    return jnp.dot(act, w2, preferred_element_type=f32)


def _ffn_kernel(h_ref, x_ref, w1_ref, w3_ref, w2_ref, o_ref):
    @pl.when(pl.program_id(1) == 0)
    def _():
        o_ref[...] = x_ref[...]

    o_ref[...] += _swiglu_partial(h_ref[...], w1_ref[...], w3_ref[...], w2_ref[...])


def _ffn(h2, x2, w1, w3, w2, tm, tf):
    t = h2.shape[0]
    ff = w1.shape[1]
    row = lambda i, j: (i, 0)
    return pl.pallas_call(
        _ffn_kernel,
        grid=(t // tm, ff // tf),
        in_specs=[
            pl.BlockSpec((tm, D_MODEL), row),
            pl.BlockSpec((tm, D_MODEL), row),
            pl.BlockSpec((D_MODEL, tf), lambda i, j: (0, j)),
            pl.BlockSpec((D_MODEL, tf), lambda i, j: (0, j)),
            pl.BlockSpec((tf, D_MODEL), lambda i, j: (j, 0)),
        ],
        out_specs=pl.BlockSpec((tm, D_MODEL), row),
        out_shape=jax.ShapeDtypeStruct((t, D_MODEL), f32),
        compiler_params=pltpu.CompilerParams(
            dimension_semantics=("arbitrary", "arbitrary"), vmem_limit_bytes=VMEM_LIMIT),
        name="ffn_dense",
    )(h2, x2, w1, w3, w2)


def _rank_kernel(route_ref, tri_ref, rk_ref, cnt_ref, carry_sc):
    @pl.when(pl.program_id(0) == 0)
    def _():
        carry_sc[...] = jnp.zeros(carry_sc.shape, f32)

    lane = lax.broadcasted_iota(i32, (1, LANES), 1)
    lanef = lane.astype(f32)
    r = route_ref[...]
    sel = jnp.where(lane < N_EXPERTS, r, 0.0)
    ranks = jnp.dot(tri_ref[...], sel.astype(bf16), preferred_element_type=f32) + carry_sc[...]
    rk1 = jnp.sum(jnp.where(lanef == r[:, ROUTE_I1:ROUTE_I1 + 1], ranks, 0.0), axis=1, keepdims=True)
    rk2 = jnp.sum(jnp.where(lanef == r[:, ROUTE_I2:ROUTE_I2 + 1], ranks, 0.0), axis=1, keepdims=True)
    rk_ref[...] = jnp.where(lane == 0, rk1, jnp.where(lane == 1, rk2, 0.0))
    carry_sc[...] += jnp.sum(sel, axis=0, keepdims=True)
    cnt_ref[...] = carry_sc[...]


def _moe_rank(route):
    t = route.shape[0]
    tm = TM_RANK
    idx = jnp.arange(tm)
    tri = (idx[None, :] < idx[:, None]).astype(bf16)
    return pl.pallas_call(
        _rank_kernel,
        grid=(t // tm,),
        in_specs=[pl.BlockSpec((tm, LANES), lambda i: (i, 0)), pl.BlockSpec((tm, tm), lambda i: (0, 0))],
        out_specs=[pl.BlockSpec((tm, LANES), lambda i: (i, 0)), pl.BlockSpec((1, LANES), lambda i: (0, 0))],
        out_shape=[jax.ShapeDtypeStruct((t, LANES), f32), jax.ShapeDtypeStruct((1, LANES), f32)],
        scratch_shapes=[pltpu.VMEM((1, LANES), f32)],
        compiler_params=pltpu.CompilerParams(dimension_semantics=("arbitrary",)),
        name="moe_rank",
    )(route, tri)


def _row_copy(src, dst, sem):
    return pltpu.make_async_copy(src, dst, sem)


def _dispatch_kernel(pos_ref, h_hbm, xs_init_hbm, xs_hbm, sem):
    del xs_init_hbm
    base = pl.program_id(0) * TM_ROWS

    def issue(r, carry):
        t = base + r
        for s in range(2):
            _row_copy(h_hbm.at[pl.ds(t, 1)], xs_hbm.at[pl.ds(pos_ref[2 * t + s], 1)], sem).start()
        return carry

    def drain(r, carry):
        for s in range(2):
            _row_copy(h_hbm.at[pl.ds(0, 1)], xs_hbm.at[pl.ds(0, 1)], sem).wait()
        return carry

    lax.fori_loop(0, TM_ROWS, issue, 0)
    lax.fori_loop(0, TM_ROWS, drain, 0)


def _moe_dispatch(pos, hf, n_rows):
    t = hf.shape[0]
    xs_init = jnp.zeros((n_rows, D_MODEL), f32)
    return pl.pallas_call(
        _dispatch_kernel,
        grid_spec=pltpu.PrefetchScalarGridSpec(
            num_scalar_prefetch=1,
            grid=(t // TM_ROWS,),
            in_specs=[pl.BlockSpec(memory_space=pl.ANY), pl.BlockSpec(memory_space=pl.ANY)],
            out_specs=pl.BlockSpec(memory_space=pl.ANY),
            scratch_shapes=[pltpu.SemaphoreType.DMA(())],
        ),
        out_shape=jax.ShapeDtypeStruct((n_rows, D_MODEL), f32),
        input_output_aliases={2: 0},
        compiler_params=pltpu.CompilerParams(dimension_semantics=("arbitrary",)),
        name="moe_dispatch",
    )(pos, hf, xs_init)


def _moe_ffn_kernel(te_ref, nu_ref, xs_ref, w1_ref, w3_ref, w2_ref, ys_ref):
    del te_ref
    j = pl.program_id(1)
    used = pl.program_id(0) < nu_ref[0]

    @pl.when(used)
    def _():
        y = _swiglu_partial(xs_ref[...].astype(bf16), w1_ref[0], w3_ref[0], w2_ref[0])

        @pl.when(j == 0)
        def _():
            ys_ref[...] = y

        @pl.when(j > 0)
        def _():
            ys_ref[...] += y

    @pl.when(jnp.logical_not(used) & (j == 0))
    def _():
        ys_ref[...] = jnp.zeros(ys_ref.shape, f32)


def _moe_ffn(tile_expert, n_used, xs, w1, w3, w2):
    n_rows = xs.shape[0]
    ff = w1.shape[2]
    nj = ff // TF_EXPERT
    jj = lambda i, j, nu: jnp.where(i < nu[0], j, nj - 1)
    return pl.pallas_call(
        _moe_ffn_kernel,
        grid_spec=pltpu.PrefetchScalarGridSpec(
            num_scalar_prefetch=2,
            grid=(n_rows // TM_EXPERT, nj),
            in_specs=[
                pl.BlockSpec((TM_EXPERT, D_MODEL), lambda i, j, te, nu: (i, 0)),
                pl.BlockSpec((1, D_MODEL, TF_EXPERT), lambda i, j, te, nu: (te[i], 0, jj(i, j, nu))),
                pl.BlockSpec((1, D_MODEL, TF_EXPERT), lambda i, j, te, nu: (te[i], 0, jj(i, j, nu))),
                pl.BlockSpec((1, TF_EXPERT, D_MODEL), lambda i, j, te, nu: (te[i], jj(i, j, nu), 0)),
            ],
            out_specs=pl.BlockSpec((TM_EXPERT, D_MODEL), lambda i, j, te, nu: (i, 0)),
        ),
        out_shape=jax.ShapeDtypeStruct((n_rows, D_MODEL), f32),
        compiler_params=pltpu.CompilerParams(
            dimension_semantics=("arbitrary", "arbitrary"), vmem_limit_bytes=VMEM_LIMIT),
        name="moe_ffn",
    )(tile_expert, n_used, xs, w1, w3, w2)


def _combine_kernel(pos_ref, ys_hbm, x2_ref, route_ref, o_ref, buf, sem):
    base = pl.program_id(0) * TM_ROWS

    def issue(r, carry):
        t = base + r
        for s in range(2):
            _row_copy(ys_hbm.at[pl.ds(pos_ref[2 * t + s], 1)], buf.at[s, pl.ds(r, 1)], sem).start()
        return carry

    def drain(r, carry):
        for s in range(2):
            _row_copy(ys_hbm.at[pl.ds(0, 1)], buf.at[s, pl.ds(0, 1)], sem).wait()
        return carry

    lax.fori_loop(0, TM_ROWS, issue, 0)
    lax.fori_loop(0, TM_ROWS, drain, 0)
    route = route_ref[...]
    o_ref[...] = (x2_ref[...] + route[:, ROUTE_W1:ROUTE_W1 + 1] * buf[0]
                  + route[:, ROUTE_W2:ROUTE_W2 + 1] * buf[1])


def _moe_combine(pos, ys, x2, route):
    t = x2.shape[0]
    row = lambda i, pos: (i, 0)
    return pl.pallas_call(
        _combine_kernel,
        grid_spec=pltpu.PrefetchScalarGridSpec(
            num_scalar_prefetch=1,
            grid=(t // TM_ROWS,),
            in_specs=[
                pl.BlockSpec(memory_space=pl.ANY),
                pl.BlockSpec((TM_ROWS, D_MODEL), row),
                pl.BlockSpec((TM_ROWS, LANES), row),
            ],
            out_specs=pl.BlockSpec((TM_ROWS, D_MODEL), row),
            scratch_shapes=[pltpu.VMEM((2, TM_ROWS, D_MODEL), f32), pltpu.SemaphoreType.DMA(())],
        ),
        out_shape=jax.ShapeDtypeStruct((t, D_MODEL), f32),
        compiler_params=pltpu.CompilerParams(dimension_semantics=("arbitrary",)),
        name="moe_combine",
    )(pos, ys, x2, route)


def _moe_layer(route, hf, x2, w1, w3, w2):
    t = hf.shape[0]
    n_tiles = 2 * t // TM_EXPERT + N_EXPERTS
    rk, cnt = _moe_rank(route)
    counts = cnt[0, :N_EXPERTS].astype(i32)
    padded = (counts + TM_EXPERT - 1) // TM_EXPERT * TM_EXPERT
    ends = jnp.cumsum(padded)
    starts = ends - padded
    n_used = (ends[-1] // TM_EXPERT).astype(i32)
    tile_id = jnp.arange(n_tiles, dtype=i32)
    tile_expert = jnp.minimum(jnp.sum((tile_id[:, None] * TM_EXPERT >= ends[None, :]).astype(i32), axis=1),
                              N_EXPERTS - 1)
    tile_expert = jnp.where(tile_id < n_used, tile_expert, tile_expert[jnp.maximum(n_used - 1, 0)])
    experts = jnp.arange(N_EXPERTS, dtype=f32)[None, :]
    start_of = lambda ids: jnp.sum(jnp.where(ids[:, None] == experts, starts[None, :], 0), axis=1)
    pos1 = start_of(route[:, ROUTE_I1]) + rk[:, 0].astype(i32)
    pos2 = start_of(route[:, ROUTE_I2]) + rk[:, 1].astype(i32)
    pos = jnp.stack([pos1, pos2], axis=1).reshape(2 * t).astype(i32)

    xs = _moe_dispatch(pos, hf, n_tiles * TM_EXPERT)
    ys = _moe_ffn(tile_expert, n_used.reshape(1), xs, w1, w3, w2)
    return _moe_combine(pos, ys, x2, route)


def _rope_tables(seq, rot_dim, period, active_lanes):
    half = rot_dim // 2
    inv = jnp.power(ROPE_THETA, -jnp.arange(0, rot_dim, 2, dtype=f32) / rot_dim)
    ang = jnp.arange(seq, dtype=f32)[:, None] * inv[None, :]
    cos, sin = jnp.cos(ang), jnp.sin(ang)
    ones = jnp.ones((seq, period - rot_dim), f32)
    zeros = lambda n: jnp.zeros((seq, n), f32)
    cos_p = jnp.concatenate([cos, cos, ones], axis=1)
    sin_a = jnp.concatenate([-sin, zeros(period - half)], axis=1)
    sin_b = jnp.concatenate([zeros(half), sin, zeros(period - rot_dim)], axis=1)
    reps = LANES // period
    tabs = [jnp.tile(tb, (1, reps)) for tb in (cos_p, sin_a, sin_b)]
    live = (jnp.arange(LANES) < active_lanes)[None, :]
    tabs = [jnp.where(live, tabs[0], 1.0), jnp.where(live, tabs[1], 0.0), jnp.where(live, tabs[2], 0.0)]
    return jnp.stack(tabs, axis=0)


def _relayout_w_in(w):
    d = w.shape[0]
    o = 0
    segs = {}
    for name, n in (("q", 512), ("k", 512), ("v", 512), ("qi", 256), ("ki", 32), ("wi", 8),
                    ("u", 512), ("ga", 1024), ("gp", 1024)):
        segs[name] = w[:, o:o + n]
        o += n
    qi = segs["qi"].reshape(d, IDX_HEADS, IDX_DIM)
    qcat = jnp.concatenate([qi, qi, qi, jnp.zeros_like(qi)], axis=-1).reshape(d, IDX_HEADS * LANES)
    kwc = jnp.concatenate([segs["ki"], segs["ki"], segs["ki"], segs["wi"],
                           jnp.zeros((d, LANES - 3 * IDX_DIM - IDX_HEADS), w.dtype)], axis=1)
    vh = segs["v"].reshape(d, N_HEADS // 2, 2, HEAD_DIM)
    zh = jnp.zeros_like(vh[:, :, 0])
    vx = jnp.stack([jnp.concatenate([vh[:, :, 0], zh], axis=-1), jnp.concatenate([zh, vh[:, :, 1]], axis=-1)],
                   axis=2).reshape(d, N_HEADS * LANES)
    out = jnp.concatenate([segs["q"], segs["k"], vx, qcat, kwc, segs["u"], segs["ga"], segs["gp"]], axis=1)
    return out.astype(bf16)


def kernel(x, mix_norm, w_in, q_norm, k_norm, w_pool, pool_scale, w_attn_proj, w_pool_proj, w_out, ffn_norm,
           dense_w1, dense_w3, dense_w2, moe_router, moe_w1, moe_w3, moe_w2):
    batch, seq, d = x.shape
    depth = w_in.shape[0]
    t = batch * seq
    assert d == D_MODEL and seq % TK == 0 and seq % TM_PROJ == 0 and seq % TM_MERGE == 0

    rqk = _rope_tables(seq, ROPE_DIM, HEAD_DIM, LANES)
    rqi = _rope_tables(seq, IDX_ROPE_DIM, IDX_DIM, LANES)
    rki = _rope_tables(seq, IDX_ROPE_DIM, IDX_DIM, 3 * IDX_DIM)
    head_of = jnp.arange(ATTN_WIDTH) // HEAD_DIM
    bd = jnp.where(head_of[:, None] == head_of[None, :], 1.0 / HEAD_DIM, 0.0).astype(bf16)
    lane_in_tile = jnp.arange(N_HEADS * LANES) % LANES
    odd_head = (jnp.arange(N_HEADS * LANES) // LANES) % 2 == 1
    ones_lane = (lane_in_tile == jnp.where(odd_head, 0, HEAD_DIM)).astype(f32)[None, :]

    xc = x.reshape(t, d)
    for layer in range(depth):
        w = _relayout_w_in(w_in[layer])
        qn = jnp.tile(q_norm[layer], N_HEADS)[None, :]
        kn = jnp.tile(k_norm[layer], N_HEADS)[None, :]
        q, k, vx, qc, kw, u, ga, gp = _in_proj(xc, mix_norm[layer][None, :], w, qn, kn, bd, ones_lane,
                                               rqk, rqi, rki, seq)
        a = _dsa_attention(q, qc, kw, k, vx, batch, seq)
        moe = layer % 2 == 1
        common = (xc, a, u, ga, gp, w_pool[layer].astype(bf16), pool_scale[layer][None, :],
                  w_attn_proj[layer].astype(bf16), w_pool_proj[layer].astype(bf16), w_out[layer].astype(bf16),
                  ffn_norm[layer][None, :])
        idx = layer // 2
        if moe:
            wr = jnp.pad(moe_router[idx], ((0, 0), (0, LANES - N_EXPERTS)))
            wrh = wr.astype(bf16)
            wrl = (wr - wrh.astype(f32)).astype(bf16)
            x2, hf, route = _merge(True, seq, *common, wrh, wrl)
            xc = _moe_layer(route, hf, x2, moe_w1[idx].astype(bf16), moe_w3[idx].astype(bf16),
                            moe_w2[idx].astype(bf16))
        else:
            x2, h2 = _merge(False, seq, *common)
            xc = _ffn(h2, x2, dense_w1[idx].astype(bf16), dense_w3[idx].astype(bf16),
                      dense_w2[idx].astype(bf16), tm=1024, tf=1408)
    return xc.reshape(batch, seq, d)
```

```python
import functools

import jax
import jax.numpy as jnp
from jax import lax
from jax.experimental import pallas as pl
from jax.experimental.pallas import tpu as pltpu

bf16 = jnp.bfloat16
f32 = jnp.float32
i32 = jnp.int32

D_MODEL = 1024
N_HEADS = 8
HEAD_DIM = 64
ATTN_WIDTH = 512
ROPE_DIM = 16
ROPE_THETA = 500000.0
IDX_HEADS = 8
IDX_DIM = 32
IDX_ROPE_DIM = 8
TOPK_MAX = 256
POOL_GROUPS = 4
POOL_GROUP_DIM = 128
POOL_WIDTH = 512
POOL_WINDOWS = (2, 4, 8, 16)
POOL_HALO = 16
N_EXPERTS = 8
EPS = 1e-6

LANES = 128
INT_MIN = -(2 ** 31)
VMEM_LIMIT = 56 * 1024 * 1024
LOG2E = 1.4426950408889634

OFF_Q = 0
OFF_K = 512
OFF_VX = 1024
OFF_QCAT = 2048
OFF_KW = 3072
OFF_U = 3200
OFF_GA = 3712
OFF_GP = 4736
W_COLS = 5760

ROUTE_W1, ROUTE_W2, ROUTE_I1, ROUTE_I2 = 8, 9, 10, 11

TM_PROJ = 512
TM_MERGE = 512
TM_RANK = 512
TM_EXPERT = 512
TF_EXPERT = 896
TM_ROWS = 256
QB = 128
TK = 512


def _sigmoid(x):
    return 1.0 / (1.0 + jnp.exp(-x))


def _in_proj_kernel(x_ref, g_ref, w_ref, qn_ref, kn_ref, bd_ref, one_ref, rqk_ref, rqi_ref, rki_ref,
                    q_ref, k_ref, vx_ref, qc_ref, kw_ref, u_ref, ga_ref, gp_ref):
    x = x_ref[...]
    ms = jnp.mean(x * x, axis=-1, keepdims=True)
    h = (x * lax.rsqrt(ms + EPS) * g_ref[...]).astype(bf16)

    def proj(lo, n):
        return jnp.dot(h, w_ref[:, lo:lo + n], preferred_element_type=f32)

    def rope(xc, tab_ref, sh):
        return (xc * tab_ref[0] + pltpu.roll(xc, LANES - sh, 1) * tab_ref[1]
                + pltpu.roll(xc, sh, 1) * tab_ref[2])

    def headnorm(z, gain_ref):
        msh = jnp.dot((z * z).astype(bf16), bd_ref[...], preferred_element_type=f32)
        return z * lax.rsqrt(msh + EPS) * gain_ref[...]

    zq = headnorm(proj(OFF_Q, ATTN_WIDTH), qn_ref) * (HEAD_DIM ** -0.5 * LOG2E)
    zk = headnorm(proj(OFF_K, ATTN_WIDTH), kn_ref)
    for c in range(ATTN_WIDTH // LANES):
        sl = slice(c * LANES, (c + 1) * LANES)
        q_ref[:, sl] = rope(zq[:, sl], rqk_ref, ROPE_DIM // 2).astype(bf16)
        k_ref[:, sl] = rope(zk[:, sl], rqk_ref, ROPE_DIM // 2).astype(bf16)
    vx_ref[...] = jnp.where(one_ref[...] > 0.0, 1.0, proj(OFF_VX, N_HEADS * LANES)).astype(bf16)

    lane = lax.broadcasted_iota(i32, (1, LANES), 1)
    zc = proj(OFF_QCAT, IDX_HEADS * LANES)
    for c in range(IDX_HEADS):
        sl = slice(c * LANES, (c + 1) * LANES)
        r = rope(zc[:, sl], rqi_ref, IDX_ROPE_DIM // 2)
        lo = r - r.astype(bf16).astype(f32)
        qc_ref[:, sl] = jnp.where((lane >= 32) & (lane < 64), lo, r).astype(bf16)

    zkw = rope(proj(OFF_KW, LANES), rki_ref, IDX_ROPE_DIM // 2)
    lo = zkw - zkw.astype(bf16).astype(f32)
    zkw = jnp.where((lane >= 64) & (lane < 96), lo, zkw)
    kw_ref[...] = jnp.where(lane >= 96, zkw * ((IDX_HEADS * IDX_DIM) ** -0.5), zkw)

    u_ref[...] = proj(OFF_U, POOL_WIDTH)
    ga_ref[...] = proj(OFF_GA, D_MODEL)
    gp_ref[...] = proj(OFF_GP, D_MODEL)


def _in_proj(x2d, g, w, qn, kn, bd, ones_lane, rqk, rqi, rki, seq):
    t = x2d.shape[0]
    tm = TM_PROJ
    ns = seq // tm
    row = lambda i: (i, 0)
    const = lambda i: (0, 0)
    tab = lambda i: (0, i % ns, 0)
    out_shapes = (
        jax.ShapeDtypeStruct((t, ATTN_WIDTH), bf16),
        jax.ShapeDtypeStruct((t, ATTN_WIDTH), bf16),
        jax.ShapeDtypeStruct((t, N_HEADS * LANES), bf16),
        jax.ShapeDtypeStruct((t, IDX_HEADS * LANES), bf16),
        jax.ShapeDtypeStruct((t, LANES), f32),
        jax.ShapeDtypeStruct((t, POOL_WIDTH), f32),
        jax.ShapeDtypeStruct((t, D_MODEL), f32),
        jax.ShapeDtypeStruct((t, D_MODEL), f32),
    )
    return pl.pallas_call(
        _in_proj_kernel,
        grid=(t // tm,),
        in_specs=[
            pl.BlockSpec((tm, D_MODEL), row),
            pl.BlockSpec((1, D_MODEL), const),
            pl.BlockSpec((D_MODEL, W_COLS), const, pipeline_mode=pl.Buffered(1)),
            pl.BlockSpec((1, ATTN_WIDTH), const),
            pl.BlockSpec((1, ATTN_WIDTH), const),
            pl.BlockSpec((ATTN_WIDTH, ATTN_WIDTH), const),
            pl.BlockSpec((1, N_HEADS * LANES), const),
            pl.BlockSpec((3, tm, LANES), tab),
            pl.BlockSpec((3, tm, LANES), tab),
            pl.BlockSpec((3, tm, LANES), tab),
        ],
        out_specs=[pl.BlockSpec((tm, s.shape[1]), row) for s in out_shapes],
        out_shape=out_shapes,
        compiler_params=pltpu.CompilerParams(
            dimension_semantics=("arbitrary",), vmem_limit_bytes=VMEM_LIMIT),
        name="in_proj",
    )(x2d, g, w, qn, kn, bd, ones_lane, rqk, rqi, rki)


def _lane_fold(x, op):
    out = x[:, 0:LANES]
    for j in range(1, x.shape[1] // LANES):
        out = op(out, x[:, j * LANES:(j + 1) * LANES])
    return out


def _dsa_kernel(topk, q_ref, qc_ref, kwq_ref, k_ref, vx_ref, kw_ref, o_ref,
                key_sc, kcat_sc, qm2_sc, mrun_sc, m_sc, acc_sc):
    i = pl.program_id(1)
    nk = (i * QB + QB + TK - 1) // TK
    kf = float(topk)

    @pl.when(i == 0)
    def _():
        kcat_sc[...] = kw_ref[...].astype(bf16)

    qpos = i * QB + lax.broadcasted_iota(i32, (QB, 1), 0)
    lane_tk = lax.broadcasted_iota(i32, (1, TK), 1)
    wq = kwq_ref[...]

    def to_key(score):
        bits = pltpu.bitcast(score, i32)
        return bits ^ ((bits >> 31) & 0x7FFFFFFF)

    def score_chunk(c, carry):
        koff = pl.multiple_of(c * TK, TK)
        kc = kcat_sc[pl.ds(koff, TK), :]
        acc = jnp.zeros((QB, TK), f32)
        for h in range(IDX_HEADS):
            d = lax.dot_general(qc_ref[:, h * LANES:(h + 1) * LANES], kc,
                                (((1,), (1,)), ((), ())), preferred_element_type=f32)
            acc = acc + jnp.maximum(d, 0.0) * wq[:, 96 + h:97 + h]
        causal = (koff + lane_tk) <= qpos
        key_sc[c] = jnp.where(causal, to_key(acc), INT_MIN)
        return carry

    lax.fori_loop(0, nk, score_chunk, 0)

    def count_ge(cand):
        def body(c, part):
            return part + _lane_fold(jnp.where(key_sc[c] >= cand, 1.0, 0.0), jnp.add)
        part = lax.fori_loop(0, nk, body, jnp.zeros((QB, LANES), f32))
        return jnp.sum(part, axis=1, keepdims=True)

    zero = jnp.zeros((QB, 1), i32)
    prefix = jnp.where(count_ge(zero) >= kf, zero, INT_MIN)

    def bit_body(b, prefix):
        cand = prefix | jnp.left_shift(1, 30 - b)
        return jnp.where(count_ge(cand) >= kf, cand, prefix)

    thr_raw = lax.fori_loop(0, 31, bit_body, prefix)
    thr = jnp.maximum(thr_raw, INT_MIN + 1)

    is_tie = (count_ge(thr) > kf) & (thr_raw > INT_MIN)

    @pl.when(jnp.max(jnp.where(is_tie, 1.0, 0.0)) > 0.0)
    def _():
        need = kf - count_ge(thr + 1)

        def count_eq_below(m):
            def body(c, part):
                hit = (key_sc[c] == thr) & ((c * TK + lane_tk) < m)
                return part + _lane_fold(jnp.where(hit, 1.0, 0.0), jnp.add)
            part = lax.fori_loop(0, nk, body, jnp.zeros((QB, LANES), f32))
            return jnp.sum(part, axis=1, keepdims=True)

        mprime = jnp.zeros((QB, 1), i32)
        bit = key_sc.shape[0] * TK // 2
        while bit >= 1:
            cand = mprime | bit
            mprime = jnp.where(count_eq_below(cand) < need, cand, mprime)
            bit //= 2

        def drop_ties(c, carry):
            key = key_sc[c]
            drop = is_tie & (key == thr) & ((c * TK + lane_tk) > mprime)
            key_sc[c] = jnp.where(drop, thr - 1, key)
            return carry

        lax.fori_loop(0, nk, drop_ties, 0)

    lane = lax.broadcasted_iota(i32, (1, LANES), 1)
    for pr in range(N_HEADS // 2):
        qp = q_ref[:, pr * LANES:(pr + 1) * LANES]
        qm2_sc[pr, 0:QB, :] = jnp.where(lane < HEAD_DIM, qp, jnp.zeros((), bf16))
        qm2_sc[pr, QB:2 * QB, :] = jnp.where(lane >= HEAD_DIM, qp, jnp.zeros((), bf16))
    mrun_sc[...] = jnp.full(mrun_sc.shape, -jnp.inf, f32)
    acc_sc[...] = jnp.zeros(acc_sc.shape, f32)

    def pair_logits(c, pr, bias):
        koff = pl.multiple_of(c * TK, TK)
        kc = k_ref[pl.ds(koff, TK), pr * LANES:(pr + 1) * LANES]
        s2 = lax.dot_general(qm2_sc[pr], kc, (((1,), (1,)), ((), ())), preferred_element_type=f32)
        return s2[0:QB] + bias, s2[QB:2 * QB] + bias

    def max_chunk(c, carry):
        bias = jnp.where(key_sc[c] >= thr, 0.0, -jnp.inf)
        for pr in range(N_HEADS // 2):
            sa, sb = pair_logits(c, pr, bias)
            mrun_sc[pr, 0:QB, :] = jnp.maximum(mrun_sc[pr, 0:QB, :], _lane_fold(sa, jnp.maximum))
            mrun_sc[pr, QB:2 * QB, :] = jnp.maximum(mrun_sc[pr, QB:2 * QB, :], _lane_fold(sb, jnp.maximum))
        return carry

    lax.fori_loop(0, nk, max_chunk, 0)

    for pr in range(N_HEADS // 2):
        m = jnp.max(mrun_sc[pr], axis=1, keepdims=True)
        m_sc[pr] = jnp.where(m == -jnp.inf, 0.0, m)

    def acc_chunk(c, carry):
        koff = pl.multiple_of(c * TK, TK)
        bias = jnp.where(key_sc[c] >= thr, 0.0, -jnp.inf)
        for pr in range(N_HEADS // 2):
            sa, sb = pair_logits(c, pr, bias)
            m = m_sc[pr]
            pa = jnp.exp2(sa - m[0:QB]).astype(bf16)
            pb = jnp.exp2(sb - m[QB:2 * QB]).astype(bf16)
            va = vx_ref[pl.ds(koff, TK), (2 * pr) * LANES:(2 * pr + 1) * LANES]
            vb = vx_ref[pl.ds(koff, TK), (2 * pr + 1) * LANES:(2 * pr + 2) * LANES]
            acc_sc[2 * pr] += jnp.dot(pa, va, preferred_element_type=f32)
            acc_sc[2 * pr + 1] += jnp.dot(pb, vb, preferred_element_type=f32)
        return carry

    lax.fori_loop(0, nk, acc_chunk, 0)

    for pr in range(N_HEADS // 2):
        a0 = acc_sc[2 * pr]
        a1 = acc_sc[2 * pr + 1]
        o0 = a0 / a0[:, HEAD_DIM:HEAD_DIM + 1]
        o1 = a1 / a1[:, 0:1]
        o_ref[:, pr * LANES:(pr + 1) * LANES] = jnp.where(lane < HEAD_DIM, o0, o1).astype(bf16)


def _dsa_attention(q, qc, kw, k, vx, batch, seq):
    t = q.shape[0]
    nq = seq // QB
    topk = min(TOPK_MAX, seq // 4)
    qrow = lambda b, i: (b * nq + i, 0)
    per_batch = lambda b, i: (b, 0)
    return pl.pallas_call(
        functools.partial(_dsa_kernel, topk),
        grid=(batch, nq),
        in_specs=[
            pl.BlockSpec((QB, ATTN_WIDTH), qrow),
            pl.BlockSpec((QB, IDX_HEADS * LANES), qrow),
            pl.BlockSpec((QB, LANES), qrow),
            pl.BlockSpec((seq, ATTN_WIDTH), per_batch, pipeline_mode=pl.Buffered(1)),
            pl.BlockSpec((seq, N_HEADS * LANES), per_batch, pipeline_mode=pl.Buffered(1)),
            pl.BlockSpec((seq, LANES), per_batch, pipeline_mode=pl.Buffered(1)),
        ],
        out_specs=pl.BlockSpec((QB, ATTN_WIDTH), qrow),
        out_shape=jax.ShapeDtypeStruct((t, ATTN_WIDTH), bf16),
        scratch_shapes=[
            pltpu.VMEM((seq // TK, QB, TK), i32),
            pltpu.VMEM((seq, LANES), bf16),
            pltpu.VMEM((N_HEADS // 2, 2 * QB, LANES), bf16),
            pltpu.VMEM((N_HEADS // 2, 2 * QB, LANES), f32),
            pltpu.VMEM((N_HEADS // 2, 2 * QB, 1), f32),
            pltpu.VMEM((N_HEADS, QB, LANES), f32),
        ],
        compiler_params=pltpu.CompilerParams(
            dimension_semantics=("arbitrary", "arbitrary"), vmem_limit_bytes=VMEM_LIMIT),
        name="dsa_attention",
    )(q, qc, kw, k, vx, kw)


def _merge_kernel(moe, seq, *refs):
    if moe:
        (x_ref, a_ref, u_ref, uh_ref, ga_ref, gp_ref, wp_ref, ps_ref, pa_ref, pb_ref, wo_ref, g2_ref,
         wrh_ref, wrl_ref, x2_ref, h2_ref, gate_ref, e_sc) = refs
    else:
        (x_ref, a_ref, u_ref, uh_ref, ga_ref, gp_ref, wp_ref, ps_ref, pa_ref, pb_ref, wo_ref, g2_ref,
         x2_ref, h2_ref, e_sc) = refs
    tm = TM_MERGE
    i = pl.program_id(0)
    ti = i % (seq // tm)
    e_sc[0:POOL_HALO, :] = jnp.where(ti == 0, 0.0, uh_ref[...])
    e_sc[POOL_HALO:POOL_HALO + tm, :] = u_ref[...]
    npos = (ti * tm + 1 + lax.broadcasted_iota(i32, (tm, 1), 0)).astype(f32)

    parts = []
    for g, w in enumerate(POOL_WINDOWS):
        sl = slice(g * POOL_GROUP_DIM, (g + 1) * POOL_GROUP_DIM)
        tot = e_sc[POOL_HALO:POOL_HALO + tm, sl]
        for j in range(1, w):
            tot = tot + e_sc[POOL_HALO - j:POOL_HALO - j + tm, sl]
        diff = tot / jnp.minimum(npos, float(w)) - u_ref[:, sl]
        parts.append(jnp.dot(diff.astype(bf16), wp_ref[g], preferred_element_type=f32))
    p = jnp.concatenate(parts, axis=1) * ps_ref[...]

    ab = jnp.dot(a_ref[...], pa_ref[...], preferred_element_type=f32)
    pb = jnp.dot(p.astype(bf16), pb_ref[...], preferred_element_type=f32)
    merged = _sigmoid(ga_ref[...]) * ab + _sigmoid(gp_ref[...]) * pb
    x2 = x_ref[...] + jnp.dot(merged.astype(bf16), wo_ref[...], preferred_element_type=f32)
    x2_ref[...] = x2
    ms = jnp.mean(x2 * x2, axis=-1, keepdims=True)
    hf = x2 * lax.rsqrt(ms + EPS) * g2_ref[...]
    h2_ref[...] = hf.astype(h2_ref.dtype)

    if moe:
        hi = hf.astype(bf16)
        lo = (hf - hi.astype(f32)).astype(bf16)
        logits = (jnp.dot(hi, wrh_ref[...], preferred_element_type=f32)
                  + jnp.dot(lo, wrh_ref[...], preferred_element_type=f32)
                  + jnp.dot(hi, wrl_ref[...], preferred_element_type=f32))
        lanef = lax.broadcasted_iota(i32, (1, LANES), 1).astype(f32)
        lg = jnp.where(lanef < N_EXPERTS, logits, -jnp.inf)
        v1 = jnp.max(lg, axis=1, keepdims=True)
        i1 = jnp.min(jnp.where(lg == v1, lanef, float(LANES)), axis=1, keepdims=True)
        lg2 = jnp.where(lanef == i1, -jnp.inf, lg)
        v2 = jnp.max(lg2, axis=1, keepdims=True)
        i2 = jnp.min(jnp.where(lg2 == v2, lanef, float(LANES)), axis=1, keepdims=True)
        tt = jnp.exp(v2 - v1)
        w1 = 1.0 / (1.0 + tt)
        w2 = tt / (1.0 + tt)
        route = jnp.where((lanef == i1) | (lanef == i2), 1.0, 0.0)
        for ln, val in ((ROUTE_W1, w1), (ROUTE_W2, w2), (ROUTE_I1, i1), (ROUTE_I2, i2)):
            route = jnp.where(lanef == float(ln), val, route)
        gate_ref[...] = route


def _merge(moe, seq, x2d, a, u, ga, gp, wp, ps, pa, pb, wo, g2, wrh=None, wrl=None):
    t = x2d.shape[0]
    tm = TM_MERGE
    row = lambda i: (i, 0)
    const2 = lambda i: (0, 0)
    const3 = lambda i: (0, 0, 0)
    halo = lambda i: (jnp.maximum(i * (tm // POOL_HALO) - 1, 0), 0)
    in_specs = [
        pl.BlockSpec((tm, D_MODEL), row),
        pl.BlockSpec((tm, ATTN_WIDTH), row),
        pl.BlockSpec((tm, POOL_WIDTH), row),
        pl.BlockSpec((POOL_HALO, POOL_WIDTH), halo),
        pl.BlockSpec((tm, D_MODEL), row),
        pl.BlockSpec((tm, D_MODEL), row),
        pl.BlockSpec((POOL_GROUPS, POOL_GROUP_DIM, POOL_GROUP_DIM), const3),
        pl.BlockSpec((1, POOL_WIDTH), const2),
        pl.BlockSpec((ATTN_WIDTH, D_MODEL), const2),
        pl.BlockSpec((POOL_WIDTH, D_MODEL), const2),
        pl.BlockSpec((D_MODEL, D_MODEL), const2),
        pl.BlockSpec((1, D_MODEL), const2),
    ]
    args = [x2d, a, u, u, ga, gp, wp, ps, pa, pb, wo, g2]
    out_shapes = [jax.ShapeDtypeStruct((t, D_MODEL), f32), jax.ShapeDtypeStruct((t, D_MODEL), f32 if moe else bf16)]
    out_specs = [pl.BlockSpec((tm, D_MODEL), row), pl.BlockSpec((tm, D_MODEL), row)]
    if moe:
        in_specs += [pl.BlockSpec((D_MODEL, LANES), const2), pl.BlockSpec((D_MODEL, LANES), const2)]
        args += [wrh, wrl]
        out_shapes.append(jax.ShapeDtypeStruct((t, LANES), f32))
        out_specs.append(pl.BlockSpec((tm, LANES), row))
    return pl.pallas_call(
        functools.partial(_merge_kernel, moe, seq),
        grid=(t // tm,),
        in_specs=in_specs,
        out_specs=out_specs,
        out_shape=out_shapes,
        scratch_shapes=[pltpu.VMEM((tm + POOL_HALO, POOL_WIDTH), f32)],
        compiler_params=pltpu.CompilerParams(
            dimension_semantics=("arbitrary",), vmem_limit_bytes=VMEM_LIMIT),
        name="merge_moe" if moe else "merge_dense",
    )(*args)


def _swiglu_partial(h, w1, w3, w2):
    a = jnp.dot(h, w1, preferred_element_type=f32)
    b = jnp.dot(h, w3, preferred_element_type=f32)
    act = (a * _sigmoid(a) * b).astype(bf16)
    return jnp.dot(act, w2, preferred_element_type=f32)


def _ffn_kernel(h_ref, x_ref, w1_ref, w3_ref, w2_ref, o_ref):
    @pl.when(pl.program_id(1) == 0)
    def _():
        o_ref[...] = x_ref[...]

    o_ref[...] += _swiglu_partial(h_ref[...], w1_ref[...], w3_ref[...], w2_ref[...])


def _ffn(h2, x2, w1, w3, w2, tm, tf):
    t = h2.shape[0]
    ff = w1.shape[1]
    row = lambda i, j: (i, 0)
    return pl.pallas_call(
        _ffn_kernel,
        grid=(t // tm, ff // tf),
        in_specs=[
            pl.BlockSpec((tm, D_MODEL), row),
            pl.BlockSpec((tm, D_MODEL), row),
            pl.BlockSpec((D_MODEL, tf), lambda i, j: (0, j)),
            pl.BlockSpec((D_MODEL, tf), lambda i, j: (0, j)),
            pl.BlockSpec((tf, D_MODEL), lambda i, j: (j, 0)),
        ],
        out_specs=pl.BlockSpec((tm, D_MODEL), row),
        out_shape=jax.ShapeDtypeStruct((t, D_MODEL), f32),
        compiler_params=pltpu.CompilerParams(
            dimension_semantics=("arbitrary", "arbitrary"), vmem_limit_bytes=VMEM_LIMIT),
        name="ffn_dense",
    )(h2, x2, w1, w3, w2)


def _rank_kernel(route_ref, tri_ref, rk_ref, cnt_ref, carry_sc):
    @pl.when(pl.program_id(0) == 0)
    def _():
        carry_sc[...] = jnp.zeros(carry_sc.shape, f32)

    lane = lax.broadcasted_iota(i32, (1, LANES), 1)
    lanef = lane.astype(f32)
    r = route_ref[...]
    sel = jnp.where(lane < N_EXPERTS, r, 0.0)
    ranks = jnp.dot(tri_ref[...], sel.astype(bf16), preferred_element_type=f32) + carry_sc[...]
    rk1 = jnp.sum(jnp.where(lanef == r[:, ROUTE_I1:ROUTE_I1 + 1], ranks, 0.0), axis=1, keepdims=True)
    rk2 = jnp.sum(jnp.where(lanef == r[:, ROUTE_I2:ROUTE_I2 + 1], ranks, 0.0), axis=1, keepdims=True)
    rk_ref[...] = jnp.where(lane == 0, rk1, jnp.where(lane == 1, rk2, 0.0))
    carry_sc[...] += jnp.sum(sel, axis=0, keepdims=True)
    cnt_ref[...] = carry_sc[...]


def _moe_rank(route):
    t = route.shape[0]
    tm = TM_RANK
    idx = jnp.arange(tm)
    tri = (idx[None, :] < idx[:, None]).astype(bf16)
    return pl.pallas_call(
        _rank_kernel,
        grid=(t // tm,),
        in_specs=[pl.BlockSpec((tm, LANES), lambda i: (i, 0)), pl.BlockSpec((tm, tm), lambda i: (0, 0))],
        out_specs=[pl.BlockSpec((tm, LANES), lambda i: (i, 0)), pl.BlockSpec((1, LANES), lambda i: (0, 0))],
        out_shape=[jax.ShapeDtypeStruct((t, LANES), f32), jax.ShapeDtypeStruct((1, LANES), f32)],
        scratch_shapes=[pltpu.VMEM((1, LANES), f32)],
        compiler_params=pltpu.CompilerParams(dimension_semantics=("arbitrary",)),
        name="moe_rank",
    )(route, tri)


def _row_copy(src, dst, sem):
    return pltpu.make_async_copy(src, dst, sem)


def _moe_ffn_kernel(pos_ref, te_ref, meta_ref, h_hbm, w1_ref, w3_ref, w2_ref, ys_ref,
                    inv_sm, gbuf, xb, sem):
    del te_ref
    i = pl.program_id(0)
    j = pl.program_id(1)
    n_used = meta_ref[0]
    n_slots = pos_ref.shape[0]

    def start_gather(tile, slot):
        def issue(r, carry):
            tok = inv_sm[tile * TM_EXPERT + r]
            _row_copy(h_hbm.at[pl.ds(tok, 1)], gbuf.at[slot, pl.ds(r, 1)], sem.at[slot]).start()
            return carry
        lax.fori_loop(0, TM_EXPERT, issue, 0)

    def wait_gather(slot):
        def drain(r, carry):
            _row_copy(h_hbm.at[pl.ds(0, 1)], gbuf.at[slot, pl.ds(0, 1)], sem.at[slot]).wait()
            return carry
        lax.fori_loop(0, TM_EXPERT, drain, 0)

    @pl.when((i == 0) & (j == 0))
    def _():
        for e in range(N_EXPERTS):
            def pad(r, carry):
                inv_sm[r] = 0
                return carry
            lax.fori_loop(meta_ref[1 + e], meta_ref[1 + N_EXPERTS + e], pad, 0)

        def scatter(n, carry):
            inv_sm[pos_ref[n]] = lax.shift_right_logical(n, 1)
            return carry
        lax.fori_loop(0, n_slots, scatter, 0)
        start_gather(0, 0)

    used = i < n_used

    @pl.when(used & (j == 0))
    def _():
        slot = i % 2
        wait_gather(slot)

        @pl.when(i + 1 < n_used)
        def _():
            start_gather(i + 1, 1 - slot)

        xb[...] = gbuf[slot].astype(bf16)

    @pl.when(used)
    def _():
        y = _swiglu_partial(xb[...], w1_ref[0], w3_ref[0], w2_ref[0])

        @pl.when(j == 0)
        def _():
            ys_ref[...] = y

        @pl.when(j > 0)
        def _():
            ys_ref[...] += y

    @pl.when(jnp.logical_not(used) & (j == 0))
    def _():
        ys_ref[...] = jnp.zeros(ys_ref.shape, f32)


def _moe_ffn(pos, tile_expert, meta, hf, w1, w3, w2, n_rows):
    ff = w1.shape[2]
    nj = ff // TF_EXPERT
    jj = lambda i, j, meta: jnp.where(i < meta[0], j, nj - 1)
    return pl.pallas_call(
        _moe_ffn_kernel,
        grid_spec=pltpu.PrefetchScalarGridSpec(
            num_scalar_prefetch=3,
            grid=(n_rows // TM_EXPERT, nj),
            in_specs=[
                pl.BlockSpec(memory_space=pl.ANY),
                pl.BlockSpec((1, D_MODEL, TF_EXPERT), lambda i, j, pos, te, meta: (te[i], 0, jj(i, j, meta))),
                pl.BlockSpec((1, D_MODEL, TF_EXPERT), lambda i, j, pos, te, meta: (te[i], 0, jj(i, j, meta))),
                pl.BlockSpec((1, TF_EXPERT, D_MODEL), lambda i, j, pos, te, meta: (te[i], jj(i, j, meta), 0)),
            ],
            out_specs=pl.BlockSpec((TM_EXPERT, D_MODEL), lambda i, j, pos, te, meta: (i, 0)),
            scratch_shapes=[
                pltpu.SMEM((n_rows,), i32),
                pltpu.VMEM((2, TM_EXPERT, D_MODEL), f32),
                pltpu.VMEM((TM_EXPERT, D_MODEL), bf16),
                pltpu.SemaphoreType.DMA((2,)),
            ],
        ),
        out_shape=jax.ShapeDtypeStruct((n_rows, D_MODEL), f32),
        compiler_params=pltpu.CompilerParams(
            dimension_semantics=("arbitrary", "arbitrary"), vmem_limit_bytes=VMEM_LIMIT),
        name="moe_ffn",
    )(pos, tile_expert, meta, hf, w1, w3, w2)


def _combine_kernel(pos_ref, ys_hbm, x2_ref, route_ref, o_ref, buf, sem):
    base = pl.program_id(0) * TM_ROWS

    def issue(r, carry):
        t = base + r
        for s in range(2):
            _row_copy(ys_hbm.at[pl.ds(pos_ref[2 * t + s], 1)], buf.at[s, pl.ds(r, 1)], sem).start()
        return carry

    def drain(r, carry):
        for s in range(2):
            _row_copy(ys_hbm.at[pl.ds(0, 1)], buf.at[s, pl.ds(0, 1)], sem).wait()
        return carry

    lax.fori_loop(0, TM_ROWS, issue, 0)
    lax.fori_loop(0, TM_ROWS, drain, 0)
    route = route_ref[...]
    o_ref[...] = (x2_ref[...] + route[:, ROUTE_W1:ROUTE_W1 + 1] * buf[0]
                  + route[:, ROUTE_W2:ROUTE_W2 + 1] * buf[1])


def _moe_combine(pos, ys, x2, route):
    t = x2.shape[0]
    row = lambda i, pos: (i, 0)
    return pl.pallas_call(
        _combine_kernel,
        grid_spec=pltpu.PrefetchScalarGridSpec(
            num_scalar_prefetch=1,
            grid=(t // TM_ROWS,),
            in_specs=[
                pl.BlockSpec(memory_space=pl.ANY),
                pl.BlockSpec((TM_ROWS, D_MODEL), row),
                pl.BlockSpec((TM_ROWS, LANES), row),
            ],
            out_specs=pl.BlockSpec((TM_ROWS, D_MODEL), row),
            scratch_shapes=[pltpu.VMEM((2, TM_ROWS, D_MODEL), f32), pltpu.SemaphoreType.DMA(())],
        ),
        out_shape=jax.ShapeDtypeStruct((t, D_MODEL), f32),
        compiler_params=pltpu.CompilerParams(dimension_semantics=("arbitrary",)),
        name="moe_combine",
    )(pos, ys, x2, route)


def _moe_layer(route, hf, x2, w1, w3, w2):
    t = hf.shape[0]
    n_tiles = 2 * t // TM_EXPERT + N_EXPERTS
    rk, cnt = _moe_rank(route)
    counts = cnt[0, :N_EXPERTS].astype(i32)
    padded = (counts + TM_EXPERT - 1) // TM_EXPERT * TM_EXPERT
    ends = jnp.cumsum(padded)
    starts = ends - padded
    n_used = (ends[-1] // TM_EXPERT).astype(i32)
    tile_id = jnp.arange(n_tiles, dtype=i32)
    tile_expert = jnp.minimum(jnp.sum((tile_id[:, None] * TM_EXPERT >= ends[None, :]).astype(i32), axis=1),
                              N_EXPERTS - 1)
    tile_expert = jnp.where(tile_id < n_used, tile_expert, tile_expert[jnp.maximum(n_used - 1, 0)])
    experts = jnp.arange(N_EXPERTS, dtype=f32)[None, :]
    start_of = lambda ids: jnp.sum(jnp.where(ids[:, None] == experts, starts[None, :], 0), axis=1)
    pos1 = start_of(route[:, ROUTE_I1]) + rk[:, 0].astype(i32)
    pos2 = start_of(route[:, ROUTE_I2]) + rk[:, 1].astype(i32)
    pos = jnp.stack([pos1, pos2], axis=1).reshape(2 * t).astype(i32)

    meta = jnp.concatenate([n_used.reshape(1), starts + counts, ends]).astype(i32)
    ys = _moe_ffn(pos, tile_expert, meta, hf, w1, w3, w2, n_tiles * TM_EXPERT)
    return _moe_combine(pos, ys, x2, route)


def _rope_tables(seq, rot_dim, period, active_lanes):
    half = rot_dim // 2
    inv = jnp.power(ROPE_THETA, -jnp.arange(0, rot_dim, 2, dtype=f32) / rot_dim)
    ang = jnp.arange(seq, dtype=f32)[:, None] * inv[None, :]
    cos, sin = jnp.cos(ang), jnp.sin(ang)
    ones = jnp.ones((seq, period - rot_dim), f32)
    zeros = lambda n: jnp.zeros((seq, n), f32)
    cos_p = jnp.concatenate([cos, cos, ones], axis=1)
    sin_a = jnp.concatenate([-sin, zeros(period - half)], axis=1)
    sin_b = jnp.concatenate([zeros(half), sin, zeros(period - rot_dim)], axis=1)
    reps = LANES // period
    tabs = [jnp.tile(tb, (1, reps)) for tb in (cos_p, sin_a, sin_b)]
    live = (jnp.arange(LANES) < active_lanes)[None, :]
    tabs = [jnp.where(live, tabs[0], 1.0), jnp.where(live, tabs[1], 0.0), jnp.where(live, tabs[2], 0.0)]
    return jnp.stack(tabs, axis=0)


def _relayout_w_in(w):
    d = w.shape[0]
    o = 0
    segs = {}
    for name, n in (("q", 512), ("k", 512), ("v", 512), ("qi", 256), ("ki", 32), ("wi", 8),
                    ("u", 512), ("ga", 1024), ("gp", 1024)):
        segs[name] = w[:, o:o + n]
        o += n
    qi = segs["qi"].reshape(d, IDX_HEADS, IDX_DIM)
    qcat = jnp.concatenate([qi, qi, qi, jnp.zeros_like(qi)], axis=-1).reshape(d, IDX_HEADS * LANES)
    kwc = jnp.concatenate([segs["ki"], segs["ki"], segs["ki"], segs["wi"],
                           jnp.zeros((d, LANES - 3 * IDX_DIM - IDX_HEADS), w.dtype)], axis=1)
    vh = segs["v"].reshape(d, N_HEADS // 2, 2, HEAD_DIM)
    zh = jnp.zeros_like(vh[:, :, 0])
    vx = jnp.stack([jnp.concatenate([vh[:, :, 0], zh], axis=-1), jnp.concatenate([zh, vh[:, :, 1]], axis=-1)],
                   axis=2).reshape(d, N_HEADS * LANES)
    out = jnp.concatenate([segs["q"], segs["k"], vx, qcat, kwc, segs["u"], segs["ga"], segs["gp"]], axis=1)
    return out.astype(bf16)


def kernel(x, mix_norm, w_in, q_norm, k_norm, w_pool, pool_scale, w_attn_proj, w_pool_proj, w_out, ffn_norm,
           dense_w1, dense_w3, dense_w2, moe_router, moe_w1, moe_w3, moe_w2):
    batch, seq, d = x.shape
    depth = w_in.shape[0]
    t = batch * seq
    assert d == D_MODEL and seq % TK == 0 and seq % TM_PROJ == 0 and seq % TM_MERGE == 0

    rqk = _rope_tables(seq, ROPE_DIM, HEAD_DIM, LANES)
    rqi = _rope_tables(seq, IDX_ROPE_DIM, IDX_DIM, LANES)
    rki = _rope_tables(seq, IDX_ROPE_DIM, IDX_DIM, 3 * IDX_DIM)
    head_of = jnp.arange(ATTN_WIDTH) // HEAD_DIM
    bd = jnp.where(head_of[:, None] == head_of[None, :], 1.0 / HEAD_DIM, 0.0).astype(bf16)
    lane_in_tile = jnp.arange(N_HEADS * LANES) % LANES
    odd_head = (jnp.arange(N_HEADS * LANES) // LANES) % 2 == 1
    ones_lane = (lane_in_tile == jnp.where(odd_head, 0, HEAD_DIM)).astype(f32)[None, :]

    xc = x.reshape(t, d)
    for layer in range(depth):
        w = _relayout_w_in(w_in[layer])
        qn = jnp.tile(q_norm[layer], N_HEADS)[None, :]
        kn = jnp.tile(k_norm[layer], N_HEADS)[None, :]
        q, k, vx, qc, kw, u, ga, gp = _in_proj(xc, mix_norm[layer][None, :], w, qn, kn, bd, ones_lane,
                                               rqk, rqi, rki, seq)
        a = _dsa_attention(q, qc, kw, k, vx, batch, seq)
        moe = layer % 2 == 1
        common = (xc, a, u, ga, gp, w_pool[layer].astype(bf16), pool_scale[layer][None, :],
                  w_attn_proj[layer].astype(bf16), w_pool_proj[layer].astype(bf16), w_out[layer].astype(bf16),
                  ffn_norm[layer][None, :])
        idx = layer // 2
        if moe:
            wr = jnp.pad(moe_router[idx], ((0, 0), (0, LANES - N_EXPERTS)))
            wrh = wr.astype(bf16)
            wrl = (wr - wrh.astype(f32)).astype(bf16)
            x2, hf, route = _merge(True, seq, *common, wrh, wrl)
            xc = _moe_layer(route, hf, x2, moe_w1[idx].astype(bf16), moe_w3[idx].astype(bf16),
                            moe_w2[idx].astype(bf16))
        else:
            x2, h2 = _merge(False, seq, *common)
            xc = _ffn(h2, x2, dense_w1[idx].astype(bf16), dense_w3[idx].astype(bf16),
                      dense_w2[idx].astype(bf16), tm=1024, tf=1408)
    return xc.reshape(batch, seq, d)
```

```python
import functools

import jax
import jax.numpy as jnp
from jax import lax
from jax.experimental import pallas as pl
from jax.experimental.pallas import tpu as pltpu

bf16 = jnp.bfloat16
f32 = jnp.float32
i32 = jnp.int32
i16 = jnp.int16

D_MODEL = 1024
N_HEADS = 8
HEAD_DIM = 64
ATTN_WIDTH = 512
ROPE_DIM = 16
ROPE_THETA = 500000.0
IDX_HEADS = 8
IDX_DIM = 32
IDX_ROPE_DIM = 8
TOPK_MAX = 256
POOL_GROUPS = 4
POOL_GROUP_DIM = 128
POOL_WIDTH = 512
POOL_WINDOWS = (2, 4, 8, 16)
POOL_HALO = 16
N_EXPERTS = 8
EPS = 1e-6

LANES = 128
INT_MIN = -(2 ** 31)
I16_MIN = -(2 ** 15)
VMEM_LIMIT = 56 * 1024 * 1024
LOG2E = 1.4426950408889634

OFF_Q = 0
OFF_K = 512
OFF_VX = 1024
OFF_QCAT = 2048
OFF_KW = 3072
OFF_U = 3200
OFF_GA = 3712
OFF_GP = 4736
W_COLS = 5760

ROUTE_W1, ROUTE_W2, ROUTE_I1, ROUTE_I2 = 8, 9, 10, 11

TM_PROJ = 512
TM_MERGE = 512
TM_RANK = 512
TM_EXPERT = 512
TF_EXPERT = 896
TM_ROWS = 256
ROW_UNROLL = 8
QB = 128
TK = 512


def _sigmoid(x):
    return 1.0 / (1.0 + jnp.exp(-x))


def _in_proj_kernel(x_ref, g_ref, w_ref, qn_ref, kn_ref, bd_ref, one_ref, rqk_ref, rqi_ref, rki_ref,
                    q_ref, k_ref, vx_ref, qc_ref, kw_ref, u_ref, ga_ref, gp_ref):
    x = x_ref[...]
    ms = jnp.mean(x * x, axis=-1, keepdims=True)
    h = (x * lax.rsqrt(ms + EPS) * g_ref[...]).astype(bf16)

    def proj(lo, n):
        return jnp.dot(h, w_ref[:, lo:lo + n], preferred_element_type=f32)

    def rope(xc, tab_ref, sh):
        return (xc * tab_ref[0] + pltpu.roll(xc, LANES - sh, 1) * tab_ref[1]
                + pltpu.roll(xc, sh, 1) * tab_ref[2])

    def headnorm(z, gain_ref):
        msh = jnp.dot((z * z).astype(bf16), bd_ref[...], preferred_element_type=f32)
        return z * lax.rsqrt(msh + EPS) * gain_ref[...]

    zq = headnorm(proj(OFF_Q, ATTN_WIDTH), qn_ref) * (HEAD_DIM ** -0.5 * LOG2E)
    zk = headnorm(proj(OFF_K, ATTN_WIDTH), kn_ref)
    for c in range(ATTN_WIDTH // LANES):
        sl = slice(c * LANES, (c + 1) * LANES)
        q_ref[:, sl] = rope(zq[:, sl], rqk_ref, ROPE_DIM // 2).astype(bf16)
        k_ref[:, sl] = rope(zk[:, sl], rqk_ref, ROPE_DIM // 2).astype(bf16)
    vx_ref[...] = jnp.where(one_ref[...] > 0.0, 1.0, proj(OFF_VX, N_HEADS * LANES)).astype(bf16)

    lane = lax.broadcasted_iota(i32, (1, LANES), 1)
    zc = proj(OFF_QCAT, IDX_HEADS * LANES)
    for c in range(IDX_HEADS):
        sl = slice(c * LANES, (c + 1) * LANES)
        r = rope(zc[:, sl], rqi_ref, IDX_ROPE_DIM // 2)
        lo = r - r.astype(bf16).astype(f32)
        qc_ref[:, sl] = jnp.where((lane >= 32) & (lane < 64), lo, r).astype(bf16)

    zkw = rope(proj(OFF_KW, LANES), rki_ref, IDX_ROPE_DIM // 2)
    lo = zkw - zkw.astype(bf16).astype(f32)
    zkw = jnp.where((lane >= 64) & (lane < 96), lo, zkw)
    kw_ref[...] = jnp.where(lane >= 96, zkw * ((IDX_HEADS * IDX_DIM) ** -0.5), zkw)

    u_ref[...] = proj(OFF_U, POOL_WIDTH)
    ga_ref[...] = proj(OFF_GA, D_MODEL)
    gp_ref[...] = proj(OFF_GP, D_MODEL)


def _in_proj(x2d, g, w, qn, kn, bd, ones_lane, rqk, rqi, rki, seq):
    t = x2d.shape[0]
    tm = TM_PROJ
    ns = seq // tm
    row = lambda i: (i, 0)
    const = lambda i: (0, 0)
    tab = lambda i: (0, i % ns, 0)
    out_shapes = (
        jax.ShapeDtypeStruct((t, ATTN_WIDTH), bf16),
        jax.ShapeDtypeStruct((t, ATTN_WIDTH), bf16),
        jax.ShapeDtypeStruct((t, N_HEADS * LANES), bf16),
        jax.ShapeDtypeStruct((t, IDX_HEADS * LANES), bf16),
        jax.ShapeDtypeStruct((t, LANES), f32),
        jax.ShapeDtypeStruct((t, POOL_WIDTH), f32),
        jax.ShapeDtypeStruct((t, D_MODEL), f32),
        jax.ShapeDtypeStruct((t, D_MODEL), f32),
    )
    return pl.pallas_call(
        _in_proj_kernel,
        grid=(t // tm,),
        in_specs=[
            pl.BlockSpec((tm, D_MODEL), row),
            pl.BlockSpec((1, D_MODEL), const),
            pl.BlockSpec((D_MODEL, W_COLS), const, pipeline_mode=pl.Buffered(1)),
            pl.BlockSpec((1, ATTN_WIDTH), const),
            pl.BlockSpec((1, ATTN_WIDTH), const),
            pl.BlockSpec((ATTN_WIDTH, ATTN_WIDTH), const),
            pl.BlockSpec((1, N_HEADS * LANES), const),
            pl.BlockSpec((3, tm, LANES), tab),
            pl.BlockSpec((3, tm, LANES), tab),
            pl.BlockSpec((3, tm, LANES), tab),
        ],
        out_specs=[pl.BlockSpec((tm, s.shape[1]), row) for s in out_shapes],
        out_shape=out_shapes,
        compiler_params=pltpu.CompilerParams(
            dimension_semantics=("arbitrary",), vmem_limit_bytes=VMEM_LIMIT),
        name="in_proj",
    )(x2d, g, w, qn, kn, bd, ones_lane, rqk, rqi, rki)


def _lane_fold(x, op):
    out = x[:, 0:LANES]
    for j in range(1, x.shape[1] // LANES):
        out = op(out, x[:, j * LANES:(j + 1) * LANES])
    return out


def _dsa_kernel(topk, q_ref, qc_ref, kwq_ref, k_ref, vx_ref, kw_ref, o_ref,
                key_sc, hi_sc, lo_sc, kcat_sc, qm2_sc, mrun_sc, m_sc, acc_sc):
    i = pl.program_id(1)
    nk = (i * QB + QB + TK - 1) // TK
    kf = float(topk)

    @pl.when(i == 0)
    def _():
        kcat_sc[...] = kw_ref[...].astype(bf16)

    qpos = i * QB + lax.broadcasted_iota(i32, (QB, 1), 0)
    lane_tk = lax.broadcasted_iota(i32, (1, TK), 1)
    wq = kwq_ref[...]

    def to_key(score):
        bits = pltpu.bitcast(score, i32)
        return bits ^ ((bits >> 31) & 0x7FFFFFFF)

    def score_chunk(c, carry):
        koff = pl.multiple_of(c * TK, TK)
        kc = kcat_sc[pl.ds(koff, TK), :]
        acc = jnp.zeros((QB, TK), f32)
        for h in range(IDX_HEADS):
            d = lax.dot_general(qc_ref[:, h * LANES:(h + 1) * LANES], kc,
                                (((1,), (1,)), ((), ())), preferred_element_type=f32)
            acc = acc + jnp.maximum(d, 0.0) * wq[:, 96 + h:97 + h]
        causal = (koff + lane_tk) <= qpos
        key = jnp.where(causal, to_key(acc), INT_MIN)
        key_sc[c] = key
        hi_sc[c] = (key >> 16).astype(i16)
        lo_sc[c] = ((key & 0xFFFF) - 32768).astype(i16)
        return carry

    lax.fori_loop(0, nk, score_chunk, 0)

    def count_ge(cand):
        def body(c, part):
            return part + _lane_fold(jnp.where(key_sc[c] >= cand, 1.0, 0.0), jnp.add)
        part = lax.fori_loop(0, nk, body, jnp.zeros((QB, LANES), f32))
        return jnp.sum(part, axis=1, keepdims=True)

    def count16(sc, cand):
        candb = jnp.broadcast_to(cand, (QB, LANES)).astype(i16)

        def body(c, part):
            kk = sc[c]
            for j in range(TK // LANES):
                hit = kk[:, j * LANES:(j + 1) * LANES] >= candb
                part = part + jnp.where(hit, jnp.int16(1), jnp.int16(0))
            return part

        part = lax.fori_loop(0, nk, body, jnp.zeros((QB, LANES), i16))
        return jnp.sum(part.astype(f32), axis=1, keepdims=True)

    def search16(sc, target):
        zero = jnp.zeros((QB, 1), i32)
        prefix = jnp.where(count16(sc, zero) >= target, zero, I16_MIN)

        def bit_body(b, prefix):
            cand = prefix | jnp.left_shift(1, 14 - b)
            return jnp.where(count16(sc, cand) >= target, cand, prefix)

        return lax.fori_loop(0, 15, bit_body, prefix)

    p1 = search16(hi_sc, kf)
    above = jnp.where(p1 >= -I16_MIN - 1, 0.0, count16(hi_sc, p1 + 1))
    p1b = jnp.broadcast_to(p1, (QB, LANES)).astype(i16)

    def keep_bucket(c, carry):
        hi = hi_sc[c]
        lo = lo_sc[c]
        for j in range(TK // LANES):
            sl = slice(j * LANES, (j + 1) * LANES)
            lo_sc[c, :, sl] = jnp.where(hi[:, sl] == p1b, lo[:, sl], jnp.int16(I16_MIN))
        return carry

    lax.fori_loop(0, nk, keep_bucket, 0)
    p2 = search16(lo_sc, kf - above)
    thr_raw = (p1 << 16) | ((p2 - I16_MIN) & 0xFFFF)
    is_short = (qpos + 1) < topk
    thr = jnp.where(is_short, INT_MIN + 1, thr_raw)

    is_tie = (count_ge(thr) > kf) & jnp.logical_not(is_short)

    @pl.when(jnp.max(jnp.where(is_tie, 1.0, 0.0)) > 0.0)
    def _():
        need = kf - count_ge(thr + 1)

        def count_eq_below(m):
            def body(c, part):
                hit = (key_sc[c] == thr) & ((c * TK + lane_tk) < m)
                return part + _lane_fold(jnp.where(hit, 1.0, 0.0), jnp.add)
            part = lax.fori_loop(0, nk, body, jnp.zeros((QB, LANES), f32))
            return jnp.sum(part, axis=1, keepdims=True)

        mprime = jnp.zeros((QB, 1), i32)
        bit = key_sc.shape[0] * TK // 2
        while bit >= 1:
            cand = mprime | bit
            mprime = jnp.where(count_eq_below(cand) < need, cand, mprime)
            bit //= 2

        def drop_ties(c, carry):
            key = key_sc[c]
            drop = is_tie & (key == thr) & ((c * TK + lane_tk) > mprime)
            key_sc[c] = jnp.where(drop, thr - 1, key)
            return carry

        lax.fori_loop(0, nk, drop_ties, 0)

    lane = lax.broadcasted_iota(i32, (1, LANES), 1)
    for pr in range(N_HEADS // 2):
        qp = q_ref[:, pr * LANES:(pr + 1) * LANES]
        qm2_sc[pr, 0:QB, :] = jnp.where(lane < HEAD_DIM, qp, jnp.zeros((), bf16))
        qm2_sc[pr, QB:2 * QB, :] = jnp.where(lane >= HEAD_DIM, qp, jnp.zeros((), bf16))
    mrun_sc[...] = jnp.full(mrun_sc.shape, -jnp.inf, f32)
    acc_sc[...] = jnp.zeros(acc_sc.shape, f32)

    def pair_logits(c, pr, bias):
        koff = pl.multiple_of(c * TK, TK)
        kc = k_ref[pl.ds(koff, TK), pr * LANES:(pr + 1) * LANES]
        s2 = lax.dot_general(qm2_sc[pr], kc, (((1,), (1,)), ((), ())), preferred_element_type=f32)
        return s2[0:QB] + bias, s2[QB:2 * QB] + bias

    def max_chunk(c, carry):
        bias = jnp.where(key_sc[c] >= thr, 0.0, -jnp.inf)
        for pr in range(N_HEADS // 2):
            sa, sb = pair_logits(c, pr, bias)
            mrun_sc[pr, 0:QB, :] = jnp.maximum(mrun_sc[pr, 0:QB, :], _lane_fold(sa, jnp.maximum))
            mrun_sc[pr, QB:2 * QB, :] = jnp.maximum(mrun_sc[pr, QB:2 * QB, :], _lane_fold(sb, jnp.maximum))
        return carry

    lax.fori_loop(0, nk, max_chunk, 0)

    for pr in range(N_HEADS // 2):
        m = jnp.max(mrun_sc[pr], axis=1, keepdims=True)
        m_sc[pr] = jnp.where(m == -jnp.inf, 0.0, m)

    def acc_chunk(c, carry):
        koff = pl.multiple_of(c * TK, TK)
        bias = jnp.where(key_sc[c] >= thr, 0.0, -jnp.inf)
        for pr in range(N_HEADS // 2):
            sa, sb = pair_logits(c, pr, bias)
            m = m_sc[pr]
            pa = jnp.exp2(sa - m[0:QB]).astype(bf16)
            pb = jnp.exp2(sb - m[QB:2 * QB]).astype(bf16)
            va = vx_ref[pl.ds(koff, TK), (2 * pr) * LANES:(2 * pr + 1) * LANES]
            vb = vx_ref[pl.ds(koff, TK), (2 * pr + 1) * LANES:(2 * pr + 2) * LANES]
            acc_sc[2 * pr] += jnp.dot(pa, va, preferred_element_type=f32)
            acc_sc[2 * pr + 1] += jnp.dot(pb, vb, preferred_element_type=f32)
        return carry

    lax.fori_loop(0, nk, acc_chunk, 0)

    for pr in range(N_HEADS // 2):
        a0 = acc_sc[2 * pr]
        a1 = acc_sc[2 * pr + 1]
        o0 = a0 / a0[:, HEAD_DIM:HEAD_DIM + 1]
        o1 = a1 / a1[:, 0:1]
        o_ref[:, pr * LANES:(pr + 1) * LANES] = jnp.where(lane < HEAD_DIM, o0, o1).astype(bf16)


def _dsa_attention(q, qc, kw, k, vx, batch, seq):
    t = q.shape[0]
    nq = seq // QB
    topk = min(TOPK_MAX, seq // 4)
    qrow = lambda b, i: (b * nq + i, 0)
    per_batch = lambda b, i: (b, 0)
    return pl.pallas_call(
        functools.partial(_dsa_kernel, topk),
        grid=(batch, nq),
        in_specs=[
            pl.BlockSpec((QB, ATTN_WIDTH), qrow),
            pl.BlockSpec((QB, IDX_HEADS * LANES), qrow),
            pl.BlockSpec((QB, LANES), qrow),
            pl.BlockSpec((seq, ATTN_WIDTH), per_batch, pipeline_mode=pl.Buffered(1)),
            pl.BlockSpec((seq, N_HEADS * LANES), per_batch, pipeline_mode=pl.Buffered(1)),
            pl.BlockSpec((seq, LANES), per_batch, pipeline_mode=pl.Buffered(1)),
        ],
        out_specs=pl.BlockSpec((QB, ATTN_WIDTH), qrow),
        out_shape=jax.ShapeDtypeStruct((t, ATTN_WIDTH), bf16),
        scratch_shapes=[
            pltpu.VMEM((seq // TK, QB, TK), i32),
            pltpu.VMEM((seq // TK, QB, TK), i16),
            pltpu.VMEM((seq // TK, QB, TK), i16),
            pltpu.VMEM((seq, LANES), bf16),
            pltpu.VMEM((N_HEADS // 2, 2 * QB, LANES), bf16),
            pltpu.VMEM((N_HEADS // 2, 2 * QB, LANES), f32),
            pltpu.VMEM((N_HEADS // 2, 2 * QB, 1), f32),
            pltpu.VMEM((N_HEADS, QB, LANES), f32),
        ],
        compiler_params=pltpu.CompilerParams(
            dimension_semantics=("arbitrary", "arbitrary"), vmem_limit_bytes=VMEM_LIMIT),
        name="dsa_attention",
    )(q, qc, kw, k, vx, kw)


def _merge_kernel(moe, seq, *refs):
    if moe:
        (x_ref, a_ref, u_ref, uh_ref, ga_ref, gp_ref, wp_ref, ps_ref, pa_ref, pb_ref, wo_ref, g2_ref,
         wrh_ref, wrl_ref, x2_ref, h2_ref, gate_ref, e_sc) = refs
    else:
        (x_ref, a_ref, u_ref, uh_ref, ga_ref, gp_ref, wp_ref, ps_ref, pa_ref, pb_ref, wo_ref, g2_ref,
         x2_ref, h2_ref, e_sc) = refs
    tm = TM_MERGE
    i = pl.program_id(0)
    ti = i % (seq // tm)
    e_sc[0:POOL_HALO, :] = jnp.where(ti == 0, 0.0, uh_ref[...])
    e_sc[POOL_HALO:POOL_HALO + tm, :] = u_ref[...]
    npos = (ti * tm + 1 + lax.broadcasted_iota(i32, (tm, 1), 0)).astype(f32)

    parts = []
    for g, w in enumerate(POOL_WINDOWS):
        sl = slice(g * POOL_GROUP_DIM, (g + 1) * POOL_GROUP_DIM)
        tot = e_sc[POOL_HALO:POOL_HALO + tm, sl]
        for j in range(1, w):
            tot = tot + e_sc[POOL_HALO - j:POOL_HALO - j + tm, sl]
        diff = tot / jnp.minimum(npos, float(w)) - u_ref[:, sl]
        parts.append(jnp.dot(diff.astype(bf16), wp_ref[g], preferred_element_type=f32))
    p = jnp.concatenate(parts, axis=1) * ps_ref[...]

    ab = jnp.dot(a_ref[...], pa_ref[...], preferred_element_type=f32)
    pb = jnp.dot(p.astype(bf16), pb_ref[...], preferred_element_type=f32)
    merged = _sigmoid(ga_ref[...]) * ab + _sigmoid(gp_ref[...]) * pb
    x2 = x_ref[...] + jnp.dot(merged.astype(bf16), wo_ref[...], preferred_element_type=f32)
    x2_ref[...] = x2
    ms = jnp.mean(x2 * x2, axis=-1, keepdims=True)
    hf = x2 * lax.rsqrt(ms + EPS) * g2_ref[...]
    h2_ref[...] = hf.astype(h2_ref.dtype)

    if moe:
        hi = hf.astype(bf16)
        lo = (hf - hi.astype(f32)).astype(bf16)
        logits = (jnp.dot(hi, wrh_ref[...], preferred_element_type=f32)
                  + jnp.dot(lo, wrh_ref[...], preferred_element_type=f32)
                  + jnp.dot(hi, wrl_ref[...], preferred_element_type=f32))
        lanef = lax.broadcasted_iota(i32, (1, LANES), 1).astype(f32)
        lg = jnp.where(lanef < N_EXPERTS, logits, -jnp.inf)
        v1 = jnp.max(lg, axis=1, keepdims=True)
        i1 = jnp.min(jnp.where(lg == v1, lanef, float(LANES)), axis=1, keepdims=True)
        lg2 = jnp.where(lanef == i1, -jnp.inf, lg)
        v2 = jnp.max(lg2, axis=1, keepdims=True)
        i2 = jnp.min(jnp.where(lg2 == v2, lanef, float(LANES)), axis=1, keepdims=True)
        tt = jnp.exp(v2 - v1)
        w1 = 1.0 / (1.0 + tt)
        w2 = tt / (1.0 + tt)
        route = jnp.where((lanef == i1) | (lanef == i2), 1.0, 0.0)
        for ln, val in ((ROUTE_W1, w1), (ROUTE_W2, w2), (ROUTE_I1, i1), (ROUTE_I2, i2)):
            route = jnp.where(lanef == float(ln), val, route)
        gate_ref[...] = route


def _merge(moe, seq, x2d, a, u, ga, gp, wp, ps, pa, pb, wo, g2, wrh=None, wrl=None):
    t = x2d.shape[0]
    tm = TM_MERGE
    row = lambda i: (i, 0)
    const2 = lambda i: (0, 0)
    const3 = lambda i: (0, 0, 0)
    halo = lambda i: (jnp.maximum(i * (tm // POOL_HALO) - 1, 0), 0)
    in_specs = [
        pl.BlockSpec((tm, D_MODEL), row),
        pl.BlockSpec((tm, ATTN_WIDTH), row),
        pl.BlockSpec((tm, POOL_WIDTH), row),
        pl.BlockSpec((POOL_HALO, POOL_WIDTH), halo),
        pl.BlockSpec((tm, D_MODEL), row),
        pl.BlockSpec((tm, D_MODEL), row),
        pl.BlockSpec((POOL_GROUPS, POOL_GROUP_DIM, POOL_GROUP_DIM), const3),
        pl.BlockSpec((1, POOL_WIDTH), const2),
        pl.BlockSpec((ATTN_WIDTH, D_MODEL), const2),
        pl.BlockSpec((POOL_WIDTH, D_MODEL), const2),
        pl.BlockSpec((D_MODEL, D_MODEL), const2),
        pl.BlockSpec((1, D_MODEL), const2),
    ]
    args = [x2d, a, u, u, ga, gp, wp, ps, pa, pb, wo, g2]
    out_shapes = [jax.ShapeDtypeStruct((t, D_MODEL), f32), jax.ShapeDtypeStruct((t, D_MODEL), f32 if moe else bf16)]
    out_specs = [pl.BlockSpec((tm, D_MODEL), row), pl.BlockSpec((tm, D_MODEL), row)]
    if moe:
        in_specs += [pl.BlockSpec((D_MODEL, LANES), const2), pl.BlockSpec((D_MODEL, LANES), const2)]
        args += [wrh, wrl]
        out_shapes.append(jax.ShapeDtypeStruct((t, LANES), f32))
        out_specs.append(pl.BlockSpec((tm, LANES), row))
    return pl.pallas_call(
        functools.partial(_merge_kernel, moe, seq),
        grid=(t // tm,),
        in_specs=in_specs,
        out_specs=out_specs,
        out_shape=out_shapes,
        scratch_shapes=[pltpu.VMEM((tm + POOL_HALO, POOL_WIDTH), f32)],
        compiler_params=pltpu.CompilerParams(
            dimension_semantics=("arbitrary",), vmem_limit_bytes=VMEM_LIMIT),
        name="merge_moe" if moe else "merge_dense",
    )(*args)


def _swiglu_partial(h, w1, w3, w2):
    a = jnp.dot(h, w1, preferred_element_type=f32)
    b = jnp.dot(h, w3, preferred_element_type=f32)
    act = (a * _sigmoid(a) * b).astype(bf16)
    return jnp.dot(act, w2, preferred_element_type=f32)


def _ffn_kernel(h_ref, x_ref, w1_ref, w3_ref, w2_ref, o_ref):
    @pl.when(pl.program_id(1) == 0)
    def _():
        o_ref[...] = x_ref[...]

    o_ref[...] += _swiglu_partial(h_ref[...], w1_ref[...], w3_ref[...], w2_ref[...])


def _ffn(h2, x2, w1, w3, w2, tm, tf):
    t = h2.shape[0]
    ff = w1.shape[1]
    row = lambda i, j: (i, 0)
    return pl.pallas_call(
        _ffn_kernel,
        grid=(t // tm, ff // tf),
        in_specs=[
            pl.BlockSpec((tm, D_MODEL), row),
            pl.BlockSpec((tm, D_MODEL), row),
            pl.BlockSpec((D_MODEL, tf), lambda i, j: (0, j)),
            pl.BlockSpec((D_MODEL, tf), lambda i, j: (0, j)),
            pl.BlockSpec((tf, D_MODEL), lambda i, j: (j, 0)),
        ],
        out_specs=pl.BlockSpec((tm, D_MODEL), row),
        out_shape=jax.ShapeDtypeStruct((t, D_MODEL), f32),
        compiler_params=pltpu.CompilerParams(
            dimension_semantics=("arbitrary", "arbitrary"), vmem_limit_bytes=VMEM_LIMIT),
        name="ffn_dense",
    )(h2, x2, w1, w3, w2)


def _rank_kernel(route_ref, tri_ref, rk_ref, cnt_ref, carry_sc):
    @pl.when(pl.program_id(0) == 0)
    def _():
        carry_sc[...] = jnp.zeros(carry_sc.shape, f32)

    lane = lax.broadcasted_iota(i32, (1, LANES), 1)
    lanef = lane.astype(f32)
    r = route_ref[...]
    sel = jnp.where(lane < N_EXPERTS, r, 0.0)
    ranks = jnp.dot(tri_ref[...], sel.astype(bf16), preferred_element_type=f32) + carry_sc[...]
    rk1 = jnp.sum(jnp.where(lanef == r[:, ROUTE_I1:ROUTE_I1 + 1], ranks, 0.0), axis=1, keepdims=True)
    rk2 = jnp.sum(jnp.where(lanef == r[:, ROUTE_I2:ROUTE_I2 + 1], ranks, 0.0), axis=1, keepdims=True)
    rk_ref[...] = jnp.where(lane == 0, rk1, jnp.where(lane == 1, rk2, 0.0))
    carry_sc[...] += jnp.sum(sel, axis=0, keepdims=True)
    cnt_ref[...] = carry_sc[...]


def _moe_rank(route):
    t = route.shape[0]
    tm = TM_RANK
    idx = jnp.arange(tm)
    tri = (idx[None, :] < idx[:, None]).astype(bf16)
    return pl.pallas_call(
        _rank_kernel,
        grid=(t // tm,),
        in_specs=[pl.BlockSpec((tm, LANES), lambda i: (i, 0)), pl.BlockSpec((tm, tm), lambda i: (0, 0))],
        out_specs=[pl.BlockSpec((tm, LANES), lambda i: (i, 0)), pl.BlockSpec((1, LANES), lambda i: (0, 0))],
        out_shape=[jax.ShapeDtypeStruct((t, LANES), f32), jax.ShapeDtypeStruct((1, LANES), f32)],
        scratch_shapes=[pltpu.VMEM((1, LANES), f32)],
        compiler_params=pltpu.CompilerParams(dimension_semantics=("arbitrary",)),
        name="moe_rank",
    )(route, tri)


def _row_copy(src, dst, sem):
    return pltpu.make_async_copy(src, dst, sem)


def _moe_ffn_kernel(pos_ref, te_ref, meta_ref, h_hbm, w1_ref, w3_ref, w2_ref, ys_ref,
                    inv_sm, gbuf, xb, sem):
    del te_ref
    i = pl.program_id(0)
    j = pl.program_id(1)
    n_used = meta_ref[0]
    n_slots = pos_ref.shape[0]

    def start_gather(tile, slot):
        def issue(g, carry):
            for u in range(ROW_UNROLL):
                r = g * ROW_UNROLL + u
                tok = inv_sm[tile * TM_EXPERT + r]
                _row_copy(h_hbm.at[pl.ds(tok, 1)], gbuf.at[slot, pl.ds(r, 1)],
                          sem.at[slot]).start(priority=u % 2)
            return carry
        lax.fori_loop(0, TM_EXPERT // ROW_UNROLL, issue, 0)

    def wait_gather(slot):
        _row_copy(h_hbm.at[pl.ds(0, TM_EXPERT)], gbuf.at[slot], sem.at[slot]).wait()

    @pl.when((i == 0) & (j == 0))
    def _():
        for e in range(N_EXPERTS):
            def pad(r, carry):
                inv_sm[r] = 0
                return carry
            lax.fori_loop(meta_ref[1 + e], meta_ref[1 + N_EXPERTS + e], pad, 0)

        def scatter(g, carry):
            for u in range(ROW_UNROLL):
                n = g * ROW_UNROLL + u
                inv_sm[pos_ref[n]] = lax.shift_right_logical(n, 1)
            return carry
        lax.fori_loop(0, n_slots // ROW_UNROLL, scatter, 0)
        start_gather(0, 0)

    used = i < n_used

    @pl.when(used & (j == 0))
    def _():
        slot = i % 2
        wait_gather(slot)

        @pl.when(i + 1 < n_used)
        def _():
            start_gather(i + 1, 1 - slot)

        xb[...] = gbuf[slot].astype(bf16)

    @pl.when(used)
    def _():
        y = _swiglu_partial(xb[...], w1_ref[0], w3_ref[0], w2_ref[0])

        @pl.when(j == 0)
        def _():
            ys_ref[...] = y

        @pl.when(j > 0)
        def _():
            ys_ref[...] += y

    @pl.when(jnp.logical_not(used) & (j == 0))
    def _():
        ys_ref[...] = jnp.zeros(ys_ref.shape, f32)


def _moe_ffn(pos, tile_expert, meta, hf, w1, w3, w2, n_rows):
    ff = w1.shape[2]
    nj = ff // TF_EXPERT
    jj = lambda i, j, meta: jnp.where(i < meta[0], j, nj - 1)
    return pl.pallas_call(
        _moe_ffn_kernel,
        grid_spec=pltpu.PrefetchScalarGridSpec(
            num_scalar_prefetch=3,
            grid=(n_rows // TM_EXPERT, nj),
            in_specs=[
                pl.BlockSpec(memory_space=pl.ANY),
                pl.BlockSpec((1, D_MODEL, TF_EXPERT), lambda i, j, pos, te, meta: (te[i], 0, jj(i, j, meta))),
                pl.BlockSpec((1, D_MODEL, TF_EXPERT), lambda i, j, pos, te, meta: (te[i], 0, jj(i, j, meta))),
                pl.BlockSpec((1, TF_EXPERT, D_MODEL), lambda i, j, pos, te, meta: (te[i], jj(i, j, meta), 0)),
            ],
            out_specs=pl.BlockSpec((TM_EXPERT, D_MODEL), lambda i, j, pos, te, meta: (i, 0)),
            scratch_shapes=[
                pltpu.SMEM((n_rows,), i32),
                pltpu.VMEM((2, TM_EXPERT, D_MODEL), f32),
                pltpu.VMEM((TM_EXPERT, D_MODEL), bf16),
                pltpu.SemaphoreType.DMA((2,)),
            ],
        ),
        out_shape=jax.ShapeDtypeStruct((n_rows, D_MODEL), f32),
        compiler_params=pltpu.CompilerParams(
            dimension_semantics=("arbitrary", "arbitrary"), vmem_limit_bytes=VMEM_LIMIT),
        name="moe_ffn",
    )(pos, tile_expert, meta, hf, w1, w3, w2)


def _combine_kernel(pos_ref, ys_hbm, x2_ref, route_ref, o_ref, buf, sem):
    base = pl.program_id(0) * TM_ROWS

    def issue(g, carry):
        for u in range(ROW_UNROLL // 2):
            r = g * (ROW_UNROLL // 2) + u
            for s in range(2):
                _row_copy(ys_hbm.at[pl.ds(pos_ref[2 * (base + r) + s], 1)], buf.at[s, pl.ds(r, 1)],
                          sem).start(priority=s)
        return carry

    lax.fori_loop(0, TM_ROWS // (ROW_UNROLL // 2), issue, 0)
    for s in range(2):
        _row_copy(ys_hbm.at[pl.ds(0, TM_ROWS)], buf.at[s], sem).wait()
    route = route_ref[...]
    o_ref[...] = (x2_ref[...] + route[:, ROUTE_W1:ROUTE_W1 + 1] * buf[0]
                  + route[:, ROUTE_W2:ROUTE_W2 + 1] * buf[1])


def _moe_combine(pos, ys, x2, route):
    t = x2.shape[0]
    row = lambda i, pos: (i, 0)
    return pl.pallas_call(
        _combine_kernel,
        grid_spec=pltpu.PrefetchScalarGridSpec(
            num_scalar_prefetch=1,
            grid=(t // TM_ROWS,),
            in_specs=[
                pl.BlockSpec(memory_space=pl.ANY),
                pl.BlockSpec((TM_ROWS, D_MODEL), row),
                pl.BlockSpec((TM_ROWS, LANES), row),
            ],
            out_specs=pl.BlockSpec((TM_ROWS, D_MODEL), row),
            scratch_shapes=[pltpu.VMEM((2, TM_ROWS, D_MODEL), f32), pltpu.SemaphoreType.DMA(())],
        ),
        out_shape=jax.ShapeDtypeStruct((t, D_MODEL), f32),
        compiler_params=pltpu.CompilerParams(dimension_semantics=("arbitrary",)),
        name="moe_combine",
    )(pos, ys, x2, route)


def _moe_layer(route, hf, x2, w1, w3, w2):
    t = hf.shape[0]
    n_tiles = 2 * t // TM_EXPERT + N_EXPERTS
    rk, cnt = _moe_rank(route)
    counts = cnt[0, :N_EXPERTS].astype(i32)
    padded = (counts + TM_EXPERT - 1) // TM_EXPERT * TM_EXPERT
    ends = jnp.cumsum(padded)
    starts = ends - padded
    n_used = (ends[-1] // TM_EXPERT).astype(i32)
    tile_id = jnp.arange(n_tiles, dtype=i32)
    tile_expert = jnp.minimum(jnp.sum((tile_id[:, None] * TM_EXPERT >= ends[None, :]).astype(i32), axis=1),
                              N_EXPERTS - 1)
    tile_expert = jnp.where(tile_id < n_used, tile_expert, tile_expert[jnp.maximum(n_used - 1, 0)])
    experts = jnp.arange(N_EXPERTS, dtype=f32)[None, :]
    start_of = lambda ids: jnp.sum(jnp.where(ids[:, None] == experts, starts[None, :], 0), axis=1)
    pos1 = start_of(route[:, ROUTE_I1]) + rk[:, 0].astype(i32)
    pos2 = start_of(route[:, ROUTE_I2]) + rk[:, 1].astype(i32)
    pos = jnp.stack([pos1, pos2], axis=1).reshape(2 * t).astype(i32)

    meta = jnp.concatenate([n_used.reshape(1), starts + counts, ends]).astype(i32)
    ys = _moe_ffn(pos, tile_expert, meta, hf, w1, w3, w2, n_tiles * TM_EXPERT)
    return _moe_combine(pos, ys, x2, route)


def _rope_tables(seq, rot_dim, period, active_lanes):
    half = rot_dim // 2
    inv = jnp.power(ROPE_THETA, -jnp.arange(0, rot_dim, 2, dtype=f32) / rot_dim)
    ang = jnp.arange(seq, dtype=f32)[:, None] * inv[None, :]
    cos, sin = jnp.cos(ang), jnp.sin(ang)
    ones = jnp.ones((seq, period - rot_dim), f32)
    zeros = lambda n: jnp.zeros((seq, n), f32)
    cos_p = jnp.concatenate([cos, cos, ones], axis=1)
    sin_a = jnp.concatenate([-sin, zeros(period - half)], axis=1)
    sin_b = jnp.concatenate([zeros(half), sin, zeros(period - rot_dim)], axis=1)
    reps = LANES // period
    tabs = [jnp.tile(tb, (1, reps)) for tb in (cos_p, sin_a, sin_b)]
    live = (jnp.arange(LANES) < active_lanes)[None, :]
    tabs = [jnp.where(live, tabs[0], 1.0), jnp.where(live, tabs[1], 0.0), jnp.where(live, tabs[2], 0.0)]
    return jnp.stack(tabs, axis=0)


def _relayout_w_in(w):
    d = w.shape[0]
    o = 0
    segs = {}
    for name, n in (("q", 512), ("k", 512), ("v", 512), ("qi", 256), ("ki", 32), ("wi", 8),
                    ("u", 512), ("ga", 1024), ("gp", 1024)):
        segs[name] = w[:, o:o + n]
        o += n
    qi = segs["qi"].reshape(d, IDX_HEADS, IDX_DIM)
    qcat = jnp.concatenate([qi, qi, qi, jnp.zeros_like(qi)], axis=-1).reshape(d, IDX_HEADS * LANES)
    kwc = jnp.concatenate([segs["ki"], segs["ki"], segs["ki"], segs["wi"],
                           jnp.zeros((d, LANES - 3 * IDX_DIM - IDX_HEADS), w.dtype)], axis=1)
    vh = segs["v"].reshape(d, N_HEADS // 2, 2, HEAD_DIM)
    zh = jnp.zeros_like(vh[:, :, 0])
    vx = jnp.stack([jnp.concatenate([vh[:, :, 0], zh], axis=-1), jnp.concatenate([zh, vh[:, :, 1]], axis=-1)],
                   axis=2).reshape(d, N_HEADS * LANES)
    out = jnp.concatenate([segs["q"], segs["k"], vx, qcat, kwc, segs["u"], segs["ga"], segs["gp"]], axis=1)
    return out.astype(bf16)


def kernel(x, mix_norm, w_in, q_norm, k_norm, w_pool, pool_scale, w_attn_proj, w_pool_proj, w_out, ffn_norm,
           dense_w1, dense_w3, dense_w2, moe_router, moe_w1, moe_w3, moe_w2):
    batch, seq, d = x.shape
    depth = w_in.shape[0]
    t = batch * seq
    assert d == D_MODEL and seq % TK == 0 and seq % TM_PROJ == 0 and seq % TM_MERGE == 0

    rqk = _rope_tables(seq, ROPE_DIM, HEAD_DIM, LANES)
    rqi = _rope_tables(seq, IDX_ROPE_DIM, IDX_DIM, LANES)
    rki = _rope_tables(seq, IDX_ROPE_DIM, IDX_DIM, 3 * IDX_DIM)
    head_of = jnp.arange(ATTN_WIDTH) // HEAD_DIM
    bd = jnp.where(head_of[:, None] == head_of[None, :], 1.0 / HEAD_DIM, 0.0).astype(bf16)
    lane_in_tile = jnp.arange(N_HEADS * LANES) % LANES
    odd_head = (jnp.arange(N_HEADS * LANES) // LANES) % 2 == 1
    ones_lane = (lane_in_tile == jnp.where(odd_head, 0, HEAD_DIM)).astype(f32)[None, :]

    xc = x.reshape(t, d)
    for layer in range(depth):
        w = _relayout_w_in(w_in[layer])
        qn = jnp.tile(q_norm[layer], N_HEADS)[None, :]
        kn = jnp.tile(k_norm[layer], N_HEADS)[None, :]
        q, k, vx, qc, kw, u, ga, gp = _in_proj(xc, mix_norm[layer][None, :], w, qn, kn, bd, ones_lane,
                                               rqk, rqi, rki, seq)
        a = _dsa_attention(q, qc, kw, k, vx, batch, seq)
        moe = layer % 2 == 1
        common = (xc, a, u, ga, gp, w_pool[layer].astype(bf16), pool_scale[layer][None, :],
                  w_attn_proj[layer].astype(bf16), w_pool_proj[layer].astype(bf16), w_out[layer].astype(bf16),
                  ffn_norm[layer][None, :])
        idx = layer // 2
        if moe:
            wr = jnp.pad(moe_router[idx], ((0, 0), (0, LANES - N_EXPERTS)))
            wrh = wr.astype(bf16)
            wrl = (wr - wrh.astype(f32)).astype(bf16)
            x2, hf, route = _merge(True, seq, *common, wrh, wrl)
            xc = _moe_layer(route, hf, x2, moe_w1[idx].astype(bf16), moe_w3[idx].astype(bf16),
                            moe_w2[idx].astype(bf16))
        else:
            x2, h2 = _merge(False, seq, *common)
            xc = _ffn(h2, x2, dense_w1[idx].astype(bf16), dense_w3[idx].astype(bf16),
                      dense_w2[idx].astype(bf16), tm=1024, tf=1408)
    return xc.reshape(batch, seq, d)
```

```python
import functools

import jax
import jax.numpy as jnp
from jax import lax
from jax.experimental import pallas as pl
from jax.experimental.pallas import tpu as pltpu

bf16 = jnp.bfloat16
f32 = jnp.float32
i32 = jnp.int32

D_MODEL = 1024
N_HEADS = 8
HEAD_DIM = 64
ATTN_WIDTH = 512
ROPE_DIM = 16
ROPE_THETA = 500000.0
IDX_HEADS = 8
IDX_DIM = 32
IDX_ROPE_DIM = 8
TOPK_MAX = 256
POOL_GROUPS = 4
POOL_GROUP_DIM = 128
POOL_WIDTH = 512
POOL_WINDOWS = (2, 4, 8, 16)
POOL_HALO = 16
N_EXPERTS = 8
EPS = 1e-6

LANES = 128
INT_MIN = -(2 ** 31)
VMEM_LIMIT = 56 * 1024 * 1024
LOG2E = 1.4426950408889634
SOFTMAX_DENOM_FLOOR = 2.0 ** -90

OFF_Q = 0
OFF_K = 512
OFF_VX = 1024
OFF_QCAT = 2048
OFF_KW = 3072
OFF_U = 3200
OFF_GA = 3712
OFF_GP = 4736
W_COLS = 5760

ROUTE_W1, ROUTE_W2, ROUTE_I1, ROUTE_I2 = 8, 9, 10, 11

TM_PROJ = 512
TM_MERGE = 512
TM_RANK = 512
TM_EXPERT = 512
TF_EXPERT = 896
TM_ROWS = 256
ROW_UNROLL = 8
QB = 128
TK = 512


def _sigmoid(x):
    return 1.0 / (1.0 + jnp.exp(-x))


def _in_proj_kernel(x_ref, g_ref, w_ref, qn_ref, kn_ref, bd_ref, one_ref, rqk_ref, rqi_ref, rki_ref,
                    q_ref, k_ref, vx_ref, qc_ref, kw_ref, u_ref, ga_ref, gp_ref):
    x = x_ref[...]
    ms = jnp.mean(x * x, axis=-1, keepdims=True)
    h = (x * lax.rsqrt(ms + EPS) * g_ref[...]).astype(bf16)

    def proj(lo, n):
        return jnp.dot(h, w_ref[:, lo:lo + n], preferred_element_type=f32)

    def rope(xc, tab_ref, sh):
        return (xc * tab_ref[0] + pltpu.roll(xc, LANES - sh, 1) * tab_ref[1]
                + pltpu.roll(xc, sh, 1) * tab_ref[2])

    def headnorm(z, gain_ref):
        msh = jnp.dot((z * z).astype(bf16), bd_ref[...], preferred_element_type=f32)
        return z * lax.rsqrt(msh + EPS) * gain_ref[...]

    zq = headnorm(proj(OFF_Q, ATTN_WIDTH), qn_ref) * (HEAD_DIM ** -0.5 * LOG2E)
    zk = headnorm(proj(OFF_K, ATTN_WIDTH), kn_ref)
    for c in range(ATTN_WIDTH // LANES):
        sl = slice(c * LANES, (c + 1) * LANES)
        q_ref[:, sl] = rope(zq[:, sl], rqk_ref, ROPE_DIM // 2).astype(bf16)
        k_ref[:, sl] = rope(zk[:, sl], rqk_ref, ROPE_DIM // 2).astype(bf16)
    vx_ref[...] = jnp.where(one_ref[...] > 0.0, 1.0, proj(OFF_VX, N_HEADS * LANES)).astype(bf16)

    lane = lax.broadcasted_iota(i32, (1, LANES), 1)
    zc = proj(OFF_QCAT, IDX_HEADS * LANES)
    for c in range(IDX_HEADS):
        sl = slice(c * LANES, (c + 1) * LANES)
        r = rope(zc[:, sl], rqi_ref, IDX_ROPE_DIM // 2)
        lo = r - r.astype(bf16).astype(f32)
        qc_ref[:, sl] = jnp.where((lane >= 32) & (lane < 64), lo, r).astype(bf16)

    zkw = rope(proj(OFF_KW, LANES), rki_ref, IDX_ROPE_DIM // 2)
    lo = zkw - zkw.astype(bf16).astype(f32)
    zkw = jnp.where((lane >= 64) & (lane < 96), lo, zkw)
    kw_ref[...] = jnp.where(lane >= 96, zkw * ((IDX_HEADS * IDX_DIM) ** -0.5), zkw)

    u_ref[...] = proj(OFF_U, POOL_WIDTH)
    ga_ref[...] = proj(OFF_GA, D_MODEL)
    gp_ref[...] = proj(OFF_GP, D_MODEL)


def _in_proj(x2d, g, w, qn, kn, bd, ones_lane, rqk, rqi, rki, seq):
    t = x2d.shape[0]
    tm = TM_PROJ
    ns = seq // tm
    row = lambda i: (i, 0)
    const = lambda i: (0, 0)
    tab = lambda i: (0, i % ns, 0)
    out_shapes = (
        jax.ShapeDtypeStruct((t, ATTN_WIDTH), bf16),
        jax.ShapeDtypeStruct((t, ATTN_WIDTH), bf16),
        jax.ShapeDtypeStruct((t, N_HEADS * LANES), bf16),
        jax.ShapeDtypeStruct((t, IDX_HEADS * LANES), bf16),
        jax.ShapeDtypeStruct((t, LANES), f32),
        jax.ShapeDtypeStruct((t, POOL_WIDTH), f32),
        jax.ShapeDtypeStruct((t, D_MODEL), f32),
        jax.ShapeDtypeStruct((t, D_MODEL), f32),
    )
    return pl.pallas_call(
        _in_proj_kernel,
        grid=(t // tm,),
        in_specs=[
            pl.BlockSpec((tm, D_MODEL), row),
            pl.BlockSpec((1, D_MODEL), const),
            pl.BlockSpec((D_MODEL, W_COLS), const, pipeline_mode=pl.Buffered(1)),
            pl.BlockSpec((1, ATTN_WIDTH), const),
            pl.BlockSpec((1, ATTN_WIDTH), const),
            pl.BlockSpec((ATTN_WIDTH, ATTN_WIDTH), const),
            pl.BlockSpec((1, N_HEADS * LANES), const),
            pl.BlockSpec((3, tm, LANES), tab),
            pl.BlockSpec((3, tm, LANES), tab),
            pl.BlockSpec((3, tm, LANES), tab),
        ],
        out_specs=[pl.BlockSpec((tm, s.shape[1]), row) for s in out_shapes],
        out_shape=out_shapes,
        compiler_params=pltpu.CompilerParams(
            dimension_semantics=("arbitrary",), vmem_limit_bytes=VMEM_LIMIT),
        name="in_proj",
    )(x2d, g, w, qn, kn, bd, ones_lane, rqk, rqi, rki)


def _lane_fold(x, op):
    out = x[:, 0:LANES]
    for j in range(1, x.shape[1] // LANES):
        out = op(out, x[:, j * LANES:(j + 1) * LANES])
    return out


def _dsa_kernel(topk, q_ref, qc_ref, kwq_ref, k_ref, vx_ref, kw_ref, o_ref,
                key_sc, kcat_sc, qm2_sc, mrun_sc, m_sc, acc_sc, kmax_sm):
    i = pl.program_id(1)
    nk = (i * QB + QB + TK - 1) // TK
    kf = float(topk)

    lane = lax.broadcasted_iota(i32, (1, LANES), 1)

    @pl.when(i == 0)
    def _():
        kcat_sc[...] = kw_ref[...].astype(bf16)
        for pr in range(N_HEADS // 2):
            def norms(c, carry):
                kk = k_ref[pl.ds(pl.multiple_of(c * TK, TK), TK), pr * LANES:(pr + 1) * LANES].astype(f32)
                sq = kk * kk
                n0 = jnp.sqrt(jnp.sum(jnp.where(lane < HEAD_DIM, sq, 0.0), axis=1, keepdims=True))
                n1 = jnp.sqrt(jnp.sum(jnp.where(lane >= HEAD_DIM, sq, 0.0), axis=1, keepdims=True))
                return jnp.maximum(carry[0], jnp.max(n0)), jnp.maximum(carry[1], jnp.max(n1))
            k0, k1 = lax.fori_loop(0, k_ref.shape[0] // TK, norms, (jnp.float32(0.0), jnp.float32(0.0)))
            kmax_sm[2 * pr] = k0
            kmax_sm[2 * pr + 1] = k1

    qpos = i * QB + lax.broadcasted_iota(i32, (QB, 1), 0)
    lane_tk = lax.broadcasted_iota(i32, (1, TK), 1)
    wq = kwq_ref[...]

    def to_key(score):
        bits = pltpu.bitcast(score, i32)
        return bits ^ ((bits >> 31) & 0x7FFFFFFF)

    def score_chunk(c, carry):
        koff = pl.multiple_of(c * TK, TK)
        kc = kcat_sc[pl.ds(koff, TK), :]
        acc = jnp.zeros((QB, TK), f32)
        for h in range(IDX_HEADS):
            d = lax.dot_general(qc_ref[:, h * LANES:(h + 1) * LANES], kc,
                                (((1,), (1,)), ((), ())), preferred_element_type=f32)
            acc = acc + jnp.maximum(d, 0.0) * wq[:, 96 + h:97 + h]
        causal = (koff + lane_tk) <= qpos
        key_sc[c] = jnp.where(causal, to_key(acc), INT_MIN)
        return carry

    lax.fori_loop(0, nk, score_chunk, 0)

    def count_ge(cand):
        def body(c, part):
            return part + _lane_fold(jnp.where(key_sc[c] >= cand, 1.0, 0.0), jnp.add)
        part = lax.fori_loop(0, nk, body, jnp.zeros((QB, LANES), f32))
        return jnp.sum(part, axis=1, keepdims=True)

    zero = jnp.zeros((QB, 1), i32)
    prefix = jnp.where(count_ge(zero) >= kf, zero, INT_MIN)

    def bit_body(b, prefix):
        cand = prefix | jnp.left_shift(1, 30 - b)
        return jnp.where(count_ge(cand) >= kf, cand, prefix)

    thr_raw = lax.fori_loop(0, 31, bit_body, prefix)
    thr = jnp.maximum(thr_raw, INT_MIN + 1)

    is_tie = (count_ge(thr) > kf) & (thr_raw > INT_MIN)

    @pl.when(jnp.max(jnp.where(is_tie, 1.0, 0.0)) > 0.0)
    def _():
        need = kf - count_ge(thr + 1)

        def count_eq_below(m):
            def body(c, part):
                hit = (key_sc[c] == thr) & ((c * TK + lane_tk) < m)
                return part + _lane_fold(jnp.where(hit, 1.0, 0.0), jnp.add)
            part = lax.fori_loop(0, nk, body, jnp.zeros((QB, LANES), f32))
            return jnp.sum(part, axis=1, keepdims=True)

        mprime = jnp.zeros((QB, 1), i32)
        bit = key_sc.shape[0] * TK // 2
        while bit >= 1:
            cand = mprime | bit
            mprime = jnp.where(count_eq_below(cand) < need, cand, mprime)
            bit //= 2

        def drop_ties(c, carry):
            key = key_sc[c]
            drop = is_tie & (key == thr) & ((c * TK + lane_tk) > mprime)
            key_sc[c] = jnp.where(drop, thr - 1, key)
            return carry

        lax.fori_loop(0, nk, drop_ties, 0)

    row2 = lax.broadcasted_iota(i32, (2 * QB, 1), 0)
    for pr in range(N_HEADS // 2):
        qp = q_ref[:, pr * LANES:(pr + 1) * LANES]
        qm2_sc[pr, 0:QB, :] = jnp.where(lane < HEAD_DIM, qp, jnp.zeros((), bf16))
        qm2_sc[pr, QB:2 * QB, :] = jnp.where(lane >= HEAD_DIM, qp, jnp.zeros((), bf16))
        q2 = qm2_sc[pr].astype(f32)
        qn = jnp.sqrt(jnp.sum(q2 * q2, axis=1, keepdims=True))
        m_sc[pr] = qn * jnp.where(row2 < QB, kmax_sm[2 * pr], kmax_sm[2 * pr + 1])
    acc_sc[...] = jnp.zeros(acc_sc.shape, f32)

    def pair_logits(c, pr, bias):
        koff = pl.multiple_of(c * TK, TK)
        kc = k_ref[pl.ds(koff, TK), pr * LANES:(pr + 1) * LANES]
        s2 = lax.dot_general(qm2_sc[pr], kc, (((1,), (1,)), ((), ())), preferred_element_type=f32)
        return s2[0:QB] + bias, s2[QB:2 * QB] + bias

    def max_chunk(c, carry):
        bias = jnp.where(key_sc[c] >= thr, 0.0, -jnp.inf)
        for pr in range(N_HEADS // 2):
            sa, sb = pair_logits(c, pr, bias)
            mrun_sc[pr, 0:QB, :] = jnp.maximum(mrun_sc[pr, 0:QB, :], _lane_fold(sa, jnp.maximum))
            mrun_sc[pr, QB:2 * QB, :] = jnp.maximum(mrun_sc[pr, QB:2 * QB, :], _lane_fold(sb, jnp.maximum))
        return carry

    def acc_chunk(c, carry):
        koff = pl.multiple_of(c * TK, TK)
        bias = jnp.where(key_sc[c] >= thr, 0.0, -jnp.inf)
        for pr in range(N_HEADS // 2):
            sa, sb = pair_logits(c, pr, bias)
            m = m_sc[pr]
            pa = jnp.exp2(sa - m[0:QB]).astype(bf16)
            pb = jnp.exp2(sb - m[QB:2 * QB]).astype(bf16)
            va = vx_ref[pl.ds(koff, TK), (2 * pr) * LANES:(2 * pr + 1) * LANES]
            vb = vx_ref[pl.ds(koff, TK), (2 * pr + 1) * LANES:(2 * pr + 2) * LANES]
            acc_sc[2 * pr] += jnp.dot(pa, va, preferred_element_type=f32)
            acc_sc[2 * pr + 1] += jnp.dot(pb, vb, preferred_element_type=f32)
        return carry

    lax.fori_loop(0, nk, acc_chunk, 0)

    lmin = jnp.float32(jnp.inf)
    for pr in range(N_HEADS // 2):
        lmin = jnp.minimum(lmin, jnp.min(acc_sc[2 * pr][:, HEAD_DIM:HEAD_DIM + 1]))
        lmin = jnp.minimum(lmin, jnp.min(acc_sc[2 * pr + 1][:, 0:1]))

    @pl.when(jnp.logical_not(lmin >= SOFTMAX_DENOM_FLOOR))
    def _():
        mrun_sc[...] = jnp.full(mrun_sc.shape, -jnp.inf, f32)
        lax.fori_loop(0, nk, max_chunk, 0)
        for pr in range(N_HEADS // 2):
            m = jnp.max(mrun_sc[pr], axis=1, keepdims=True)
            m_sc[pr] = jnp.where(m == -jnp.inf, 0.0, m)
        acc_sc[...] = jnp.zeros(acc_sc.shape, f32)
        lax.fori_loop(0, nk, acc_chunk, 0)

    for pr in range(N_HEADS // 2):
        a0 = acc_sc[2 * pr]
        a1 = acc_sc[2 * pr + 1]
        o0 = a0 / a0[:, HEAD_DIM:HEAD_DIM + 1]
        o1 = a1 / a1[:, 0:1]
        o_ref[:, pr * LANES:(pr + 1) * LANES] = jnp.where(lane < HEAD_DIM, o0, o1).astype(bf16)


def _dsa_attention(q, qc, kw, k, vx, batch, seq):
    t = q.shape[0]
    nq = seq // QB
    topk = min(TOPK_MAX, seq // 4)
    qrow = lambda b, i: (b * nq + i, 0)
    per_batch = lambda b, i: (b, 0)
    return pl.pallas_call(
        functools.partial(_dsa_kernel, topk),
        grid=(batch, nq),
        in_specs=[
            pl.BlockSpec((QB, ATTN_WIDTH), qrow),
            pl.BlockSpec((QB, IDX_HEADS * LANES), qrow),
            pl.BlockSpec((QB, LANES), qrow),
            pl.BlockSpec((seq, ATTN_WIDTH), per_batch, pipeline_mode=pl.Buffered(1)),
            pl.BlockSpec((seq, N_HEADS * LANES), per_batch, pipeline_mode=pl.Buffered(1)),
            pl.BlockSpec((seq, LANES), per_batch, pipeline_mode=pl.Buffered(1)),
        ],
        out_specs=pl.BlockSpec((QB, ATTN_WIDTH), qrow),
        out_shape=jax.ShapeDtypeStruct((t, ATTN_WIDTH), bf16),
        scratch_shapes=[
            pltpu.VMEM((seq // TK, QB, TK), i32),
            pltpu.VMEM((seq, LANES), bf16),
            pltpu.VMEM((N_HEADS // 2, 2 * QB, LANES), bf16),
            pltpu.VMEM((N_HEADS // 2, 2 * QB, LANES), f32),
            pltpu.VMEM((N_HEADS // 2, 2 * QB, 1), f32),
            pltpu.VMEM((N_HEADS, QB, LANES), f32),
            pltpu.SMEM((N_HEADS,), f32),
        ],
        compiler_params=pltpu.CompilerParams(
            dimension_semantics=("arbitrary", "arbitrary"), vmem_limit_bytes=VMEM_LIMIT),
        name="dsa_attention",
    )(q, qc, kw, k, vx, kw)


def _merge_kernel(moe, seq, *refs):
    if moe:
        (x_ref, a_ref, u_ref, uh_ref, ga_ref, gp_ref, wp_ref, ps_ref, pa_ref, pb_ref, wo_ref, g2_ref,
         wrh_ref, wrl_ref, x2_ref, h2_ref, gate_ref, e_sc) = refs
    else:
        (x_ref, a_ref, u_ref, uh_ref, ga_ref, gp_ref, wp_ref, ps_ref, pa_ref, pb_ref, wo_ref, g2_ref,
         x2_ref, h2_ref, e_sc) = refs
    tm = TM_MERGE
    i = pl.program_id(0)
    ti = i % (seq // tm)
    e_sc[0:POOL_HALO, :] = jnp.where(ti == 0, 0.0, uh_ref[...])
    e_sc[POOL_HALO:POOL_HALO + tm, :] = u_ref[...]
    npos = (ti * tm + 1 + lax.broadcasted_iota(i32, (tm, 1), 0)).astype(f32)

    parts = []
    for g, w in enumerate(POOL_WINDOWS):
        sl = slice(g * POOL_GROUP_DIM, (g + 1) * POOL_GROUP_DIM)
        tot = e_sc[POOL_HALO:POOL_HALO + tm, sl]
        for j in range(1, w):
            tot = tot + e_sc[POOL_HALO - j:POOL_HALO - j + tm, sl]
        diff = tot / jnp.minimum(npos, float(w)) - u_ref[:, sl]
        parts.append(jnp.dot(diff.astype(bf16), wp_ref[g], preferred_element_type=f32))
    p = jnp.concatenate(parts, axis=1) * ps_ref[...]

    ab = jnp.dot(a_ref[...], pa_ref[...], preferred_element_type=f32)
    pb = jnp.dot(p.astype(bf16), pb_ref[...], preferred_element_type=f32)
    merged = _sigmoid(ga_ref[...]) * ab + _sigmoid(gp_ref[...]) * pb
    x2 = x_ref[...] + jnp.dot(merged.astype(bf16), wo_ref[...], preferred_element_type=f32)
    x2_ref[...] = x2
    ms = jnp.mean(x2 * x2, axis=-1, keepdims=True)
    hf = x2 * lax.rsqrt(ms + EPS) * g2_ref[...]
    h2_ref[...] = hf.astype(h2_ref.dtype)

    if moe:
        hi = hf.astype(bf16)
        lo = (hf - hi.astype(f32)).astype(bf16)
        logits = (jnp.dot(hi, wrh_ref[...], preferred_element_type=f32)
                  + jnp.dot(lo, wrh_ref[...], preferred_element_type=f32)
                  + jnp.dot(hi, wrl_ref[...], preferred_element_type=f32))
        lanef = lax.broadcasted_iota(i32, (1, LANES), 1).astype(f32)
        lg = jnp.where(lanef < N_EXPERTS, logits, -jnp.inf)
        v1 = jnp.max(lg, axis=1, keepdims=True)
        i1 = jnp.min(jnp.where(lg == v1, lanef, float(LANES)), axis=1, keepdims=True)
        lg2 = jnp.where(lanef == i1, -jnp.inf, lg)
        v2 = jnp.max(lg2, axis=1, keepdims=True)
        i2 = jnp.min(jnp.where(lg2 == v2, lanef, float(LANES)), axis=1, keepdims=True)
        tt = jnp.exp(v2 - v1)
        w1 = 1.0 / (1.0 + tt)
        w2 = tt / (1.0 + tt)
        route = jnp.where((lanef == i1) | (lanef == i2), 1.0, 0.0)
        for ln, val in ((ROUTE_W1, w1), (ROUTE_W2, w2), (ROUTE_I1, i1), (ROUTE_I2, i2)):
            route = jnp.where(lanef == float(ln), val, route)
        gate_ref[...] = route


def _merge(moe, seq, x2d, a, u, ga, gp, wp, ps, pa, pb, wo, g2, wrh=None, wrl=None):
    t = x2d.shape[0]
    tm = TM_MERGE
    row = lambda i: (i, 0)
    const2 = lambda i: (0, 0)
    const3 = lambda i: (0, 0, 0)
    halo = lambda i: (jnp.maximum(i * (tm // POOL_HALO) - 1, 0), 0)
    in_specs = [
        pl.BlockSpec((tm, D_MODEL), row),
        pl.BlockSpec((tm, ATTN_WIDTH), row),
        pl.BlockSpec((tm, POOL_WIDTH), row),
        pl.BlockSpec((POOL_HALO, POOL_WIDTH), halo),
        pl.BlockSpec((tm, D_MODEL), row),
        pl.BlockSpec((tm, D_MODEL), row),
        pl.BlockSpec((POOL_GROUPS, POOL_GROUP_DIM, POOL_GROUP_DIM), const3),
        pl.BlockSpec((1, POOL_WIDTH), const2),
        pl.BlockSpec((ATTN_WIDTH, D_MODEL), const2),
        pl.BlockSpec((POOL_WIDTH, D_MODEL), const2),
        pl.BlockSpec((D_MODEL, D_MODEL), const2),
        pl.BlockSpec((1, D_MODEL), const2),
    ]
    args = [x2d, a, u, u, ga, gp, wp, ps, pa, pb, wo, g2]
    out_shapes = [jax.ShapeDtypeStruct((t, D_MODEL), f32), jax.ShapeDtypeStruct((t, D_MODEL), f32 if moe else bf16)]
    out_specs = [pl.BlockSpec((tm, D_MODEL), row), pl.BlockSpec((tm, D_MODEL), row)]
    if moe:
        in_specs += [pl.BlockSpec((D_MODEL, LANES), const2), pl.BlockSpec((D_MODEL, LANES), const2)]
        args += [wrh, wrl]
        out_shapes.append(jax.ShapeDtypeStruct((t, LANES), f32))
        out_specs.append(pl.BlockSpec((tm, LANES), row))
    return pl.pallas_call(
        functools.partial(_merge_kernel, moe, seq),
        grid=(t // tm,),
        in_specs=in_specs,
        out_specs=out_specs,
        out_shape=out_shapes,
        scratch_shapes=[pltpu.VMEM((tm + POOL_HALO, POOL_WIDTH), f32)],
        compiler_params=pltpu.CompilerParams(
            dimension_semantics=("arbitrary",), vmem_limit_bytes=VMEM_LIMIT),
        name="merge_moe" if moe else "merge_dense",
    )(*args)


def _swiglu_partial(h, w1, w3, w2):
    a = jnp.dot(h, w1, preferred_element_type=f32)
    b = jnp.dot(h, w3, preferred_element_type=f32)
    act = (a * _sigmoid(a) * b).astype(bf16)
    return jnp.dot(act, w2, preferred_element_type=f32)


def _ffn_kernel(h_ref, x_ref, w1_ref, w3_ref, w2_ref, o_ref):
    @pl.when(pl.program_id(1) == 0)
    def _():
        o_ref[...] = x_ref[...]

    o_ref[...] += _swiglu_partial(h_ref[...], w1_ref[...], w3_ref[...], w2_ref[...])


def _ffn(h2, x2, w1, w3, w2, tm, tf):
    t = h2.shape[0]
    ff = w1.shape[1]
    row = lambda i, j: (i, 0)
    return pl.pallas_call(
        _ffn_kernel,
        grid=(t // tm, ff // tf),
        in_specs=[
            pl.BlockSpec((tm, D_MODEL), row),
            pl.BlockSpec((tm, D_MODEL), row),
            pl.BlockSpec((D_MODEL, tf), lambda i, j: (0, j)),
            pl.BlockSpec((D_MODEL, tf), lambda i, j: (0, j)),
            pl.BlockSpec((tf, D_MODEL), lambda i, j: (j, 0)),
        ],
        out_specs=pl.BlockSpec((tm, D_MODEL), row),
        out_shape=jax.ShapeDtypeStruct((t, D_MODEL), f32),
        compiler_params=pltpu.CompilerParams(
            dimension_semantics=("arbitrary", "arbitrary"), vmem_limit_bytes=VMEM_LIMIT),
        name="ffn_dense",
    )(h2, x2, w1, w3, w2)


def _rank_kernel(route_ref, tri_ref, rk_ref, cnt_ref, carry_sc):
    @pl.when(pl.program_id(0) == 0)
    def _():
        carry_sc[...] = jnp.zeros(carry_sc.shape, f32)

    lane = lax.broadcasted_iota(i32, (1, LANES), 1)
    lanef = lane.astype(f32)
    r = route_ref[...]
    sel = jnp.where(lane < N_EXPERTS, r, 0.0)
    ranks = jnp.dot(tri_ref[...], sel.astype(bf16), preferred_element_type=f32) + carry_sc[...]
    rk1 = jnp.sum(jnp.where(lanef == r[:, ROUTE_I1:ROUTE_I1 + 1], ranks, 0.0), axis=1, keepdims=True)
    rk2 = jnp.sum(jnp.where(lanef == r[:, ROUTE_I2:ROUTE_I2 + 1], ranks, 0.0), axis=1, keepdims=True)
    rk_ref[...] = jnp.where(lane == 0, rk1, jnp.where(lane == 1, rk2, 0.0))
    carry_sc[...] += jnp.sum(sel, axis=0, keepdims=True)
    cnt_ref[...] = carry_sc[...]


def _moe_rank(route):
    t = route.shape[0]
    tm = TM_RANK
    idx = jnp.arange(tm)
    tri = (idx[None, :] < idx[:, None]).astype(bf16)
    return pl.pallas_call(
        _rank_kernel,
        grid=(t // tm,),
        in_specs=[pl.BlockSpec((tm, LANES), lambda i: (i, 0)), pl.BlockSpec((tm, tm), lambda i: (0, 0))],
        out_specs=[pl.BlockSpec((tm, LANES), lambda i: (i, 0)), pl.BlockSpec((1, LANES), lambda i: (0, 0))],
        out_shape=[jax.ShapeDtypeStruct((t, LANES), f32), jax.ShapeDtypeStruct((1, LANES), f32)],
        scratch_shapes=[pltpu.VMEM((1, LANES), f32)],
        compiler_params=pltpu.CompilerParams(dimension_semantics=("arbitrary",)),
        name="moe_rank",
    )(route, tri)


def _row_copy(src, dst, sem):
    return pltpu.make_async_copy(src, dst, sem)


def _moe_ffn_kernel(pos_ref, te_ref, meta_ref, h_hbm, w1_ref, w3_ref, w2_ref, ys_ref,
                    inv_sm, gbuf, xb, sem):
    del te_ref
    i = pl.program_id(0)
    j = pl.program_id(1)
    n_used = meta_ref[0]
    n_slots = pos_ref.shape[0]

    def start_gather(tile, slot):
        def issue(g, carry):
            for u in range(ROW_UNROLL):
                r = g * ROW_UNROLL + u
                tok = inv_sm[tile * TM_EXPERT + r]
                _row_copy(h_hbm.at[pl.ds(tok, 1)], gbuf.at[slot, pl.ds(r, 1)],
                          sem.at[slot]).start(priority=u % 2)
            return carry
        lax.fori_loop(0, TM_EXPERT // ROW_UNROLL, issue, 0)

    def wait_gather(slot):
        _row_copy(h_hbm.at[pl.ds(0, TM_EXPERT)], gbuf.at[slot], sem.at[slot]).wait()

    @pl.when((i == 0) & (j == 0))
    def _():
        for e in range(N_EXPERTS):
            def pad(r, carry):
                inv_sm[r] = 0
                return carry
            lax.fori_loop(meta_ref[1 + e], meta_ref[1 + N_EXPERTS + e], pad, 0)

        def scatter(g, carry):
            for u in range(ROW_UNROLL):
                n = g * ROW_UNROLL + u
                inv_sm[pos_ref[n]] = lax.shift_right_logical(n, 1)
            return carry
        lax.fori_loop(0, n_slots // ROW_UNROLL, scatter, 0)

        @pl.when(n_used > 0)
        def _():
            start_gather(0, 0)

    used = i < n_used

    @pl.when(used & (j == 0))
    def _():
        slot = i % 2
        wait_gather(slot)

        @pl.when(i + 1 < n_used)
        def _():
            start_gather(i + 1, 1 - slot)

        xb[...] = gbuf[slot].astype(bf16)

    @pl.when(used)
    def _():
        y = _swiglu_partial(xb[...], w1_ref[0], w3_ref[0], w2_ref[0])

        @pl.when(j == 0)
        def _():
            ys_ref[...] = y

        @pl.when(j > 0)
        def _():
            ys_ref[...] += y

    @pl.when(jnp.logical_not(used) & (j == 0))
    def _():
        ys_ref[...] = jnp.zeros(ys_ref.shape, f32)


def _moe_ffn(pos, tile_expert, meta, hf, w1, w3, w2, n_rows):
    ff = w1.shape[2]
    nj = ff // TF_EXPERT
    jj = lambda i, j, meta: jnp.where(i < meta[0], j, nj - 1)
    return pl.pallas_call(
        _moe_ffn_kernel,
        grid_spec=pltpu.PrefetchScalarGridSpec(
            num_scalar_prefetch=3,
            grid=(n_rows // TM_EXPERT, nj),
            in_specs=[
                pl.BlockSpec(memory_space=pl.ANY),
                pl.BlockSpec((1, D_MODEL, TF_EXPERT), lambda i, j, pos, te, meta: (te[i], 0, jj(i, j, meta))),
                pl.BlockSpec((1, D_MODEL, TF_EXPERT), lambda i, j, pos, te, meta: (te[i], 0, jj(i, j, meta))),
                pl.BlockSpec((1, TF_EXPERT, D_MODEL), lambda i, j, pos, te, meta: (te[i], jj(i, j, meta), 0)),
            ],
            out_specs=pl.BlockSpec((TM_EXPERT, D_MODEL), lambda i, j, pos, te, meta: (i, 0)),
            scratch_shapes=[
                pltpu.SMEM((n_rows,), i32),
                pltpu.VMEM((2, TM_EXPERT, D_MODEL), f32),
                pltpu.VMEM((TM_EXPERT, D_MODEL), bf16),
                pltpu.SemaphoreType.DMA((2,)),
            ],
        ),
        out_shape=jax.ShapeDtypeStruct((n_rows, D_MODEL), f32),
        compiler_params=pltpu.CompilerParams(
            dimension_semantics=("arbitrary", "arbitrary"), vmem_limit_bytes=VMEM_LIMIT),
        name="moe_ffn",
    )(pos, tile_expert, meta, hf, w1, w3, w2)


def _combine_kernel(pos_ref, ys_hbm, x2_ref, route_ref, o_ref, buf, sem):
    base = pl.program_id(0) * TM_ROWS

    def issue(g, carry):
        for u in range(ROW_UNROLL // 2):
            r = g * (ROW_UNROLL // 2) + u
            for s in range(2):
                _row_copy(ys_hbm.at[pl.ds(pos_ref[2 * (base + r) + s], 1)], buf.at[s, pl.ds(r, 1)],
                          sem).start(priority=s)
        return carry

    lax.fori_loop(0, TM_ROWS // (ROW_UNROLL // 2), issue, 0)
    for s in range(2):
        _row_copy(ys_hbm.at[pl.ds(0, TM_ROWS)], buf.at[s], sem).wait()
    route = route_ref[...]
    o_ref[...] = (x2_ref[...] + route[:, ROUTE_W1:ROUTE_W1 + 1] * buf[0]
                  + route[:, ROUTE_W2:ROUTE_W2 + 1] * buf[1])


def _moe_combine(pos, ys, x2, route):
    t = x2.shape[0]
    row = lambda i, pos: (i, 0)
    return pl.pallas_call(
        _combine_kernel,
        grid_spec=pltpu.PrefetchScalarGridSpec(
            num_scalar_prefetch=1,
            grid=(t // TM_ROWS,),
            in_specs=[
                pl.BlockSpec(memory_space=pl.ANY),
                pl.BlockSpec((TM_ROWS, D_MODEL), row),
                pl.BlockSpec((TM_ROWS, LANES), row),
            ],
            out_specs=pl.BlockSpec((TM_ROWS, D_MODEL), row),
            scratch_shapes=[pltpu.VMEM((2, TM_ROWS, D_MODEL), f32), pltpu.SemaphoreType.DMA(())],
        ),
        out_shape=jax.ShapeDtypeStruct((t, D_MODEL), f32),
        compiler_params=pltpu.CompilerParams(dimension_semantics=("arbitrary",)),
        name="moe_combine",
    )(pos, ys, x2, route)


def _moe_layer(route, hf, x2, w1, w3, w2):
    t = hf.shape[0]
    n_tiles = 2 * t // TM_EXPERT + N_EXPERTS
    rk, cnt = _moe_rank(route)
    counts = cnt[0, :N_EXPERTS].astype(i32)
    padded = (counts + TM_EXPERT - 1) // TM_EXPERT * TM_EXPERT
    ends = jnp.cumsum(padded)
    starts = ends - padded
    n_used = (ends[-1] // TM_EXPERT).astype(i32)
    tile_id = jnp.arange(n_tiles, dtype=i32)
    tile_expert = jnp.minimum(jnp.sum((tile_id[:, None] * TM_EXPERT >= ends[None, :]).astype(i32), axis=1),
                              N_EXPERTS - 1)
    tile_expert = jnp.where(tile_id < n_used, tile_expert, tile_expert[jnp.maximum(n_used - 1, 0)])
    experts = jnp.arange(N_EXPERTS, dtype=f32)[None, :]
    start_of = lambda ids: jnp.sum(jnp.where(ids[:, None] == experts, starts[None, :], 0), axis=1)
    pos1 = start_of(route[:, ROUTE_I1]) + rk[:, 0].astype(i32)
    pos2 = start_of(route[:, ROUTE_I2]) + rk[:, 1].astype(i32)
    pos = jnp.stack([pos1, pos2], axis=1).reshape(2 * t).astype(i32)

    meta = jnp.concatenate([n_used.reshape(1), starts + counts, ends]).astype(i32)
    ys = _moe_ffn(pos, tile_expert, meta, hf, w1, w3, w2, n_tiles * TM_EXPERT)
    return _moe_combine(pos, ys, x2, route)


def _rope_tables(seq, rot_dim, period, active_lanes):
    half = rot_dim // 2
    inv = jnp.power(ROPE_THETA, -jnp.arange(0, rot_dim, 2, dtype=f32) / rot_dim)
    ang = jnp.arange(seq, dtype=f32)[:, None] * inv[None, :]
    cos, sin = jnp.cos(ang), jnp.sin(ang)
    ones = jnp.ones((seq, period - rot_dim), f32)
    zeros = lambda n: jnp.zeros((seq, n), f32)
    cos_p = jnp.concatenate([cos, cos, ones], axis=1)
    sin_a = jnp.concatenate([-sin, zeros(period - half)], axis=1)
    sin_b = jnp.concatenate([zeros(half), sin, zeros(period - rot_dim)], axis=1)
    reps = LANES // period
    tabs = [jnp.tile(tb, (1, reps)) for tb in (cos_p, sin_a, sin_b)]
    live = (jnp.arange(LANES) < active_lanes)[None, :]
    tabs = [jnp.where(live, tabs[0], 1.0), jnp.where(live, tabs[1], 0.0), jnp.where(live, tabs[2], 0.0)]
    return jnp.stack(tabs, axis=0)


def _relayout_w_in(w):
    d = w.shape[0]
    o = 0
    segs = {}
    for name, n in (("q", 512), ("k", 512), ("v", 512), ("qi", 256), ("ki", 32), ("wi", 8),
                    ("u", 512), ("ga", 1024), ("gp", 1024)):
        segs[name] = w[:, o:o + n]
        o += n
    qi = segs["qi"].reshape(d, IDX_HEADS, IDX_DIM)
    qcat = jnp.concatenate([qi, qi, qi, jnp.zeros_like(qi)], axis=-1).reshape(d, IDX_HEADS * LANES)
    kwc = jnp.concatenate([segs["ki"], segs["ki"], segs["ki"], segs["wi"],
                           jnp.zeros((d, LANES - 3 * IDX_DIM - IDX_HEADS), w.dtype)], axis=1)
    vh = segs["v"].reshape(d, N_HEADS // 2, 2, HEAD_DIM)
    zh = jnp.zeros_like(vh[:, :, 0])
    vx = jnp.stack([jnp.concatenate([vh[:, :, 0], zh], axis=-1), jnp.concatenate([zh, vh[:, :, 1]], axis=-1)],
                   axis=2).reshape(d, N_HEADS * LANES)
    out = jnp.concatenate([segs["q"], segs["k"], vx, qcat, kwc, segs["u"], segs["ga"], segs["gp"]], axis=1)
    return out.astype(bf16)


def kernel(x, mix_norm, w_in, q_norm, k_norm, w_pool, pool_scale, w_attn_proj, w_pool_proj, w_out, ffn_norm,
           dense_w1, dense_w3, dense_w2, moe_router, moe_w1, moe_w3, moe_w2):
    batch, seq, d = x.shape
    depth = w_in.shape[0]
    t = batch * seq
    assert d == D_MODEL and seq % TK == 0 and seq % TM_PROJ == 0 and seq % TM_MERGE == 0

    rqk = _rope_tables(seq, ROPE_DIM, HEAD_DIM, LANES)
    rqi = _rope_tables(seq, IDX_ROPE_DIM, IDX_DIM, LANES)
    rki = _rope_tables(seq, IDX_ROPE_DIM, IDX_DIM, 3 * IDX_DIM)
    head_of = jnp.arange(ATTN_WIDTH) // HEAD_DIM
    bd = jnp.where(head_of[:, None] == head_of[None, :], 1.0 / HEAD_DIM, 0.0).astype(bf16)
    lane_in_tile = jnp.arange(N_HEADS * LANES) % LANES
    odd_head = (jnp.arange(N_HEADS * LANES) // LANES) % 2 == 1
    ones_lane = (lane_in_tile == jnp.where(odd_head, 0, HEAD_DIM)).astype(f32)[None, :]

    xc = x.reshape(t, d)
    for layer in range(depth):
        w = _relayout_w_in(w_in[layer])
        qn = jnp.tile(q_norm[layer], N_HEADS)[None, :]
        kn = jnp.tile(k_norm[layer], N_HEADS)[None, :]
        q, k, vx, qc, kw, u, ga, gp = _in_proj(xc, mix_norm[layer][None, :], w, qn, kn, bd, ones_lane,
                                               rqk, rqi, rki, seq)
        a = _dsa_attention(q, qc, kw, k, vx, batch, seq)
        moe = layer % 2 == 1
        common = (xc, a, u, ga, gp, w_pool[layer].astype(bf16), pool_scale[layer][None, :],
                  w_attn_proj[layer].astype(bf16), w_pool_proj[layer].astype(bf16), w_out[layer].astype(bf16),
                  ffn_norm[layer][None, :])
        idx = layer // 2
        if moe:
            wr = jnp.pad(moe_router[idx], ((0, 0), (0, LANES - N_EXPERTS)))
            wrh = wr.astype(bf16)
            wrl = (wr - wrh.astype(f32)).astype(bf16)
            x2, hf, route = _merge(True, seq, *common, wrh, wrl)
            xc = _moe_layer(route, hf, x2, moe_w1[idx].astype(bf16), moe_w3[idx].astype(bf16),
                            moe_w2[idx].astype(bf16))
        else:
            x2, h2 = _merge(False, seq, *common)
            xc = _ffn(h2, x2, dense_w1[idx].astype(bf16), dense_w3[idx].astype(bf16),
                      dense_w2[idx].astype(bf16), tm=1024, tf=1408)
    return xc.reshape(batch, seq, d)
```

```python
import functools

import jax
import jax.numpy as jnp
from jax import lax
from jax.experimental import pallas as pl
from jax.experimental.pallas import tpu as pltpu

bf16 = jnp.bfloat16
f32 = jnp.float32
i32 = jnp.int32

D_MODEL = 1024
N_HEADS = 8
HEAD_DIM = 64
ATTN_WIDTH = 512
ROPE_DIM = 16
ROPE_THETA = 500000.0
IDX_HEADS = 8
IDX_DIM = 32
IDX_ROPE_DIM = 8
TOPK_MAX = 256
POOL_GROUPS = 4
POOL_GROUP_DIM = 128
POOL_WIDTH = 512
POOL_WINDOWS = (2, 4, 8, 16)
POOL_HALO = 16
N_EXPERTS = 8
EPS = 1e-6

LANES = 128
INT_MIN = -(2 ** 31)
VMEM_LIMIT = 56 * 1024 * 1024
LOG2E = 1.4426950408889634
SOFTMAX_DENOM_FLOOR = 2.0 ** -90

OFF_Q = 0
OFF_K = 512
OFF_VX = 1024
OFF_QCAT = 2048
OFF_KW = 3072
OFF_U = 3200
OFF_GA = 3712
OFF_GP = 4736
W_COLS = 5760

ROUTE_W1, ROUTE_W2, ROUTE_I1, ROUTE_I2 = 8, 9, 10, 11

TM_PROJ = 512
TM_MERGE = 512
TM_RANK = 512
TM_EXPERT = 512
TF_EXPERT = 896
TM_ROWS = 256
ROW_UNROLL = 8
QB = 256
COUNT_ROWS = 128
TK = 512


def _sigmoid(x):
    return 1.0 / (1.0 + jnp.exp(-x))


def _in_proj_kernel(x_ref, g_ref, w_ref, qn_ref, kn_ref, bd_ref, one_ref, rqk_ref, rqi_ref, rki_ref,
                    q_ref, k_ref, vx_ref, qc_ref, kw_ref, u_ref, ga_ref, gp_ref):
    x = x_ref[...]
    ms = jnp.mean(x * x, axis=-1, keepdims=True)
    h = (x * lax.rsqrt(ms + EPS) * g_ref[...]).astype(bf16)

    def proj(lo, n):
        return jnp.dot(h, w_ref[:, lo:lo + n], preferred_element_type=f32)

    def rope(xc, tab_ref, sh):
        return (xc * tab_ref[0] + pltpu.roll(xc, LANES - sh, 1) * tab_ref[1]
                + pltpu.roll(xc, sh, 1) * tab_ref[2])

    def headnorm(z, gain_ref):
        msh = jnp.dot((z * z).astype(bf16), bd_ref[...], preferred_element_type=f32)
        return z * lax.rsqrt(msh + EPS) * gain_ref[...]

    zq = headnorm(proj(OFF_Q, ATTN_WIDTH), qn_ref) * (HEAD_DIM ** -0.5 * LOG2E)
    zk = headnorm(proj(OFF_K, ATTN_WIDTH), kn_ref)
    for c in range(ATTN_WIDTH // LANES):
        sl = slice(c * LANES, (c + 1) * LANES)
        q_ref[:, sl] = rope(zq[:, sl], rqk_ref, ROPE_DIM // 2).astype(bf16)
        k_ref[:, sl] = rope(zk[:, sl], rqk_ref, ROPE_DIM // 2).astype(bf16)
    vx_ref[...] = jnp.where(one_ref[...] > 0.0, 1.0, proj(OFF_VX, N_HEADS * LANES)).astype(bf16)

    lane = lax.broadcasted_iota(i32, (1, LANES), 1)
    zc = proj(OFF_QCAT, IDX_HEADS * LANES)
    for c in range(IDX_HEADS):
        sl = slice(c * LANES, (c + 1) * LANES)
        r = rope(zc[:, sl], rqi_ref, IDX_ROPE_DIM // 2)
        lo = r - r.astype(bf16).astype(f32)
        qc_ref[:, sl] = jnp.where((lane >= 32) & (lane < 64), lo, r).astype(bf16)

    zkw = rope(proj(OFF_KW, LANES), rki_ref, IDX_ROPE_DIM // 2)
    lo = zkw - zkw.astype(bf16).astype(f32)
    zkw = jnp.where((lane >= 64) & (lane < 96), lo, zkw)
    kw_ref[...] = jnp.where(lane >= 96, zkw * ((IDX_HEADS * IDX_DIM) ** -0.5), zkw)

    u_ref[...] = proj(OFF_U, POOL_WIDTH)
    ga_ref[...] = proj(OFF_GA, D_MODEL)
    gp_ref[...] = proj(OFF_GP, D_MODEL)


def _in_proj(x2d, g, w, qn, kn, bd, ones_lane, rqk, rqi, rki, seq):
    t = x2d.shape[0]
    tm = TM_PROJ
    ns = seq // tm
    row = lambda i: (i, 0)
    const = lambda i: (0, 0)
    tab = lambda i: (0, i % ns, 0)
    out_shapes = (
        jax.ShapeDtypeStruct((t, ATTN_WIDTH), bf16),
        jax.ShapeDtypeStruct((t, ATTN_WIDTH), bf16),
        jax.ShapeDtypeStruct((t, N_HEADS * LANES), bf16),
        jax.ShapeDtypeStruct((t, IDX_HEADS * LANES), bf16),
        jax.ShapeDtypeStruct((t, LANES), f32),
        jax.ShapeDtypeStruct((t, POOL_WIDTH), f32),
        jax.ShapeDtypeStruct((t, D_MODEL), f32),
        jax.ShapeDtypeStruct((t, D_MODEL), f32),
    )
    return pl.pallas_call(
        _in_proj_kernel,
        grid=(t // tm,),
        in_specs=[
            pl.BlockSpec((tm, D_MODEL), row),
            pl.BlockSpec((1, D_MODEL), const),
            pl.BlockSpec((D_MODEL, W_COLS), const, pipeline_mode=pl.Buffered(1)),
            pl.BlockSpec((1, ATTN_WIDTH), const),
            pl.BlockSpec((1, ATTN_WIDTH), const),
            pl.BlockSpec((ATTN_WIDTH, ATTN_WIDTH), const),
            pl.BlockSpec((1, N_HEADS * LANES), const),
            pl.BlockSpec((3, tm, LANES), tab),
            pl.BlockSpec((3, tm, LANES), tab),
            pl.BlockSpec((3, tm, LANES), tab),
        ],
        out_specs=[pl.BlockSpec((tm, s.shape[1]), row) for s in out_shapes],
        out_shape=out_shapes,
        compiler_params=pltpu.CompilerParams(
            dimension_semantics=("arbitrary",), vmem_limit_bytes=VMEM_LIMIT),
        name="in_proj",
    )(x2d, g, w, qn, kn, bd, ones_lane, rqk, rqi, rki)


def _lane_fold(x, op):
    out = x[:, 0:LANES]
    for j in range(1, x.shape[1] // LANES):
        out = op(out, x[:, j * LANES:(j + 1) * LANES])
    return out


def _dsa_kernel(topk, q_ref, qc_ref, kwq_ref, k_ref, vx_ref, kw_ref, o_ref,
                key_sc, kcat_sc, qm2_sc, mrun_sc, m_sc, acc_sc, kmax_sm):
    i = pl.program_id(1)
    nk = (i * QB + QB + TK - 1) // TK
    kf = float(topk)

    lane = lax.broadcasted_iota(i32, (1, LANES), 1)

    @pl.when(i == 0)
    def _():
        kcat_sc[...] = kw_ref[...].astype(bf16)
        for pr in range(N_HEADS // 2):
            def norms(c, carry):
                kk = k_ref[pl.ds(pl.multiple_of(c * TK, TK), TK), pr * LANES:(pr + 1) * LANES].astype(f32)
                sq = kk * kk
                n0 = jnp.sqrt(jnp.sum(jnp.where(lane < HEAD_DIM, sq, 0.0), axis=1, keepdims=True))
                n1 = jnp.sqrt(jnp.sum(jnp.where(lane >= HEAD_DIM, sq, 0.0), axis=1, keepdims=True))
                return jnp.maximum(carry[0], jnp.max(n0)), jnp.maximum(carry[1], jnp.max(n1))
            k0, k1 = lax.fori_loop(0, k_ref.shape[0] // TK, norms, (jnp.float32(0.0), jnp.float32(0.0)))
            kmax_sm[2 * pr] = k0
            kmax_sm[2 * pr + 1] = k1

    qpos = i * QB + lax.broadcasted_iota(i32, (QB, 1), 0)
    lane_tk = lax.broadcasted_iota(i32, (1, TK), 1)
    wq = kwq_ref[...]

    def to_key(score):
        bits = pltpu.bitcast(score, i32)
        return bits ^ ((bits >> 31) & 0x7FFFFFFF)

    def score_chunk(c, carry):
        koff = pl.multiple_of(c * TK, TK)
        kc = kcat_sc[pl.ds(koff, TK), :]
        acc = jnp.zeros((QB, TK), f32)
        for h in range(IDX_HEADS):
            d = lax.dot_general(qc_ref[:, h * LANES:(h + 1) * LANES], kc,
                                (((1,), (1,)), ((), ())), preferred_element_type=f32)
            acc = acc + jnp.maximum(d, 0.0) * wq[:, 96 + h:97 + h]
        causal = (koff + lane_tk) <= qpos
        key_sc[c] = jnp.where(causal, to_key(acc), INT_MIN)
        return carry

    lax.fori_loop(0, nk, score_chunk, 0)

    def count_ge(cand):
        counts = []
        for r0 in range(0, QB, COUNT_ROWS):
            cand_g = cand[r0:r0 + COUNT_ROWS]

            def body(c, part, r0=r0, cand_g=cand_g):
                for j in range(TK // LANES):
                    kk = key_sc[c, r0:r0 + COUNT_ROWS, j * LANES:(j + 1) * LANES]
                    part = part + jnp.where(kk >= cand_g, 1, 0)
                return part

            counts.append(lax.fori_loop(0, nk, body, jnp.zeros((COUNT_ROWS, LANES), i32)))
        return jnp.sum(jnp.concatenate(counts, axis=0).astype(f32), axis=1, keepdims=True)

    zero = jnp.zeros((QB, 1), i32)
    prefix = jnp.where(count_ge(zero) >= kf, zero, INT_MIN)

    def bit_body(b, prefix):
        cand = prefix | jnp.left_shift(1, 30 - b)
        return jnp.where(count_ge(cand) >= kf, cand, prefix)

    thr_raw = lax.fori_loop(0, 31, bit_body, prefix)
    thr = jnp.maximum(thr_raw, INT_MIN + 1)

    is_tie = (count_ge(thr) > kf) & (thr_raw > INT_MIN)

    @pl.when(jnp.max(jnp.where(is_tie, 1.0, 0.0)) > 0.0)
    def _():
        need = kf - count_ge(thr + 1)

        def count_eq_below(m):
            def body(c, part):
                hit = (key_sc[c] == thr) & ((c * TK + lane_tk) < m)
                return part + _lane_fold(jnp.where(hit, 1.0, 0.0), jnp.add)
            part = lax.fori_loop(0, nk, body, jnp.zeros((QB, LANES), f32))
            return jnp.sum(part, axis=1, keepdims=True)

        mprime = jnp.zeros((QB, 1), i32)
        bit = key_sc.shape[0] * TK // 2
        while bit >= 1:
            cand = mprime | bit
            mprime = jnp.where(count_eq_below(cand) < need, cand, mprime)
            bit //= 2

        def drop_ties(c, carry):
            key = key_sc[c]
            drop = is_tie & (key == thr) & ((c * TK + lane_tk) > mprime)
            key_sc[c] = jnp.where(drop, thr - 1, key)
            return carry

        lax.fori_loop(0, nk, drop_ties, 0)

    row2 = lax.broadcasted_iota(i32, (2 * QB, 1), 0)
    for pr in range(N_HEADS // 2):
        qp = q_ref[:, pr * LANES:(pr + 1) * LANES]
        qm2_sc[pr, 0:QB, :] = jnp.where(lane < HEAD_DIM, qp, jnp.zeros((), bf16))
        qm2_sc[pr, QB:2 * QB, :] = jnp.where(lane >= HEAD_DIM, qp, jnp.zeros((), bf16))
        q2 = qm2_sc[pr].astype(f32)
        qn = jnp.sqrt(jnp.sum(q2 * q2, axis=1, keepdims=True))
        m_sc[pr] = qn * jnp.where(row2 < QB, kmax_sm[2 * pr], kmax_sm[2 * pr + 1])
    acc_sc[...] = jnp.zeros(acc_sc.shape, f32)

    def pair_logits(c, pr, bias):
        koff = pl.multiple_of(c * TK, TK)
        kc = k_ref[pl.ds(koff, TK), pr * LANES:(pr + 1) * LANES]
        s2 = lax.dot_general(qm2_sc[pr], kc, (((1,), (1,)), ((), ())), preferred_element_type=f32)
        return s2[0:QB] + bias, s2[QB:2 * QB] + bias

    def max_chunk(c, carry):
        bias = jnp.where(key_sc[c] >= thr, 0.0, -jnp.inf)
        for pr in range(N_HEADS // 2):
            sa, sb = pair_logits(c, pr, bias)
            mrun_sc[pr, 0:QB, :] = jnp.maximum(mrun_sc[pr, 0:QB, :], _lane_fold(sa, jnp.maximum))
            mrun_sc[pr, QB:2 * QB, :] = jnp.maximum(mrun_sc[pr, QB:2 * QB, :], _lane_fold(sb, jnp.maximum))
        return carry

    def acc_chunk(c, carry):
        koff = pl.multiple_of(c * TK, TK)
        bias = jnp.where(key_sc[c] >= thr, 0.0, -jnp.inf)
        for pr in range(N_HEADS // 2):
            sa, sb = pair_logits(c, pr, bias)
            m = m_sc[pr]
            pa = jnp.exp2(sa - m[0:QB]).astype(bf16)
            pb = jnp.exp2(sb - m[QB:2 * QB]).astype(bf16)
            va = vx_ref[pl.ds(koff, TK), (2 * pr) * LANES:(2 * pr + 1) * LANES]
            vb = vx_ref[pl.ds(koff, TK), (2 * pr + 1) * LANES:(2 * pr + 2) * LANES]
            acc_sc[2 * pr] += jnp.dot(pa, va, preferred_element_type=f32)
            acc_sc[2 * pr + 1] += jnp.dot(pb, vb, preferred_element_type=f32)
        return carry

    lax.fori_loop(0, nk, acc_chunk, 0)

    lmin = jnp.float32(jnp.inf)
    for pr in range(N_HEADS // 2):
        lmin = jnp.minimum(lmin, jnp.min(acc_sc[2 * pr][:, HEAD_DIM:HEAD_DIM + 1]))
        lmin = jnp.minimum(lmin, jnp.min(acc_sc[2 * pr + 1][:, 0:1]))

    @pl.when(jnp.logical_not(lmin >= SOFTMAX_DENOM_FLOOR))
    def _():
        mrun_sc[...] = jnp.full(mrun_sc.shape, -jnp.inf, f32)
        lax.fori_loop(0, nk, max_chunk, 0)
        for pr in range(N_HEADS // 2):
            m = jnp.max(mrun_sc[pr], axis=1, keepdims=True)
            m_sc[pr] = jnp.where(m == -jnp.inf, 0.0, m)
        acc_sc[...] = jnp.zeros(acc_sc.shape, f32)
        lax.fori_loop(0, nk, acc_chunk, 0)

    for pr in range(N_HEADS // 2):
        a0 = acc_sc[2 * pr]
        a1 = acc_sc[2 * pr + 1]
        o0 = a0 / a0[:, HEAD_DIM:HEAD_DIM + 1]
        o1 = a1 / a1[:, 0:1]
        o_ref[:, pr * LANES:(pr + 1) * LANES] = jnp.where(lane < HEAD_DIM, o0, o1).astype(bf16)


def _dsa_attention(q, qc, kw, k, vx, batch, seq):
    t = q.shape[0]
    nq = seq // QB
    topk = min(TOPK_MAX, seq // 4)
    qrow = lambda b, i: (b * nq + i, 0)
    per_batch = lambda b, i: (b, 0)
    return pl.pallas_call(
        functools.partial(_dsa_kernel, topk),
        grid=(batch, nq),
        in_specs=[
            pl.BlockSpec((QB, ATTN_WIDTH), qrow),
            pl.BlockSpec((QB, IDX_HEADS * LANES), qrow),
            pl.BlockSpec((QB, LANES), qrow),
            pl.BlockSpec((seq, ATTN_WIDTH), per_batch, pipeline_mode=pl.Buffered(1)),
            pl.BlockSpec((seq, N_HEADS * LANES), per_batch, pipeline_mode=pl.Buffered(1)),
            pl.BlockSpec((seq, LANES), per_batch, pipeline_mode=pl.Buffered(1)),
        ],
        out_specs=pl.BlockSpec((QB, ATTN_WIDTH), qrow),
        out_shape=jax.ShapeDtypeStruct((t, ATTN_WIDTH), bf16),
        scratch_shapes=[
            pltpu.VMEM((seq // TK, QB, TK), i32),
            pltpu.VMEM((seq, LANES), bf16),
            pltpu.VMEM((N_HEADS // 2, 2 * QB, LANES), bf16),
            pltpu.VMEM((N_HEADS // 2, 2 * QB, LANES), f32),
            pltpu.VMEM((N_HEADS // 2, 2 * QB, 1), f32),
            pltpu.VMEM((N_HEADS, QB, LANES), f32),
            pltpu.SMEM((N_HEADS,), f32),
        ],
        compiler_params=pltpu.CompilerParams(
            dimension_semantics=("arbitrary", "arbitrary"), vmem_limit_bytes=VMEM_LIMIT),
        name="dsa_attention",
    )(q, qc, kw, k, vx, kw)


def _merge_kernel(moe, seq, *refs):
    if moe:
        (x_ref, a_ref, u_ref, uh_ref, ga_ref, gp_ref, wp_ref, ps_ref, pa_ref, pb_ref, wo_ref, g2_ref,
         wrh_ref, wrl_ref, x2_ref, h2_ref, gate_ref, e_sc) = refs
    else:
        (x_ref, a_ref, u_ref, uh_ref, ga_ref, gp_ref, wp_ref, ps_ref, pa_ref, pb_ref, wo_ref, g2_ref,
         x2_ref, h2_ref, e_sc) = refs
    tm = TM_MERGE
    i = pl.program_id(0)
    ti = i % (seq // tm)
    e_sc[0:POOL_HALO, :] = jnp.where(ti == 0, 0.0, uh_ref[...])
    e_sc[POOL_HALO:POOL_HALO + tm, :] = u_ref[...]
    npos = (ti * tm + 1 + lax.broadcasted_iota(i32, (tm, 1), 0)).astype(f32)

    parts = []
    for g, w in enumerate(POOL_WINDOWS):
        sl = slice(g * POOL_GROUP_DIM, (g + 1) * POOL_GROUP_DIM)
        tot = e_sc[POOL_HALO:POOL_HALO + tm, sl]
        for j in range(1, w):
            tot = tot + e_sc[POOL_HALO - j:POOL_HALO - j + tm, sl]
        diff = tot / jnp.minimum(npos, float(w)) - u_ref[:, sl]
        parts.append(jnp.dot(diff.astype(bf16), wp_ref[g], preferred_element_type=f32))
    p = jnp.concatenate(parts, axis=1) * ps_ref[...]

    ab = jnp.dot(a_ref[...], pa_ref[...], preferred_element_type=f32)
    pb = jnp.dot(p.astype(bf16), pb_ref[...], preferred_element_type=f32)
    merged = _sigmoid(ga_ref[...]) * ab + _sigmoid(gp_ref[...]) * pb
    x2 = x_ref[...] + jnp.dot(merged.astype(bf16), wo_ref[...], preferred_element_type=f32)
    x2_ref[...] = x2
    ms = jnp.mean(x2 * x2, axis=-1, keepdims=True)
    hf = x2 * lax.rsqrt(ms + EPS) * g2_ref[...]
    h2_ref[...] = hf.astype(h2_ref.dtype)

    if moe:
        hi = hf.astype(bf16)
        lo = (hf - hi.astype(f32)).astype(bf16)
        logits = (jnp.dot(hi, wrh_ref[...], preferred_element_type=f32)
                  + jnp.dot(lo, wrh_ref[...], preferred_element_type=f32)
                  + jnp.dot(hi, wrl_ref[...], preferred_element_type=f32))
        lanef = lax.broadcasted_iota(i32, (1, LANES), 1).astype(f32)
        lg = jnp.where(lanef < N_EXPERTS, logits, -jnp.inf)
        v1 = jnp.max(lg, axis=1, keepdims=True)
        i1 = jnp.min(jnp.where(lg == v1, lanef, float(LANES)), axis=1, keepdims=True)
        lg2 = jnp.where(lanef == i1, -jnp.inf, lg)
        v2 = jnp.max(lg2, axis=1, keepdims=True)
        i2 = jnp.min(jnp.where(lg2 == v2, lanef, float(LANES)), axis=1, keepdims=True)
        tt = jnp.exp(v2 - v1)
        w1 = 1.0 / (1.0 + tt)
        w2 = tt / (1.0 + tt)
        route = jnp.where((lanef == i1) | (lanef == i2), 1.0, 0.0)
        for ln, val in ((ROUTE_W1, w1), (ROUTE_W2, w2), (ROUTE_I1, i1), (ROUTE_I2, i2)):
            route = jnp.where(lanef == float(ln), val, route)
        gate_ref[...] = route


def _merge(moe, seq, x2d, a, u, ga, gp, wp, ps, pa, pb, wo, g2, wrh=None, wrl=None):
    t = x2d.shape[0]
    tm = TM_MERGE
    row = lambda i: (i, 0)
    const2 = lambda i: (0, 0)
    const3 = lambda i: (0, 0, 0)
    halo = lambda i: (jnp.maximum(i * (tm // POOL_HALO) - 1, 0), 0)
    in_specs = [
        pl.BlockSpec((tm, D_MODEL), row),
        pl.BlockSpec((tm, ATTN_WIDTH), row),
        pl.BlockSpec((tm, POOL_WIDTH), row),
        pl.BlockSpec((POOL_HALO, POOL_WIDTH), halo),
        pl.BlockSpec((tm, D_MODEL), row),
        pl.BlockSpec((tm, D_MODEL), row),
        pl.BlockSpec((POOL_GROUPS, POOL_GROUP_DIM, POOL_GROUP_DIM), const3),
        pl.BlockSpec((1, POOL_WIDTH), const2),
        pl.BlockSpec((ATTN_WIDTH, D_MODEL), const2),
        pl.BlockSpec((POOL_WIDTH, D_MODEL), const2),
        pl.BlockSpec((D_MODEL, D_MODEL), const2),
        pl.BlockSpec((1, D_MODEL), const2),
    ]
    args = [x2d, a, u, u, ga, gp, wp, ps, pa, pb, wo, g2]
    out_shapes = [jax.ShapeDtypeStruct((t, D_MODEL), f32), jax.ShapeDtypeStruct((t, D_MODEL), f32 if moe else bf16)]
    out_specs = [pl.BlockSpec((tm, D_MODEL), row), pl.BlockSpec((tm, D_MODEL), row)]
    if moe:
        in_specs += [pl.BlockSpec((D_MODEL, LANES), const2), pl.BlockSpec((D_MODEL, LANES), const2)]
        args += [wrh, wrl]
        out_shapes.append(jax.ShapeDtypeStruct((t, LANES), f32))
        out_specs.append(pl.BlockSpec((tm, LANES), row))
    return pl.pallas_call(
        functools.partial(_merge_kernel, moe, seq),
        grid=(t // tm,),
        in_specs=in_specs,
        out_specs=out_specs,
        out_shape=out_shapes,
        scratch_shapes=[pltpu.VMEM((tm + POOL_HALO, POOL_WIDTH), f32)],
        compiler_params=pltpu.CompilerParams(
            dimension_semantics=("arbitrary",), vmem_limit_bytes=VMEM_LIMIT),
        name="merge_moe" if moe else "merge_dense",
    )(*args)


def _swiglu_partial(h, w1, w3, w2):
    a = jnp.dot(h, w1, preferred_element_type=f32)
    b = jnp.dot(h, w3, preferred_element_type=f32)
    act = (a * _sigmoid(a) * b).astype(bf16)
    return jnp.dot(act, w2, preferred_element_type=f32)


def _ffn_kernel(h_ref, x_ref, w1_ref, w3_ref, w2_ref, o_ref):
    @pl.when(pl.program_id(1) == 0)
    def _():
        o_ref[...] = x_ref[...]

    o_ref[...] += _swiglu_partial(h_ref[...], w1_ref[...], w3_ref[...], w2_ref[...])


def _ffn(h2, x2, w1, w3, w2, tm, tf):
    t = h2.shape[0]
    ff = w1.shape[1]
    row = lambda i, j: (i, 0)
    return pl.pallas_call(
        _ffn_kernel,
        grid=(t // tm, ff // tf),
        in_specs=[
            pl.BlockSpec((tm, D_MODEL), row),
            pl.BlockSpec((tm, D_MODEL), row),
            pl.BlockSpec((D_MODEL, tf), lambda i, j: (0, j)),
            pl.BlockSpec((D_MODEL, tf), lambda i, j: (0, j)),
            pl.BlockSpec((tf, D_MODEL), lambda i, j: (j, 0)),
        ],
        out_specs=pl.BlockSpec((tm, D_MODEL), row),
        out_shape=jax.ShapeDtypeStruct((t, D_MODEL), f32),
        compiler_params=pltpu.CompilerParams(
            dimension_semantics=("arbitrary", "arbitrary"), vmem_limit_bytes=VMEM_LIMIT),
        name="ffn_dense",
    )(h2, x2, w1, w3, w2)


def _rank_kernel(route_ref, tri_ref, rk_ref, cnt_ref, carry_sc):
    @pl.when(pl.program_id(0) == 0)
    def _():
        carry_sc[...] = jnp.zeros(carry_sc.shape, f32)

    lane = lax.broadcasted_iota(i32, (1, LANES), 1)
    lanef = lane.astype(f32)
    r = route_ref[...]
    sel = jnp.where(lane < N_EXPERTS, r, 0.0)
    ranks = jnp.dot(tri_ref[...], sel.astype(bf16), preferred_element_type=f32) + carry_sc[...]
    rk1 = jnp.sum(jnp.where(lanef == r[:, ROUTE_I1:ROUTE_I1 + 1], ranks, 0.0), axis=1, keepdims=True)
    rk2 = jnp.sum(jnp.where(lanef == r[:, ROUTE_I2:ROUTE_I2 + 1], ranks, 0.0), axis=1, keepdims=True)
    rk_ref[...] = jnp.where(lane == 0, rk1, jnp.where(lane == 1, rk2, 0.0))
    carry_sc[...] += jnp.sum(sel, axis=0, keepdims=True)
    cnt_ref[...] = carry_sc[...]


def _moe_rank(route):
    t = route.shape[0]
    tm = TM_RANK
    idx = jnp.arange(tm)
    tri = (idx[None, :] < idx[:, None]).astype(bf16)
    return pl.pallas_call(
        _rank_kernel,
        grid=(t // tm,),
        in_specs=[pl.BlockSpec((tm, LANES), lambda i: (i, 0)), pl.BlockSpec((tm, tm), lambda i: (0, 0))],
        out_specs=[pl.BlockSpec((tm, LANES), lambda i: (i, 0)), pl.BlockSpec((1, LANES), lambda i: (0, 0))],
        out_shape=[jax.ShapeDtypeStruct((t, LANES), f32), jax.ShapeDtypeStruct((1, LANES), f32)],
        scratch_shapes=[pltpu.VMEM((1, LANES), f32)],
        compiler_params=pltpu.CompilerParams(dimension_semantics=("arbitrary",)),
        name="moe_rank",
    )(route, tri)


def _row_copy(src, dst, sem):
    return pltpu.make_async_copy(src, dst, sem)


def _moe_ffn_kernel(pos_ref, te_ref, meta_ref, h_hbm, w1_ref, w3_ref, w2_ref, ys_ref,
                    inv_sm, gbuf, xb, sem):
    del te_ref
    i = pl.program_id(0)
    j = pl.program_id(1)
    n_used = meta_ref[0]
    n_slots = pos_ref.shape[0]

    def start_gather(tile, slot):
        def issue(g, carry):
            for u in range(ROW_UNROLL):
                r = g * ROW_UNROLL + u
                tok = inv_sm[tile * TM_EXPERT + r]
                _row_copy(h_hbm.at[pl.ds(tok, 1)], gbuf.at[slot, pl.ds(r, 1)],
                          sem.at[slot]).start(priority=u % 2)
            return carry
        lax.fori_loop(0, TM_EXPERT // ROW_UNROLL, issue, 0)

    def wait_gather(slot):
        _row_copy(h_hbm.at[pl.ds(0, TM_EXPERT)], gbuf.at[slot], sem.at[slot]).wait()

    @pl.when((i == 0) & (j == 0))
    def _():
        for e in range(N_EXPERTS):
            def pad(r, carry):
                inv_sm[r] = 0
                return carry
            lax.fori_loop(meta_ref[1 + e], meta_ref[1 + N_EXPERTS + e], pad, 0)

        def scatter(g, carry):
            for u in range(ROW_UNROLL):
                n = g * ROW_UNROLL + u
                inv_sm[pos_ref[n]] = lax.shift_right_logical(n, 1)
            return carry
        lax.fori_loop(0, n_slots // ROW_UNROLL, scatter, 0)

        @pl.when(n_used > 0)
        def _():
            start_gather(0, 0)

    used = i < n_used

    @pl.when(used & (j == 0))
    def _():
        slot = i % 2
        wait_gather(slot)

        @pl.when(i + 1 < n_used)
        def _():
            start_gather(i + 1, 1 - slot)

        xb[...] = gbuf[slot].astype(bf16)

    @pl.when(used)
    def _():
        y = _swiglu_partial(xb[...], w1_ref[0], w3_ref[0], w2_ref[0])

        @pl.when(j == 0)
        def _():
            ys_ref[...] = y

        @pl.when(j > 0)
        def _():
            ys_ref[...] += y

    @pl.when(jnp.logical_not(used) & (j == 0))
    def _():
        ys_ref[...] = jnp.zeros(ys_ref.shape, f32)


def _moe_ffn(pos, tile_expert, meta, hf, w1, w3, w2, n_rows):
    ff = w1.shape[2]
    nj = ff // TF_EXPERT
    jj = lambda i, j, meta: jnp.where(i < meta[0], j, nj - 1)
    return pl.pallas_call(
        _moe_ffn_kernel,
        grid_spec=pltpu.PrefetchScalarGridSpec(
            num_scalar_prefetch=3,
            grid=(n_rows // TM_EXPERT, nj),
            in_specs=[
                pl.BlockSpec(memory_space=pl.ANY),
                pl.BlockSpec((1, D_MODEL, TF_EXPERT), lambda i, j, pos, te, meta: (te[i], 0, jj(i, j, meta))),
                pl.BlockSpec((1, D_MODEL, TF_EXPERT), lambda i, j, pos, te, meta: (te[i], 0, jj(i, j, meta))),
                pl.BlockSpec((1, TF_EXPERT, D_MODEL), lambda i, j, pos, te, meta: (te[i], jj(i, j, meta), 0)),
            ],
            out_specs=pl.BlockSpec((TM_EXPERT, D_MODEL), lambda i, j, pos, te, meta: (i, 0)),
            scratch_shapes=[
                pltpu.SMEM((n_rows,), i32),
                pltpu.VMEM((2, TM_EXPERT, D_MODEL), f32),
                pltpu.VMEM((TM_EXPERT, D_MODEL), bf16),
                pltpu.SemaphoreType.DMA((2,)),
            ],
        ),
        out_shape=jax.ShapeDtypeStruct((n_rows, D_MODEL), f32),
        compiler_params=pltpu.CompilerParams(
            dimension_semantics=("arbitrary", "arbitrary"), vmem_limit_bytes=VMEM_LIMIT),
        name="moe_ffn",
    )(pos, tile_expert, meta, hf, w1, w3, w2)


def _combine_kernel(pos_ref, ys_hbm, x2_ref, route_ref, o_ref, buf, sem):
    base = pl.program_id(0) * TM_ROWS

    def issue(g, carry):
        for u in range(ROW_UNROLL // 2):
            r = g * (ROW_UNROLL // 2) + u
            for s in range(2):
                _row_copy(ys_hbm.at[pl.ds(pos_ref[2 * (base + r) + s], 1)], buf.at[s, pl.ds(r, 1)],
                          sem).start(priority=s)
        return carry

    lax.fori_loop(0, TM_ROWS // (ROW_UNROLL // 2), issue, 0)
    for s in range(2):
        _row_copy(ys_hbm.at[pl.ds(0, TM_ROWS)], buf.at[s], sem).wait()
    route = route_ref[...]
    o_ref[...] = (x2_ref[...] + route[:, ROUTE_W1:ROUTE_W1 + 1] * buf[0]
                  + route[:, ROUTE_W2:ROUTE_W2 + 1] * buf[1])


def _moe_combine(pos, ys, x2, route):
    t = x2.shape[0]
    row = lambda i, pos: (i, 0)
    return pl.pallas_call(
        _combine_kernel,
        grid_spec=pltpu.PrefetchScalarGridSpec(
            num_scalar_prefetch=1,
            grid=(t // TM_ROWS,),
            in_specs=[
                pl.BlockSpec(memory_space=pl.ANY),
                pl.BlockSpec((TM_ROWS, D_MODEL), row),
                pl.BlockSpec((TM_ROWS, LANES), row),
            ],
            out_specs=pl.BlockSpec((TM_ROWS, D_MODEL), row),
            scratch_shapes=[pltpu.VMEM((2, TM_ROWS, D_MODEL), f32), pltpu.SemaphoreType.DMA(())],
        ),
        out_shape=jax.ShapeDtypeStruct((t, D_MODEL), f32),
        compiler_params=pltpu.CompilerParams(dimension_semantics=("arbitrary",)),
        name="moe_combine",
    )(pos, ys, x2, route)


def _moe_layer(route, hf, x2, w1, w3, w2):
    t = hf.shape[0]
    n_tiles = 2 * t // TM_EXPERT + N_EXPERTS
    rk, cnt = _moe_rank(route)
    counts = cnt[0, :N_EXPERTS].astype(i32)
    padded = (counts + TM_EXPERT - 1) // TM_EXPERT * TM_EXPERT
    ends = jnp.cumsum(padded)
    starts = ends - padded
    n_used = (ends[-1] // TM_EXPERT).astype(i32)
    tile_id = jnp.arange(n_tiles, dtype=i32)
    tile_expert = jnp.minimum(jnp.sum((tile_id[:, None] * TM_EXPERT >= ends[None, :]).astype(i32), axis=1),
                              N_EXPERTS - 1)
    tile_expert = jnp.where(tile_id < n_used, tile_expert, tile_expert[jnp.maximum(n_used - 1, 0)])
    experts = jnp.arange(N_EXPERTS, dtype=f32)[None, :]
    start_of = lambda ids: jnp.sum(jnp.where(ids[:, None] == experts, starts[None, :], 0), axis=1)
    pos1 = start_of(route[:, ROUTE_I1]) + rk[:, 0].astype(i32)
    pos2 = start_of(route[:, ROUTE_I2]) + rk[:, 1].astype(i32)
    pos = jnp.stack([pos1, pos2], axis=1).reshape(2 * t).astype(i32)

    meta = jnp.concatenate([n_used.reshape(1), starts + counts, ends]).astype(i32)
    ys = _moe_ffn(pos, tile_expert, meta, hf, w1, w3, w2, n_tiles * TM_EXPERT)
    return _moe_combine(pos, ys, x2, route)


def _rope_tables(seq, rot_dim, period, active_lanes):
    half = rot_dim // 2
    inv = jnp.power(ROPE_THETA, -jnp.arange(0, rot_dim, 2, dtype=f32) / rot_dim)
    ang = jnp.arange(seq, dtype=f32)[:, None] * inv[None, :]
    cos, sin = jnp.cos(ang), jnp.sin(ang)
    ones = jnp.ones((seq, period - rot_dim), f32)
    zeros = lambda n: jnp.zeros((seq, n), f32)
    cos_p = jnp.concatenate([cos, cos, ones], axis=1)
    sin_a = jnp.concatenate([-sin, zeros(period - half)], axis=1)
    sin_b = jnp.concatenate([zeros(half), sin, zeros(period - rot_dim)], axis=1)
    reps = LANES // period
    tabs = [jnp.tile(tb, (1, reps)) for tb in (cos_p, sin_a, sin_b)]
    live = (jnp.arange(LANES) < active_lanes)[None, :]
    tabs = [jnp.where(live, tabs[0], 1.0), jnp.where(live, tabs[1], 0.0), jnp.where(live, tabs[2], 0.0)]
    return jnp.stack(tabs, axis=0)


def _relayout_w_in(w):
    d = w.shape[0]
    o = 0
    segs = {}
    for name, n in (("q", 512), ("k", 512), ("v", 512), ("qi", 256), ("ki", 32), ("wi", 8),
                    ("u", 512), ("ga", 1024), ("gp", 1024)):
        segs[name] = w[:, o:o + n]
        o += n
    qi = segs["qi"].reshape(d, IDX_HEADS, IDX_DIM)
    qcat = jnp.concatenate([qi, qi, qi, jnp.zeros_like(qi)], axis=-1).reshape(d, IDX_HEADS * LANES)
    kwc = jnp.concatenate([segs["ki"], segs["ki"], segs["ki"], segs["wi"],
                           jnp.zeros((d, LANES - 3 * IDX_DIM - IDX_HEADS), w.dtype)], axis=1)
    vh = segs["v"].reshape(d, N_HEADS // 2, 2, HEAD_DIM)
    zh = jnp.zeros_like(vh[:, :, 0])
    vx = jnp.stack([jnp.concatenate([vh[:, :, 0], zh], axis=-1), jnp.concatenate([zh, vh[:, :, 1]], axis=-1)],
                   axis=2).reshape(d, N_HEADS * LANES)
    out = jnp.concatenate([segs["q"], segs["k"], vx, qcat, kwc, segs["u"], segs["ga"], segs["gp"]], axis=1)
    return out.astype(bf16)


def kernel(x, mix_norm, w_in, q_norm, k_norm, w_pool, pool_scale, w_attn_proj, w_pool_proj, w_out, ffn_norm,
           dense_w1, dense_w3, dense_w2, moe_router, moe_w1, moe_w3, moe_w2):
    batch, seq, d = x.shape
    depth = w_in.shape[0]
    t = batch * seq
    assert d == D_MODEL and seq % TK == 0 and seq % TM_PROJ == 0 and seq % TM_MERGE == 0

    rqk = _rope_tables(seq, ROPE_DIM, HEAD_DIM, LANES)
    rqi = _rope_tables(seq, IDX_ROPE_DIM, IDX_DIM, LANES)
    rki = _rope_tables(seq, IDX_ROPE_DIM, IDX_DIM, 3 * IDX_DIM)
    head_of = jnp.arange(ATTN_WIDTH) // HEAD_DIM
    bd = jnp.where(head_of[:, None] == head_of[None, :], 1.0 / HEAD_DIM, 0.0).astype(bf16)
    lane_in_tile = jnp.arange(N_HEADS * LANES) % LANES
    odd_head = (jnp.arange(N_HEADS * LANES) // LANES) % 2 == 1
    ones_lane = (lane_in_tile == jnp.where(odd_head, 0, HEAD_DIM)).astype(f32)[None, :]

    xc = x.reshape(t, d)
    for layer in range(depth):
        w = _relayout_w_in(w_in[layer])
        qn = jnp.tile(q_norm[layer], N_HEADS)[None, :]
        kn = jnp.tile(k_norm[layer], N_HEADS)[None, :]
        q, k, vx, qc, kw, u, ga, gp = _in_proj(xc, mix_norm[layer][None, :], w, qn, kn, bd, ones_lane,
                                               rqk, rqi, rki, seq)
        a = _dsa_attention(q, qc, kw, k, vx, batch, seq)
        moe = layer % 2 == 1
        common = (xc, a, u, ga, gp, w_pool[layer].astype(bf16), pool_scale[layer][None, :],
                  w_attn_proj[layer].astype(bf16), w_pool_proj[layer].astype(bf16), w_out[layer].astype(bf16),
                  ffn_norm[layer][None, :])
        idx = layer // 2
        if moe:
            wr = jnp.pad(moe_router[idx], ((0, 0), (0, LANES - N_EXPERTS)))
            wrh = wr.astype(bf16)
            wrl = (wr - wrh.astype(f32)).astype(bf16)
            x2, hf, route = _merge(True, seq, *common, wrh, wrl)
            xc = _moe_layer(route, hf, x2, moe_w1[idx].astype(bf16), moe_w3[idx].astype(bf16),
                            moe_w2[idx].astype(bf16))
        else:
            x2, h2 = _merge(False, seq, *common)
            xc = _ffn(h2, x2, dense_w1[idx].astype(bf16), dense_w3[idx].astype(bf16),
                      dense_w2[idx].astype(bf16), tm=1024, tf=1408)
    return xc.reshape(batch, seq, d)
```

```python
import functools

import jax
import jax.numpy as jnp
from jax import lax
from jax.experimental import pallas as pl
from jax.experimental.pallas import tpu as pltpu

bf16 = jnp.bfloat16
f32 = jnp.float32
i32 = jnp.int32

D_MODEL = 1024
N_HEADS = 8
HEAD_DIM = 64
ATTN_WIDTH = 512
ROPE_DIM = 16
ROPE_THETA = 500000.0
IDX_HEADS = 8
IDX_DIM = 32
IDX_ROPE_DIM = 8
TOPK_MAX = 256
POOL_GROUPS = 4
POOL_GROUP_DIM = 128
POOL_WIDTH = 512
POOL_WINDOWS = (2, 4, 8, 16)
POOL_HALO = 16
N_EXPERTS = 8
EPS = 1e-6

LANES = 128
INT_MIN = -(2 ** 31)
VMEM_LIMIT = 56 * 1024 * 1024
LOG2E = 1.4426950408889634
SOFTMAX_DENOM_FLOOR = 2.0 ** -90

OFF_Q = 0
OFF_K = 512
OFF_VX = 1024
OFF_QCAT = 2048
OFF_KW = 3072
OFF_U = 3200
OFF_GA = 3712
OFF_GP = 4736
W_COLS = 5760

ROUTE_W1, ROUTE_W2, ROUTE_I1, ROUTE_I2 = 8, 9, 10, 11

TM_PROJ = 512
TM_MERGE = 512
TM_RANK = 512
TM_EXPERT = 512
TF_EXPERT = 896
TM_ROWS = 256
ROW_UNROLL = 8
QB = 256
COUNT_ROWS = 128
SEARCH_FIXED_BITS = 15
SEARCH_GROUP_BITS = 4
TK = 512


def _sigmoid(x):
    return 1.0 / (1.0 + jnp.exp(-x))


def _in_proj_kernel(x_ref, g_ref, w_ref, qn_ref, kn_ref, bd_ref, one_ref, rqk_ref, rqi_ref, rki_ref,
                    q_ref, k_ref, vx_ref, qc_ref, kw_ref, u_ref, ga_ref, gp_ref):
    x = x_ref[...]
    ms = jnp.mean(x * x, axis=-1, keepdims=True)
    h = (x * lax.rsqrt(ms + EPS) * g_ref[...]).astype(bf16)

    def proj(lo, n):
        return jnp.dot(h, w_ref[:, lo:lo + n], preferred_element_type=f32)

    def rope(xc, tab_ref, sh):
        return (xc * tab_ref[0] + pltpu.roll(xc, LANES - sh, 1) * tab_ref[1]
                + pltpu.roll(xc, sh, 1) * tab_ref[2])

    def headnorm(z, gain_ref):
        msh = jnp.dot((z * z).astype(bf16), bd_ref[...], preferred_element_type=f32)
        return z * lax.rsqrt(msh + EPS) * gain_ref[...]

    zq = headnorm(proj(OFF_Q, ATTN_WIDTH), qn_ref) * (HEAD_DIM ** -0.5 * LOG2E)
    zk = headnorm(proj(OFF_K, ATTN_WIDTH), kn_ref)
    for c in range(ATTN_WIDTH // LANES):
        sl = slice(c * LANES, (c + 1) * LANES)
        q_ref[:, sl] = rope(zq[:, sl], rqk_ref, ROPE_DIM // 2).astype(bf16)
        k_ref[:, sl] = rope(zk[:, sl], rqk_ref, ROPE_DIM // 2).astype(bf16)
    vx_ref[...] = jnp.where(one_ref[...] > 0.0, 1.0, proj(OFF_VX, N_HEADS * LANES)).astype(bf16)

    lane = lax.broadcasted_iota(i32, (1, LANES), 1)
    zc = proj(OFF_QCAT, IDX_HEADS * LANES)
    for c in range(IDX_HEADS):
        sl = slice(c * LANES, (c + 1) * LANES)
        r = rope(zc[:, sl], rqi_ref, IDX_ROPE_DIM // 2)
        lo = r - r.astype(bf16).astype(f32)
        qc_ref[:, sl] = jnp.where((lane >= 32) & (lane < 64), lo, r).astype(bf16)

    zkw = rope(proj(OFF_KW, LANES), rki_ref, IDX_ROPE_DIM // 2)
    lo = zkw - zkw.astype(bf16).astype(f32)
    zkw = jnp.where((lane >= 64) & (lane < 96), lo, zkw)
    kw_ref[...] = jnp.where(lane >= 96, zkw * ((IDX_HEADS * IDX_DIM) ** -0.5), zkw)

    u_ref[...] = proj(OFF_U, POOL_WIDTH)
    ga_ref[...] = proj(OFF_GA, D_MODEL)
    gp_ref[...] = proj(OFF_GP, D_MODEL)


def _in_proj(x2d, g, w, qn, kn, bd, ones_lane, rqk, rqi, rki, seq):
    t = x2d.shape[0]
    tm = TM_PROJ
    ns = seq // tm
    row = lambda i: (i, 0)
    const = lambda i: (0, 0)
    tab = lambda i: (0, i % ns, 0)
    out_shapes = (
        jax.ShapeDtypeStruct((t, ATTN_WIDTH), bf16),
        jax.ShapeDtypeStruct((t, ATTN_WIDTH), bf16),
        jax.ShapeDtypeStruct((t, N_HEADS * LANES), bf16),
        jax.ShapeDtypeStruct((t, IDX_HEADS * LANES), bf16),
        jax.ShapeDtypeStruct((t, LANES), f32),
        jax.ShapeDtypeStruct((t, POOL_WIDTH), f32),
        jax.ShapeDtypeStruct((t, D_MODEL), f32),
        jax.ShapeDtypeStruct((t, D_MODEL), f32),
    )
    return pl.pallas_call(
        _in_proj_kernel,
        grid=(t // tm,),
        in_specs=[
            pl.BlockSpec((tm, D_MODEL), row),
            pl.BlockSpec((1, D_MODEL), const),
            pl.BlockSpec((D_MODEL, W_COLS), const, pipeline_mode=pl.Buffered(1)),
            pl.BlockSpec((1, ATTN_WIDTH), const),
            pl.BlockSpec((1, ATTN_WIDTH), const),
            pl.BlockSpec((ATTN_WIDTH, ATTN_WIDTH), const),
            pl.BlockSpec((1, N_HEADS * LANES), const),
            pl.BlockSpec((3, tm, LANES), tab),
            pl.BlockSpec((3, tm, LANES), tab),
            pl.BlockSpec((3, tm, LANES), tab),
        ],
        out_specs=[pl.BlockSpec((tm, s.shape[1]), row) for s in out_shapes],
        out_shape=out_shapes,
        compiler_params=pltpu.CompilerParams(
            dimension_semantics=("arbitrary",), vmem_limit_bytes=VMEM_LIMIT),
        name="in_proj",
    )(x2d, g, w, qn, kn, bd, ones_lane, rqk, rqi, rki)


def _lane_fold(x, op):
    out = x[:, 0:LANES]
    for j in range(1, x.shape[1] // LANES):
        out = op(out, x[:, j * LANES:(j + 1) * LANES])
    return out


def _dsa_kernel(topk, q_ref, qc_ref, kwq_ref, k_ref, vx_ref, kw_ref, o_ref,
                key_sc, kcat_sc, qm2_sc, mrun_sc, m_sc, acc_sc, prefix_sc, cnt_sc, kmax_sm):
    i = pl.program_id(1)
    nk = (i * QB + QB + TK - 1) // TK
    kf = float(topk)

    lane = lax.broadcasted_iota(i32, (1, LANES), 1)

    @pl.when(i == 0)
    def _():
        kcat_sc[...] = kw_ref[...].astype(bf16)
        for pr in range(N_HEADS // 2):
            def norms(c, carry):
                kk = k_ref[pl.ds(pl.multiple_of(c * TK, TK), TK), pr * LANES:(pr + 1) * LANES].astype(f32)
                sq = kk * kk
                n0 = jnp.sqrt(jnp.sum(jnp.where(lane < HEAD_DIM, sq, 0.0), axis=1, keepdims=True))
                n1 = jnp.sqrt(jnp.sum(jnp.where(lane >= HEAD_DIM, sq, 0.0), axis=1, keepdims=True))
                return jnp.maximum(carry[0], jnp.max(n0)), jnp.maximum(carry[1], jnp.max(n1))
            k0, k1 = lax.fori_loop(0, k_ref.shape[0] // TK, norms, (jnp.float32(0.0), jnp.float32(0.0)))
            kmax_sm[2 * pr] = k0
            kmax_sm[2 * pr + 1] = k1

    qpos = i * QB + lax.broadcasted_iota(i32, (QB, 1), 0)
    lane_tk = lax.broadcasted_iota(i32, (1, TK), 1)
    wq = kwq_ref[...]

    def to_key(score):
        bits = pltpu.bitcast(score, i32)
        return bits ^ ((bits >> 31) & 0x7FFFFFFF)

    def score_chunk(c, carry):
        koff = pl.multiple_of(c * TK, TK)
        kc = kcat_sc[pl.ds(koff, TK), :]
        acc = jnp.zeros((QB, TK), f32)
        for h in range(IDX_HEADS):
            d = lax.dot_general(qc_ref[:, h * LANES:(h + 1) * LANES], kc,
                                (((1,), (1,)), ((), ())), preferred_element_type=f32)
            acc = acc + jnp.maximum(d, 0.0) * wq[:, 96 + h:97 + h]
        causal = (koff + lane_tk) <= qpos
        key_sc[c] = jnp.where(causal, to_key(acc), INT_MIN)
        return carry

    lax.fori_loop(0, nk, score_chunk, 0)

    def count_ge(cand):
        counts = []
        for r0 in range(0, QB, COUNT_ROWS):
            cand_g = cand[r0:r0 + COUNT_ROWS]

            def body(c, part, r0=r0, cand_g=cand_g):
                for j in range(TK // LANES):
                    kk = key_sc[c, r0:r0 + COUNT_ROWS, j * LANES:(j + 1) * LANES]
                    part = part + jnp.where(kk >= cand_g, 1, 0)
                return part

            counts.append(lax.fori_loop(0, nk, body, jnp.zeros((COUNT_ROWS, LANES), i32)))
        return jnp.sum(jnp.concatenate(counts, axis=0).astype(f32), axis=1, keepdims=True)

    zero = jnp.zeros((QB, 1), i32)
    cnt0 = count_ge(zero)
    prefix_sc[...] = jnp.where(cnt0 >= kf, zero, INT_MIN)
    cnt_sc[...] = jnp.where(cnt0 >= kf, cnt0, (qpos + 1).astype(f32))

    def run_bits(first, count):
        def bit_body(b, carry):
            prefix = prefix_sc[...]
            cand = prefix | jnp.left_shift(1, 30 - b)
            cnt = count_ge(cand)
            up = cnt >= kf
            prefix_sc[...] = jnp.where(up, cand, prefix)
            cnt_sc[...] = jnp.where(up, cnt, cnt_sc[...])
            return carry
        lax.fori_loop(first, first + count, bit_body, 0)

    run_bits(0, SEARCH_FIXED_BITS)
    for first in range(SEARCH_FIXED_BITS, 31, SEARCH_GROUP_BITS):
        @pl.when(jnp.max(jnp.where(cnt_sc[...] > kf, 1.0, 0.0)) > 0.0)
        def _(first=first):
            run_bits(first, min(SEARCH_GROUP_BITS, 31 - first))

    thr_raw = prefix_sc[...]
    thr = jnp.maximum(thr_raw, INT_MIN + 1)

    is_tie = (cnt_sc[...] > kf) & (thr_raw > INT_MIN)

    @pl.when(jnp.max(jnp.where(is_tie, 1.0, 0.0)) > 0.0)
    def _():
        need = kf - count_ge(thr + 1)

        def count_eq_below(m):
            def body(c, part):
                hit = (key_sc[c] == thr) & ((c * TK + lane_tk) < m)
                return part + _lane_fold(jnp.where(hit, 1.0, 0.0), jnp.add)
            part = lax.fori_loop(0, nk, body, jnp.zeros((QB, LANES), f32))
            return jnp.sum(part, axis=1, keepdims=True)

        mprime = jnp.zeros((QB, 1), i32)
        bit = key_sc.shape[0] * TK // 2
        while bit >= 1:
            cand = mprime | bit
            mprime = jnp.where(count_eq_below(cand) < need, cand, mprime)
            bit //= 2

        def drop_ties(c, carry):
            key = key_sc[c]
            drop = is_tie & (key == thr) & ((c * TK + lane_tk) > mprime)
            key_sc[c] = jnp.where(drop, thr - 1, key)
            return carry

        lax.fori_loop(0, nk, drop_ties, 0)

    row2 = lax.broadcasted_iota(i32, (2 * QB, 1), 0)
    for pr in range(N_HEADS // 2):
        qp = q_ref[:, pr * LANES:(pr + 1) * LANES]
        qm2_sc[pr, 0:QB, :] = jnp.where(lane < HEAD_DIM, qp, jnp.zeros((), bf16))
        qm2_sc[pr, QB:2 * QB, :] = jnp.where(lane >= HEAD_DIM, qp, jnp.zeros((), bf16))
        q2 = qm2_sc[pr].astype(f32)
        qn = jnp.sqrt(jnp.sum(q2 * q2, axis=1, keepdims=True))
        m_sc[pr] = qn * jnp.where(row2 < QB, kmax_sm[2 * pr], kmax_sm[2 * pr + 1])
    acc_sc[...] = jnp.zeros(acc_sc.shape, f32)

    def pair_logits(c, pr, bias):
        koff = pl.multiple_of(c * TK, TK)
        kc = k_ref[pl.ds(koff, TK), pr * LANES:(pr + 1) * LANES]
        s2 = lax.dot_general(qm2_sc[pr], kc, (((1,), (1,)), ((), ())), preferred_element_type=f32)
        return s2[0:QB] + bias, s2[QB:2 * QB] + bias

    def max_chunk(c, carry):
        bias = jnp.where(key_sc[c] >= thr, 0.0, -jnp.inf)
        for pr in range(N_HEADS // 2):
            sa, sb = pair_logits(c, pr, bias)
            mrun_sc[pr, 0:QB, :] = jnp.maximum(mrun_sc[pr, 0:QB, :], _lane_fold(sa, jnp.maximum))
            mrun_sc[pr, QB:2 * QB, :] = jnp.maximum(mrun_sc[pr, QB:2 * QB, :], _lane_fold(sb, jnp.maximum))
        return carry

    def acc_chunk(c, carry):
        koff = pl.multiple_of(c * TK, TK)
        bias = jnp.where(key_sc[c] >= thr, 0.0, -jnp.inf)
        for pr in range(N_HEADS // 2):
            sa, sb = pair_logits(c, pr, bias)
            m = m_sc[pr]
            pa = jnp.exp2(sa - m[0:QB]).astype(bf16)
            pb = jnp.exp2(sb - m[QB:2 * QB]).astype(bf16)
            va = vx_ref[pl.ds(koff, TK), (2 * pr) * LANES:(2 * pr + 1) * LANES]
            vb = vx_ref[pl.ds(koff, TK), (2 * pr + 1) * LANES:(2 * pr + 2) * LANES]
            acc_sc[2 * pr] += jnp.dot(pa, va, preferred_element_type=f32)
            acc_sc[2 * pr + 1] += jnp.dot(pb, vb, preferred_element_type=f32)
        return carry

    lax.fori_loop(0, nk, acc_chunk, 0)

    lmin = jnp.float32(jnp.inf)
    for pr in range(N_HEADS // 2):
        lmin = jnp.minimum(lmin, jnp.min(acc_sc[2 * pr][:, HEAD_DIM:HEAD_DIM + 1]))
        lmin = jnp.minimum(lmin, jnp.min(acc_sc[2 * pr + 1][:, 0:1]))

    @pl.when(jnp.logical_not(lmin >= SOFTMAX_DENOM_FLOOR))
    def _():
        mrun_sc[...] = jnp.full(mrun_sc.shape, -jnp.inf, f32)
        lax.fori_loop(0, nk, max_chunk, 0)
        for pr in range(N_HEADS // 2):
            m = jnp.max(mrun_sc[pr], axis=1, keepdims=True)
            m_sc[pr] = jnp.where(m == -jnp.inf, 0.0, m)
        acc_sc[...] = jnp.zeros(acc_sc.shape, f32)
        lax.fori_loop(0, nk, acc_chunk, 0)

    for pr in range(N_HEADS // 2):
        a0 = acc_sc[2 * pr]
        a1 = acc_sc[2 * pr + 1]
        o0 = a0 / a0[:, HEAD_DIM:HEAD_DIM + 1]
        o1 = a1 / a1[:, 0:1]
        o_ref[:, pr * LANES:(pr + 1) * LANES] = jnp.where(lane < HEAD_DIM, o0, o1).astype(bf16)


def _dsa_attention(q, qc, kw, k, vx, batch, seq):
    t = q.shape[0]
    nq = seq // QB
    topk = min(TOPK_MAX, seq // 4)
    qrow = lambda b, i: (b * nq + i, 0)
    per_batch = lambda b, i: (b, 0)
    return pl.pallas_call(
        functools.partial(_dsa_kernel, topk),
        grid=(batch, nq),
        in_specs=[
            pl.BlockSpec((QB, ATTN_WIDTH), qrow),
            pl.BlockSpec((QB, IDX_HEADS * LANES), qrow),
            pl.BlockSpec((QB, LANES), qrow),
            pl.BlockSpec((seq, ATTN_WIDTH), per_batch, pipeline_mode=pl.Buffered(1)),
            pl.BlockSpec((seq, N_HEADS * LANES), per_batch, pipeline_mode=pl.Buffered(1)),
            pl.BlockSpec((seq, LANES), per_batch, pipeline_mode=pl.Buffered(1)),
        ],
        out_specs=pl.BlockSpec((QB, ATTN_WIDTH), qrow),
        out_shape=jax.ShapeDtypeStruct((t, ATTN_WIDTH), bf16),
        scratch_shapes=[
            pltpu.VMEM((seq // TK, QB, TK), i32),
            pltpu.VMEM((seq, LANES), bf16),
            pltpu.VMEM((N_HEADS // 2, 2 * QB, LANES), bf16),
            pltpu.VMEM((N_HEADS // 2, 2 * QB, LANES), f32),
            pltpu.VMEM((N_HEADS // 2, 2 * QB, 1), f32),
            pltpu.VMEM((N_HEADS, QB, LANES), f32),
            pltpu.VMEM((QB, 1), i32),
            pltpu.VMEM((QB, 1), f32),
            pltpu.SMEM((N_HEADS,), f32),
        ],
        compiler_params=pltpu.CompilerParams(
            dimension_semantics=("arbitrary", "arbitrary"), vmem_limit_bytes=VMEM_LIMIT),
        name="dsa_attention",
    )(q, qc, kw, k, vx, kw)


def _merge_kernel(moe, seq, *refs):
    if moe:
        (x_ref, a_ref, u_ref, uh_ref, ga_ref, gp_ref, wp_ref, ps_ref, pa_ref, pb_ref, wo_ref, g2_ref,
         wrh_ref, wrl_ref, x2_ref, h2_ref, gate_ref, e_sc) = refs
    else:
        (x_ref, a_ref, u_ref, uh_ref, ga_ref, gp_ref, wp_ref, ps_ref, pa_ref, pb_ref, wo_ref, g2_ref,
         x2_ref, h2_ref, e_sc) = refs
    tm = TM_MERGE
    i = pl.program_id(0)
    ti = i % (seq // tm)
    e_sc[0:POOL_HALO, :] = jnp.where(ti == 0, 0.0, uh_ref[...])
    e_sc[POOL_HALO:POOL_HALO + tm, :] = u_ref[...]
    npos = (ti * tm + 1 + lax.broadcasted_iota(i32, (tm, 1), 0)).astype(f32)

    parts = []
    for g, w in enumerate(POOL_WINDOWS):
        sl = slice(g * POOL_GROUP_DIM, (g + 1) * POOL_GROUP_DIM)
        tot = e_sc[POOL_HALO:POOL_HALO + tm, sl]
        for j in range(1, w):
            tot = tot + e_sc[POOL_HALO - j:POOL_HALO - j + tm, sl]
        diff = tot / jnp.minimum(npos, float(w)) - u_ref[:, sl]
        parts.append(jnp.dot(diff.astype(bf16), wp_ref[g], preferred_element_type=f32))
    p = jnp.concatenate(parts, axis=1) * ps_ref[...]

    ab = jnp.dot(a_ref[...], pa_ref[...], preferred_element_type=f32)
    pb = jnp.dot(p.astype(bf16), pb_ref[...], preferred_element_type=f32)
    merged = _sigmoid(ga_ref[...]) * ab + _sigmoid(gp_ref[...]) * pb
    x2 = x_ref[...] + jnp.dot(merged.astype(bf16), wo_ref[...], preferred_element_type=f32)
    x2_ref[...] = x2
    ms = jnp.mean(x2 * x2, axis=-1, keepdims=True)
    hf = x2 * lax.rsqrt(ms + EPS) * g2_ref[...]
    h2_ref[...] = hf.astype(h2_ref.dtype)

    if moe:
        hi = hf.astype(bf16)
        lo = (hf - hi.astype(f32)).astype(bf16)
        logits = (jnp.dot(hi, wrh_ref[...], preferred_element_type=f32)
                  + jnp.dot(lo, wrh_ref[...], preferred_element_type=f32)
                  + jnp.dot(hi, wrl_ref[...], preferred_element_type=f32))
        lanef = lax.broadcasted_iota(i32, (1, LANES), 1).astype(f32)
        lg = jnp.where(lanef < N_EXPERTS, logits, -jnp.inf)
        v1 = jnp.max(lg, axis=1, keepdims=True)
        i1 = jnp.min(jnp.where(lg == v1, lanef, float(LANES)), axis=1, keepdims=True)
        lg2 = jnp.where(lanef == i1, -jnp.inf, lg)
        v2 = jnp.max(lg2, axis=1, keepdims=True)
        i2 = jnp.min(jnp.where(lg2 == v2, lanef, float(LANES)), axis=1, keepdims=True)
        tt = jnp.exp(v2 - v1)
        w1 = 1.0 / (1.0 + tt)
        w2 = tt / (1.0 + tt)
        route = jnp.where((lanef == i1) | (lanef == i2), 1.0, 0.0)
        for ln, val in ((ROUTE_W1, w1), (ROUTE_W2, w2), (ROUTE_I1, i1), (ROUTE_I2, i2)):
            route = jnp.where(lanef == float(ln), val, route)
        gate_ref[...] = route


def _merge(moe, seq, x2d, a, u, ga, gp, wp, ps, pa, pb, wo, g2, wrh=None, wrl=None):
    t = x2d.shape[0]
    tm = TM_MERGE
    row = lambda i: (i, 0)
    const2 = lambda i: (0, 0)
    const3 = lambda i: (0, 0, 0)
    halo = lambda i: (jnp.maximum(i * (tm // POOL_HALO) - 1, 0), 0)
    in_specs = [
        pl.BlockSpec((tm, D_MODEL), row),
        pl.BlockSpec((tm, ATTN_WIDTH), row),
        pl.BlockSpec((tm, POOL_WIDTH), row),
        pl.BlockSpec((POOL_HALO, POOL_WIDTH), halo),
        pl.BlockSpec((tm, D_MODEL), row),
        pl.BlockSpec((tm, D_MODEL), row),
        pl.BlockSpec((POOL_GROUPS, POOL_GROUP_DIM, POOL_GROUP_DIM), const3),
        pl.BlockSpec((1, POOL_WIDTH), const2),
        pl.BlockSpec((ATTN_WIDTH, D_MODEL), const2),
        pl.BlockSpec((POOL_WIDTH, D_MODEL), const2),
        pl.BlockSpec((D_MODEL, D_MODEL), const2),
        pl.BlockSpec((1, D_MODEL), const2),
    ]
    args = [x2d, a, u, u, ga, gp, wp, ps, pa, pb, wo, g2]
    out_shapes = [jax.ShapeDtypeStruct((t, D_MODEL), f32), jax.ShapeDtypeStruct((t, D_MODEL), f32 if moe else bf16)]
    out_specs = [pl.BlockSpec((tm, D_MODEL), row), pl.BlockSpec((tm, D_MODEL), row)]
    if moe:
        in_specs += [pl.BlockSpec((D_MODEL, LANES), const2), pl.BlockSpec((D_MODEL, LANES), const2)]
        args += [wrh, wrl]
        out_shapes.append(jax.ShapeDtypeStruct((t, LANES), f32))
        out_specs.append(pl.BlockSpec((tm, LANES), row))
    return pl.pallas_call(
        functools.partial(_merge_kernel, moe, seq),
        grid=(t // tm,),
        in_specs=in_specs,
        out_specs=out_specs,
        out_shape=out_shapes,
        scratch_shapes=[pltpu.VMEM((tm + POOL_HALO, POOL_WIDTH), f32)],
        compiler_params=pltpu.CompilerParams(
            dimension_semantics=("arbitrary",), vmem_limit_bytes=VMEM_LIMIT),
        name="merge_moe" if moe else "merge_dense",
    )(*args)


def _swiglu_partial(h, w1, w3, w2):
    a = jnp.dot(h, w1, preferred_element_type=f32)
    b = jnp.dot(h, w3, preferred_element_type=f32)
    act = (a * _sigmoid(a) * b).astype(bf16)
    return jnp.dot(act, w2, preferred_element_type=f32)


def _ffn_kernel(h_ref, x_ref, w1_ref, w3_ref, w2_ref, o_ref):
    @pl.when(pl.program_id(1) == 0)
    def _():
        o_ref[...] = x_ref[...]

    o_ref[...] += _swiglu_partial(h_ref[...], w1_ref[...], w3_ref[...], w2_ref[...])


def _ffn(h2, x2, w1, w3, w2, tm, tf):
    t = h2.shape[0]
    ff = w1.shape[1]
    row = lambda i, j: (i, 0)
    return pl.pallas_call(
        _ffn_kernel,
        grid=(t // tm, ff // tf),
        in_specs=[
            pl.BlockSpec((tm, D_MODEL), row),
            pl.BlockSpec((tm, D_MODEL), row),
            pl.BlockSpec((D_MODEL, tf), lambda i, j: (0, j)),
            pl.BlockSpec((D_MODEL, tf), lambda i, j: (0, j)),
            pl.BlockSpec((tf, D_MODEL), lambda i, j: (j, 0)),
        ],
        out_specs=pl.BlockSpec((tm, D_MODEL), row),
        out_shape=jax.ShapeDtypeStruct((t, D_MODEL), f32),
        compiler_params=pltpu.CompilerParams(
            dimension_semantics=("arbitrary", "arbitrary"), vmem_limit_bytes=VMEM_LIMIT),
        name="ffn_dense",
    )(h2, x2, w1, w3, w2)


def _rank_kernel(route_ref, tri_ref, rk_ref, cnt_ref, carry_sc):
    @pl.when(pl.program_id(0) == 0)
    def _():
        carry_sc[...] = jnp.zeros(carry_sc.shape, f32)

    lane = lax.broadcasted_iota(i32, (1, LANES), 1)
    lanef = lane.astype(f32)
    r = route_ref[...]
    sel = jnp.where(lane < N_EXPERTS, r, 0.0)
    ranks = jnp.dot(tri_ref[...], sel.astype(bf16), preferred_element_type=f32) + carry_sc[...]
    rk1 = jnp.sum(jnp.where(lanef == r[:, ROUTE_I1:ROUTE_I1 + 1], ranks, 0.0), axis=1, keepdims=True)
    rk2 = jnp.sum(jnp.where(lanef == r[:, ROUTE_I2:ROUTE_I2 + 1], ranks, 0.0), axis=1, keepdims=True)
    rk_ref[...] = jnp.where(lane == 0, rk1, jnp.where(lane == 1, rk2, 0.0))
    carry_sc[...] += jnp.sum(sel, axis=0, keepdims=True)
    cnt_ref[...] = carry_sc[...]


def _moe_rank(route):
    t = route.shape[0]
    tm = TM_RANK
    idx = jnp.arange(tm)
    tri = (idx[None, :] < idx[:, None]).astype(bf16)
    return pl.pallas_call(
        _rank_kernel,
        grid=(t // tm,),
        in_specs=[pl.BlockSpec((tm, LANES), lambda i: (i, 0)), pl.BlockSpec((tm, tm), lambda i: (0, 0))],
        out_specs=[pl.BlockSpec((tm, LANES), lambda i: (i, 0)), pl.BlockSpec((1, LANES), lambda i: (0, 0))],
        out_shape=[jax.ShapeDtypeStruct((t, LANES), f32), jax.ShapeDtypeStruct((1, LANES), f32)],
        scratch_shapes=[pltpu.VMEM((1, LANES), f32)],
        compiler_params=pltpu.CompilerParams(dimension_semantics=("arbitrary",)),
        name="moe_rank",
    )(route, tri)


def _row_copy(src, dst, sem):
    return pltpu.make_async_copy(src, dst, sem)


def _moe_ffn_kernel(pos_ref, te_ref, meta_ref, h_hbm, w1_ref, w3_ref, w2_ref, ys_ref,
                    inv_sm, gbuf, xb, sem):
    del te_ref
    i = pl.program_id(0)
    j = pl.program_id(1)
    n_used = meta_ref[0]
    n_slots = pos_ref.shape[0]

    def start_gather(tile, slot):
        def issue(g, carry):
            for u in range(ROW_UNROLL):
                r = g * ROW_UNROLL + u
                tok = inv_sm[tile * TM_EXPERT + r]
                _row_copy(h_hbm.at[pl.ds(tok, 1)], gbuf.at[slot, pl.ds(r, 1)],
                          sem.at[slot]).start(priority=u % 2)
            return carry
        lax.fori_loop(0, TM_EXPERT // ROW_UNROLL, issue, 0)

    def wait_gather(slot):
        _row_copy(h_hbm.at[pl.ds(0, TM_EXPERT)], gbuf.at[slot], sem.at[slot]).wait()

    @pl.when((i == 0) & (j == 0))
    def _():
        for e in range(N_EXPERTS):
            def pad(r, carry):
                inv_sm[r] = 0
                return carry
            lax.fori_loop(meta_ref[1 + e], meta_ref[1 + N_EXPERTS + e], pad, 0)

        def scatter(g, carry):
            for u in range(ROW_UNROLL):
                n = g * ROW_UNROLL + u
                inv_sm[pos_ref[n]] = lax.shift_right_logical(n, 1)
            return carry
        lax.fori_loop(0, n_slots // ROW_UNROLL, scatter, 0)

        @pl.when(n_used > 0)
        def _():
            start_gather(0, 0)

    used = i < n_used

    @pl.when(used & (j == 0))
    def _():
        slot = i % 2
        wait_gather(slot)

        @pl.when(i + 1 < n_used)
        def _():
            start_gather(i + 1, 1 - slot)

        xb[...] = gbuf[slot].astype(bf16)

    @pl.when(used)
    def _():
        y = _swiglu_partial(xb[...], w1_ref[0], w3_ref[0], w2_ref[0])

        @pl.when(j == 0)
        def _():
            ys_ref[...] = y

        @pl.when(j > 0)
        def _():
            ys_ref[...] += y

    @pl.when(jnp.logical_not(used) & (j == 0))
    def _():
        ys_ref[...] = jnp.zeros(ys_ref.shape, f32)


def _moe_ffn(pos, tile_expert, meta, hf, w1, w3, w2, n_rows):
    ff = w1.shape[2]
    nj = ff // TF_EXPERT
    jj = lambda i, j, meta: jnp.where(i < meta[0], j, nj - 1)
    return pl.pallas_call(
        _moe_ffn_kernel,
        grid_spec=pltpu.PrefetchScalarGridSpec(
            num_scalar_prefetch=3,
            grid=(n_rows // TM_EXPERT, nj),
            in_specs=[
                pl.BlockSpec(memory_space=pl.ANY),
                pl.BlockSpec((1, D_MODEL, TF_EXPERT), lambda i, j, pos, te, meta: (te[i], 0, jj(i, j, meta))),
                pl.BlockSpec((1, D_MODEL, TF_EXPERT), lambda i, j, pos, te, meta: (te[i], 0, jj(i, j, meta))),
                pl.BlockSpec((1, TF_EXPERT, D_MODEL), lambda i, j, pos, te, meta: (te[i], jj(i, j, meta), 0)),
            ],
            out_specs=pl.BlockSpec((TM_EXPERT, D_MODEL), lambda i, j, pos, te, meta: (i, 0)),
            scratch_shapes=[
                pltpu.SMEM((n_rows,), i32),
                pltpu.VMEM((2, TM_EXPERT, D_MODEL), f32),
                pltpu.VMEM((TM_EXPERT, D_MODEL), bf16),
                pltpu.SemaphoreType.DMA((2,)),
            ],
        ),
        out_shape=jax.ShapeDtypeStruct((n_rows, D_MODEL), f32),
        compiler_params=pltpu.CompilerParams(
            dimension_semantics=("arbitrary", "arbitrary"), vmem_limit_bytes=VMEM_LIMIT),
        name="moe_ffn",
    )(pos, tile_expert, meta, hf, w1, w3, w2)


def _combine_kernel(pos_ref, ys_hbm, x2_ref, route_ref, o_ref, buf, sem):
    base = pl.program_id(0) * TM_ROWS

    def issue(g, carry):
        for u in range(ROW_UNROLL // 2):
            r = g * (ROW_UNROLL // 2) + u
            for s in range(2):
                _row_copy(ys_hbm.at[pl.ds(pos_ref[2 * (base + r) + s], 1)], buf.at[s, pl.ds(r, 1)],
                          sem).start(priority=s)
        return carry

    lax.fori_loop(0, TM_ROWS // (ROW_UNROLL // 2), issue, 0)
    for s in range(2):
        _row_copy(ys_hbm.at[pl.ds(0, TM_ROWS)], buf.at[s], sem).wait()
    route = route_ref[...]
    o_ref[...] = (x2_ref[...] + route[:, ROUTE_W1:ROUTE_W1 + 1] * buf[0]
                  + route[:, ROUTE_W2:ROUTE_W2 + 1] * buf[1])


def _moe_combine(pos, ys, x2, route):
    t = x2.shape[0]
    row = lambda i, pos: (i, 0)
    return pl.pallas_call(
        _combine_kernel,
        grid_spec=pltpu.PrefetchScalarGridSpec(
            num_scalar_prefetch=1,
            grid=(t // TM_ROWS,),
            in_specs=[
                pl.BlockSpec(memory_space=pl.ANY),
                pl.BlockSpec((TM_ROWS, D_MODEL), row),
                pl.BlockSpec((TM_ROWS, LANES), row),
            ],
            out_specs=pl.BlockSpec((TM_ROWS, D_MODEL), row),
            scratch_shapes=[pltpu.VMEM((2, TM_ROWS, D_MODEL), f32), pltpu.SemaphoreType.DMA(())],
        ),
        out_shape=jax.ShapeDtypeStruct((t, D_MODEL), f32),
        compiler_params=pltpu.CompilerParams(dimension_semantics=("arbitrary",)),
        name="moe_combine",
    )(pos, ys, x2, route)


def _moe_layer(route, hf, x2, w1, w3, w2):
    t = hf.shape[0]
    n_tiles = 2 * t // TM_EXPERT + N_EXPERTS
    rk, cnt = _moe_rank(route)
    counts = cnt[0, :N_EXPERTS].astype(i32)
    padded = (counts + TM_EXPERT - 1) // TM_EXPERT * TM_EXPERT
    ends = jnp.cumsum(padded)
    starts = ends - padded
    n_used = (ends[-1] // TM_EXPERT).astype(i32)
    tile_id = jnp.arange(n_tiles, dtype=i32)
    tile_expert = jnp.minimum(jnp.sum((tile_id[:, None] * TM_EXPERT >= ends[None, :]).astype(i32), axis=1),
                              N_EXPERTS - 1)
    tile_expert = jnp.where(tile_id < n_used, tile_expert, tile_expert[jnp.maximum(n_used - 1, 0)])
    experts = jnp.arange(N_EXPERTS, dtype=f32)[None, :]
    start_of = lambda ids: jnp.sum(jnp.where(ids[:, None] == experts, starts[None, :], 0), axis=1)
    pos1 = start_of(route[:, ROUTE_I1]) + rk[:, 0].astype(i32)
    pos2 = start_of(route[:, ROUTE_I2]) + rk[:, 1].astype(i32)
    pos = jnp.stack([pos1, pos2], axis=1).reshape(2 * t).astype(i32)

    meta = jnp.concatenate([n_used.reshape(1), starts + counts, ends]).astype(i32)
    ys = _moe_ffn(pos, tile_expert, meta, hf, w1, w3, w2, n_tiles * TM_EXPERT)
    return _moe_combine(pos, ys, x2, route)


def _rope_tables(seq, rot_dim, period, active_lanes):
    half = rot_dim // 2
    inv = jnp.power(ROPE_THETA, -jnp.arange(0, rot_dim, 2, dtype=f32) / rot_dim)
    ang = jnp.arange(seq, dtype=f32)[:, None] * inv[None, :]
    cos, sin = jnp.cos(ang), jnp.sin(ang)
    ones = jnp.ones((seq, period - rot_dim), f32)
    zeros = lambda n: jnp.zeros((seq, n), f32)
    cos_p = jnp.concatenate([cos, cos, ones], axis=1)
    sin_a = jnp.concatenate([-sin, zeros(period - half)], axis=1)
    sin_b = jnp.concatenate([zeros(half), sin, zeros(period - rot_dim)], axis=1)
    reps = LANES // period
    tabs = [jnp.tile(tb, (1, reps)) for tb in (cos_p, sin_a, sin_b)]
    live = (jnp.arange(LANES) < active_lanes)[None, :]
    tabs = [jnp.where(live, tabs[0], 1.0), jnp.where(live, tabs[1], 0.0), jnp.where(live, tabs[2], 0.0)]
    return jnp.stack(tabs, axis=0)


def _relayout_w_in(w):
    d = w.shape[0]
    o = 0
    segs = {}
    for name, n in (("q", 512), ("k", 512), ("v", 512), ("qi", 256), ("ki", 32), ("wi", 8),
                    ("u", 512), ("ga", 1024), ("gp", 1024)):
        segs[name] = w[:, o:o + n]
        o += n
    qi = segs["qi"].reshape(d, IDX_HEADS, IDX_DIM)
    qcat = jnp.concatenate([qi, qi, qi, jnp.zeros_like(qi)], axis=-1).reshape(d, IDX_HEADS * LANES)
    kwc = jnp.concatenate([segs["ki"], segs["ki"], segs["ki"], segs["wi"],
                           jnp.zeros((d, LANES - 3 * IDX_DIM - IDX_HEADS), w.dtype)], axis=1)
    vh = segs["v"].reshape(d, N_HEADS // 2, 2, HEAD_DIM)
    zh = jnp.zeros_like(vh[:, :, 0])
    vx = jnp.stack([jnp.concatenate([vh[:, :, 0], zh], axis=-1), jnp.concatenate([zh, vh[:, :, 1]], axis=-1)],
                   axis=2).reshape(d, N_HEADS * LANES)
    out = jnp.concatenate([segs["q"], segs["k"], vx, qcat, kwc, segs["u"], segs["ga"], segs["gp"]], axis=1)
    return out.astype(bf16)


def kernel(x, mix_norm, w_in, q_norm, k_norm, w_pool, pool_scale, w_attn_proj, w_pool_proj, w_out, ffn_norm,
           dense_w1, dense_w3, dense_w2, moe_router, moe_w1, moe_w3, moe_w2):
    batch, seq, d = x.shape
    depth = w_in.shape[0]
    t = batch * seq
    assert d == D_MODEL and seq % TK == 0 and seq % TM_PROJ == 0 and seq % TM_MERGE == 0

    rqk = _rope_tables(seq, ROPE_DIM, HEAD_DIM, LANES)
    rqi = _rope_tables(seq, IDX_ROPE_DIM, IDX_DIM, LANES)
    rki = _rope_tables(seq, IDX_ROPE_DIM, IDX_DIM, 3 * IDX_DIM)
    head_of = jnp.arange(ATTN_WIDTH) // HEAD_DIM
    bd = jnp.where(head_of[:, None] == head_of[None, :], 1.0 / HEAD_DIM, 0.0).astype(bf16)
    lane_in_tile = jnp.arange(N_HEADS * LANES) % LANES
    odd_head = (jnp.arange(N_HEADS * LANES) // LANES) % 2 == 1
    ones_lane = (lane_in_tile == jnp.where(odd_head, 0, HEAD_DIM)).astype(f32)[None, :]

    xc = x.reshape(t, d)
    for layer in range(depth):
        w = _relayout_w_in(w_in[layer])
        qn = jnp.tile(q_norm[layer], N_HEADS)[None, :]
        kn = jnp.tile(k_norm[layer], N_HEADS)[None, :]
        q, k, vx, qc, kw, u, ga, gp = _in_proj(xc, mix_norm[layer][None, :], w, qn, kn, bd, ones_lane,
                                               rqk, rqi, rki, seq)
        a = _dsa_attention(q, qc, kw, k, vx, batch, seq)
        moe = layer % 2 == 1
        common = (xc, a, u, ga, gp, w_pool[layer].astype(bf16), pool_scale[layer][None, :],
                  w_attn_proj[layer].astype(bf16), w_pool_proj[layer].astype(bf16), w_out[layer].astype(bf16),
                  ffn_norm[layer][None, :])
        idx = layer // 2
        if moe:
            wr = jnp.pad(moe_router[idx], ((0, 0), (0, LANES - N_EXPERTS)))
            wrh = wr.astype(bf16)
            wrl = (wr - wrh.astype(f32)).astype(bf16)
            x2, hf, route = _merge(True, seq, *common, wrh, wrl)
            xc = _moe_layer(route, hf, x2, moe_w1[idx].astype(bf16), moe_w3[idx].astype(bf16),
                            moe_w2[idx].astype(bf16))
        else:
            x2, h2 = _merge(False, seq, *common)
            xc = _ffn(h2, x2, dense_w1[idx].astype(bf16), dense_w3[idx].astype(bf16),
                      dense_w2[idx].astype(bf16), tm=1024, tf=1408)
    return xc.reshape(batch, seq, d)
```

```python
import functools

import jax
import jax.numpy as jnp
from jax import lax
from jax.experimental import pallas as pl
from jax.experimental.pallas import tpu as pltpu

bf16 = jnp.bfloat16
f32 = jnp.float32
i32 = jnp.int32

D_MODEL = 1024
N_HEADS = 8
HEAD_DIM = 64
ATTN_WIDTH = 512
ROPE_DIM = 16
ROPE_THETA = 500000.0
IDX_HEADS = 8
IDX_DIM = 32
IDX_ROPE_DIM = 8
TOPK_MAX = 256
POOL_GROUPS = 4
POOL_GROUP_DIM = 128
POOL_WIDTH = 512
POOL_WINDOWS = (2, 4, 8, 16)
POOL_HALO = 16
N_EXPERTS = 8
EPS = 1e-6

LANES = 128
INT_MIN = -(2 ** 31)
VMEM_LIMIT = 56 * 1024 * 1024
LOG2E = 1.4426950408889634
SOFTMAX_DENOM_FLOOR = 2.0 ** -90

OFF_Q = 0
OFF_K = 512
OFF_VX = 1024
OFF_QCAT = 2048
OFF_KW = 3072
OFF_U = 3200
OFF_GA = 3712
OFF_GP = 4736
W_COLS = 5760

ROUTE_W1, ROUTE_W2, ROUTE_I1, ROUTE_I2 = 8, 9, 10, 11

TM_PROJ = 512
TM_MERGE = 512
TM_RANK = 512
TM_EXPERT = 512
TF_EXPERT = 896
TM_ROWS = 256
ROW_UNROLL = 8
QB = 256
COUNT_ROWS = 128
SEARCH_FIXED_BITS = 15
SEARCH_GROUP_BITS = 4
TK = 512


def _sigmoid(x):
    return 1.0 / (1.0 + jnp.exp(-x))


def _in_proj_kernel(x_ref, g_ref, w_ref, qn_ref, kn_ref, bd_ref, one_ref, rqk_ref, rqi_ref, rki_ref,
                    q_ref, k_ref, vx_ref, qc_ref, kw_ref, u_ref, ga_ref, gp_ref):
    x = x_ref[...]
    ms = jnp.mean(x * x, axis=-1, keepdims=True)
    h = (x * lax.rsqrt(ms + EPS) * g_ref[...]).astype(bf16)

    def proj(lo, n):
        return jnp.dot(h, w_ref[:, lo:lo + n], preferred_element_type=f32)

    def rope(xc, tab_ref, sh):
        return (xc * tab_ref[0] + pltpu.roll(xc, LANES - sh, 1) * tab_ref[1]
                + pltpu.roll(xc, sh, 1) * tab_ref[2])

    def headnorm(z, gain_ref):
        msh = jnp.dot((z * z).astype(bf16), bd_ref[...], preferred_element_type=f32)
        return z * lax.rsqrt(msh + EPS) * gain_ref[...]

    zq = headnorm(proj(OFF_Q, ATTN_WIDTH), qn_ref) * (HEAD_DIM ** -0.5 * LOG2E)
    zk = headnorm(proj(OFF_K, ATTN_WIDTH), kn_ref)
    for c in range(ATTN_WIDTH // LANES):
        sl = slice(c * LANES, (c + 1) * LANES)
        q_ref[:, sl] = rope(zq[:, sl], rqk_ref, ROPE_DIM // 2).astype(bf16)
        k_ref[:, sl] = rope(zk[:, sl], rqk_ref, ROPE_DIM // 2).astype(bf16)
    vx_ref[...] = jnp.where(one_ref[...] > 0.0, 1.0, proj(OFF_VX, N_HEADS * LANES)).astype(bf16)

    lane = lax.broadcasted_iota(i32, (1, LANES), 1)
    zc = proj(OFF_QCAT, IDX_HEADS * LANES)
    for c in range(IDX_HEADS):
        sl = slice(c * LANES, (c + 1) * LANES)
        r = rope(zc[:, sl], rqi_ref, IDX_ROPE_DIM // 2)
        lo = r - r.astype(bf16).astype(f32)
        qc_ref[:, sl] = jnp.where((lane >= 32) & (lane < 64), lo, r).astype(bf16)

    zkw = rope(proj(OFF_KW, LANES), rki_ref, IDX_ROPE_DIM // 2)
    lo = zkw - zkw.astype(bf16).astype(f32)
    zkw = jnp.where((lane >= 64) & (lane < 96), lo, zkw)
    kw_ref[...] = jnp.where(lane >= 96, zkw * ((IDX_HEADS * IDX_DIM) ** -0.5), zkw)

    u_ref[...] = proj(OFF_U, POOL_WIDTH)
    ga_ref[...] = proj(OFF_GA, D_MODEL)
    gp_ref[...] = proj(OFF_GP, D_MODEL)


def _in_proj(x2d, g, w, qn, kn, bd, ones_lane, rqk, rqi, rki, seq):
    t = x2d.shape[0]
    tm = TM_PROJ
    ns = seq // tm
    row = lambda i: (i, 0)
    const = lambda i: (0, 0)
    tab = lambda i: (0, i % ns, 0)
    out_shapes = (
        jax.ShapeDtypeStruct((t, ATTN_WIDTH), bf16),
        jax.ShapeDtypeStruct((t, ATTN_WIDTH), bf16),
        jax.ShapeDtypeStruct((t, N_HEADS * LANES), bf16),
        jax.ShapeDtypeStruct((t, IDX_HEADS * LANES), bf16),
        jax.ShapeDtypeStruct((t, LANES), f32),
        jax.ShapeDtypeStruct((t, POOL_WIDTH), f32),
        jax.ShapeDtypeStruct((t, D_MODEL), f32),
        jax.ShapeDtypeStruct((t, D_MODEL), f32),
    )
    return pl.pallas_call(
        _in_proj_kernel,
        grid=(t // tm,),
        in_specs=[
            pl.BlockSpec((tm, D_MODEL), row),
            pl.BlockSpec((1, D_MODEL), const),
            pl.BlockSpec((D_MODEL, W_COLS), const, pipeline_mode=pl.Buffered(1)),
            pl.BlockSpec((1, ATTN_WIDTH), const),
            pl.BlockSpec((1, ATTN_WIDTH), const),
            pl.BlockSpec((ATTN_WIDTH, ATTN_WIDTH), const),
            pl.BlockSpec((1, N_HEADS * LANES), const),
            pl.BlockSpec((3, tm, LANES), tab),
            pl.BlockSpec((3, tm, LANES), tab),
            pl.BlockSpec((3, tm, LANES), tab),
        ],
        out_specs=[pl.BlockSpec((tm, s.shape[1]), row) for s in out_shapes],
        out_shape=out_shapes,
        compiler_params=pltpu.CompilerParams(
            dimension_semantics=("arbitrary",), vmem_limit_bytes=VMEM_LIMIT),
        name="in_proj",
    )(x2d, g, w, qn, kn, bd, ones_lane, rqk, rqi, rki)


def _lane_fold(x, op):
    out = x[:, 0:LANES]
    for j in range(1, x.shape[1] // LANES):
        out = op(out, x[:, j * LANES:(j + 1) * LANES])
    return out


def _dsa_kernel(topk, q_ref, qc_ref, kwq_ref, k_ref, vx_ref, kw_ref, o_ref,
                key_sc, kcat_sc, qm2_sc, mrun_sc, m_sc, acc_sc, prefix_sc, cnt_sc, kmax_sm):
    i = pl.program_id(1)
    nk = (i * QB + QB + TK - 1) // TK
    kf = float(topk)

    lane = lax.broadcasted_iota(i32, (1, LANES), 1)

    @pl.when(i == 0)
    def _():
        kcat_sc[...] = kw_ref[...].astype(bf16)
        for pr in range(N_HEADS // 2):
            def norms(c, carry):
                kk = k_ref[pl.ds(pl.multiple_of(c * TK, TK), TK), pr * LANES:(pr + 1) * LANES].astype(f32)
                sq = kk * kk
                n0 = jnp.sqrt(jnp.sum(jnp.where(lane < HEAD_DIM, sq, 0.0), axis=1, keepdims=True))
                n1 = jnp.sqrt(jnp.sum(jnp.where(lane >= HEAD_DIM, sq, 0.0), axis=1, keepdims=True))
                return jnp.maximum(carry[0], jnp.max(n0)), jnp.maximum(carry[1], jnp.max(n1))
            k0, k1 = lax.fori_loop(0, k_ref.shape[0] // TK, norms, (jnp.float32(0.0), jnp.float32(0.0)))
            kmax_sm[2 * pr] = k0
            kmax_sm[2 * pr + 1] = k1

    qpos = i * QB + lax.broadcasted_iota(i32, (QB, 1), 0)
    lane_tk = lax.broadcasted_iota(i32, (1, TK), 1)
    wq = kwq_ref[...]

    def to_key(score):
        bits = pltpu.bitcast(score, i32)
        return bits ^ ((bits >> 31) & 0x7FFFFFFF)

    def score_chunk(c, carry):
        koff = pl.multiple_of(c * TK, TK)
        kc = kcat_sc[pl.ds(koff, TK), :]
        acc = jnp.zeros((QB, TK), f32)
        for h in range(IDX_HEADS):
            d = lax.dot_general(qc_ref[:, h * LANES:(h + 1) * LANES], kc,
                                (((1,), (1,)), ((), ())), preferred_element_type=f32)
            acc = acc + jnp.maximum(d, 0.0) * wq[:, 96 + h:97 + h]
        causal = (koff + lane_tk) <= qpos
        key_sc[c] = jnp.where(causal, to_key(acc), INT_MIN)
        return carry

    lax.fori_loop(0, nk, score_chunk, 0)

    def count_ge(cand):
        counts = []
        for r0 in range(0, QB, COUNT_ROWS):
            cand_g = cand[r0:r0 + COUNT_ROWS]

            def body(c, part, r0=r0, cand_g=cand_g):
                for j in range(TK // LANES):
                    kk = key_sc[c, r0:r0 + COUNT_ROWS, j * LANES:(j + 1) * LANES]
                    part = part + jnp.where(kk >= cand_g, 1, 0)
                return part

            counts.append(lax.fori_loop(0, nk, body, jnp.zeros((COUNT_ROWS, LANES), i32)))
        total = jnp.sum(jnp.concatenate(counts, axis=0).astype(f32), axis=1, keepdims=True)
        return jnp.broadcast_to(total, (QB, LANES))

    zero = jnp.zeros((QB, LANES), i32)
    cnt0 = count_ge(zero)
    n_causal = jnp.broadcast_to((qpos + 1).astype(f32), (QB, LANES))
    prefix_sc[...] = jnp.where(cnt0 >= kf, zero, INT_MIN)
    cnt_sc[...] = jnp.where(cnt0 >= kf, cnt0, n_causal)

    def run_bits(first, count):
        def bit_body(b, carry):
            prefix, cnt_prefix = carry
            cand = prefix | jnp.left_shift(1, 30 - b)
            cnt = count_ge(cand)
            up = cnt >= kf
            return jnp.where(up, cand, prefix), jnp.where(up, cnt, cnt_prefix)
        prefix, cnt_prefix = lax.fori_loop(first, first + count, bit_body, (prefix_sc[...], cnt_sc[...]))
        prefix_sc[...] = prefix
        cnt_sc[...] = cnt_prefix

    run_bits(0, SEARCH_FIXED_BITS)
    for first in range(SEARCH_FIXED_BITS, 31, SEARCH_GROUP_BITS):
        @pl.when(jnp.max(jnp.where(cnt_sc[...] > kf, 1.0, 0.0)) > 0.0)
        def _(first=first):
            run_bits(first, min(SEARCH_GROUP_BITS, 31 - first))

    thr_b = jnp.maximum(prefix_sc[...], INT_MIN + 1)
    thr = thr_b[:, 0:1]

    is_tie = (cnt_sc[:, 0:1] > kf) & (prefix_sc[:, 0:1] > INT_MIN)

    @pl.when(jnp.max(jnp.where(is_tie, 1.0, 0.0)) > 0.0)
    def _():
        need = kf - count_ge(thr_b + 1)[:, 0:1]

        def count_eq_below(m):
            def body(c, part):
                hit = (key_sc[c] == thr) & ((c * TK + lane_tk) < m)
                return part + _lane_fold(jnp.where(hit, 1.0, 0.0), jnp.add)
            part = lax.fori_loop(0, nk, body, jnp.zeros((QB, LANES), f32))
            return jnp.sum(part, axis=1, keepdims=True)

        mprime = jnp.zeros((QB, 1), i32)
        bit = key_sc.shape[0] * TK // 2
        while bit >= 1:
            cand = mprime | bit
            mprime = jnp.where(count_eq_below(cand) < need, cand, mprime)
            bit //= 2

        def drop_ties(c, carry):
            key = key_sc[c]
            drop = is_tie & (key == thr) & ((c * TK + lane_tk) > mprime)
            key_sc[c] = jnp.where(drop, thr - 1, key)
            return carry

        lax.fori_loop(0, nk, drop_ties, 0)

    row2 = lax.broadcasted_iota(i32, (2 * QB, 1), 0)
    for pr in range(N_HEADS // 2):
        qp = q_ref[:, pr * LANES:(pr + 1) * LANES]
        qm2_sc[pr, 0:QB, :] = jnp.where(lane < HEAD_DIM, qp, jnp.zeros((), bf16))
        qm2_sc[pr, QB:2 * QB, :] = jnp.where(lane >= HEAD_DIM, qp, jnp.zeros((), bf16))
        q2 = qm2_sc[pr].astype(f32)
        qn = jnp.sqrt(jnp.sum(q2 * q2, axis=1, keepdims=True))
        m_sc[pr] = qn * jnp.where(row2 < QB, kmax_sm[2 * pr], kmax_sm[2 * pr + 1])
    acc_sc[...] = jnp.zeros(acc_sc.shape, f32)

    def selected_bias(c):
        thr_t = jnp.maximum(prefix_sc[...], INT_MIN + 1)
        return jnp.concatenate(
            [jnp.where(key_sc[c, :, j * LANES:(j + 1) * LANES] >= thr_t, 0.0, -jnp.inf)
             for j in range(TK // LANES)], axis=1)

    def pair_logits(c, pr, bias):
        koff = pl.multiple_of(c * TK, TK)
        kc = k_ref[pl.ds(koff, TK), pr * LANES:(pr + 1) * LANES]
        s2 = lax.dot_general(qm2_sc[pr], kc, (((1,), (1,)), ((), ())), preferred_element_type=f32)
        return s2[0:QB] + bias, s2[QB:2 * QB] + bias

    def max_chunk(c, carry):
        bias = selected_bias(c)
        for pr in range(N_HEADS // 2):
            sa, sb = pair_logits(c, pr, bias)
            mrun_sc[pr, 0:QB, :] = jnp.maximum(mrun_sc[pr, 0:QB, :], _lane_fold(sa, jnp.maximum))
            mrun_sc[pr, QB:2 * QB, :] = jnp.maximum(mrun_sc[pr, QB:2 * QB, :], _lane_fold(sb, jnp.maximum))
        return carry

    def acc_chunk(c, carry):
        koff = pl.multiple_of(c * TK, TK)
        bias = selected_bias(c)
        for pr in range(N_HEADS // 2):
            sa, sb = pair_logits(c, pr, bias)
            m = m_sc[pr]
            pa = jnp.exp2(sa - m[0:QB]).astype(bf16)
            pb = jnp.exp2(sb - m[QB:2 * QB]).astype(bf16)
            va = vx_ref[pl.ds(koff, TK), (2 * pr) * LANES:(2 * pr + 1) * LANES]
            vb = vx_ref[pl.ds(koff, TK), (2 * pr + 1) * LANES:(2 * pr + 2) * LANES]
            acc_sc[2 * pr] += jnp.dot(pa, va, preferred_element_type=f32)
            acc_sc[2 * pr + 1] += jnp.dot(pb, vb, preferred_element_type=f32)
        return carry

    lax.fori_loop(0, nk, acc_chunk, 0)

    lmin = jnp.float32(jnp.inf)
    for pr in range(N_HEADS // 2):
        lmin = jnp.minimum(lmin, jnp.min(acc_sc[2 * pr][:, HEAD_DIM:HEAD_DIM + 1]))
        lmin = jnp.minimum(lmin, jnp.min(acc_sc[2 * pr + 1][:, 0:1]))

    @pl.when(jnp.logical_not(lmin >= SOFTMAX_DENOM_FLOOR))
    def _():
        mrun_sc[...] = jnp.full(mrun_sc.shape, -jnp.inf, f32)
        lax.fori_loop(0, nk, max_chunk, 0)
        for pr in range(N_HEADS // 2):
            m = jnp.max(mrun_sc[pr], axis=1, keepdims=True)
            m_sc[pr] = jnp.where(m == -jnp.inf, 0.0, m)
        acc_sc[...] = jnp.zeros(acc_sc.shape, f32)
        lax.fori_loop(0, nk, acc_chunk, 0)

    for pr in range(N_HEADS // 2):
        a0 = acc_sc[2 * pr]
        a1 = acc_sc[2 * pr + 1]
        o0 = a0 / a0[:, HEAD_DIM:HEAD_DIM + 1]
        o1 = a1 / a1[:, 0:1]
        o_ref[:, pr * LANES:(pr + 1) * LANES] = jnp.where(lane < HEAD_DIM, o0, o1).astype(bf16)


def _dsa_attention(q, qc, kw, k, vx, batch, seq):
    t = q.shape[0]
    nq = seq // QB
    topk = min(TOPK_MAX, seq // 4)
    qrow = lambda b, i: (b * nq + i, 0)
    per_batch = lambda b, i: (b, 0)
    return pl.pallas_call(
        functools.partial(_dsa_kernel, topk),
        grid=(batch, nq),
        in_specs=[
            pl.BlockSpec((QB, ATTN_WIDTH), qrow),
            pl.BlockSpec((QB, IDX_HEADS * LANES), qrow),
            pl.BlockSpec((QB, LANES), qrow),
            pl.BlockSpec((seq, ATTN_WIDTH), per_batch, pipeline_mode=pl.Buffered(1)),
            pl.BlockSpec((seq, N_HEADS * LANES), per_batch, pipeline_mode=pl.Buffered(1)),
            pl.BlockSpec((seq, LANES), per_batch, pipeline_mode=pl.Buffered(1)),
        ],
        out_specs=pl.BlockSpec((QB, ATTN_WIDTH), qrow),
        out_shape=jax.ShapeDtypeStruct((t, ATTN_WIDTH), bf16),
        scratch_shapes=[
            pltpu.VMEM((seq // TK, QB, TK), i32),
            pltpu.VMEM((seq, LANES), bf16),
            pltpu.VMEM((N_HEADS // 2, 2 * QB, LANES), bf16),
            pltpu.VMEM((N_HEADS // 2, 2 * QB, LANES), f32),
            pltpu.VMEM((N_HEADS // 2, 2 * QB, 1), f32),
            pltpu.VMEM((N_HEADS, QB, LANES), f32),
            pltpu.VMEM((QB, LANES), i32),
            pltpu.VMEM((QB, LANES), f32),
            pltpu.SMEM((N_HEADS,), f32),
        ],
        compiler_params=pltpu.CompilerParams(
            dimension_semantics=("arbitrary", "arbitrary"), vmem_limit_bytes=VMEM_LIMIT),
        name="dsa_attention",
    )(q, qc, kw, k, vx, kw)


def _merge_kernel(moe, seq, *refs):
    if moe:
        (x_ref, a_ref, u_ref, uh_ref, ga_ref, gp_ref, wp_ref, ps_ref, pa_ref, pb_ref, wo_ref, g2_ref,
         wrh_ref, wrl_ref, x2_ref, h2_ref, gate_ref, e_sc) = refs
    else:
        (x_ref, a_ref, u_ref, uh_ref, ga_ref, gp_ref, wp_ref, ps_ref, pa_ref, pb_ref, wo_ref, g2_ref,
         x2_ref, h2_ref, e_sc) = refs
    tm = TM_MERGE
    i = pl.program_id(0)
    ti = i % (seq // tm)
    e_sc[0:POOL_HALO, :] = jnp.where(ti == 0, 0.0, uh_ref[...])
    e_sc[POOL_HALO:POOL_HALO + tm, :] = u_ref[...]
    npos = (ti * tm + 1 + lax.broadcasted_iota(i32, (tm, 1), 0)).astype(f32)

    parts = []
    for g, w in enumerate(POOL_WINDOWS):
        sl = slice(g * POOL_GROUP_DIM, (g + 1) * POOL_GROUP_DIM)
        tot = e_sc[POOL_HALO:POOL_HALO + tm, sl]
        for j in range(1, w):
            tot = tot + e_sc[POOL_HALO - j:POOL_HALO - j + tm, sl]
        diff = tot / jnp.minimum(npos, float(w)) - u_ref[:, sl]
        parts.append(jnp.dot(diff.astype(bf16), wp_ref[g], preferred_element_type=f32))
    p = jnp.concatenate(parts, axis=1) * ps_ref[...]

    ab = jnp.dot(a_ref[...], pa_ref[...], preferred_element_type=f32)
    pb = jnp.dot(p.astype(bf16), pb_ref[...], preferred_element_type=f32)
    merged = _sigmoid(ga_ref[...]) * ab + _sigmoid(gp_ref[...]) * pb
    x2 = x_ref[...] + jnp.dot(merged.astype(bf16), wo_ref[...], preferred_element_type=f32)
    x2_ref[...] = x2
    ms = jnp.mean(x2 * x2, axis=-1, keepdims=True)
    hf = x2 * lax.rsqrt(ms + EPS) * g2_ref[...]
    h2_ref[...] = hf.astype(h2_ref.dtype)

    if moe:
        hi = hf.astype(bf16)
        lo = (hf - hi.astype(f32)).astype(bf16)
        logits = (jnp.dot(hi, wrh_ref[...], preferred_element_type=f32)
                  + jnp.dot(lo, wrh_ref[...], preferred_element_type=f32)
                  + jnp.dot(hi, wrl_ref[...], preferred_element_type=f32))
        lanef = lax.broadcasted_iota(i32, (1, LANES), 1).astype(f32)
        lg = jnp.where(lanef < N_EXPERTS, logits, -jnp.inf)
        v1 = jnp.max(lg, axis=1, keepdims=True)
        i1 = jnp.min(jnp.where(lg == v1, lanef, float(LANES)), axis=1, keepdims=True)
        lg2 = jnp.where(lanef == i1, -jnp.inf, lg)
        v2 = jnp.max(lg2, axis=1, keepdims=True)
        i2 = jnp.min(jnp.where(lg2 == v2, lanef, float(LANES)), axis=1, keepdims=True)
        tt = jnp.exp(v2 - v1)
        w1 = 1.0 / (1.0 + tt)
        w2 = tt / (1.0 + tt)
        route = jnp.where((lanef == i1) | (lanef == i2), 1.0, 0.0)
        for ln, val in ((ROUTE_W1, w1), (ROUTE_W2, w2), (ROUTE_I1, i1), (ROUTE_I2, i2)):
            route = jnp.where(lanef == float(ln), val, route)
        gate_ref[...] = route


def _merge(moe, seq, x2d, a, u, ga, gp, wp, ps, pa, pb, wo, g2, wrh=None, wrl=None):
    t = x2d.shape[0]
    tm = TM_MERGE
    row = lambda i: (i, 0)
    const2 = lambda i: (0, 0)
    const3 = lambda i: (0, 0, 0)
    halo = lambda i: (jnp.maximum(i * (tm // POOL_HALO) - 1, 0), 0)
    in_specs = [
        pl.BlockSpec((tm, D_MODEL), row),
        pl.BlockSpec((tm, ATTN_WIDTH), row),
        pl.BlockSpec((tm, POOL_WIDTH), row),
        pl.BlockSpec((POOL_HALO, POOL_WIDTH), halo),
        pl.BlockSpec((tm, D_MODEL), row),
        pl.BlockSpec((tm, D_MODEL), row),
        pl.BlockSpec((POOL_GROUPS, POOL_GROUP_DIM, POOL_GROUP_DIM), const3),
        pl.BlockSpec((1, POOL_WIDTH), const2),
        pl.BlockSpec((ATTN_WIDTH, D_MODEL), const2),
        pl.BlockSpec((POOL_WIDTH, D_MODEL), const2),
        pl.BlockSpec((D_MODEL, D_MODEL), const2),
        pl.BlockSpec((1, D_MODEL), const2),
    ]
    args = [x2d, a, u, u, ga, gp, wp, ps, pa, pb, wo, g2]
    out_shapes = [jax.ShapeDtypeStruct((t, D_MODEL), f32), jax.ShapeDtypeStruct((t, D_MODEL), f32 if moe else bf16)]
    out_specs = [pl.BlockSpec((tm, D_MODEL), row), pl.BlockSpec((tm, D_MODEL), row)]
    if moe:
        in_specs += [pl.BlockSpec((D_MODEL, LANES), const2), pl.BlockSpec((D_MODEL, LANES), const2)]
        args += [wrh, wrl]
        out_shapes.append(jax.ShapeDtypeStruct((t, LANES), f32))
        out_specs.append(pl.BlockSpec((tm, LANES), row))
    return pl.pallas_call(
        functools.partial(_merge_kernel, moe, seq),
        grid=(t // tm,),
        in_specs=in_specs,
        out_specs=out_specs,
        out_shape=out_shapes,
        scratch_shapes=[pltpu.VMEM((tm + POOL_HALO, POOL_WIDTH), f32)],
        compiler_params=pltpu.CompilerParams(
            dimension_semantics=("arbitrary",), vmem_limit_bytes=VMEM_LIMIT),
        name="merge_moe" if moe else "merge_dense",
    )(*args)


def _swiglu_partial(h, w1, w3, w2):
    a = jnp.dot(h, w1, preferred_element_type=f32)
    b = jnp.dot(h, w3, preferred_element_type=f32)
    act = (a * _sigmoid(a) * b).astype(bf16)
    return jnp.dot(act, w2, preferred_element_type=f32)


def _ffn_kernel(h_ref, x_ref, w1_ref, w3_ref, w2_ref, o_ref):
    @pl.when(pl.program_id(1) == 0)
    def _():
        o_ref[...] = x_ref[...]

    o_ref[...] += _swiglu_partial(h_ref[...], w1_ref[...], w3_ref[...], w2_ref[...])


def _ffn(h2, x2, w1, w3, w2, tm, tf):
    t = h2.shape[0]
    ff = w1.shape[1]
    row = lambda i, j: (i, 0)
    return pl.pallas_call(
        _ffn_kernel,
        grid=(t // tm, ff // tf),
        in_specs=[
            pl.BlockSpec((tm, D_MODEL), row),
            pl.BlockSpec((tm, D_MODEL), row),
            pl.BlockSpec((D_MODEL, tf), lambda i, j: (0, j)),
            pl.BlockSpec((D_MODEL, tf), lambda i, j: (0, j)),
            pl.BlockSpec((tf, D_MODEL), lambda i, j: (j, 0)),
        ],
        out_specs=pl.BlockSpec((tm, D_MODEL), row),
        out_shape=jax.ShapeDtypeStruct((t, D_MODEL), f32),
        compiler_params=pltpu.CompilerParams(
            dimension_semantics=("arbitrary", "arbitrary"), vmem_limit_bytes=VMEM_LIMIT),
        name="ffn_dense",
    )(h2, x2, w1, w3, w2)


def _rank_kernel(route_ref, tri_ref, rk_ref, cnt_ref, carry_sc):
    @pl.when(pl.program_id(0) == 0)
    def _():
        carry_sc[...] = jnp.zeros(carry_sc.shape, f32)

    lane = lax.broadcasted_iota(i32, (1, LANES), 1)
    lanef = lane.astype(f32)
    r = route_ref[...]
    sel = jnp.where(lane < N_EXPERTS, r, 0.0)
    ranks = jnp.dot(tri_ref[...], sel.astype(bf16), preferred_element_type=f32) + carry_sc[...]
    rk1 = jnp.sum(jnp.where(lanef == r[:, ROUTE_I1:ROUTE_I1 + 1], ranks, 0.0), axis=1, keepdims=True)
    rk2 = jnp.sum(jnp.where(lanef == r[:, ROUTE_I2:ROUTE_I2 + 1], ranks, 0.0), axis=1, keepdims=True)
    rk_ref[...] = jnp.where(lane == 0, rk1, jnp.where(lane == 1, rk2, 0.0))
    carry_sc[...] += jnp.sum(sel, axis=0, keepdims=True)
    cnt_ref[...] = carry_sc[...]


def _moe_rank(route):
    t = route.shape[0]
    tm = TM_RANK
    idx = jnp.arange(tm)
    tri = (idx[None, :] < idx[:, None]).astype(bf16)
    return pl.pallas_call(
        _rank_kernel,
        grid=(t // tm,),
        in_specs=[pl.BlockSpec((tm, LANES), lambda i: (i, 0)), pl.BlockSpec((tm, tm), lambda i: (0, 0))],
        out_specs=[pl.BlockSpec((tm, LANES), lambda i: (i, 0)), pl.BlockSpec((1, LANES), lambda i: (0, 0))],
        out_shape=[jax.ShapeDtypeStruct((t, LANES), f32), jax.ShapeDtypeStruct((1, LANES), f32)],
        scratch_shapes=[pltpu.VMEM((1, LANES), f32)],
        compiler_params=pltpu.CompilerParams(dimension_semantics=("arbitrary",)),
        name="moe_rank",
    )(route, tri)


def _row_copy(src, dst, sem):
    return pltpu.make_async_copy(src, dst, sem)


def _moe_ffn_kernel(pos_ref, te_ref, meta_ref, h_hbm, w1_ref, w3_ref, w2_ref, ys_ref,
                    inv_sm, gbuf, xb, sem):
    del te_ref
    i = pl.program_id(0)
    j = pl.program_id(1)
    n_used = meta_ref[0]
    n_slots = pos_ref.shape[0]

    def start_gather(tile, slot):
        def issue(g, carry):
            for u in range(ROW_UNROLL):
                r = g * ROW_UNROLL + u
                tok = inv_sm[tile * TM_EXPERT + r]
                _row_copy(h_hbm.at[pl.ds(tok, 1)], gbuf.at[slot, pl.ds(r, 1)],
                          sem.at[slot]).start(priority=u % 2)
            return carry
        lax.fori_loop(0, TM_EXPERT // ROW_UNROLL, issue, 0)

    def wait_gather(slot):
        _row_copy(h_hbm.at[pl.ds(0, TM_EXPERT)], gbuf.at[slot], sem.at[slot]).wait()

    @pl.when((i == 0) & (j == 0))
    def _():
        for e in range(N_EXPERTS):
            def pad(r, carry):
                inv_sm[r] = 0
                return carry
            lax.fori_loop(meta_ref[1 + e], meta_ref[1 + N_EXPERTS + e], pad, 0)

        def scatter(g, carry):
            for u in range(ROW_UNROLL):
                n = g * ROW_UNROLL + u
                inv_sm[pos_ref[n]] = lax.shift_right_logical(n, 1)
            return carry
        lax.fori_loop(0, n_slots // ROW_UNROLL, scatter, 0)

        @pl.when(n_used > 0)
        def _():
            start_gather(0, 0)

    used = i < n_used

    @pl.when(used & (j == 0))
    def _():
        slot = i % 2
        wait_gather(slot)

        @pl.when(i + 1 < n_used)
        def _():
            start_gather(i + 1, 1 - slot)

        xb[...] = gbuf[slot].astype(bf16)

    @pl.when(used)
    def _():
        y = _swiglu_partial(xb[...], w1_ref[0], w3_ref[0], w2_ref[0])

        @pl.when(j == 0)
        def _():
            ys_ref[...] = y

        @pl.when(j > 0)
        def _():
            ys_ref[...] += y

    @pl.when(jnp.logical_not(used) & (j == 0))
    def _():
        ys_ref[...] = jnp.zeros(ys_ref.shape, f32)


def _moe_ffn(pos, tile_expert, meta, hf, w1, w3, w2, n_rows):
    ff = w1.shape[2]
    nj = ff // TF_EXPERT
    jj = lambda i, j, meta: jnp.where(i < meta[0], j, nj - 1)
    return pl.pallas_call(
        _moe_ffn_kernel,
        grid_spec=pltpu.PrefetchScalarGridSpec(
            num_scalar_prefetch=3,
            grid=(n_rows // TM_EXPERT, nj),
            in_specs=[
                pl.BlockSpec(memory_space=pl.ANY),
                pl.BlockSpec((1, D_MODEL, TF_EXPERT), lambda i, j, pos, te, meta: (te[i], 0, jj(i, j, meta))),
                pl.BlockSpec((1, D_MODEL, TF_EXPERT), lambda i, j, pos, te, meta: (te[i], 0, jj(i, j, meta))),
                pl.BlockSpec((1, TF_EXPERT, D_MODEL), lambda i, j, pos, te, meta: (te[i], jj(i, j, meta), 0)),
            ],
            out_specs=pl.BlockSpec((TM_EXPERT, D_MODEL), lambda i, j, pos, te, meta: (i, 0)),
            scratch_shapes=[
                pltpu.SMEM((n_rows,), i32),
                pltpu.VMEM((2, TM_EXPERT, D_MODEL), f32),
                pltpu.VMEM((TM_EXPERT, D_MODEL), bf16),
                pltpu.SemaphoreType.DMA((2,)),
            ],
        ),
        out_shape=jax.ShapeDtypeStruct((n_rows, D_MODEL), f32),
        compiler_params=pltpu.CompilerParams(
            dimension_semantics=("arbitrary", "arbitrary"), vmem_limit_bytes=VMEM_LIMIT),
        name="moe_ffn",
    )(pos, tile_expert, meta, hf, w1, w3, w2)


def _combine_kernel(pos_ref, ys_hbm, x2_ref, route_ref, o_ref, buf, sem):
    base = pl.program_id(0) * TM_ROWS

    def issue(g, carry):
        for u in range(ROW_UNROLL // 2):
            r = g * (ROW_UNROLL // 2) + u
            for s in range(2):
                _row_copy(ys_hbm.at[pl.ds(pos_ref[2 * (base + r) + s], 1)], buf.at[s, pl.ds(r, 1)],
                          sem).start(priority=s)
        return carry

    lax.fori_loop(0, TM_ROWS // (ROW_UNROLL // 2), issue, 0)
    for s in range(2):
        _row_copy(ys_hbm.at[pl.ds(0, TM_ROWS)], buf.at[s], sem).wait()
    route = route_ref[...]
    o_ref[...] = (x2_ref[...] + route[:, ROUTE_W1:ROUTE_W1 + 1] * buf[0]
                  + route[:, ROUTE_W2:ROUTE_W2 + 1] * buf[1])


def _moe_combine(pos, ys, x2, route):
    t = x2.shape[0]
    row = lambda i, pos: (i, 0)
    return pl.pallas_call(
        _combine_kernel,
        grid_spec=pltpu.PrefetchScalarGridSpec(
            num_scalar_prefetch=1,
            grid=(t // TM_ROWS,),
            in_specs=[
                pl.BlockSpec(memory_space=pl.ANY),
                pl.BlockSpec((TM_ROWS, D_MODEL), row),
                pl.BlockSpec((TM_ROWS, LANES), row),
            ],
            out_specs=pl.BlockSpec((TM_ROWS, D_MODEL), row),
            scratch_shapes=[pltpu.VMEM((2, TM_ROWS, D_MODEL), f32), pltpu.SemaphoreType.DMA(())],
        ),
        out_shape=jax.ShapeDtypeStruct((t, D_MODEL), f32),
        compiler_params=pltpu.CompilerParams(dimension_semantics=("arbitrary",)),
        name="moe_combine",
    )(pos, ys, x2, route)


def _moe_layer(route, hf, x2, w1, w3, w2):
    t = hf.shape[0]
    n_tiles = 2 * t // TM_EXPERT + N_EXPERTS
    rk, cnt = _moe_rank(route)
    counts = cnt[0, :N_EXPERTS].astype(i32)
    padded = (counts + TM_EXPERT - 1) // TM_EXPERT * TM_EXPERT
    ends = jnp.cumsum(padded)
    starts = ends - padded
    n_used = (ends[-1] // TM_EXPERT).astype(i32)
    tile_id = jnp.arange(n_tiles, dtype=i32)
    tile_expert = jnp.minimum(jnp.sum((tile_id[:, None] * TM_EXPERT >= ends[None, :]).astype(i32), axis=1),
                              N_EXPERTS - 1)
    tile_expert = jnp.where(tile_id < n_used, tile_expert, tile_expert[jnp.maximum(n_used - 1, 0)])
    experts = jnp.arange(N_EXPERTS, dtype=f32)[None, :]
    start_of = lambda ids: jnp.sum(jnp.where(ids[:, None] == experts, starts[None, :], 0), axis=1)
    pos1 = start_of(route[:, ROUTE_I1]) + rk[:, 0].astype(i32)
    pos2 = start_of(route[:, ROUTE_I2]) + rk[:, 1].astype(i32)
    pos = jnp.stack([pos1, pos2], axis=1).reshape(2 * t).astype(i32)

    meta = jnp.concatenate([n_used.reshape(1), starts + counts, ends]).astype(i32)
    ys = _moe_ffn(pos, tile_expert, meta, hf, w1, w3, w2, n_tiles * TM_EXPERT)
    return _moe_combine(pos, ys, x2, route)


def _rope_tables(seq, rot_dim, period, active_lanes):
    half = rot_dim // 2
    inv = jnp.power(ROPE_THETA, -jnp.arange(0, rot_dim, 2, dtype=f32) / rot_dim)
    ang = jnp.arange(seq, dtype=f32)[:, None] * inv[None, :]
    cos, sin = jnp.cos(ang), jnp.sin(ang)
    ones = jnp.ones((seq, period - rot_dim), f32)
    zeros = lambda n: jnp.zeros((seq, n), f32)
    cos_p = jnp.concatenate([cos, cos, ones], axis=1)
    sin_a = jnp.concatenate([-sin, zeros(period - half)], axis=1)
    sin_b = jnp.concatenate([zeros(half), sin, zeros(period - rot_dim)], axis=1)
    reps = LANES // period
    tabs = [jnp.tile(tb, (1, reps)) for tb in (cos_p, sin_a, sin_b)]
    live = (jnp.arange(LANES) < active_lanes)[None, :]
    tabs = [jnp.where(live, tabs[0], 1.0), jnp.where(live, tabs[1], 0.0), jnp.where(live, tabs[2], 0.0)]
    return jnp.stack(tabs, axis=0)


def _relayout_w_in(w):
    d = w.shape[0]
    o = 0
    segs = {}
    for name, n in (("q", 512), ("k", 512), ("v", 512), ("qi", 256), ("ki", 32), ("wi", 8),
                    ("u", 512), ("ga", 1024), ("gp", 1024)):
        segs[name] = w[:, o:o + n]
        o += n
    qi = segs["qi"].reshape(d, IDX_HEADS, IDX_DIM)
    qcat = jnp.concatenate([qi, qi, qi, jnp.zeros_like(qi)], axis=-1).reshape(d, IDX_HEADS * LANES)
    kwc = jnp.concatenate([segs["ki"], segs["ki"], segs["ki"], segs["wi"],
                           jnp.zeros((d, LANES - 3 * IDX_DIM - IDX_HEADS), w.dtype)], axis=1)
    vh = segs["v"].reshape(d, N_HEADS // 2, 2, HEAD_DIM)
    zh = jnp.zeros_like(vh[:, :, 0])
    vx = jnp.stack([jnp.concatenate([vh[:, :, 0], zh], axis=-1), jnp.concatenate([zh, vh[:, :, 1]], axis=-1)],
                   axis=2).reshape(d, N_HEADS * LANES)
    out = jnp.concatenate([segs["q"], segs["k"], vx, qcat, kwc, segs["u"], segs["ga"], segs["gp"]], axis=1)
    return out.astype(bf16)


def kernel(x, mix_norm, w_in, q_norm, k_norm, w_pool, pool_scale, w_attn_proj, w_pool_proj, w_out, ffn_norm,
           dense_w1, dense_w3, dense_w2, moe_router, moe_w1, moe_w3, moe_w2):
    batch, seq, d = x.shape
    depth = w_in.shape[0]
    t = batch * seq
    assert d == D_MODEL and seq % TK == 0 and seq % TM_PROJ == 0 and seq % TM_MERGE == 0

    rqk = _rope_tables(seq, ROPE_DIM, HEAD_DIM, LANES)
    rqi = _rope_tables(seq, IDX_ROPE_DIM, IDX_DIM, LANES)
    rki = _rope_tables(seq, IDX_ROPE_DIM, IDX_DIM, 3 * IDX_DIM)
    head_of = jnp.arange(ATTN_WIDTH) // HEAD_DIM
    bd = jnp.where(head_of[:, None] == head_of[None, :], 1.0 / HEAD_DIM, 0.0).astype(bf16)
    lane_in_tile = jnp.arange(N_HEADS * LANES) % LANES
    odd_head = (jnp.arange(N_HEADS * LANES) // LANES) % 2 == 1
    ones_lane = (lane_in_tile == jnp.where(odd_head, 0, HEAD_DIM)).astype(f32)[None, :]

    xc = x.reshape(t, d)
    for layer in range(depth):
        w = _relayout_w_in(w_in[layer])
        qn = jnp.tile(q_norm[layer], N_HEADS)[None, :]
        kn = jnp.tile(k_norm[layer], N_HEADS)[None, :]
        q, k, vx, qc, kw, u, ga, gp = _in_proj(xc, mix_norm[layer][None, :], w, qn, kn, bd, ones_lane,
                                               rqk, rqi, rki, seq)
        a = _dsa_attention(q, qc, kw, k, vx, batch, seq)
        moe = layer % 2 == 1
        common = (xc, a, u, ga, gp, w_pool[layer].astype(bf16), pool_scale[layer][None, :],
                  w_attn_proj[layer].astype(bf16), w_pool_proj[layer].astype(bf16), w_out[layer].astype(bf16),
                  ffn_norm[layer][None, :])
        idx = layer // 2
        if moe:
            wr = jnp.pad(moe_router[idx], ((0, 0), (0, LANES - N_EXPERTS)))
            wrh = wr.astype(bf16)
            wrl = (wr - wrh.astype(f32)).astype(bf16)
            x2, hf, route = _merge(True, seq, *common, wrh, wrl)
            xc = _moe_layer(route, hf, x2, moe_w1[idx].astype(bf16), moe_w3[idx].astype(bf16),
                            moe_w2[idx].astype(bf16))
        else:
            x2, h2 = _merge(False, seq, *common)
            xc = _ffn(h2, x2, dense_w1[idx].astype(bf16), dense_w3[idx].astype(bf16),
                      dense_w2[idx].astype(bf16), tm=1024, tf=1408)
    return xc.reshape(batch, seq, d)
```

```python
import functools

import jax
import jax.numpy as jnp
from jax import lax
from jax.experimental import pallas as pl
from jax.experimental.pallas import tpu as pltpu

bf16 = jnp.bfloat16
f32 = jnp.float32
i32 = jnp.int32

D_MODEL = 1024
N_HEADS = 8
HEAD_DIM = 64
ATTN_WIDTH = 512
ROPE_DIM = 16
ROPE_THETA = 500000.0
IDX_HEADS = 8
IDX_DIM = 32
IDX_ROPE_DIM = 8
TOPK_MAX = 256
POOL_GROUPS = 4
POOL_GROUP_DIM = 128
POOL_WIDTH = 512
POOL_WINDOWS = (2, 4, 8, 16)
POOL_HALO = 16
N_EXPERTS = 8
EPS = 1e-6

LANES = 128
INT_MIN = -(2 ** 31)
VMEM_LIMIT = 56 * 1024 * 1024
LOG2E = 1.4426950408889634
SOFTMAX_DENOM_FLOOR = 2.0 ** -90

OFF_Q = 0
OFF_K = 512
OFF_VX = 1024
OFF_QCAT = 2048
OFF_KW = 3072
OFF_U = 3200
OFF_GA = 3712
OFF_GP = 4736
W_COLS = 5760

ROUTE_W1, ROUTE_W2, ROUTE_I1, ROUTE_I2 = 8, 9, 10, 11

TM_PROJ = 512
TM_MERGE = 512
TM_RANK = 512
TM_EXPERT = 512
TF_EXPERT = 1792
TM_ROWS = 256
ROW_UNROLL = 8
QB = 256
COUNT_ROWS = 128
TK = 512


def _sigmoid(x):
    return 1.0 / (1.0 + jnp.exp(-x))


def _in_proj_kernel(x_ref, g_ref, w_ref, qn_ref, kn_ref, bd_ref, one_ref, rqk_ref, rqi_ref, rki_ref,
                    q_ref, k_ref, vx_ref, qc_ref, kw_ref, u_ref, ga_ref, gp_ref):
    x = x_ref[...]
    ms = jnp.mean(x * x, axis=-1, keepdims=True)
    h = (x * lax.rsqrt(ms + EPS) * g_ref[...]).astype(bf16)

    def proj(lo, n):
        return jnp.dot(h, w_ref[:, lo:lo + n], preferred_element_type=f32)

    def rope(xc, tab_ref, sh):
        return (xc * tab_ref[0] + pltpu.roll(xc, LANES - sh, 1) * tab_ref[1]
                + pltpu.roll(xc, sh, 1) * tab_ref[2])

    def headnorm(z, gain_ref):
        msh = jnp.dot((z * z).astype(bf16), bd_ref[...], preferred_element_type=f32)
        return z * lax.rsqrt(msh + EPS) * gain_ref[...]

    zq = headnorm(proj(OFF_Q, ATTN_WIDTH), qn_ref) * (HEAD_DIM ** -0.5 * LOG2E)
    zk = headnorm(proj(OFF_K, ATTN_WIDTH), kn_ref)
    for c in range(ATTN_WIDTH // LANES):
        sl = slice(c * LANES, (c + 1) * LANES)
        q_ref[:, sl] = rope(zq[:, sl], rqk_ref, ROPE_DIM // 2).astype(bf16)
        k_ref[:, sl] = rope(zk[:, sl], rqk_ref, ROPE_DIM // 2).astype(bf16)
    vx_ref[...] = jnp.where(one_ref[...] > 0.0, 1.0, proj(OFF_VX, N_HEADS * LANES)).astype(bf16)

    lane = lax.broadcasted_iota(i32, (1, LANES), 1)
    zc = proj(OFF_QCAT, IDX_HEADS * LANES)
    for c in range(IDX_HEADS):
        sl = slice(c * LANES, (c + 1) * LANES)
        r = rope(zc[:, sl], rqi_ref, IDX_ROPE_DIM // 2)
        lo = r - r.astype(bf16).astype(f32)
        qc_ref[:, sl] = jnp.where((lane >= 32) & (lane < 64), lo, r).astype(bf16)

    zkw = rope(proj(OFF_KW, LANES), rki_ref, IDX_ROPE_DIM // 2)
    lo = zkw - zkw.astype(bf16).astype(f32)
    zkw = jnp.where((lane >= 64) & (lane < 96), lo, zkw)
    kw_ref[...] = jnp.where(lane >= 96, zkw * ((IDX_HEADS * IDX_DIM) ** -0.5), zkw)

    u_ref[...] = proj(OFF_U, POOL_WIDTH)
    ga_ref[...] = proj(OFF_GA, D_MODEL)
    gp_ref[...] = proj(OFF_GP, D_MODEL)


def _in_proj(x2d, g, w, qn, kn, bd, ones_lane, rqk, rqi, rki, seq):
    t = x2d.shape[0]
    tm = TM_PROJ
    ns = seq // tm
    row = lambda i: (i, 0)
    const = lambda i: (0, 0)
    tab = lambda i: (0, i % ns, 0)
    out_shapes = (
        jax.ShapeDtypeStruct((t, ATTN_WIDTH), bf16),
        jax.ShapeDtypeStruct((t, ATTN_WIDTH), bf16),
        jax.ShapeDtypeStruct((t, N_HEADS * LANES), bf16),
        jax.ShapeDtypeStruct((t, IDX_HEADS * LANES), bf16),
        jax.ShapeDtypeStruct((t, LANES), f32),
        jax.ShapeDtypeStruct((t, POOL_WIDTH), f32),
        jax.ShapeDtypeStruct((t, D_MODEL), f32),
        jax.ShapeDtypeStruct((t, D_MODEL), f32),
    )
    return pl.pallas_call(
        _in_proj_kernel,
        grid=(t // tm,),
        in_specs=[
            pl.BlockSpec((tm, D_MODEL), row),
            pl.BlockSpec((1, D_MODEL), const),
            pl.BlockSpec((D_MODEL, W_COLS), const, pipeline_mode=pl.Buffered(1)),
            pl.BlockSpec((1, ATTN_WIDTH), const),
            pl.BlockSpec((1, ATTN_WIDTH), const),
            pl.BlockSpec((ATTN_WIDTH, ATTN_WIDTH), const),
            pl.BlockSpec((1, N_HEADS * LANES), const),
            pl.BlockSpec((3, tm, LANES), tab),
            pl.BlockSpec((3, tm, LANES), tab),
            pl.BlockSpec((3, tm, LANES), tab),
        ],
        out_specs=[pl.BlockSpec((tm, s.shape[1]), row) for s in out_shapes],
        out_shape=out_shapes,
        compiler_params=pltpu.CompilerParams(
            dimension_semantics=("arbitrary",), vmem_limit_bytes=VMEM_LIMIT),
        name="in_proj",
    )(x2d, g, w, qn, kn, bd, ones_lane, rqk, rqi, rki)


def _lane_fold(x, op):
    out = x[:, 0:LANES]
    for j in range(1, x.shape[1] // LANES):
        out = op(out, x[:, j * LANES:(j + 1) * LANES])
    return out


def _dsa_kernel(topk, q_ref, qc_ref, kwq_ref, k_ref, vx_ref, kw_ref, o_ref,
                key_sc, kcat_sc, qm2_sc, mrun_sc, m_sc, acc_sc, kmax_sm):
    i = pl.program_id(1)
    nk = (i * QB + QB + TK - 1) // TK
    kf = float(topk)

    lane = lax.broadcasted_iota(i32, (1, LANES), 1)

    @pl.when(i == 0)
    def _():
        kcat_sc[...] = kw_ref[...].astype(bf16)
        for pr in range(N_HEADS // 2):
            def norms(c, carry):
                kk = k_ref[pl.ds(pl.multiple_of(c * TK, TK), TK), pr * LANES:(pr + 1) * LANES].astype(f32)
                sq = kk * kk
                n0 = jnp.sqrt(jnp.sum(jnp.where(lane < HEAD_DIM, sq, 0.0), axis=1, keepdims=True))
                n1 = jnp.sqrt(jnp.sum(jnp.where(lane >= HEAD_DIM, sq, 0.0), axis=1, keepdims=True))
                return jnp.maximum(carry[0], jnp.max(n0)), jnp.maximum(carry[1], jnp.max(n1))
            k0, k1 = lax.fori_loop(0, k_ref.shape[0] // TK, norms, (jnp.float32(0.0), jnp.float32(0.0)))
            kmax_sm[2 * pr] = k0
            kmax_sm[2 * pr + 1] = k1

    qpos = i * QB + lax.broadcasted_iota(i32, (QB, 1), 0)
    lane_tk = lax.broadcasted_iota(i32, (1, TK), 1)
    wq = kwq_ref[...]

    def to_key(score):
        bits = pltpu.bitcast(score, i32)
        return bits ^ ((bits >> 31) & 0x7FFFFFFF)

    def score_chunk(c, carry):
        koff = pl.multiple_of(c * TK, TK)
        kc = kcat_sc[pl.ds(koff, TK), :]
        acc = jnp.zeros((QB, TK), f32)
        for h in range(IDX_HEADS):
            d = lax.dot_general(qc_ref[:, h * LANES:(h + 1) * LANES], kc,
                                (((1,), (1,)), ((), ())), preferred_element_type=f32)
            acc = acc + jnp.maximum(d, 0.0) * wq[:, 96 + h:97 + h]
        causal = (koff + lane_tk) <= qpos
        key_sc[c] = jnp.where(causal, to_key(acc), INT_MIN)
        return carry

    lax.fori_loop(0, nk, score_chunk, 0)

    def count_ge(cand):
        counts = []
        for r0 in range(0, QB, COUNT_ROWS):
            cand_g = cand[r0:r0 + COUNT_ROWS]

            def body(c, part, r0=r0, cand_g=cand_g):
                for j in range(TK // LANES):
                    kk = key_sc[c, r0:r0 + COUNT_ROWS, j * LANES:(j + 1) * LANES]
                    part = part + jnp.where(kk >= cand_g, 1, 0)
                return part

            counts.append(lax.fori_loop(0, nk, body, jnp.zeros((COUNT_ROWS, LANES), i32)))
        return jnp.sum(jnp.concatenate(counts, axis=0).astype(f32), axis=1, keepdims=True)

    zero = jnp.zeros((QB, 1), i32)
    prefix = jnp.where(count_ge(zero) >= kf, zero, INT_MIN)

    def bit_body(b, prefix):
        cand = prefix | jnp.left_shift(1, 30 - b)
        return jnp.where(count_ge(cand) >= kf, cand, prefix)

    thr_raw = lax.fori_loop(0, 31, bit_body, prefix)
    thr = jnp.maximum(thr_raw, INT_MIN + 1)

    is_tie = (count_ge(thr) > kf) & (thr_raw > INT_MIN)

    @pl.when(jnp.max(jnp.where(is_tie, 1.0, 0.0)) > 0.0)
    def _():
        need = kf - count_ge(thr + 1)

        def count_eq_below(m):
            def body(c, part):
                hit = (key_sc[c] == thr) & ((c * TK + lane_tk) < m)
                return part + _lane_fold(jnp.where(hit, 1.0, 0.0), jnp.add)
            part = lax.fori_loop(0, nk, body, jnp.zeros((QB, LANES), f32))
            return jnp.sum(part, axis=1, keepdims=True)

        mprime = jnp.zeros((QB, 1), i32)
        bit = key_sc.shape[0] * TK // 2
        while bit >= 1:
            cand = mprime | bit
            mprime = jnp.where(count_eq_below(cand) < need, cand, mprime)
            bit //= 2

        def drop_ties(c, carry):
            key = key_sc[c]
            drop = is_tie & (key == thr) & ((c * TK + lane_tk) > mprime)
            key_sc[c] = jnp.where(drop, thr - 1, key)
            return carry

        lax.fori_loop(0, nk, drop_ties, 0)

    row2 = lax.broadcasted_iota(i32, (2 * QB, 1), 0)
    for pr in range(N_HEADS // 2):
        qp = q_ref[:, pr * LANES:(pr + 1) * LANES]
        qm2_sc[pr, 0:QB, :] = jnp.where(lane < HEAD_DIM, qp, jnp.zeros((), bf16))
        qm2_sc[pr, QB:2 * QB, :] = jnp.where(lane >= HEAD_DIM, qp, jnp.zeros((), bf16))
        q2 = qm2_sc[pr].astype(f32)
        qn = jnp.sqrt(jnp.sum(q2 * q2, axis=1, keepdims=True))
        m_sc[pr] = qn * jnp.where(row2 < QB, kmax_sm[2 * pr], kmax_sm[2 * pr + 1])
    acc_sc[...] = jnp.zeros(acc_sc.shape, f32)

    def pair_logits(c, pr, bias):
        koff = pl.multiple_of(c * TK, TK)
        kc = k_ref[pl.ds(koff, TK), pr * LANES:(pr + 1) * LANES]
        s2 = lax.dot_general(qm2_sc[pr], kc, (((1,), (1,)), ((), ())), preferred_element_type=f32)
        return s2[0:QB] + bias, s2[QB:2 * QB] + bias

    def max_chunk(c, carry):
        bias = jnp.where(key_sc[c] >= thr, 0.0, -jnp.inf)
        for pr in range(N_HEADS // 2):
            sa, sb = pair_logits(c, pr, bias)
            mrun_sc[pr, 0:QB, :] = jnp.maximum(mrun_sc[pr, 0:QB, :], _lane_fold(sa, jnp.maximum))
            mrun_sc[pr, QB:2 * QB, :] = jnp.maximum(mrun_sc[pr, QB:2 * QB, :], _lane_fold(sb, jnp.maximum))
        return carry

    def acc_chunk(c, carry):
        koff = pl.multiple_of(c * TK, TK)
        bias = jnp.where(key_sc[c] >= thr, 0.0, -jnp.inf)
        for pr in range(N_HEADS // 2):
            sa, sb = pair_logits(c, pr, bias)
            m = m_sc[pr]
            pa = jnp.exp2(sa - m[0:QB]).astype(bf16)
            pb = jnp.exp2(sb - m[QB:2 * QB]).astype(bf16)
            va = vx_ref[pl.ds(koff, TK), (2 * pr) * LANES:(2 * pr + 1) * LANES]
            vb = vx_ref[pl.ds(koff, TK), (2 * pr + 1) * LANES:(2 * pr + 2) * LANES]
            acc_sc[2 * pr] += jnp.dot(pa, va, preferred_element_type=f32)
            acc_sc[2 * pr + 1] += jnp.dot(pb, vb, preferred_element_type=f32)
        return carry

    lax.fori_loop(0, nk, acc_chunk, 0)

    lmin = jnp.float32(jnp.inf)
    for pr in range(N_HEADS // 2):
        lmin = jnp.minimum(lmin, jnp.min(acc_sc[2 * pr][:, HEAD_DIM:HEAD_DIM + 1]))
        lmin = jnp.minimum(lmin, jnp.min(acc_sc[2 * pr + 1][:, 0:1]))

    @pl.when(jnp.logical_not(lmin >= SOFTMAX_DENOM_FLOOR))
    def _():
        mrun_sc[...] = jnp.full(mrun_sc.shape, -jnp.inf, f32)
        lax.fori_loop(0, nk, max_chunk, 0)
        for pr in range(N_HEADS // 2):
            m = jnp.max(mrun_sc[pr], axis=1, keepdims=True)
            m_sc[pr] = jnp.where(m == -jnp.inf, 0.0, m)
        acc_sc[...] = jnp.zeros(acc_sc.shape, f32)
        lax.fori_loop(0, nk, acc_chunk, 0)

    for pr in range(N_HEADS // 2):
        a0 = acc_sc[2 * pr]
        a1 = acc_sc[2 * pr + 1]
        o0 = a0 / a0[:, HEAD_DIM:HEAD_DIM + 1]
        o1 = a1 / a1[:, 0:1]
        o_ref[:, pr * LANES:(pr + 1) * LANES] = jnp.where(lane < HEAD_DIM, o0, o1).astype(bf16)


def _dsa_attention(q, qc, kw, k, vx, batch, seq):
    t = q.shape[0]
    nq = seq // QB
    topk = min(TOPK_MAX, seq // 4)
    qrow = lambda b, i: (b * nq + i, 0)
    per_batch = lambda b, i: (b, 0)
    return pl.pallas_call(
        functools.partial(_dsa_kernel, topk),
        grid=(batch, nq),
        in_specs=[
            pl.BlockSpec((QB, ATTN_WIDTH), qrow),
            pl.BlockSpec((QB, IDX_HEADS * LANES), qrow),
            pl.BlockSpec((QB, LANES), qrow),
            pl.BlockSpec((seq, ATTN_WIDTH), per_batch, pipeline_mode=pl.Buffered(1)),
            pl.BlockSpec((seq, N_HEADS * LANES), per_batch, pipeline_mode=pl.Buffered(1)),
            pl.BlockSpec((seq, LANES), per_batch, pipeline_mode=pl.Buffered(1)),
        ],
        out_specs=pl.BlockSpec((QB, ATTN_WIDTH), qrow),
        out_shape=jax.ShapeDtypeStruct((t, ATTN_WIDTH), bf16),
        scratch_shapes=[
            pltpu.VMEM((seq // TK, QB, TK), i32),
            pltpu.VMEM((seq, LANES), bf16),
            pltpu.VMEM((N_HEADS // 2, 2 * QB, LANES), bf16),
            pltpu.VMEM((N_HEADS // 2, 2 * QB, LANES), f32),
            pltpu.VMEM((N_HEADS // 2, 2 * QB, 1), f32),
            pltpu.VMEM((N_HEADS, QB, LANES), f32),
            pltpu.SMEM((N_HEADS,), f32),
        ],
        compiler_params=pltpu.CompilerParams(
            dimension_semantics=("arbitrary", "arbitrary"), vmem_limit_bytes=VMEM_LIMIT),
        name="dsa_attention",
    )(q, qc, kw, k, vx, kw)


def _merge_kernel(moe, seq, *refs):
    if moe:
        (x_ref, a_ref, u_ref, uh_ref, ga_ref, gp_ref, wp_ref, ps_ref, pa_ref, pb_ref, wo_ref, g2_ref,
         wrh_ref, wrl_ref, x2_ref, h2_ref, gate_ref, e_sc) = refs
    else:
        (x_ref, a_ref, u_ref, uh_ref, ga_ref, gp_ref, wp_ref, ps_ref, pa_ref, pb_ref, wo_ref, g2_ref,
         x2_ref, h2_ref, e_sc) = refs
    tm = TM_MERGE
    i = pl.program_id(0)
    ti = i % (seq // tm)
    e_sc[0:POOL_HALO, :] = jnp.where(ti == 0, 0.0, uh_ref[...])
    e_sc[POOL_HALO:POOL_HALO + tm, :] = u_ref[...]
    npos = (ti * tm + 1 + lax.broadcasted_iota(i32, (tm, 1), 0)).astype(f32)

    parts = []
    for g, w in enumerate(POOL_WINDOWS):
        sl = slice(g * POOL_GROUP_DIM, (g + 1) * POOL_GROUP_DIM)
        tot = e_sc[POOL_HALO:POOL_HALO + tm, sl]
        for j in range(1, w):
            tot = tot + e_sc[POOL_HALO - j:POOL_HALO - j + tm, sl]
        diff = tot / jnp.minimum(npos, float(w)) - u_ref[:, sl]
        parts.append(jnp.dot(diff.astype(bf16), wp_ref[g], preferred_element_type=f32))
    p = jnp.concatenate(parts, axis=1) * ps_ref[...]

    ab = jnp.dot(a_ref[...], pa_ref[...], preferred_element_type=f32)
    pb = jnp.dot(p.astype(bf16), pb_ref[...], preferred_element_type=f32)
    merged = _sigmoid(ga_ref[...]) * ab + _sigmoid(gp_ref[...]) * pb
    x2 = x_ref[...] + jnp.dot(merged.astype(bf16), wo_ref[...], preferred_element_type=f32)
    x2_ref[...] = x2
    ms = jnp.mean(x2 * x2, axis=-1, keepdims=True)
    hf = x2 * lax.rsqrt(ms + EPS) * g2_ref[...]
    h2_ref[...] = hf.astype(h2_ref.dtype)

    if moe:
        hi = hf.astype(bf16)
        lo = (hf - hi.astype(f32)).astype(bf16)
        logits = (jnp.dot(hi, wrh_ref[...], preferred_element_type=f32)
                  + jnp.dot(lo, wrh_ref[...], preferred_element_type=f32)
                  + jnp.dot(hi, wrl_ref[...], preferred_element_type=f32))
        lanef = lax.broadcasted_iota(i32, (1, LANES), 1).astype(f32)
        lg = jnp.where(lanef < N_EXPERTS, logits, -jnp.inf)
        v1 = jnp.max(lg, axis=1, keepdims=True)
        i1 = jnp.min(jnp.where(lg == v1, lanef, float(LANES)), axis=1, keepdims=True)
        lg2 = jnp.where(lanef == i1, -jnp.inf, lg)
        v2 = jnp.max(lg2, axis=1, keepdims=True)
        i2 = jnp.min(jnp.where(lg2 == v2, lanef, float(LANES)), axis=1, keepdims=True)
        tt = jnp.exp(v2 - v1)
        w1 = 1.0 / (1.0 + tt)
        w2 = tt / (1.0 + tt)
        route = jnp.where((lanef == i1) | (lanef == i2), 1.0, 0.0)
        for ln, val in ((ROUTE_W1, w1), (ROUTE_W2, w2), (ROUTE_I1, i1), (ROUTE_I2, i2)):
            route = jnp.where(lanef == float(ln), val, route)
        gate_ref[...] = route


def _merge(moe, seq, x2d, a, u, ga, gp, wp, ps, pa, pb, wo, g2, wrh=None, wrl=None):
    t = x2d.shape[0]
    tm = TM_MERGE
    row = lambda i: (i, 0)
    const2 = lambda i: (0, 0)
    const3 = lambda i: (0, 0, 0)
    halo = lambda i: (jnp.maximum(i * (tm // POOL_HALO) - 1, 0), 0)
    in_specs = [
        pl.BlockSpec((tm, D_MODEL), row),
        pl.BlockSpec((tm, ATTN_WIDTH), row),
        pl.BlockSpec((tm, POOL_WIDTH), row),
        pl.BlockSpec((POOL_HALO, POOL_WIDTH), halo),
        pl.BlockSpec((tm, D_MODEL), row),
        pl.BlockSpec((tm, D_MODEL), row),
        pl.BlockSpec((POOL_GROUPS, POOL_GROUP_DIM, POOL_GROUP_DIM), const3),
        pl.BlockSpec((1, POOL_WIDTH), const2),
        pl.BlockSpec((ATTN_WIDTH, D_MODEL), const2),
        pl.BlockSpec((POOL_WIDTH, D_MODEL), const2),
        pl.BlockSpec((D_MODEL, D_MODEL), const2),
        pl.BlockSpec((1, D_MODEL), const2),
    ]
    args = [x2d, a, u, u, ga, gp, wp, ps, pa, pb, wo, g2]
    out_shapes = [jax.ShapeDtypeStruct((t, D_MODEL), f32), jax.ShapeDtypeStruct((t, D_MODEL), f32 if moe else bf16)]
    out_specs = [pl.BlockSpec((tm, D_MODEL), row), pl.BlockSpec((tm, D_MODEL), row)]
    if moe:
        in_specs += [pl.BlockSpec((D_MODEL, LANES), const2), pl.BlockSpec((D_MODEL, LANES), const2)]
        args += [wrh, wrl]
        out_shapes.append(jax.ShapeDtypeStruct((t, LANES), f32))
        out_specs.append(pl.BlockSpec((tm, LANES), row))
    return pl.pallas_call(
        functools.partial(_merge_kernel, moe, seq),
        grid=(t // tm,),
        in_specs=in_specs,
        out_specs=out_specs,
        out_shape=out_shapes,
        scratch_shapes=[pltpu.VMEM((tm + POOL_HALO, POOL_WIDTH), f32)],
        compiler_params=pltpu.CompilerParams(
            dimension_semantics=("arbitrary",), vmem_limit_bytes=VMEM_LIMIT),
        name="merge_moe" if moe else "merge_dense",
    )(*args)


def _swiglu_partial(h, w1, w3, w2):
    a = jnp.dot(h, w1, preferred_element_type=f32)
    b = jnp.dot(h, w3, preferred_element_type=f32)
    act = (a * _sigmoid(a) * b).astype(bf16)
    return jnp.dot(act, w2, preferred_element_type=f32)


def _ffn_kernel(h_ref, x_ref, w1_ref, w3_ref, w2_ref, o_ref):
    @pl.when(pl.program_id(1) == 0)
    def _():
        o_ref[...] = x_ref[...]

    o_ref[...] += _swiglu_partial(h_ref[...], w1_ref[...], w3_ref[...], w2_ref[...])


def _ffn(h2, x2, w1, w3, w2, tm, tf):
    t = h2.shape[0]
    ff = w1.shape[1]
    row = lambda i, j: (i, 0)
    return pl.pallas_call(
        _ffn_kernel,
        grid=(t // tm, ff // tf),
        in_specs=[
            pl.BlockSpec((tm, D_MODEL), row),
            pl.BlockSpec((tm, D_MODEL), row),
            pl.BlockSpec((D_MODEL, tf), lambda i, j: (0, j)),
            pl.BlockSpec((D_MODEL, tf), lambda i, j: (0, j)),
            pl.BlockSpec((tf, D_MODEL), lambda i, j: (j, 0)),
        ],
        out_specs=pl.BlockSpec((tm, D_MODEL), row),
        out_shape=jax.ShapeDtypeStruct((t, D_MODEL), f32),
        compiler_params=pltpu.CompilerParams(
            dimension_semantics=("arbitrary", "arbitrary"), vmem_limit_bytes=VMEM_LIMIT),
        name="ffn_dense",
    )(h2, x2, w1, w3, w2)


def _rank_kernel(route_ref, tri_ref, rk_ref, cnt_ref, carry_sc):
    @pl.when(pl.program_id(0) == 0)
    def _():
        carry_sc[...] = jnp.zeros(carry_sc.shape, f32)

    lane = lax.broadcasted_iota(i32, (1, LANES), 1)
    lanef = lane.astype(f32)
    r = route_ref[...]
    sel = jnp.where(lane < N_EXPERTS, r, 0.0)
    ranks = jnp.dot(tri_ref[...], sel.astype(bf16), preferred_element_type=f32) + carry_sc[...]
    rk1 = jnp.sum(jnp.where(lanef == r[:, ROUTE_I1:ROUTE_I1 + 1], ranks, 0.0), axis=1, keepdims=True)
    rk2 = jnp.sum(jnp.where(lanef == r[:, ROUTE_I2:ROUTE_I2 + 1], ranks, 0.0), axis=1, keepdims=True)
    rk_ref[...] = jnp.where(lane == 0, rk1, jnp.where(lane == 1, rk2, 0.0))
    carry_sc[...] += jnp.sum(sel, axis=0, keepdims=True)
    cnt_ref[...] = carry_sc[...]


def _moe_rank(route):
    t = route.shape[0]
    tm = TM_RANK
    idx = jnp.arange(tm)
    tri = (idx[None, :] < idx[:, None]).astype(bf16)
    return pl.pallas_call(
        _rank_kernel,
        grid=(t // tm,),
        in_specs=[pl.BlockSpec((tm, LANES), lambda i: (i, 0)), pl.BlockSpec((tm, tm), lambda i: (0, 0))],
        out_specs=[pl.BlockSpec((tm, LANES), lambda i: (i, 0)), pl.BlockSpec((1, LANES), lambda i: (0, 0))],
        out_shape=[jax.ShapeDtypeStruct((t, LANES), f32), jax.ShapeDtypeStruct((1, LANES), f32)],
        scratch_shapes=[pltpu.VMEM((1, LANES), f32)],
        compiler_params=pltpu.CompilerParams(dimension_semantics=("arbitrary",)),
        name="moe_rank",
    )(route, tri)


def _row_copy(src, dst, sem):
    return pltpu.make_async_copy(src, dst, sem)


def _moe_ffn_kernel(pos_ref, te_ref, meta_ref, h_hbm, w1_ref, w3_ref, w2_ref, ys_ref,
                    inv_sm, gbuf, xb, sem):
    del te_ref
    i = pl.program_id(0)
    j = pl.program_id(1)
    n_used = meta_ref[0]
    n_slots = pos_ref.shape[0]

    def start_gather(tile, slot):
        def issue(g, carry):
            for u in range(ROW_UNROLL):
                r = g * ROW_UNROLL + u
                tok = inv_sm[tile * TM_EXPERT + r]
                _row_copy(h_hbm.at[pl.ds(tok, 1)], gbuf.at[slot, pl.ds(r, 1)],
                          sem.at[slot]).start(priority=u % 2)
            return carry
        lax.fori_loop(0, TM_EXPERT // ROW_UNROLL, issue, 0)

    def wait_gather(slot):
        _row_copy(h_hbm.at[pl.ds(0, TM_EXPERT)], gbuf.at[slot], sem.at[slot]).wait()

    @pl.when((i == 0) & (j == 0))
    def _():
        for e in range(N_EXPERTS):
            def pad(r, carry):
                inv_sm[r] = 0
                return carry
            lax.fori_loop(meta_ref[1 + e], meta_ref[1 + N_EXPERTS + e], pad, 0)

        def scatter(g, carry):
            for u in range(ROW_UNROLL):
                n = g * ROW_UNROLL + u
                inv_sm[pos_ref[n]] = lax.shift_right_logical(n, 1)
            return carry
        lax.fori_loop(0, n_slots // ROW_UNROLL, scatter, 0)

        @pl.when(n_used > 0)
        def _():
            start_gather(0, 0)

    used = i < n_used

    @pl.when(used & (j == 0))
    def _():
        slot = i % 2
        wait_gather(slot)

        @pl.when(i + 1 < n_used)
        def _():
            start_gather(i + 1, 1 - slot)

        xb[...] = gbuf[slot].astype(bf16)

    @pl.when(used)
    def _():
        y = _swiglu_partial(xb[...], w1_ref[0], w3_ref[0], w2_ref[0])

        @pl.when(j == 0)
        def _():
            ys_ref[...] = y

        @pl.when(j > 0)
        def _():
            ys_ref[...] += y

    @pl.when(jnp.logical_not(used) & (j == 0))
    def _():
        ys_ref[...] = jnp.zeros(ys_ref.shape, f32)


def _moe_ffn(pos, tile_expert, meta, hf, w1, w3, w2, n_rows):
    ff = w1.shape[2]
    nj = ff // TF_EXPERT
    jj = lambda i, j, meta: jnp.where(i < meta[0], j, nj - 1)
    return pl.pallas_call(
        _moe_ffn_kernel,
        grid_spec=pltpu.PrefetchScalarGridSpec(
            num_scalar_prefetch=3,
            grid=(n_rows // TM_EXPERT, nj),
            in_specs=[
                pl.BlockSpec(memory_space=pl.ANY),
                pl.BlockSpec((1, D_MODEL, TF_EXPERT), lambda i, j, pos, te, meta: (te[i], 0, jj(i, j, meta))),
                pl.BlockSpec((1, D_MODEL, TF_EXPERT), lambda i, j, pos, te, meta: (te[i], 0, jj(i, j, meta))),
                pl.BlockSpec((1, TF_EXPERT, D_MODEL), lambda i, j, pos, te, meta: (te[i], jj(i, j, meta), 0)),
            ],
            out_specs=pl.BlockSpec((TM_EXPERT, D_MODEL), lambda i, j, pos, te, meta: (i, 0)),
            scratch_shapes=[
                pltpu.SMEM((n_rows,), i32),
                pltpu.VMEM((2, TM_EXPERT, D_MODEL), f32),
                pltpu.VMEM((TM_EXPERT, D_MODEL), bf16),
                pltpu.SemaphoreType.DMA((2,)),
            ],
        ),
        out_shape=jax.ShapeDtypeStruct((n_rows, D_MODEL), f32),
        compiler_params=pltpu.CompilerParams(
            dimension_semantics=("arbitrary", "arbitrary"), vmem_limit_bytes=VMEM_LIMIT),
        name="moe_ffn",
    )(pos, tile_expert, meta, hf, w1, w3, w2)


def _combine_kernel(pos_ref, ys_hbm, x2_ref, route_ref, o_ref, buf, sem):
    base = pl.program_id(0) * TM_ROWS

    def issue(g, carry):
        for u in range(ROW_UNROLL // 2):
            r = g * (ROW_UNROLL // 2) + u
            for s in range(2):
                _row_copy(ys_hbm.at[pl.ds(pos_ref[2 * (base + r) + s], 1)], buf.at[s, pl.ds(r, 1)],
                          sem).start(priority=s)
        return carry

    lax.fori_loop(0, TM_ROWS // (ROW_UNROLL // 2), issue, 0)
    for s in range(2):
        _row_copy(ys_hbm.at[pl.ds(0, TM_ROWS)], buf.at[s], sem).wait()
    route = route_ref[...]
    o_ref[...] = (x2_ref[...] + route[:, ROUTE_W1:ROUTE_W1 + 1] * buf[0]
                  + route[:, ROUTE_W2:ROUTE_W2 + 1] * buf[1])


def _moe_combine(pos, ys, x2, route):
    t = x2.shape[0]
    row = lambda i, pos: (i, 0)
    return pl.pallas_call(
        _combine_kernel,
        grid_spec=pltpu.PrefetchScalarGridSpec(
            num_scalar_prefetch=1,
            grid=(t // TM_ROWS,),
            in_specs=[
                pl.BlockSpec(memory_space=pl.ANY),
                pl.BlockSpec((TM_ROWS, D_MODEL), row),
                pl.BlockSpec((TM_ROWS, LANES), row),
            ],
            out_specs=pl.BlockSpec((TM_ROWS, D_MODEL), row),
            scratch_shapes=[pltpu.VMEM((2, TM_ROWS, D_MODEL), f32), pltpu.SemaphoreType.DMA(())],
        ),
        out_shape=jax.ShapeDtypeStruct((t, D_MODEL), f32),
        compiler_params=pltpu.CompilerParams(dimension_semantics=("arbitrary",)),
        name="moe_combine",
    )(pos, ys, x2, route)


def _moe_layer(route, hf, x2, w1, w3, w2):
    t = hf.shape[0]
    n_tiles = 2 * t // TM_EXPERT + N_EXPERTS
    rk, cnt = _moe_rank(route)
    counts = cnt[0, :N_EXPERTS].astype(i32)
    padded = (counts + TM_EXPERT - 1) // TM_EXPERT * TM_EXPERT
    ends = jnp.cumsum(padded)
    starts = ends - padded
    n_used = (ends[-1] // TM_EXPERT).astype(i32)
    tile_id = jnp.arange(n_tiles, dtype=i32)
    tile_expert = jnp.minimum(jnp.sum((tile_id[:, None] * TM_EXPERT >= ends[None, :]).astype(i32), axis=1),
                              N_EXPERTS - 1)
    tile_expert = jnp.where(tile_id < n_used, tile_expert, tile_expert[jnp.maximum(n_used - 1, 0)])
    experts = jnp.arange(N_EXPERTS, dtype=f32)[None, :]
    start_of = lambda ids: jnp.sum(jnp.where(ids[:, None] == experts, starts[None, :], 0), axis=1)
    pos1 = start_of(route[:, ROUTE_I1]) + rk[:, 0].astype(i32)
    pos2 = start_of(route[:, ROUTE_I2]) + rk[:, 1].astype(i32)
    pos = jnp.stack([pos1, pos2], axis=1).reshape(2 * t).astype(i32)

    meta = jnp.concatenate([n_used.reshape(1), starts + counts, ends]).astype(i32)
    ys = _moe_ffn(pos, tile_expert, meta, hf, w1, w3, w2, n_tiles * TM_EXPERT)
    return _moe_combine(pos, ys, x2, route)


def _rope_tables(seq, rot_dim, period, active_lanes):
    half = rot_dim // 2
    inv = jnp.power(ROPE_THETA, -jnp.arange(0, rot_dim, 2, dtype=f32) / rot_dim)
    ang = jnp.arange(seq, dtype=f32)[:, None] * inv[None, :]
    cos, sin = jnp.cos(ang), jnp.sin(ang)
    ones = jnp.ones((seq, period - rot_dim), f32)
    zeros = lambda n: jnp.zeros((seq, n), f32)
    cos_p = jnp.concatenate([cos, cos, ones], axis=1)
    sin_a = jnp.concatenate([-sin, zeros(period - half)], axis=1)
    sin_b = jnp.concatenate([zeros(half), sin, zeros(period - rot_dim)], axis=1)
    reps = LANES // period
    tabs = [jnp.tile(tb, (1, reps)) for tb in (cos_p, sin_a, sin_b)]
    live = (jnp.arange(LANES) < active_lanes)[None, :]
    tabs = [jnp.where(live, tabs[0], 1.0), jnp.where(live, tabs[1], 0.0), jnp.where(live, tabs[2], 0.0)]
    return jnp.stack(tabs, axis=0)


def _relayout_w_in(w):
    d = w.shape[0]
    o = 0
    segs = {}
    for name, n in (("q", 512), ("k", 512), ("v", 512), ("qi", 256), ("ki", 32), ("wi", 8),
                    ("u", 512), ("ga", 1024), ("gp", 1024)):
        segs[name] = w[:, o:o + n]
        o += n
    qi = segs["qi"].reshape(d, IDX_HEADS, IDX_DIM)
    qcat = jnp.concatenate([qi, qi, qi, jnp.zeros_like(qi)], axis=-1).reshape(d, IDX_HEADS * LANES)
    kwc = jnp.concatenate([segs["ki"], segs["ki"], segs["ki"], segs["wi"],
                           jnp.zeros((d, LANES - 3 * IDX_DIM - IDX_HEADS), w.dtype)], axis=1)
    vh = segs["v"].reshape(d, N_HEADS // 2, 2, HEAD_DIM)
    zh = jnp.zeros_like(vh[:, :, 0])
    vx = jnp.stack([jnp.concatenate([vh[:, :, 0], zh], axis=-1), jnp.concatenate([zh, vh[:, :, 1]], axis=-1)],
                   axis=2).reshape(d, N_HEADS * LANES)
    out = jnp.concatenate([segs["q"], segs["k"], vx, qcat, kwc, segs["u"], segs["ga"], segs["gp"]], axis=1)
    return out.astype(bf16)


def kernel(x, mix_norm, w_in, q_norm, k_norm, w_pool, pool_scale, w_attn_proj, w_pool_proj, w_out, ffn_norm,
           dense_w1, dense_w3, dense_w2, moe_router, moe_w1, moe_w3, moe_w2):
    batch, seq, d = x.shape
    depth = w_in.shape[0]
    t = batch * seq
    assert d == D_MODEL and seq % TK == 0 and seq % TM_PROJ == 0 and seq % TM_MERGE == 0

    rqk = _rope_tables(seq, ROPE_DIM, HEAD_DIM, LANES)
    rqi = _rope_tables(seq, IDX_ROPE_DIM, IDX_DIM, LANES)
    rki = _rope_tables(seq, IDX_ROPE_DIM, IDX_DIM, 3 * IDX_DIM)
    head_of = jnp.arange(ATTN_WIDTH) // HEAD_DIM
    bd = jnp.where(head_of[:, None] == head_of[None, :], 1.0 / HEAD_DIM, 0.0).astype(bf16)
    lane_in_tile = jnp.arange(N_HEADS * LANES) % LANES
    odd_head = (jnp.arange(N_HEADS * LANES) // LANES) % 2 == 1
    ones_lane = (lane_in_tile == jnp.where(odd_head, 0, HEAD_DIM)).astype(f32)[None, :]

    xc = x.reshape(t, d)
    for layer in range(depth):
        w = _relayout_w_in(w_in[layer])
        qn = jnp.tile(q_norm[layer], N_HEADS)[None, :]
        kn = jnp.tile(k_norm[layer], N_HEADS)[None, :]
        q, k, vx, qc, kw, u, ga, gp = _in_proj(xc, mix_norm[layer][None, :], w, qn, kn, bd, ones_lane,
                                               rqk, rqi, rki, seq)
        a = _dsa_attention(q, qc, kw, k, vx, batch, seq)
        moe = layer % 2 == 1
        common = (xc, a, u, ga, gp, w_pool[layer].astype(bf16), pool_scale[layer][None, :],
                  w_attn_proj[layer].astype(bf16), w_pool_proj[layer].astype(bf16), w_out[layer].astype(bf16),
                  ffn_norm[layer][None, :])
        idx = layer // 2
        if moe:
            wr = jnp.pad(moe_router[idx], ((0, 0), (0, LANES - N_EXPERTS)))
            wrh = wr.astype(bf16)
            wrl = (wr - wrh.astype(f32)).astype(bf16)
            x2, hf, route = _merge(True, seq, *common, wrh, wrl)
            xc = _moe_layer(route, hf, x2, moe_w1[idx].astype(bf16), moe_w3[idx].astype(bf16),
                            moe_w2[idx].astype(bf16))
        else:
            x2, h2 = _merge(False, seq, *common)
            xc = _ffn(h2, x2, dense_w1[idx].astype(bf16), dense_w3[idx].astype(bf16),
                      dense_w2[idx].astype(bf16), tm=1024, tf=1408)
    return xc.reshape(batch, seq, d)
```

```python
import functools

import jax
import jax.numpy as jnp
from jax import lax
from jax.experimental import pallas as pl
from jax.experimental.pallas import tpu as pltpu

bf16 = jnp.bfloat16
f32 = jnp.float32
i32 = jnp.int32

D_MODEL = 1024
N_HEADS = 8
HEAD_DIM = 64
ATTN_WIDTH = 512
ROPE_DIM = 16
ROPE_THETA = 500000.0
IDX_HEADS = 8
IDX_DIM = 32
IDX_ROPE_DIM = 8
TOPK_MAX = 256
POOL_GROUPS = 4
POOL_GROUP_DIM = 128
POOL_WIDTH = 512
POOL_WINDOWS = (2, 4, 8, 16)
POOL_HALO = 16
N_EXPERTS = 8
EPS = 1e-6

LANES = 128
INT_MIN = -(2 ** 31)
VMEM_LIMIT = 56 * 1024 * 1024
LOG2E = 1.4426950408889634
SOFTMAX_DENOM_FLOOR = 2.0 ** -90

OFF_Q = 0
OFF_K = 512
OFF_VX = 1024
OFF_QCAT = 2048
OFF_KW = 3072
OFF_U = 3200
OFF_GA = 3712
OFF_GP = 4736
W_COLS = 5760

ROUTE_W1, ROUTE_W2, ROUTE_I1, ROUTE_I2 = 8, 9, 10, 11

TM_PROJ = 512
TM_MERGE = 512
TM_RANK = 512
TM_EXPERT = 512
TF_EXPERT = 1792
TM_ROWS = 256
ROW_UNROLL = 8
QB = 256
COUNT_ROWS = 128
TK = 512


def _sigmoid(x):
    return 1.0 / (1.0 + jnp.exp(-x))


def _in_proj_kernel(x_ref, g_ref, w_ref, qn_ref, kn_ref, bd_ref, one_ref, rqk_ref, rqi_ref, rki_ref,
                    q_ref, k_ref, vx_ref, qc_ref, kw_ref, u_ref, ga_ref, gp_ref):
    x = x_ref[...]
    ms = jnp.mean(x * x, axis=-1, keepdims=True)
    h = (x * lax.rsqrt(ms + EPS) * g_ref[...]).astype(bf16)

    def proj(lo, n):
        return jnp.dot(h, w_ref[:, lo:lo + n], preferred_element_type=f32)

    def rope(xc, tab_ref, sh):
        return (xc * tab_ref[0] + pltpu.roll(xc, LANES - sh, 1) * tab_ref[1]
                + pltpu.roll(xc, sh, 1) * tab_ref[2])

    def headnorm(z, gain_ref):
        msh = jnp.dot((z * z).astype(bf16), bd_ref[...], preferred_element_type=f32)
        return z * lax.rsqrt(msh + EPS) * gain_ref[...]

    zq = headnorm(proj(OFF_Q, ATTN_WIDTH), qn_ref) * (HEAD_DIM ** -0.5 * LOG2E)
    zk = headnorm(proj(OFF_K, ATTN_WIDTH), kn_ref)
    for c in range(ATTN_WIDTH // LANES):
        sl = slice(c * LANES, (c + 1) * LANES)
        q_ref[:, sl] = rope(zq[:, sl], rqk_ref, ROPE_DIM // 2).astype(bf16)
        k_ref[:, sl] = rope(zk[:, sl], rqk_ref, ROPE_DIM // 2).astype(bf16)
    vx_ref[...] = jnp.where(one_ref[...] > 0.0, 1.0, proj(OFF_VX, N_HEADS * LANES)).astype(bf16)

    lane = lax.broadcasted_iota(i32, (1, LANES), 1)
    zc = proj(OFF_QCAT, IDX_HEADS * LANES)
    for c in range(IDX_HEADS):
        sl = slice(c * LANES, (c + 1) * LANES)
        r = rope(zc[:, sl], rqi_ref, IDX_ROPE_DIM // 2)
        lo = r - r.astype(bf16).astype(f32)
        qc_ref[:, sl] = jnp.where((lane >= 32) & (lane < 64), lo, r).astype(bf16)

    zkw = rope(proj(OFF_KW, LANES), rki_ref, IDX_ROPE_DIM // 2)
    lo = zkw - zkw.astype(bf16).astype(f32)
    zkw = jnp.where((lane >= 64) & (lane < 96), lo, zkw)
    kw_ref[...] = jnp.where(lane >= 96, zkw * ((IDX_HEADS * IDX_DIM) ** -0.5), zkw)

    u_ref[...] = proj(OFF_U, POOL_WIDTH)
    ga_ref[...] = proj(OFF_GA, D_MODEL)
    gp_ref[...] = proj(OFF_GP, D_MODEL)


def _in_proj(x2d, g, w, qn, kn, bd, ones_lane, rqk, rqi, rki, seq):
    t = x2d.shape[0]
    tm = TM_PROJ
    ns = seq // tm
    row = lambda i: (i, 0)
    const = lambda i: (0, 0)
    tab = lambda i: (0, i % ns, 0)
    out_shapes = (
        jax.ShapeDtypeStruct((t, ATTN_WIDTH), bf16),
        jax.ShapeDtypeStruct((t, ATTN_WIDTH), bf16),
        jax.ShapeDtypeStruct((t, N_HEADS * LANES), bf16),
        jax.ShapeDtypeStruct((t, IDX_HEADS * LANES), bf16),
        jax.ShapeDtypeStruct((t, LANES), f32),
        jax.ShapeDtypeStruct((t, POOL_WIDTH), f32),
        jax.ShapeDtypeStruct((t, D_MODEL), f32),
        jax.ShapeDtypeStruct((t, D_MODEL), f32),
    )
    return pl.pallas_call(
        _in_proj_kernel,
        grid=(t // tm,),
        in_specs=[
            pl.BlockSpec((tm, D_MODEL), row),
            pl.BlockSpec((1, D_MODEL), const),
            pl.BlockSpec((D_MODEL, W_COLS), const, pipeline_mode=pl.Buffered(1)),
            pl.BlockSpec((1, ATTN_WIDTH), const),
            pl.BlockSpec((1, ATTN_WIDTH), const),
            pl.BlockSpec((ATTN_WIDTH, ATTN_WIDTH), const),
            pl.BlockSpec((1, N_HEADS * LANES), const),
            pl.BlockSpec((3, tm, LANES), tab),
            pl.BlockSpec((3, tm, LANES), tab),
            pl.BlockSpec((3, tm, LANES), tab),
        ],
        out_specs=[pl.BlockSpec((tm, s.shape[1]), row) for s in out_shapes],
        out_shape=out_shapes,
        compiler_params=pltpu.CompilerParams(
            dimension_semantics=("arbitrary",), vmem_limit_bytes=VMEM_LIMIT),
        name="in_proj",
    )(x2d, g, w, qn, kn, bd, ones_lane, rqk, rqi, rki)


def _lane_fold(x, op):
    out = x[:, 0:LANES]
    for j in range(1, x.shape[1] // LANES):
        out = op(out, x[:, j * LANES:(j + 1) * LANES])
    return out


def _paired_loop(n, body, init):
    def pair(c2, carry):
        return body(2 * c2 + 1, body(2 * c2, carry))
    carry = lax.fori_loop(0, n // 2, pair, init)
    return lax.fori_loop(n // 2 * 2, n, body, carry)


def _dsa_kernel(topk, q_ref,qc_ref, kwq_ref, k_ref, vx_ref, kw_ref, o_ref,
                key_sc, kcat_sc, qm2_sc, mrun_sc, m_sc, acc_sc, kmax_sm):
    i = pl.program_id(1)
    nk = (i * QB + QB + TK - 1) // TK
    kf = float(topk)

    lane = lax.broadcasted_iota(i32, (1, LANES), 1)

    @pl.when(i == 0)
    def _():
        kcat_sc[...] = kw_ref[...].astype(bf16)
        for pr in range(N_HEADS // 2):
            def norms(c, carry):
                kk = k_ref[pl.ds(pl.multiple_of(c * TK, TK), TK), pr * LANES:(pr + 1) * LANES].astype(f32)
                sq = kk * kk
                n0 = jnp.sqrt(jnp.sum(jnp.where(lane < HEAD_DIM, sq, 0.0), axis=1, keepdims=True))
                n1 = jnp.sqrt(jnp.sum(jnp.where(lane >= HEAD_DIM, sq, 0.0), axis=1, keepdims=True))
                return jnp.maximum(carry[0], jnp.max(n0)), jnp.maximum(carry[1], jnp.max(n1))
            k0, k1 = lax.fori_loop(0, k_ref.shape[0] // TK, norms, (jnp.float32(0.0), jnp.float32(0.0)))
            kmax_sm[2 * pr] = k0
            kmax_sm[2 * pr + 1] = k1

    qpos = i * QB + lax.broadcasted_iota(i32, (QB, 1), 0)
    lane_tk = lax.broadcasted_iota(i32, (1, TK), 1)
    wq = kwq_ref[...]

    def to_key(score):
        bits = pltpu.bitcast(score, i32)
        return bits ^ ((bits >> 31) & 0x7FFFFFFF)

    def score_chunk(c, carry):
        koff = pl.multiple_of(c * TK, TK)
        kc = kcat_sc[pl.ds(koff, TK), :]
        acc = jnp.zeros((QB, TK), f32)
        for h in range(IDX_HEADS):
            d = lax.dot_general(qc_ref[:, h * LANES:(h + 1) * LANES], kc,
                                (((1,), (1,)), ((), ())), preferred_element_type=f32)
            acc = acc + jnp.maximum(d, 0.0) * wq[:, 96 + h:97 + h]
        causal = (koff + lane_tk) <= qpos
        key_sc[c] = jnp.where(causal, to_key(acc), INT_MIN)
        return carry

    _paired_loop(nk, score_chunk, 0)

    def count_ge(cand):
        counts = []
        for r0 in range(0, QB, COUNT_ROWS):
            cand_g = cand[r0:r0 + COUNT_ROWS]

            def body(c, part, r0=r0, cand_g=cand_g):
                for j in range(TK // LANES):
                    kk = key_sc[c, r0:r0 + COUNT_ROWS, j * LANES:(j + 1) * LANES]
                    part = part + jnp.where(kk >= cand_g, 1, 0)
                return part

            counts.append(_paired_loop(nk, body, jnp.zeros((COUNT_ROWS, LANES), i32)))
        return jnp.sum(jnp.concatenate(counts, axis=0).astype(f32), axis=1, keepdims=True)

    zero = jnp.zeros((QB, 1), i32)
    prefix = jnp.where(count_ge(zero) >= kf, zero, INT_MIN)

    def bit_body(b, prefix):
        cand = prefix | jnp.left_shift(1, 30 - b)
        return jnp.where(count_ge(cand) >= kf, cand, prefix)

    thr_raw = lax.fori_loop(0, 31, bit_body, prefix)
    thr = jnp.maximum(thr_raw, INT_MIN + 1)

    is_tie = (count_ge(thr) > kf) & (thr_raw > INT_MIN)

    @pl.when(jnp.max(jnp.where(is_tie, 1.0, 0.0)) > 0.0)
    def _():
        need = kf - count_ge(thr + 1)

        def count_eq_below(m):
            def body(c, part):
                hit = (key_sc[c] == thr) & ((c * TK + lane_tk) < m)
                return part + _lane_fold(jnp.where(hit, 1.0, 0.0), jnp.add)
            part = lax.fori_loop(0, nk, body, jnp.zeros((QB, LANES), f32))
            return jnp.sum(part, axis=1, keepdims=True)

        mprime = jnp.zeros((QB, 1), i32)
        bit = key_sc.shape[0] * TK // 2
        while bit >= 1:
            cand = mprime | bit
            mprime = jnp.where(count_eq_below(cand) < need, cand, mprime)
            bit //= 2

        def drop_ties(c, carry):
            key = key_sc[c]
            drop = is_tie & (key == thr) & ((c * TK + lane_tk) > mprime)
            key_sc[c] = jnp.where(drop, thr - 1, key)
            return carry

        lax.fori_loop(0, nk, drop_ties, 0)

    row2 = lax.broadcasted_iota(i32, (2 * QB, 1), 0)
    for pr in range(N_HEADS // 2):
        qp = q_ref[:, pr * LANES:(pr + 1) * LANES]
        qm2_sc[pr, 0:QB, :] = jnp.where(lane < HEAD_DIM, qp, jnp.zeros((), bf16))
        qm2_sc[pr, QB:2 * QB, :] = jnp.where(lane >= HEAD_DIM, qp, jnp.zeros((), bf16))
        q2 = qm2_sc[pr].astype(f32)
        qn = jnp.sqrt(jnp.sum(q2 * q2, axis=1, keepdims=True))
        m_sc[pr] = qn * jnp.where(row2 < QB, kmax_sm[2 * pr], kmax_sm[2 * pr + 1])
    acc_sc[...] = jnp.zeros(acc_sc.shape, f32)

    def pair_logits(c, pr, bias):
        koff = pl.multiple_of(c * TK, TK)
        kc = k_ref[pl.ds(koff, TK), pr * LANES:(pr + 1) * LANES]
        s2 = lax.dot_general(qm2_sc[pr], kc, (((1,), (1,)), ((), ())), preferred_element_type=f32)
        return s2[0:QB] + bias, s2[QB:2 * QB] + bias

    def max_chunk(c, carry):
        bias = jnp.where(key_sc[c] >= thr, 0.0, -jnp.inf)
        for pr in range(N_HEADS // 2):
            sa, sb = pair_logits(c, pr, bias)
            mrun_sc[pr, 0:QB, :] = jnp.maximum(mrun_sc[pr, 0:QB, :], _lane_fold(sa, jnp.maximum))
            mrun_sc[pr, QB:2 * QB, :] = jnp.maximum(mrun_sc[pr, QB:2 * QB, :], _lane_fold(sb, jnp.maximum))
        return carry

    def acc_chunk(c, carry):
        koff = pl.multiple_of(c * TK, TK)
        bias = jnp.where(key_sc[c] >= thr, 0.0, -jnp.inf)
        for pr in range(N_HEADS // 2):
            sa, sb = pair_logits(c, pr, bias)
            m = m_sc[pr]
            pa = jnp.exp2(sa - m[0:QB]).astype(bf16)
            pb = jnp.exp2(sb - m[QB:2 * QB]).astype(bf16)
            va = vx_ref[pl.ds(koff, TK), (2 * pr) * LANES:(2 * pr + 1) * LANES]
            vb = vx_ref[pl.ds(koff, TK), (2 * pr + 1) * LANES:(2 * pr + 2) * LANES]
            acc_sc[2 * pr] += jnp.dot(pa, va, preferred_element_type=f32)
            acc_sc[2 * pr + 1] += jnp.dot(pb, vb, preferred_element_type=f32)
        return carry

    _paired_loop(nk, acc_chunk, 0)

    lmin = jnp.float32(jnp.inf)
    for pr in range(N_HEADS // 2):
        lmin = jnp.minimum(lmin, jnp.min(acc_sc[2 * pr][:, HEAD_DIM:HEAD_DIM + 1]))
        lmin = jnp.minimum(lmin, jnp.min(acc_sc[2 * pr + 1][:, 0:1]))

    @pl.when(jnp.logical_not(lmin >= SOFTMAX_DENOM_FLOOR))
    def _():
        mrun_sc[...] = jnp.full(mrun_sc.shape, -jnp.inf, f32)
        lax.fori_loop(0, nk, max_chunk, 0)
        for pr in range(N_HEADS // 2):
            m = jnp.max(mrun_sc[pr], axis=1, keepdims=True)
            m_sc[pr] = jnp.where(m == -jnp.inf, 0.0, m)
        acc_sc[...] = jnp.zeros(acc_sc.shape, f32)
        lax.fori_loop(0, nk, acc_chunk, 0)

    for pr in range(N_HEADS // 2):
        a0 = acc_sc[2 * pr]
        a1 = acc_sc[2 * pr + 1]
        o0 = a0 / a0[:, HEAD_DIM:HEAD_DIM + 1]
        o1 = a1 / a1[:, 0:1]
        o_ref[:, pr * LANES:(pr + 1) * LANES] = jnp.where(lane < HEAD_DIM, o0, o1).astype(bf16)


def _dsa_attention(q, qc, kw, k, vx, batch, seq):
    t = q.shape[0]
    nq = seq // QB
    topk = min(TOPK_MAX, seq // 4)
    qrow = lambda b, i: (b * nq + i, 0)
    per_batch = lambda b, i: (b, 0)
    return pl.pallas_call(
        functools.partial(_dsa_kernel, topk),
        grid=(batch, nq),
        in_specs=[
            pl.BlockSpec((QB, ATTN_WIDTH), qrow),
            pl.BlockSpec((QB, IDX_HEADS * LANES), qrow),
            pl.BlockSpec((QB, LANES), qrow),
            pl.BlockSpec((seq, ATTN_WIDTH), per_batch, pipeline_mode=pl.Buffered(1)),
            pl.BlockSpec((seq, N_HEADS * LANES), per_batch, pipeline_mode=pl.Buffered(1)),
            pl.BlockSpec((seq, LANES), per_batch, pipeline_mode=pl.Buffered(1)),
        ],
        out_specs=pl.BlockSpec((QB, ATTN_WIDTH), qrow),
        out_shape=jax.ShapeDtypeStruct((t, ATTN_WIDTH), bf16),
        scratch_shapes=[
            pltpu.VMEM((seq // TK, QB, TK), i32),
            pltpu.VMEM((seq, LANES), bf16),
            pltpu.VMEM((N_HEADS // 2, 2 * QB, LANES), bf16),
            pltpu.VMEM((N_HEADS // 2, 2 * QB, LANES), f32),
            pltpu.VMEM((N_HEADS // 2, 2 * QB, 1), f32),
            pltpu.VMEM((N_HEADS, QB, LANES), f32),
            pltpu.SMEM((N_HEADS,), f32),
        ],
        compiler_params=pltpu.CompilerParams(
            dimension_semantics=("arbitrary", "arbitrary"), vmem_limit_bytes=VMEM_LIMIT),
        name="dsa_attention",
    )(q, qc, kw, k, vx, kw)


def _merge_kernel(moe, seq, *refs):
    if moe:
        (x_ref, a_ref, u_ref, uh_ref, ga_ref, gp_ref, wp_ref, ps_ref, pa_ref, pb_ref, wo_ref, g2_ref,
         wrh_ref, wrl_ref, x2_ref, h2_ref, gate_ref, e_sc) = refs
    else:
        (x_ref, a_ref, u_ref, uh_ref, ga_ref, gp_ref, wp_ref, ps_ref, pa_ref, pb_ref, wo_ref, g2_ref,
         x2_ref, h2_ref, e_sc) = refs
    tm = TM_MERGE
    i = pl.program_id(0)
    ti = i % (seq // tm)
    e_sc[0:POOL_HALO, :] = jnp.where(ti == 0, 0.0, uh_ref[...])
    e_sc[POOL_HALO:POOL_HALO + tm, :] = u_ref[...]
    npos = (ti * tm + 1 + lax.broadcasted_iota(i32, (tm, 1), 0)).astype(f32)

    parts = []
    for g, w in enumerate(POOL_WINDOWS):
        sl = slice(g * POOL_GROUP_DIM, (g + 1) * POOL_GROUP_DIM)
        tot = e_sc[POOL_HALO:POOL_HALO + tm, sl]
        for j in range(1, w):
            tot = tot + e_sc[POOL_HALO - j:POOL_HALO - j + tm, sl]
        diff = tot / jnp.minimum(npos, float(w)) - u_ref[:, sl]
        parts.append(jnp.dot(diff.astype(bf16), wp_ref[g], preferred_element_type=f32))
    p = jnp.concatenate(parts, axis=1) * ps_ref[...]

    ab = jnp.dot(a_ref[...], pa_ref[...], preferred_element_type=f32)
    pb = jnp.dot(p.astype(bf16), pb_ref[...], preferred_element_type=f32)
    merged = _sigmoid(ga_ref[...]) * ab + _sigmoid(gp_ref[...]) * pb
    x2 = x_ref[...] + jnp.dot(merged.astype(bf16), wo_ref[...], preferred_element_type=f32)
    x2_ref[...] = x2
    ms = jnp.mean(x2 * x2, axis=-1, keepdims=True)
    hf = x2 * lax.rsqrt(ms + EPS) * g2_ref[...]
    h2_ref[...] = hf.astype(h2_ref.dtype)

    if moe:
        hi = hf.astype(bf16)
        lo = (hf - hi.astype(f32)).astype(bf16)
        logits = (jnp.dot(hi, wrh_ref[...], preferred_element_type=f32)
                  + jnp.dot(lo, wrh_ref[...], preferred_element_type=f32)
                  + jnp.dot(hi, wrl_ref[...], preferred_element_type=f32))
        lanef = lax.broadcasted_iota(i32, (1, LANES), 1).astype(f32)
        lg = jnp.where(lanef < N_EXPERTS, logits, -jnp.inf)
        v1 = jnp.max(lg, axis=1, keepdims=True)
        i1 = jnp.min(jnp.where(lg == v1, lanef, float(LANES)), axis=1, keepdims=True)
        lg2 = jnp.where(lanef == i1, -jnp.inf, lg)
        v2 = jnp.max(lg2, axis=1, keepdims=True)
        i2 = jnp.min(jnp.where(lg2 == v2, lanef, float(LANES)), axis=1, keepdims=True)
        tt = jnp.exp(v2 - v1)
        w1 = 1.0 / (1.0 + tt)
        w2 = tt / (1.0 + tt)
        route = jnp.where((lanef == i1) | (lanef == i2), 1.0, 0.0)
        for ln, val in ((ROUTE_W1, w1), (ROUTE_W2, w2), (ROUTE_I1, i1), (ROUTE_I2, i2)):
            route = jnp.where(lanef == float(ln), val, route)
        gate_ref[...] = route


def _merge(moe, seq, x2d, a, u, ga, gp, wp, ps, pa, pb, wo, g2, wrh=None, wrl=None):
    t = x2d.shape[0]
    tm = TM_MERGE
    row = lambda i: (i, 0)
    const2 = lambda i: (0, 0)
    const3 = lambda i: (0, 0, 0)
    halo = lambda i: (jnp.maximum(i * (tm // POOL_HALO) - 1, 0), 0)
    in_specs = [
        pl.BlockSpec((tm, D_MODEL), row),
        pl.BlockSpec((tm, ATTN_WIDTH), row),
        pl.BlockSpec((tm, POOL_WIDTH), row),
        pl.BlockSpec((POOL_HALO, POOL_WIDTH), halo),
        pl.BlockSpec((tm, D_MODEL), row),
        pl.BlockSpec((tm, D_MODEL), row),
        pl.BlockSpec((POOL_GROUPS, POOL_GROUP_DIM, POOL_GROUP_DIM), const3),
        pl.BlockSpec((1, POOL_WIDTH), const2),
        pl.BlockSpec((ATTN_WIDTH, D_MODEL), const2),
        pl.BlockSpec((POOL_WIDTH, D_MODEL), const2),
        pl.BlockSpec((D_MODEL, D_MODEL), const2),
        pl.BlockSpec((1, D_MODEL), const2),
    ]
    args = [x2d, a, u, u, ga, gp, wp, ps, pa, pb, wo, g2]
    out_shapes = [jax.ShapeDtypeStruct((t, D_MODEL), f32), jax.ShapeDtypeStruct((t, D_MODEL), f32 if moe else bf16)]
    out_specs = [pl.BlockSpec((tm, D_MODEL), row), pl.BlockSpec((tm, D_MODEL), row)]
    if moe:
        in_specs += [pl.BlockSpec((D_MODEL, LANES), const2), pl.BlockSpec((D_MODEL, LANES), const2)]
        args += [wrh, wrl]
        out_shapes.append(jax.ShapeDtypeStruct((t, LANES), f32))
        out_specs.append(pl.BlockSpec((tm, LANES), row))
    return pl.pallas_call(
        functools.partial(_merge_kernel, moe, seq),
        grid=(t // tm,),
        in_specs=in_specs,
        out_specs=out_specs,
        out_shape=out_shapes,
        scratch_shapes=[pltpu.VMEM((tm + POOL_HALO, POOL_WIDTH), f32)],
        compiler_params=pltpu.CompilerParams(
            dimension_semantics=("arbitrary",), vmem_limit_bytes=VMEM_LIMIT),
        name="merge_moe" if moe else "merge_dense",
    )(*args)


def _swiglu_partial(h, w1, w3, w2):
    a = jnp.dot(h, w1, preferred_element_type=f32)
    b = jnp.dot(h, w3, preferred_element_type=f32)
    act = (a * _sigmoid(a) * b).astype(bf16)
    return jnp.dot(act, w2, preferred_element_type=f32)


def _ffn_kernel(h_ref, x_ref, w1_ref, w3_ref, w2_ref, o_ref):
    @pl.when(pl.program_id(1) == 0)
    def _():
        o_ref[...] = x_ref[...]

    o_ref[...] += _swiglu_partial(h_ref[...], w1_ref[...], w3_ref[...], w2_ref[...])


def _ffn(h2, x2, w1, w3, w2, tm, tf):
    t = h2.shape[0]
    ff = w1.shape[1]
    row = lambda i, j: (i, 0)
    return pl.pallas_call(
        _ffn_kernel,
        grid=(t // tm, ff // tf),
        in_specs=[
            pl.BlockSpec((tm, D_MODEL), row),
            pl.BlockSpec((tm, D_MODEL), row),
            pl.BlockSpec((D_MODEL, tf), lambda i, j: (0, j)),
            pl.BlockSpec((D_MODEL, tf), lambda i, j: (0, j)),
            pl.BlockSpec((tf, D_MODEL), lambda i, j: (j, 0)),
        ],
        out_specs=pl.BlockSpec((tm, D_MODEL), row),
        out_shape=jax.ShapeDtypeStruct((t, D_MODEL), f32),
        compiler_params=pltpu.CompilerParams(
            dimension_semantics=("arbitrary", "arbitrary"), vmem_limit_bytes=VMEM_LIMIT),
        name="ffn_dense",
    )(h2, x2, w1, w3, w2)


def _rank_kernel(route_ref, tri_ref, rk_ref, cnt_ref, carry_sc):
    @pl.when(pl.program_id(0) == 0)
    def _():
        carry_sc[...] = jnp.zeros(carry_sc.shape, f32)

    lane = lax.broadcasted_iota(i32, (1, LANES), 1)
    lanef = lane.astype(f32)
    r = route_ref[...]
    sel = jnp.where(lane < N_EXPERTS, r, 0.0)
    ranks = jnp.dot(tri_ref[...], sel.astype(bf16), preferred_element_type=f32) + carry_sc[...]
    rk1 = jnp.sum(jnp.where(lanef == r[:, ROUTE_I1:ROUTE_I1 + 1], ranks, 0.0), axis=1, keepdims=True)
    rk2 = jnp.sum(jnp.where(lanef == r[:, ROUTE_I2:ROUTE_I2 + 1], ranks, 0.0), axis=1, keepdims=True)
    rk_ref[...] = jnp.where(lane == 0, rk1, jnp.where(lane == 1, rk2, 0.0))
    carry_sc[...] += jnp.sum(sel, axis=0, keepdims=True)
    cnt_ref[...] = carry_sc[...]


def _moe_rank(route):
    t = route.shape[0]
    tm = TM_RANK
    idx = jnp.arange(tm)
    tri = (idx[None, :] < idx[:, None]).astype(bf16)
    return pl.pallas_call(
        _rank_kernel,
        grid=(t // tm,),
        in_specs=[pl.BlockSpec((tm, LANES), lambda i: (i, 0)), pl.BlockSpec((tm, tm), lambda i: (0, 0))],
        out_specs=[pl.BlockSpec((tm, LANES), lambda i: (i, 0)), pl.BlockSpec((1, LANES), lambda i: (0, 0))],
        out_shape=[jax.ShapeDtypeStruct((t, LANES), f32), jax.ShapeDtypeStruct((1, LANES), f32)],
        scratch_shapes=[pltpu.VMEM((1, LANES), f32)],
        compiler_params=pltpu.CompilerParams(dimension_semantics=("arbitrary",)),
        name="moe_rank",
    )(route, tri)


def _row_copy(src, dst, sem):
    return pltpu.make_async_copy(src, dst, sem)


def _moe_ffn_kernel(pos_ref, te_ref, meta_ref, h_hbm, w1_ref, w3_ref, w2_ref, ys_ref,
                    inv_sm, gbuf, xb, sem):
    del te_ref
    i = pl.program_id(0)
    j = pl.program_id(1)
    n_used = meta_ref[0]
    n_slots = pos_ref.shape[0]

    def start_gather(tile, slot):
        def issue(g, carry):
            for u in range(ROW_UNROLL):
                r = g * ROW_UNROLL + u
                tok = inv_sm[tile * TM_EXPERT + r]
                _row_copy(h_hbm.at[pl.ds(tok, 1)], gbuf.at[slot, pl.ds(r, 1)],
                          sem.at[slot]).start(priority=u % 2)
            return carry
        lax.fori_loop(0, TM_EXPERT // ROW_UNROLL, issue, 0)

    def wait_gather(slot):
        _row_copy(h_hbm.at[pl.ds(0, TM_EXPERT)], gbuf.at[slot], sem.at[slot]).wait()

    @pl.when((i == 0) & (j == 0))
    def _():
        for e in range(N_EXPERTS):
            def pad(r, carry):
                inv_sm[r] = 0
                return carry
            lax.fori_loop(meta_ref[1 + e], meta_ref[1 + N_EXPERTS + e], pad, 0)

        def scatter(g, carry):
            for u in range(ROW_UNROLL):
                n = g * ROW_UNROLL + u
                inv_sm[pos_ref[n]] = lax.shift_right_logical(n, 1)
            return carry
        lax.fori_loop(0, n_slots // ROW_UNROLL, scatter, 0)

        @pl.when(n_used > 0)
        def _():
            start_gather(0, 0)

    used = i < n_used

    @pl.when(used & (j == 0))
    def _():
        slot = i % 2
        wait_gather(slot)

        @pl.when(i + 1 < n_used)
        def _():
            start_gather(i + 1, 1 - slot)

        xb[...] = gbuf[slot].astype(bf16)

    @pl.when(used)
    def _():
        y = _swiglu_partial(xb[...], w1_ref[0], w3_ref[0], w2_ref[0])

        @pl.when(j == 0)
        def _():
            ys_ref[...] = y

        @pl.when(j > 0)
        def _():
            ys_ref[...] += y

    @pl.when(jnp.logical_not(used) & (j == 0))
    def _():
        ys_ref[...] = jnp.zeros(ys_ref.shape, f32)


def _moe_ffn(pos, tile_expert, meta, hf, w1, w3, w2, n_rows):
    ff = w1.shape[2]
    nj = ff // TF_EXPERT
    jj = lambda i, j, meta: jnp.where(i < meta[0], j, nj - 1)
    return pl.pallas_call(
        _moe_ffn_kernel,
        grid_spec=pltpu.PrefetchScalarGridSpec(
            num_scalar_prefetch=3,
            grid=(n_rows // TM_EXPERT, nj),
            in_specs=[
                pl.BlockSpec(memory_space=pl.ANY),
                pl.BlockSpec((1, D_MODEL, TF_EXPERT), lambda i, j, pos, te, meta: (te[i], 0, jj(i, j, meta))),
                pl.BlockSpec((1, D_MODEL, TF_EXPERT), lambda i, j, pos, te, meta: (te[i], 0, jj(i, j, meta))),
                pl.BlockSpec((1, TF_EXPERT, D_MODEL), lambda i, j, pos, te, meta: (te[i], jj(i, j, meta), 0)),
            ],
            out_specs=pl.BlockSpec((TM_EXPERT, D_MODEL), lambda i, j, pos, te, meta: (i, 0)),
            scratch_shapes=[
                pltpu.SMEM((n_rows,), i32),
                pltpu.VMEM((2, TM_EXPERT, D_MODEL), f32),
                pltpu.VMEM((TM_EXPERT, D_MODEL), bf16),
                pltpu.SemaphoreType.DMA((2,)),
            ],
        ),
        out_shape=jax.ShapeDtypeStruct((n_rows, D_MODEL), f32),
        compiler_params=pltpu.CompilerParams(
            dimension_semantics=("arbitrary", "arbitrary"), vmem_limit_bytes=VMEM_LIMIT),
        name="moe_ffn",
    )(pos, tile_expert, meta, hf, w1, w3, w2)


def _combine_kernel(pos_ref, ys_hbm, x2_ref, route_ref, o_ref, buf, sem):
    base = pl.program_id(0) * TM_ROWS

    def issue(g, carry):
        for u in range(ROW_UNROLL // 2):
            r = g * (ROW_UNROLL // 2) + u
            for s in range(2):
                _row_copy(ys_hbm.at[pl.ds(pos_ref[2 * (base + r) + s], 1)], buf.at[s, pl.ds(r, 1)],
                          sem).start(priority=s)
        return carry

    lax.fori_loop(0, TM_ROWS // (ROW_UNROLL // 2), issue, 0)
    for s in range(2):
        _row_copy(ys_hbm.at[pl.ds(0, TM_ROWS)], buf.at[s], sem).wait()
    route = route_ref[...]
    o_ref[...] = (x2_ref[...] + route[:, ROUTE_W1:ROUTE_W1 + 1] * buf[0]
                  + route[:, ROUTE_W2:ROUTE_W2 + 1] * buf[1])


def _moe_combine(pos, ys, x2, route):
    t = x2.shape[0]
    row = lambda i, pos: (i, 0)
    return pl.pallas_call(
        _combine_kernel,
        grid_spec=pltpu.PrefetchScalarGridSpec(
            num_scalar_prefetch=1,
            grid=(t // TM_ROWS,),
            in_specs=[
                pl.BlockSpec(memory_space=pl.ANY),
                pl.BlockSpec((TM_ROWS, D_MODEL), row),
                pl.BlockSpec((TM_ROWS, LANES), row),
            ],
            out_specs=pl.BlockSpec((TM_ROWS, D_MODEL), row),
            scratch_shapes=[pltpu.VMEM((2, TM_ROWS, D_MODEL), f32), pltpu.SemaphoreType.DMA(())],
        ),
        out_shape=jax.ShapeDtypeStruct((t, D_MODEL), f32),
        compiler_params=pltpu.CompilerParams(dimension_semantics=("arbitrary",)),
        name="moe_combine",
    )(pos, ys, x2, route)


def _moe_layer(route, hf, x2, w1, w3, w2):
    t = hf.shape[0]
    n_tiles = 2 * t // TM_EXPERT + N_EXPERTS
    rk, cnt = _moe_rank(route)
    counts = cnt[0, :N_EXPERTS].astype(i32)
    padded = (counts + TM_EXPERT - 1) // TM_EXPERT * TM_EXPERT
    ends = jnp.cumsum(padded)
    starts = ends - padded
    n_used = (ends[-1] // TM_EXPERT).astype(i32)
    tile_id = jnp.arange(n_tiles, dtype=i32)
    tile_expert = jnp.minimum(jnp.sum((tile_id[:, None] * TM_EXPERT >= ends[None, :]).astype(i32), axis=1),
                              N_EXPERTS - 1)
    tile_expert = jnp.where(tile_id < n_used, tile_expert, tile_expert[jnp.maximum(n_used - 1, 0)])
    experts = jnp.arange(N_EXPERTS, dtype=f32)[None, :]
    start_of = lambda ids: jnp.sum(jnp.where(ids[:, None] == experts, starts[None, :], 0), axis=1)
    pos1 = start_of(route[:, ROUTE_I1]) + rk[:, 0].astype(i32)
    pos2 = start_of(route[:, ROUTE_I2]) + rk[:, 1].astype(i32)
    pos = jnp.stack([pos1, pos2], axis=1).reshape(2 * t).astype(i32)

    meta = jnp.concatenate([n_used.reshape(1), starts + counts, ends]).astype(i32)
    ys = _moe_ffn(pos, tile_expert, meta, hf, w1, w3, w2, n_tiles * TM_EXPERT)
    return _moe_combine(pos, ys, x2, route)


def _rope_tables(seq, rot_dim, period, active_lanes):
    half = rot_dim // 2
    inv = jnp.power(ROPE_THETA, -jnp.arange(0, rot_dim, 2, dtype=f32) / rot_dim)
    ang = jnp.arange(seq, dtype=f32)[:, None] * inv[None, :]
    cos, sin = jnp.cos(ang), jnp.sin(ang)
    ones = jnp.ones((seq, period - rot_dim), f32)
    zeros = lambda n: jnp.zeros((seq, n), f32)
    cos_p = jnp.concatenate([cos, cos, ones], axis=1)
    sin_a = jnp.concatenate([-sin, zeros(period - half)], axis=1)
    sin_b = jnp.concatenate([zeros(half), sin, zeros(period - rot_dim)], axis=1)
    reps = LANES // period
    tabs = [jnp.tile(tb, (1, reps)) for tb in (cos_p, sin_a, sin_b)]
    live = (jnp.arange(LANES) < active_lanes)[None, :]
    tabs = [jnp.where(live, tabs[0], 1.0), jnp.where(live, tabs[1], 0.0), jnp.where(live, tabs[2], 0.0)]
    return jnp.stack(tabs, axis=0)


def _relayout_w_in(w):
    d = w.shape[0]
    o = 0
    segs = {}
    for name, n in (("q", 512), ("k", 512), ("v", 512), ("qi", 256), ("ki", 32), ("wi", 8),
                    ("u", 512), ("ga", 1024), ("gp", 1024)):
        segs[name] = w[:, o:o + n]
        o += n
    qi = segs["qi"].reshape(d, IDX_HEADS, IDX_DIM)
    qcat = jnp.concatenate([qi, qi, qi, jnp.zeros_like(qi)], axis=-1).reshape(d, IDX_HEADS * LANES)
    kwc = jnp.concatenate([segs["ki"], segs["ki"], segs["ki"], segs["wi"],
                           jnp.zeros((d, LANES - 3 * IDX_DIM - IDX_HEADS), w.dtype)], axis=1)
    vh = segs["v"].reshape(d, N_HEADS // 2, 2, HEAD_DIM)
    zh = jnp.zeros_like(vh[:, :, 0])
    vx = jnp.stack([jnp.concatenate([vh[:, :, 0], zh], axis=-1), jnp.concatenate([zh, vh[:, :, 1]], axis=-1)],
                   axis=2).reshape(d, N_HEADS * LANES)
    out = jnp.concatenate([segs["q"], segs["k"], vx, qcat, kwc, segs["u"], segs["ga"], segs["gp"]], axis=1)
    return out.astype(bf16)


def kernel(x, mix_norm, w_in, q_norm, k_norm, w_pool, pool_scale, w_attn_proj, w_pool_proj, w_out, ffn_norm,
           dense_w1, dense_w3, dense_w2, moe_router, moe_w1, moe_w3, moe_w2):
    batch, seq, d = x.shape
    depth = w_in.shape[0]
    t = batch * seq
    assert d == D_MODEL and seq % TK == 0 and seq % TM_PROJ == 0 and seq % TM_MERGE == 0

    rqk = _rope_tables(seq, ROPE_DIM, HEAD_DIM, LANES)
    rqi = _rope_tables(seq, IDX_ROPE_DIM, IDX_DIM, LANES)
    rki = _rope_tables(seq, IDX_ROPE_DIM, IDX_DIM, 3 * IDX_DIM)
    head_of = jnp.arange(ATTN_WIDTH) // HEAD_DIM
    bd = jnp.where(head_of[:, None] == head_of[None, :], 1.0 / HEAD_DIM, 0.0).astype(bf16)
    lane_in_tile = jnp.arange(N_HEADS * LANES) % LANES
    odd_head = (jnp.arange(N_HEADS * LANES) // LANES) % 2 == 1
    ones_lane = (lane_in_tile == jnp.where(odd_head, 0, HEAD_DIM)).astype(f32)[None, :]

    xc = x.reshape(t, d)
    for layer in range(depth):
        w = _relayout_w_in(w_in[layer])
        qn = jnp.tile(q_norm[layer], N_HEADS)[None, :]
        kn = jnp.tile(k_norm[layer], N_HEADS)[None, :]
        q, k, vx, qc, kw, u, ga, gp = _in_proj(xc, mix_norm[layer][None, :], w, qn, kn, bd, ones_lane,
                                               rqk, rqi, rki, seq)
        a = _dsa_attention(q, qc, kw, k, vx, batch, seq)
        moe = layer % 2 == 1
        common = (xc, a, u, ga, gp, w_pool[layer].astype(bf16), pool_scale[layer][None, :],
                  w_attn_proj[layer].astype(bf16), w_pool_proj[layer].astype(bf16), w_out[layer].astype(bf16),
                  ffn_norm[layer][None, :])
        idx = layer // 2
        if moe:
            wr = jnp.pad(moe_router[idx], ((0, 0), (0, LANES - N_EXPERTS)))
            wrh = wr.astype(bf16)
            wrl = (wr - wrh.astype(f32)).astype(bf16)
            x2, hf, route = _merge(True, seq, *common, wrh, wrl)
            xc = _moe_layer(route, hf, x2, moe_w1[idx].astype(bf16), moe_w3[idx].astype(bf16),
                            moe_w2[idx].astype(bf16))
        else:
            x2, h2 = _merge(False, seq, *common)
            xc = _ffn(h2, x2, dense_w1[idx].astype(bf16), dense_w3[idx].astype(bf16),
                      dense_w2[idx].astype(bf16), tm=1024, tf=1408)
    return xc.reshape(batch, seq, d)
```

```python
import functools

import jax
import jax.numpy as jnp
from jax import lax
from jax.experimental import pallas as pl
from jax.experimental.pallas import tpu as pltpu

bf16 = jnp.bfloat16
f32 = jnp.float32
i32 = jnp.int32

D_MODEL = 1024
N_HEADS = 8
HEAD_DIM = 64
ATTN_WIDTH = 512
ROPE_DIM = 16
ROPE_THETA = 500000.0
IDX_HEADS = 8
IDX_DIM = 32
IDX_ROPE_DIM = 8
TOPK_MAX = 256
POOL_GROUPS = 4
POOL_GROUP_DIM = 128
POOL_WIDTH = 512
POOL_WINDOWS = (2, 4, 8, 16)
POOL_HALO = 16
N_EXPERTS = 8
EPS = 1e-6

LANES = 128
INT_MIN = -(2 ** 31)
VMEM_LIMIT = 56 * 1024 * 1024
LOG2E = 1.4426950408889634
SOFTMAX_DENOM_FLOOR = 2.0 ** -90

OFF_Q = 0
OFF_K = 512
OFF_VX = 1024
OFF_QCAT = 2048
OFF_KW = 3072
OFF_U = 3200
OFF_GA = 3712
OFF_GP = 4736
W_COLS = 5760

ROUTE_W1, ROUTE_W2, ROUTE_I1, ROUTE_I2 = 8, 9, 10, 11

TM_PROJ = 512
TM_MERGE = 512
TM_RANK = 512
TM_EXPERT = 512
TF_EXPERT = 1792
TM_ROWS = 256
ROW_UNROLL = 8
QB = 256
COUNT_ROWS = 128
CHUNKS_PER_TRIP = 4
TK = 512


def _sigmoid(x):
    return 1.0 / (1.0 + jnp.exp(-x))


def _in_proj_kernel(x_ref, g_ref, w_ref, qn_ref, kn_ref, bd_ref, one_ref, rqk_ref, rqi_ref, rki_ref,
                    q_ref, k_ref, vx_ref, qc_ref, kw_ref, u_ref, ga_ref, gp_ref):
    x = x_ref[...]
    ms = jnp.mean(x * x, axis=-1, keepdims=True)
    h = (x * lax.rsqrt(ms + EPS) * g_ref[...]).astype(bf16)

    def proj(lo, n):
        return jnp.dot(h, w_ref[:, lo:lo + n], preferred_element_type=f32)

    def rope(xc, tab_ref, sh):
        return (xc * tab_ref[0] + pltpu.roll(xc, LANES - sh, 1) * tab_ref[1]
                + pltpu.roll(xc, sh, 1) * tab_ref[2])

    def headnorm(z, gain_ref):
        msh = jnp.dot((z * z).astype(bf16), bd_ref[...], preferred_element_type=f32)
        return z * lax.rsqrt(msh + EPS) * gain_ref[...]

    zq = headnorm(proj(OFF_Q, ATTN_WIDTH), qn_ref) * (HEAD_DIM ** -0.5 * LOG2E)
    zk = headnorm(proj(OFF_K, ATTN_WIDTH), kn_ref)
    for c in range(ATTN_WIDTH // LANES):
        sl = slice(c * LANES, (c + 1) * LANES)
        q_ref[:, sl] = rope(zq[:, sl], rqk_ref, ROPE_DIM // 2).astype(bf16)
        k_ref[:, sl] = rope(zk[:, sl], rqk_ref, ROPE_DIM // 2).astype(bf16)
    vx_ref[...] = jnp.where(one_ref[...] > 0.0, 1.0, proj(OFF_VX, N_HEADS * LANES)).astype(bf16)

    lane = lax.broadcasted_iota(i32, (1, LANES), 1)
    zc = proj(OFF_QCAT, IDX_HEADS * LANES)
    for c in range(IDX_HEADS):
        sl = slice(c * LANES, (c + 1) * LANES)
        r = rope(zc[:, sl], rqi_ref, IDX_ROPE_DIM // 2)
        lo = r - r.astype(bf16).astype(f32)
        qc_ref[:, sl] = jnp.where((lane >= 32) & (lane < 64), lo, r).astype(bf16)

    zkw = rope(proj(OFF_KW, LANES), rki_ref, IDX_ROPE_DIM // 2)
    lo = zkw - zkw.astype(bf16).astype(f32)
    zkw = jnp.where((lane >= 64) & (lane < 96), lo, zkw)
    kw_ref[...] = jnp.where(lane >= 96, zkw * ((IDX_HEADS * IDX_DIM) ** -0.5), zkw)

    u_ref[...] = proj(OFF_U, POOL_WIDTH)
    ga_ref[...] = proj(OFF_GA, D_MODEL)
    gp_ref[...] = proj(OFF_GP, D_MODEL)


def _in_proj(x2d, g, w, qn, kn, bd, ones_lane, rqk, rqi, rki, seq):
    t = x2d.shape[0]
    tm = TM_PROJ
    ns = seq // tm
    row = lambda i: (i, 0)
    const = lambda i: (0, 0)
    tab = lambda i: (0, i % ns, 0)
    out_shapes = (
        jax.ShapeDtypeStruct((t, ATTN_WIDTH), bf16),
        jax.ShapeDtypeStruct((t, ATTN_WIDTH), bf16),
        jax.ShapeDtypeStruct((t, N_HEADS * LANES), bf16),
        jax.ShapeDtypeStruct((t, IDX_HEADS * LANES), bf16),
        jax.ShapeDtypeStruct((t, LANES), f32),
        jax.ShapeDtypeStruct((t, POOL_WIDTH), f32),
        jax.ShapeDtypeStruct((t, D_MODEL), f32),
        jax.ShapeDtypeStruct((t, D_MODEL), f32),
    )
    return pl.pallas_call(
        _in_proj_kernel,
        grid=(t // tm,),
        in_specs=[
            pl.BlockSpec((tm, D_MODEL), row),
            pl.BlockSpec((1, D_MODEL), const),
            pl.BlockSpec((D_MODEL, W_COLS), const, pipeline_mode=pl.Buffered(1)),
            pl.BlockSpec((1, ATTN_WIDTH), const),
            pl.BlockSpec((1, ATTN_WIDTH), const),
            pl.BlockSpec((ATTN_WIDTH, ATTN_WIDTH), const),
            pl.BlockSpec((1, N_HEADS * LANES), const),
            pl.BlockSpec((3, tm, LANES), tab),
            pl.BlockSpec((3, tm, LANES), tab),
            pl.BlockSpec((3, tm, LANES), tab),
        ],
        out_specs=[pl.BlockSpec((tm, s.shape[1]), row) for s in out_shapes],
        out_shape=out_shapes,
        compiler_params=pltpu.CompilerParams(
            dimension_semantics=("arbitrary",), vmem_limit_bytes=VMEM_LIMIT),
        name="in_proj",
    )(x2d, g, w, qn, kn, bd, ones_lane, rqk, rqi, rki)


def _lane_fold(x, op):
    out = x[:, 0:LANES]
    for j in range(1, x.shape[1] // LANES):
        out = op(out, x[:, j * LANES:(j + 1) * LANES])
    return out


def _grouped_loop(n, body, init):
    def group(g, carry):
        for u in range(CHUNKS_PER_TRIP):
            carry = body(CHUNKS_PER_TRIP * g + u, carry)
        return carry
    carry = lax.fori_loop(0, n // CHUNKS_PER_TRIP, group, init)
    return lax.fori_loop(n // CHUNKS_PER_TRIP * CHUNKS_PER_TRIP, n, body, carry)


def _dsa_kernel(topk, q_ref, qc_ref, kwq_ref, k_ref, vx_ref, kw_ref, o_ref,
                key_sc, kcat_sc, qm2_sc, mrun_sc, m_sc, acc_sc, kmax_sm):
    i = pl.program_id(1)
    nk = (i * QB + QB + TK - 1) // TK
    kf = float(topk)

    lane = lax.broadcasted_iota(i32, (1, LANES), 1)

    @pl.when(i == 0)
    def _():
        kcat_sc[...] = kw_ref[...].astype(bf16)
        for pr in range(N_HEADS // 2):
            def norms(c, carry):
                kk = k_ref[pl.ds(pl.multiple_of(c * TK, TK), TK), pr * LANES:(pr + 1) * LANES].astype(f32)
                sq = kk * kk
                n0 = jnp.sqrt(jnp.sum(jnp.where(lane < HEAD_DIM, sq, 0.0), axis=1, keepdims=True))
                n1 = jnp.sqrt(jnp.sum(jnp.where(lane >= HEAD_DIM, sq, 0.0), axis=1, keepdims=True))
                return jnp.maximum(carry[0], jnp.max(n0)), jnp.maximum(carry[1], jnp.max(n1))
            k0, k1 = lax.fori_loop(0, k_ref.shape[0] // TK, norms, (jnp.float32(0.0), jnp.float32(0.0)))
            kmax_sm[2 * pr] = k0
            kmax_sm[2 * pr + 1] = k1

    qpos = i * QB + lax.broadcasted_iota(i32, (QB, 1), 0)
    lane_tk = lax.broadcasted_iota(i32, (1, TK), 1)
    wq = kwq_ref[...]

    def to_key(score):
        bits = pltpu.bitcast(score, i32)
        return bits ^ ((bits >> 31) & 0x7FFFFFFF)

    def score_chunk(c, carry):
        koff = pl.multiple_of(c * TK, TK)
        kc = kcat_sc[pl.ds(koff, TK), :]
        acc = jnp.zeros((QB, TK), f32)
        for h in range(IDX_HEADS):
            d = lax.dot_general(qc_ref[:, h * LANES:(h + 1) * LANES], kc,
                                (((1,), (1,)), ((), ())), preferred_element_type=f32)
            acc = acc + jnp.maximum(d, 0.0) * wq[:, 96 + h:97 + h]
        causal = (koff + lane_tk) <= qpos
        key_sc[c] = jnp.where(causal, to_key(acc), INT_MIN)
        return carry

    _grouped_loop(nk, score_chunk, 0)

    def count_ge(cand):
        counts = []
        for r0 in range(0, QB, COUNT_ROWS):
            cand_g = cand[r0:r0 + COUNT_ROWS]

            def body(c, part, r0=r0, cand_g=cand_g):
                for j in range(TK // LANES):
                    kk = key_sc[c, r0:r0 + COUNT_ROWS, j * LANES:(j + 1) * LANES]
                    part = part + jnp.where(kk >= cand_g, 1, 0)
                return part

            counts.append(_grouped_loop(nk, body, jnp.zeros((COUNT_ROWS, LANES), i32)))
        return jnp.sum(jnp.concatenate(counts, axis=0).astype(f32), axis=1, keepdims=True)

    zero = jnp.zeros((QB, 1), i32)
    prefix = jnp.where(count_ge(zero) >= kf, zero, INT_MIN)

    def bit_body(b, prefix):
        cand = prefix | jnp.left_shift(1, 30 - b)
        return jnp.where(count_ge(cand) >= kf, cand, prefix)

    thr_raw = lax.fori_loop(0, 31, bit_body, prefix)
    thr = jnp.maximum(thr_raw, INT_MIN + 1)

    is_tie = (count_ge(thr) > kf) & (thr_raw > INT_MIN)

    @pl.when(jnp.max(jnp.where(is_tie, 1.0, 0.0)) > 0.0)
    def _():
        need = kf - count_ge(thr + 1)

        def count_eq_below(m):
            def body(c, part):
                hit = (key_sc[c] == thr) & ((c * TK + lane_tk) < m)
                return part + _lane_fold(jnp.where(hit, 1.0, 0.0), jnp.add)
            part = lax.fori_loop(0, nk, body, jnp.zeros((QB, LANES), f32))
            return jnp.sum(part, axis=1, keepdims=True)

        mprime = jnp.zeros((QB, 1), i32)
        bit = key_sc.shape[0] * TK // 2
        while bit >= 1:
            cand = mprime | bit
            mprime = jnp.where(count_eq_below(cand) < need, cand, mprime)
            bit //= 2

        def drop_ties(c, carry):
            key = key_sc[c]
            drop = is_tie & (key == thr) & ((c * TK + lane_tk) > mprime)
            key_sc[c] = jnp.where(drop, thr - 1, key)
            return carry

        lax.fori_loop(0, nk, drop_ties, 0)

    row2 = lax.broadcasted_iota(i32, (2 * QB, 1), 0)
    for pr in range(N_HEADS // 2):
        qp = q_ref[:, pr * LANES:(pr + 1) * LANES]
        qm2_sc[pr, 0:QB, :] = jnp.where(lane < HEAD_DIM, qp, jnp.zeros((), bf16))
        qm2_sc[pr, QB:2 * QB, :] = jnp.where(lane >= HEAD_DIM, qp, jnp.zeros((), bf16))
        q2 = qm2_sc[pr].astype(f32)
        qn = jnp.sqrt(jnp.sum(q2 * q2, axis=1, keepdims=True))
        m_sc[pr] = qn * jnp.where(row2 < QB, kmax_sm[2 * pr], kmax_sm[2 * pr + 1])
    acc_sc[...] = jnp.zeros(acc_sc.shape, f32)

    def pair_logits(c, pr, bias):
        koff = pl.multiple_of(c * TK, TK)
        kc = k_ref[pl.ds(koff, TK), pr * LANES:(pr + 1) * LANES]
        s2 = lax.dot_general(qm2_sc[pr], kc, (((1,), (1,)), ((), ())), preferred_element_type=f32)
        return s2[0:QB] + bias, s2[QB:2 * QB] + bias

    def max_chunk(c, carry):
        bias = jnp.where(key_sc[c] >= thr, 0.0, -jnp.inf)
        for pr in range(N_HEADS // 2):
            sa, sb = pair_logits(c, pr, bias)
            mrun_sc[pr, 0:QB, :] = jnp.maximum(mrun_sc[pr, 0:QB, :], _lane_fold(sa, jnp.maximum))
            mrun_sc[pr, QB:2 * QB, :] = jnp.maximum(mrun_sc[pr, QB:2 * QB, :], _lane_fold(sb, jnp.maximum))
        return carry

    def acc_chunk(c, carry):
        koff = pl.multiple_of(c * TK, TK)
        bias = jnp.where(key_sc[c] >= thr, 0.0, -jnp.inf)
        for pr in range(N_HEADS // 2):
            sa, sb = pair_logits(c, pr, bias)
            m = m_sc[pr]
            pa = jnp.exp2(sa - m[0:QB]).astype(bf16)
            pb = jnp.exp2(sb - m[QB:2 * QB]).astype(bf16)
            va = vx_ref[pl.ds(koff, TK), (2 * pr) * LANES:(2 * pr + 1) * LANES]
            vb = vx_ref[pl.ds(koff, TK), (2 * pr + 1) * LANES:(2 * pr + 2) * LANES]
            acc_sc[2 * pr] += jnp.dot(pa, va, preferred_element_type=f32)
            acc_sc[2 * pr + 1] += jnp.dot(pb, vb, preferred_element_type=f32)
        return carry

    _grouped_loop(nk, acc_chunk, 0)

    lmin = jnp.float32(jnp.inf)
    for pr in range(N_HEADS // 2):
        lmin = jnp.minimum(lmin, jnp.min(acc_sc[2 * pr][:, HEAD_DIM:HEAD_DIM + 1]))
        lmin = jnp.minimum(lmin, jnp.min(acc_sc[2 * pr + 1][:, 0:1]))

    @pl.when(jnp.logical_not(lmin >= SOFTMAX_DENOM_FLOOR))
    def _():
        mrun_sc[...] = jnp.full(mrun_sc.shape, -jnp.inf, f32)
        lax.fori_loop(0, nk, max_chunk, 0)
        for pr in range(N_HEADS // 2):
            m = jnp.max(mrun_sc[pr], axis=1, keepdims=True)
            m_sc[pr] = jnp.where(m == -jnp.inf, 0.0, m)
        acc_sc[...] = jnp.zeros(acc_sc.shape, f32)
        lax.fori_loop(0, nk, acc_chunk, 0)

    for pr in range(N_HEADS // 2):
        a0 = acc_sc[2 * pr]
        a1 = acc_sc[2 * pr + 1]
        o0 = a0 / a0[:, HEAD_DIM:HEAD_DIM + 1]
        o1 = a1 / a1[:, 0:1]
        o_ref[:, pr * LANES:(pr + 1) * LANES] = jnp.where(lane < HEAD_DIM, o0, o1).astype(bf16)


def _dsa_attention(q, qc, kw, k, vx, batch, seq):
    t = q.shape[0]
    nq = seq // QB
    topk = min(TOPK_MAX, seq // 4)
    qrow = lambda b, i: (b * nq + i, 0)
    per_batch = lambda b, i: (b, 0)
    return pl.pallas_call(
        functools.partial(_dsa_kernel, topk),
        grid=(batch, nq),
        in_specs=[
            pl.BlockSpec((QB, ATTN_WIDTH), qrow),
            pl.BlockSpec((QB, IDX_HEADS * LANES), qrow),
            pl.BlockSpec((QB, LANES), qrow),
            pl.BlockSpec((seq, ATTN_WIDTH), per_batch, pipeline_mode=pl.Buffered(1)),
            pl.BlockSpec((seq, N_HEADS * LANES), per_batch, pipeline_mode=pl.Buffered(1)),
            pl.BlockSpec((seq, LANES), per_batch, pipeline_mode=pl.Buffered(1)),
        ],
        out_specs=pl.BlockSpec((QB, ATTN_WIDTH), qrow),
        out_shape=jax.ShapeDtypeStruct((t, ATTN_WIDTH), bf16),
        scratch_shapes=[
            pltpu.VMEM((seq // TK, QB, TK), i32),
            pltpu.VMEM((seq, LANES), bf16),
            pltpu.VMEM((N_HEADS // 2, 2 * QB, LANES), bf16),
            pltpu.VMEM((N_HEADS // 2, 2 * QB, LANES), f32),
            pltpu.VMEM((N_HEADS // 2, 2 * QB, 1), f32),
            pltpu.VMEM((N_HEADS, QB, LANES), f32),
            pltpu.SMEM((N_HEADS,), f32),
        ],
        compiler_params=pltpu.CompilerParams(
            dimension_semantics=("arbitrary", "arbitrary"), vmem_limit_bytes=VMEM_LIMIT),
        name="dsa_attention",
    )(q, qc, kw, k, vx, kw)


def _merge_kernel(moe, seq, *refs):
    if moe:
        (x_ref, a_ref, u_ref, uh_ref, ga_ref, gp_ref, wp_ref, ps_ref, pa_ref, pb_ref, wo_ref, g2_ref,
         wrh_ref, wrl_ref, x2_ref, h2_ref, gate_ref, e_sc) = refs
    else:
        (x_ref, a_ref, u_ref, uh_ref, ga_ref, gp_ref, wp_ref, ps_ref, pa_ref, pb_ref, wo_ref, g2_ref,
         x2_ref, h2_ref, e_sc) = refs
    tm = TM_MERGE
    i = pl.program_id(0)
    ti = i % (seq // tm)
    e_sc[0:POOL_HALO, :] = jnp.where(ti == 0, 0.0, uh_ref[...])
    e_sc[POOL_HALO:POOL_HALO + tm, :] = u_ref[...]
    npos = (ti * tm + 1 + lax.broadcasted_iota(i32, (tm, 1), 0)).astype(f32)

    parts = []
    for g, w in enumerate(POOL_WINDOWS):
        sl = slice(g * POOL_GROUP_DIM, (g + 1) * POOL_GROUP_DIM)
        tot = e_sc[POOL_HALO:POOL_HALO + tm, sl]
        for j in range(1, w):
            tot = tot + e_sc[POOL_HALO - j:POOL_HALO - j + tm, sl]
        diff = tot / jnp.minimum(npos, float(w)) - u_ref[:, sl]
        parts.append(jnp.dot(diff.astype(bf16), wp_ref[g], preferred_element_type=f32))
    p = jnp.concatenate(parts, axis=1) * ps_ref[...]

    ab = jnp.dot(a_ref[...], pa_ref[...], preferred_element_type=f32)
    pb = jnp.dot(p.astype(bf16), pb_ref[...], preferred_element_type=f32)
    merged = _sigmoid(ga_ref[...]) * ab + _sigmoid(gp_ref[...]) * pb
    x2 = x_ref[...] + jnp.dot(merged.astype(bf16), wo_ref[...], preferred_element_type=f32)
    x2_ref[...] = x2
    ms = jnp.mean(x2 * x2, axis=-1, keepdims=True)
    hf = x2 * lax.rsqrt(ms + EPS) * g2_ref[...]
    h2_ref[...] = hf.astype(h2_ref.dtype)

    if moe:
        hi = hf.astype(bf16)
        lo = (hf - hi.astype(f32)).astype(bf16)
        logits = (jnp.dot(hi, wrh_ref[...], preferred_element_type=f32)
                  + jnp.dot(lo, wrh_ref[...], preferred_element_type=f32)
                  + jnp.dot(hi, wrl_ref[...], preferred_element_type=f32))
        lanef = lax.broadcasted_iota(i32, (1, LANES), 1).astype(f32)
        lg = jnp.where(lanef < N_EXPERTS, logits, -jnp.inf)
        v1 = jnp.max(lg, axis=1, keepdims=True)
        i1 = jnp.min(jnp.where(lg == v1, lanef, float(LANES)), axis=1, keepdims=True)
        lg2 = jnp.where(lanef == i1, -jnp.inf, lg)
        v2 = jnp.max(lg2, axis=1, keepdims=True)
        i2 = jnp.min(jnp.where(lg2 == v2, lanef, float(LANES)), axis=1, keepdims=True)
        tt = jnp.exp(v2 - v1)
        w1 = 1.0 / (1.0 + tt)
        w2 = tt / (1.0 + tt)
        route = jnp.where((lanef == i1) | (lanef == i2), 1.0, 0.0)
        for ln, val in ((ROUTE_W1, w1), (ROUTE_W2, w2), (ROUTE_I1, i1), (ROUTE_I2, i2)):
            route = jnp.where(lanef == float(ln), val, route)
        gate_ref[...] = route


def _merge(moe, seq, x2d, a, u, ga, gp, wp, ps, pa, pb, wo, g2, wrh=None, wrl=None):
    t = x2d.shape[0]
    tm = TM_MERGE
    row = lambda i: (i, 0)
    const2 = lambda i: (0, 0)
    const3 = lambda i: (0, 0, 0)
    halo = lambda i: (jnp.maximum(i * (tm // POOL_HALO) - 1, 0), 0)
    in_specs = [
        pl.BlockSpec((tm, D_MODEL), row),
        pl.BlockSpec((tm, ATTN_WIDTH), row),
        pl.BlockSpec((tm, POOL_WIDTH), row),
        pl.BlockSpec((POOL_HALO, POOL_WIDTH), halo),
        pl.BlockSpec((tm, D_MODEL), row),
        pl.BlockSpec((tm, D_MODEL), row),
        pl.BlockSpec((POOL_GROUPS, POOL_GROUP_DIM, POOL_GROUP_DIM), const3),
        pl.BlockSpec((1, POOL_WIDTH), const2),
        pl.BlockSpec((ATTN_WIDTH, D_MODEL), const2),
        pl.BlockSpec((POOL_WIDTH, D_MODEL), const2),
        pl.BlockSpec((D_MODEL, D_MODEL), const2),
        pl.BlockSpec((1, D_MODEL), const2),
    ]
    args = [x2d, a, u, u, ga, gp, wp, ps, pa, pb, wo, g2]
    out_shapes = [jax.ShapeDtypeStruct((t, D_MODEL), f32), jax.ShapeDtypeStruct((t, D_MODEL), f32 if moe else bf16)]
    out_specs = [pl.BlockSpec((tm, D_MODEL), row), pl.BlockSpec((tm, D_MODEL), row)]
    if moe:
        in_specs += [pl.BlockSpec((D_MODEL, LANES), const2), pl.BlockSpec((D_MODEL, LANES), const2)]
        args += [wrh, wrl]
        out_shapes.append(jax.ShapeDtypeStruct((t, LANES), f32))
        out_specs.append(pl.BlockSpec((tm, LANES), row))
    return pl.pallas_call(
        functools.partial(_merge_kernel, moe, seq),
        grid=(t // tm,),
        in_specs=in_specs,
        out_specs=out_specs,
        out_shape=out_shapes,
        scratch_shapes=[pltpu.VMEM((tm + POOL_HALO, POOL_WIDTH), f32)],
        compiler_params=pltpu.CompilerParams(
            dimension_semantics=("arbitrary",), vmem_limit_bytes=VMEM_LIMIT),
        name="merge_moe" if moe else "merge_dense",
    )(*args)


def _swiglu_partial(h, w1, w3, w2):
    a = jnp.dot(h, w1, preferred_element_type=f32)
    b = jnp.dot(h, w3, preferred_element_type=f32)
    act = (a * _sigmoid(a) * b).astype(bf16)
    return jnp.dot(act, w2, preferred_element_type=f32)


def _ffn_kernel(h_ref, x_ref, w1_ref, w3_ref, w2_ref, o_ref):
    @pl.when(pl.program_id(1) == 0)
    def _():
        o_ref[...] = x_ref[...]

    o_ref[...] += _swiglu_partial(h_ref[...], w1_ref[...], w3_ref[...], w2_ref[...])


def _ffn(h2, x2, w1, w3, w2, tm, tf):
    t = h2.shape[0]
    ff = w1.shape[1]
    row = lambda i, j: (i, 0)
    return pl.pallas_call(
        _ffn_kernel,
        grid=(t // tm, ff // tf),
        in_specs=[
            pl.BlockSpec((tm, D_MODEL), row),
            pl.BlockSpec((tm, D_MODEL), row),
            pl.BlockSpec((D_MODEL, tf), lambda i, j: (0, j)),
            pl.BlockSpec((D_MODEL, tf), lambda i, j: (0, j)),
            pl.BlockSpec((tf, D_MODEL), lambda i, j: (j, 0)),
        ],
        out_specs=pl.BlockSpec((tm, D_MODEL), row),
        out_shape=jax.ShapeDtypeStruct((t, D_MODEL), f32),
        compiler_params=pltpu.CompilerParams(
            dimension_semantics=("arbitrary", "arbitrary"), vmem_limit_bytes=VMEM_LIMIT),
        name="ffn_dense",
    )(h2, x2, w1, w3, w2)


def _rank_kernel(route_ref, tri_ref, rk_ref, cnt_ref, carry_sc):
    @pl.when(pl.program_id(0) == 0)
    def _():
        carry_sc[...] = jnp.zeros(carry_sc.shape, f32)

    lane = lax.broadcasted_iota(i32, (1, LANES), 1)
    lanef = lane.astype(f32)
    r = route_ref[...]
    sel = jnp.where(lane < N_EXPERTS, r, 0.0)
    ranks = jnp.dot(tri_ref[...], sel.astype(bf16), preferred_element_type=f32) + carry_sc[...]
    rk1 = jnp.sum(jnp.where(lanef == r[:, ROUTE_I1:ROUTE_I1 + 1], ranks, 0.0), axis=1, keepdims=True)
    rk2 = jnp.sum(jnp.where(lanef == r[:, ROUTE_I2:ROUTE_I2 + 1], ranks, 0.0), axis=1, keepdims=True)
    rk_ref[...] = jnp.where(lane == 0, rk1, jnp.where(lane == 1, rk2, 0.0))
    carry_sc[...] += jnp.sum(sel, axis=0, keepdims=True)
    cnt_ref[...] = carry_sc[...]


def _moe_rank(route):
    t = route.shape[0]
    tm = TM_RANK
    idx = jnp.arange(tm)
    tri = (idx[None, :] < idx[:, None]).astype(bf16)
    return pl.pallas_call(
        _rank_kernel,
        grid=(t // tm,),
        in_specs=[pl.BlockSpec((tm, LANES), lambda i: (i, 0)), pl.BlockSpec((tm, tm), lambda i: (0, 0))],
        out_specs=[pl.BlockSpec((tm, LANES), lambda i: (i, 0)), pl.BlockSpec((1, LANES), lambda i: (0, 0))],
        out_shape=[jax.ShapeDtypeStruct((t, LANES), f32), jax.ShapeDtypeStruct((1, LANES), f32)],
        scratch_shapes=[pltpu.VMEM((1, LANES), f32)],
        compiler_params=pltpu.CompilerParams(dimension_semantics=("arbitrary",)),
        name="moe_rank",
    )(route, tri)


def _row_copy(src, dst, sem):
    return pltpu.make_async_copy(src, dst, sem)


def _moe_ffn_kernel(pos_ref, te_ref, meta_ref, h_hbm, w1_ref, w3_ref, w2_ref, ys_ref,
                    inv_sm, gbuf, xb, sem):
    del te_ref
    i = pl.program_id(0)
    j = pl.program_id(1)
    n_used = meta_ref[0]
    n_slots = pos_ref.shape[0]

    def start_gather(tile, slot):
        def issue(g, carry):
            for u in range(ROW_UNROLL):
                r = g * ROW_UNROLL + u
                tok = inv_sm[tile * TM_EXPERT + r]
                _row_copy(h_hbm.at[pl.ds(tok, 1)], gbuf.at[slot, pl.ds(r, 1)],
                          sem.at[slot]).start(priority=u % 2)
            return carry
        lax.fori_loop(0, TM_EXPERT // ROW_UNROLL, issue, 0)

    def wait_gather(slot):
        _row_copy(h_hbm.at[pl.ds(0, TM_EXPERT)], gbuf.at[slot], sem.at[slot]).wait()

    @pl.when((i == 0) & (j == 0))
    def _():
        for e in range(N_EXPERTS):
            def pad(r, carry):
                inv_sm[r] = 0
                return carry
            lax.fori_loop(meta_ref[1 + e], meta_ref[1 + N_EXPERTS + e], pad, 0)

        def scatter(g, carry):
            for u in range(ROW_UNROLL):
                n = g * ROW_UNROLL + u
                inv_sm[pos_ref[n]] = lax.shift_right_logical(n, 1)
            return carry
        lax.fori_loop(0, n_slots // ROW_UNROLL, scatter, 0)

        @pl.when(n_used > 0)
        def _():
            start_gather(0, 0)

    used = i < n_used

    @pl.when(used & (j == 0))
    def _():
        slot = i % 2
        wait_gather(slot)

        @pl.when(i + 1 < n_used)
        def _():
            start_gather(i + 1, 1 - slot)

        xb[...] = gbuf[slot].astype(bf16)

    @pl.when(used)
    def _():
        y = _swiglu_partial(xb[...], w1_ref[0], w3_ref[0], w2_ref[0])

        @pl.when(j == 0)
        def _():
            ys_ref[...] = y

        @pl.when(j > 0)
        def _():
            ys_ref[...] += y

    @pl.when(jnp.logical_not(used) & (j == 0))
    def _():
        ys_ref[...] = jnp.zeros(ys_ref.shape, f32)


def _moe_ffn(pos, tile_expert, meta, hf, w1, w3, w2, n_rows):
    ff = w1.shape[2]
    nj = ff // TF_EXPERT
    jj = lambda i, j, meta: jnp.where(i < meta[0], j, nj - 1)
    return pl.pallas_call(
        _moe_ffn_kernel,
        grid_spec=pltpu.PrefetchScalarGridSpec(
            num_scalar_prefetch=3,
            grid=(n_rows // TM_EXPERT, nj),
            in_specs=[
                pl.BlockSpec(memory_space=pl.ANY),
                pl.BlockSpec((1, D_MODEL, TF_EXPERT), lambda i, j, pos, te, meta: (te[i], 0, jj(i, j, meta))),
                pl.BlockSpec((1, D_MODEL, TF_EXPERT), lambda i, j, pos, te, meta: (te[i], 0, jj(i, j, meta))),
                pl.BlockSpec((1, TF_EXPERT, D_MODEL), lambda i, j, pos, te, meta: (te[i], jj(i, j, meta), 0)),
            ],
            out_specs=pl.BlockSpec((TM_EXPERT, D_MODEL), lambda i, j, pos, te, meta: (i, 0)),
            scratch_shapes=[
                pltpu.SMEM((n_rows,), i32),
                pltpu.VMEM((2, TM_EXPERT, D_MODEL), f32),
                pltpu.VMEM((TM_EXPERT, D_MODEL), bf16),
                pltpu.SemaphoreType.DMA((2,)),
            ],
        ),
        out_shape=jax.ShapeDtypeStruct((n_rows, D_MODEL), f32),
        compiler_params=pltpu.CompilerParams(
            dimension_semantics=("arbitrary", "arbitrary"), vmem_limit_bytes=VMEM_LIMIT),
        name="moe_ffn",
    )(pos, tile_expert, meta, hf, w1, w3, w2)


def _combine_kernel(pos_ref, ys_hbm, x2_ref, route_ref, o_ref, buf, sem):
    base = pl.program_id(0) * TM_ROWS

    def issue(g, carry):
        for u in range(ROW_UNROLL // 2):
            r = g * (ROW_UNROLL // 2) + u
            for s in range(2):
                _row_copy(ys_hbm.at[pl.ds(pos_ref[2 * (base + r) + s], 1)], buf.at[s, pl.ds(r, 1)],
                          sem).start(priority=s)
        return carry

    lax.fori_loop(0, TM_ROWS // (ROW_UNROLL // 2), issue, 0)
    for s in range(2):
        _row_copy(ys_hbm.at[pl.ds(0, TM_ROWS)], buf.at[s], sem).wait()
    route = route_ref[...]
    o_ref[...] = (x2_ref[...] + route[:, ROUTE_W1:ROUTE_W1 + 1] * buf[0]
                  + route[:, ROUTE_W2:ROUTE_W2 + 1] * buf[1])


def _moe_combine(pos, ys, x2, route):
    t = x2.shape[0]
    row = lambda i, pos: (i, 0)
    return pl.pallas_call(
        _combine_kernel,
        grid_spec=pltpu.PrefetchScalarGridSpec(
            num_scalar_prefetch=1,
            grid=(t // TM_ROWS,),
            in_specs=[
                pl.BlockSpec(memory_space=pl.ANY),
                pl.BlockSpec((TM_ROWS, D_MODEL), row),
                pl.BlockSpec((TM_ROWS, LANES), row),
            ],
            out_specs=pl.BlockSpec((TM_ROWS, D_MODEL), row),
            scratch_shapes=[pltpu.VMEM((2, TM_ROWS, D_MODEL), f32), pltpu.SemaphoreType.DMA(())],
        ),
        out_shape=jax.ShapeDtypeStruct((t, D_MODEL), f32),
        compiler_params=pltpu.CompilerParams(dimension_semantics=("arbitrary",)),
        name="moe_combine",
    )(pos, ys, x2, route)


def _moe_layer(route, hf, x2, w1, w3, w2):
    t = hf.shape[0]
    n_tiles = 2 * t // TM_EXPERT + N_EXPERTS
    rk, cnt = _moe_rank(route)
    counts = cnt[0, :N_EXPERTS].astype(i32)
    padded = (counts + TM_EXPERT - 1) // TM_EXPERT * TM_EXPERT
    ends = jnp.cumsum(padded)
    starts = ends - padded
    n_used = (ends[-1] // TM_EXPERT).astype(i32)
    tile_id = jnp.arange(n_tiles, dtype=i32)
    tile_expert = jnp.minimum(jnp.sum((tile_id[:, None] * TM_EXPERT >= ends[None, :]).astype(i32), axis=1),
                              N_EXPERTS - 1)
    tile_expert = jnp.where(tile_id < n_used, tile_expert, tile_expert[jnp.maximum(n_used - 1, 0)])
    experts = jnp.arange(N_EXPERTS, dtype=f32)[None, :]
    start_of = lambda ids: jnp.sum(jnp.where(ids[:, None] == experts, starts[None, :], 0), axis=1)
    pos1 = start_of(route[:, ROUTE_I1]) + rk[:, 0].astype(i32)
    pos2 = start_of(route[:, ROUTE_I2]) + rk[:, 1].astype(i32)
    pos = jnp.stack([pos1, pos2], axis=1).reshape(2 * t).astype(i32)

    meta = jnp.concatenate([n_used.reshape(1), starts + counts, ends]).astype(i32)
    ys = _moe_ffn(pos, tile_expert, meta, hf, w1, w3, w2, n_tiles * TM_EXPERT)
    return _moe_combine(pos, ys, x2, route)


def _rope_tables(seq, rot_dim, period, active_lanes):
    half = rot_dim // 2
    inv = jnp.power(ROPE_THETA, -jnp.arange(0, rot_dim, 2, dtype=f32) / rot_dim)
    ang = jnp.arange(seq, dtype=f32)[:, None] * inv[None, :]
    cos, sin = jnp.cos(ang), jnp.sin(ang)
    ones = jnp.ones((seq, period - rot_dim), f32)
    zeros = lambda n: jnp.zeros((seq, n), f32)
    cos_p = jnp.concatenate([cos, cos, ones], axis=1)
    sin_a = jnp.concatenate([-sin, zeros(period - half)], axis=1)
    sin_b = jnp.concatenate([zeros(half), sin, zeros(period - rot_dim)], axis=1)
    reps = LANES // period
    tabs = [jnp.tile(tb, (1, reps)) for tb in (cos_p, sin_a, sin_b)]
    live = (jnp.arange(LANES) < active_lanes)[None, :]
    tabs = [jnp.where(live, tabs[0], 1.0), jnp.where(live, tabs[1], 0.0), jnp.where(live, tabs[2], 0.0)]
    return jnp.stack(tabs, axis=0)


def _relayout_w_in(w):
    d = w.shape[0]
    o = 0
    segs = {}
    for name, n in (("q", 512), ("k", 512), ("v", 512), ("qi", 256), ("ki", 32), ("wi", 8),
                    ("u", 512), ("ga", 1024), ("gp", 1024)):
        segs[name] = w[:, o:o + n]
        o += n
    qi = segs["qi"].reshape(d, IDX_HEADS, IDX_DIM)
    qcat = jnp.concatenate([qi, qi, qi, jnp.zeros_like(qi)], axis=-1).reshape(d, IDX_HEADS * LANES)
    kwc = jnp.concatenate([segs["ki"], segs["ki"], segs["ki"], segs["wi"],
                           jnp.zeros((d, LANES - 3 * IDX_DIM - IDX_HEADS), w.dtype)], axis=1)
    vh = segs["v"].reshape(d, N_HEADS // 2, 2, HEAD_DIM)
    zh = jnp.zeros_like(vh[:, :, 0])
    vx = jnp.stack([jnp.concatenate([vh[:, :, 0], zh], axis=-1), jnp.concatenate([zh, vh[:, :, 1]], axis=-1)],
                   axis=2).reshape(d, N_HEADS * LANES)
    out = jnp.concatenate([segs["q"], segs["k"], vx, qcat, kwc, segs["u"], segs["ga"], segs["gp"]], axis=1)
    return out.astype(bf16)


def kernel(x, mix_norm, w_in, q_norm, k_norm, w_pool, pool_scale, w_attn_proj, w_pool_proj, w_out, ffn_norm,
           dense_w1, dense_w3, dense_w2, moe_router, moe_w1, moe_w3, moe_w2):
    batch, seq, d = x.shape
    depth = w_in.shape[0]
    t = batch * seq
    assert d == D_MODEL and seq % TK == 0 and seq % TM_PROJ == 0 and seq % TM_MERGE == 0

    rqk = _rope_tables(seq, ROPE_DIM, HEAD_DIM, LANES)
    rqi = _rope_tables(seq, IDX_ROPE_DIM, IDX_DIM, LANES)
    rki = _rope_tables(seq, IDX_ROPE_DIM, IDX_DIM, 3 * IDX_DIM)
    head_of = jnp.arange(ATTN_WIDTH) // HEAD_DIM
    bd = jnp.where(head_of[:, None] == head_of[None, :], 1.0 / HEAD_DIM, 0.0).astype(bf16)
    lane_in_tile = jnp.arange(N_HEADS * LANES) % LANES
    odd_head = (jnp.arange(N_HEADS * LANES) // LANES) % 2 == 1
    ones_lane = (lane_in_tile == jnp.where(odd_head, 0, HEAD_DIM)).astype(f32)[None, :]

    xc = x.reshape(t, d)
    for layer in range(depth):
        w = _relayout_w_in(w_in[layer])
        qn = jnp.tile(q_norm[layer], N_HEADS)[None, :]
        kn = jnp.tile(k_norm[layer], N_HEADS)[None, :]
        q, k, vx, qc, kw, u, ga, gp = _in_proj(xc, mix_norm[layer][None, :], w, qn, kn, bd, ones_lane,
                                               rqk, rqi, rki, seq)
        a = _dsa_attention(q, qc, kw, k, vx, batch, seq)
        moe = layer % 2 == 1
        common = (xc, a, u, ga, gp, w_pool[layer].astype(bf16), pool_scale[layer][None, :],
                  w_attn_proj[layer].astype(bf16), w_pool_proj[layer].astype(bf16), w_out[layer].astype(bf16),
                  ffn_norm[layer][None, :])
        idx = layer // 2
        if moe:
            wr = jnp.pad(moe_router[idx], ((0, 0), (0, LANES - N_EXPERTS)))
            wrh = wr.astype(bf16)
            wrl = (wr - wrh.astype(f32)).astype(bf16)
            x2, hf, route = _merge(True, seq, *common, wrh, wrl)
            xc = _moe_layer(route, hf, x2, moe_w1[idx].astype(bf16), moe_w3[idx].astype(bf16),
                            moe_w2[idx].astype(bf16))
        else:
            x2, h2 = _merge(False, seq, *common)
            xc = _ffn(h2, x2, dense_w1[idx].astype(bf16), dense_w3[idx].astype(bf16),
                      dense_w2[idx].astype(bf16), tm=1024, tf=1408)
    return xc.reshape(batch, seq, d)
```

```python
import functools

import jax
import jax.numpy as jnp
from jax import lax
from jax.experimental import pallas as pl
from jax.experimental.pallas import tpu as pltpu

bf16 = jnp.bfloat16
f32 = jnp.float32
i32 = jnp.int32

D_MODEL = 1024
N_HEADS = 8
HEAD_DIM = 64
ATTN_WIDTH = 512
ROPE_DIM = 16
ROPE_THETA = 500000.0
IDX_HEADS = 8
IDX_DIM = 32
IDX_ROPE_DIM = 8
TOPK_MAX = 256
POOL_GROUPS = 4
POOL_GROUP_DIM = 128
POOL_WIDTH = 512
POOL_WINDOWS = (2, 4, 8, 16)
POOL_HALO = 16
N_EXPERTS = 8
EPS = 1e-6

LANES = 128
INT_MIN = -(2 ** 31)
VMEM_LIMIT = 56 * 1024 * 1024
LOG2E = 1.4426950408889634
SOFTMAX_DENOM_FLOOR = 2.0 ** -90

OFF_Q = 0
OFF_K = 512
OFF_VX = 1024
OFF_QCAT = 2048
OFF_KW = 3072
OFF_U = 3200
OFF_GA = 3712
OFF_GP = 4736
W_COLS = 5760

ROUTE_W1, ROUTE_W2, ROUTE_I1, ROUTE_I2 = 8, 9, 10, 11

TM_PROJ = 512
TM_MERGE = 512
TM_RANK = 512
TM_EXPERT = 512
TF_EXPERT = 1792
TM_ROWS = 256
ROW_UNROLL = 8
QB = 256
COUNT_ROWS = 128
CHUNKS_PER_TRIP = 4
TK = 512


def _sigmoid(x):
    return 1.0 / (1.0 + jnp.exp(-x))


def _in_proj_kernel(x_ref, g_ref, w_ref, qn_ref, kn_ref, bd_ref, one_ref, rqk_ref, rqi_ref, rki_ref,
                    q_ref, k_ref, vx_ref, qc_ref, kw_ref, u_ref, ga_ref, gp_ref):
    x = x_ref[...]
    ms = jnp.mean(x * x, axis=-1, keepdims=True)
    h = (x * lax.rsqrt(ms + EPS) * g_ref[...]).astype(bf16)

    def proj(lo, n):
        return jnp.dot(h, w_ref[:, lo:lo + n], preferred_element_type=f32)

    def rope(xc, tab_ref, sh):
        return (xc * tab_ref[0] + pltpu.roll(xc, LANES - sh, 1) * tab_ref[1]
                + pltpu.roll(xc, sh, 1) * tab_ref[2])

    def headnorm(z, gain_ref):
        msh = jnp.dot((z * z).astype(bf16), bd_ref[...], preferred_element_type=f32)
        return z * lax.rsqrt(msh + EPS) * gain_ref[...]

    zq = headnorm(proj(OFF_Q, ATTN_WIDTH), qn_ref) * (HEAD_DIM ** -0.5 * LOG2E)
    zk = headnorm(proj(OFF_K, ATTN_WIDTH), kn_ref)
    for c in range(ATTN_WIDTH // LANES):
        sl = slice(c * LANES, (c + 1) * LANES)
        q_ref[:, sl] = rope(zq[:, sl], rqk_ref, ROPE_DIM // 2).astype(bf16)
        k_ref[:, sl] = rope(zk[:, sl], rqk_ref, ROPE_DIM // 2).astype(bf16)
    vx_ref[...] = jnp.where(one_ref[...] > 0.0, 1.0, proj(OFF_VX, N_HEADS * LANES)).astype(bf16)

    lane = lax.broadcasted_iota(i32, (1, LANES), 1)
    zc = proj(OFF_QCAT, IDX_HEADS * LANES)
    for c in range(IDX_HEADS):
        sl = slice(c * LANES, (c + 1) * LANES)
        r = rope(zc[:, sl], rqi_ref, IDX_ROPE_DIM // 2)
        lo = r - r.astype(bf16).astype(f32)
        qc_ref[:, sl] = jnp.where((lane >= 32) & (lane < 64), lo, r).astype(bf16)

    zkw = rope(proj(OFF_KW, LANES), rki_ref, IDX_ROPE_DIM // 2)
    lo = zkw - zkw.astype(bf16).astype(f32)
    zkw = jnp.where((lane >= 64) & (lane < 96), lo, zkw)
    kw_ref[...] = jnp.where(lane >= 96, zkw * ((IDX_HEADS * IDX_DIM) ** -0.5), zkw)

    u_ref[...] = proj(OFF_U, POOL_WIDTH)
    ga_ref[...] = proj(OFF_GA, D_MODEL)
    gp_ref[...] = proj(OFF_GP, D_MODEL)


def _in_proj(x2d, g, w, qn, kn, bd, ones_lane, rqk, rqi, rki, seq):
    t = x2d.shape[0]
    tm = TM_PROJ
    ns = seq // tm
    row = lambda i: (i, 0)
    const = lambda i: (0, 0)
    tab = lambda i: (0, i % ns, 0)
    out_shapes = (
        jax.ShapeDtypeStruct((t, ATTN_WIDTH), bf16),
        jax.ShapeDtypeStruct((t, ATTN_WIDTH), bf16),
        jax.ShapeDtypeStruct((t, N_HEADS * LANES), bf16),
        jax.ShapeDtypeStruct((t, IDX_HEADS * LANES), bf16),
        jax.ShapeDtypeStruct((t, LANES), f32),
        jax.ShapeDtypeStruct((t, POOL_WIDTH), f32),
        jax.ShapeDtypeStruct((t, D_MODEL), f32),
        jax.ShapeDtypeStruct((t, D_MODEL), f32),
    )
    return pl.pallas_call(
        _in_proj_kernel,
        grid=(t // tm,),
        in_specs=[
            pl.BlockSpec((tm, D_MODEL), row),
            pl.BlockSpec((1, D_MODEL), const),
            pl.BlockSpec((D_MODEL, W_COLS), const, pipeline_mode=pl.Buffered(1)),
            pl.BlockSpec((1, ATTN_WIDTH), const),
            pl.BlockSpec((1, ATTN_WIDTH), const),
            pl.BlockSpec((ATTN_WIDTH, ATTN_WIDTH), const),
            pl.BlockSpec((1, N_HEADS * LANES), const),
            pl.BlockSpec((3, tm, LANES), tab),
            pl.BlockSpec((3, tm, LANES), tab),
            pl.BlockSpec((3, tm, LANES), tab),
        ],
        out_specs=[pl.BlockSpec((tm, s.shape[1]), row) for s in out_shapes],
        out_shape=out_shapes,
        compiler_params=pltpu.CompilerParams(
            dimension_semantics=("arbitrary",), vmem_limit_bytes=VMEM_LIMIT),
        name="in_proj",
    )(x2d, g, w, qn, kn, bd, ones_lane, rqk, rqi, rki)


def _lane_fold(x, op):
    out = x[:, 0:LANES]
    for j in range(1, x.shape[1] // LANES):
        out = op(out, x[:, j * LANES:(j + 1) * LANES])
    return out


def _grouped_loop(n, body, init):
    def group(g, carry):
        for u in range(CHUNKS_PER_TRIP):
            carry = body(CHUNKS_PER_TRIP * g + u, carry)
        return carry
    carry = lax.fori_loop(0, n // CHUNKS_PER_TRIP, group, init)
    return lax.fori_loop(n // CHUNKS_PER_TRIP * CHUNKS_PER_TRIP, n, body, carry)


def _dsa_kernel(topk, q_ref, qc_ref, kwq_ref, k_ref, vx_ref, kw_ref, o_ref,
                key_sc, kcat_sc, qm2_sc, mrun_sc, m_sc, acc_sc, kmax_sm):
    i = pl.program_id(1)
    nk = (i * QB + QB + TK - 1) // TK
    kf = float(topk)

    lane = lax.broadcasted_iota(i32, (1, LANES), 1)

    @pl.when(i == 0)
    def _():
        kcat_sc[...] = kw_ref[...].astype(bf16)
        for pr in range(N_HEADS // 2):
            def norms(c, carry):
                kk = k_ref[pl.ds(pl.multiple_of(c * TK, TK), TK), pr * LANES:(pr + 1) * LANES].astype(f32)
                sq = kk * kk
                n0 = jnp.sqrt(jnp.sum(jnp.where(lane < HEAD_DIM, sq, 0.0), axis=1, keepdims=True))
                n1 = jnp.sqrt(jnp.sum(jnp.where(lane >= HEAD_DIM, sq, 0.0), axis=1, keepdims=True))
                return jnp.maximum(carry[0], jnp.max(n0)), jnp.maximum(carry[1], jnp.max(n1))
            k0, k1 = lax.fori_loop(0, k_ref.shape[0] // TK, norms, (jnp.float32(0.0), jnp.float32(0.0)))
            kmax_sm[2 * pr] = k0
            kmax_sm[2 * pr + 1] = k1

    qpos = i * QB + lax.broadcasted_iota(i32, (QB, 1), 0)
    lane_tk = lax.broadcasted_iota(i32, (1, TK), 1)
    wq = kwq_ref[...]

    def to_key(score):
        bits = pltpu.bitcast(score, i32)
        return bits ^ ((bits >> 31) & 0x7FFFFFFF)

    def score_chunk(c, carry):
        koff = pl.multiple_of(c * TK, TK)
        kc = kcat_sc[pl.ds(koff, TK), :]
        acc = jnp.zeros((QB, TK), f32)
        for h in range(IDX_HEADS):
            d = lax.dot_general(qc_ref[:, h * LANES:(h + 1) * LANES], kc,
                                (((1,), (1,)), ((), ())), preferred_element_type=f32)
            acc = acc + jnp.maximum(d, 0.0) * wq[:, 96 + h:97 + h]
        causal = (koff + lane_tk) <= qpos
        key_sc[c] = jnp.where(causal, to_key(acc), INT_MIN)
        return carry

    _grouped_loop(nk, score_chunk, 0)

    def count_ge(cand):
        counts = []
        for r0 in range(0, QB, COUNT_ROWS):
            cand_g = cand[r0:r0 + COUNT_ROWS]

            def body(c, part, r0=r0, cand_g=cand_g):
                for j in range(TK // LANES):
                    kk = key_sc[c, r0:r0 + COUNT_ROWS, j * LANES:(j + 1) * LANES]
                    part = part + jnp.where(kk >= cand_g, 1, 0)
                return part

            counts.append(_grouped_loop(nk, body, jnp.zeros((COUNT_ROWS, LANES), i32)))
        return jnp.sum(jnp.concatenate(counts, axis=0).astype(f32), axis=1, keepdims=True)

    zero = jnp.zeros((QB, 1), i32)
    prefix = jnp.where(count_ge(zero) >= kf, zero, INT_MIN)

    def bit_body(b, prefix):
        cand = prefix | jnp.left_shift(1, 30 - b)
        return jnp.where(count_ge(cand) >= kf, cand, prefix)

    thr_raw = lax.fori_loop(0, 31, bit_body, prefix)
    thr = jnp.maximum(thr_raw, INT_MIN + 1)

    is_tie = (count_ge(thr) > kf) & (thr_raw > INT_MIN)

    @pl.when(jnp.max(jnp.where(is_tie, 1.0, 0.0)) > 0.0)
    def _():
        need = kf - count_ge(thr + 1)
        upper = jnp.where(lax.broadcasted_iota(i32, (TK, TK), 0) <= lax.broadcasted_iota(i32, (TK, TK), 1),
                          1.0, 0.0).astype(bf16)

        def drop_ties(c, seen):
            key = key_sc[c]
            tied = jnp.where(key == thr, 1.0, 0.0)
            rank = jnp.dot(tied.astype(bf16), upper, preferred_element_type=f32) + seen
            drop = is_tie & (key == thr) & (rank > need)
            key_sc[c] = jnp.where(drop, thr - 1, key)
            return seen + jnp.sum(tied, axis=1, keepdims=True)

        lax.fori_loop(0, nk, drop_ties, jnp.zeros((QB, 1), f32))

    row2 = lax.broadcasted_iota(i32, (2 * QB, 1), 0)
    for pr in range(N_HEADS // 2):
        qp = q_ref[:, pr * LANES:(pr + 1) * LANES]
        qm2_sc[pr, 0:QB, :] = jnp.where(lane < HEAD_DIM, qp, jnp.zeros((), bf16))
        qm2_sc[pr, QB:2 * QB, :] = jnp.where(lane >= HEAD_DIM, qp, jnp.zeros((), bf16))
        q2 = qm2_sc[pr].astype(f32)
        qn = jnp.sqrt(jnp.sum(q2 * q2, axis=1, keepdims=True))
        m_sc[pr] = qn * jnp.where(row2 < QB, kmax_sm[2 * pr], kmax_sm[2 * pr + 1])
    acc_sc[...] = jnp.zeros(acc_sc.shape, f32)

    def pair_logits(c, pr, bias):
        koff = pl.multiple_of(c * TK, TK)
        kc = k_ref[pl.ds(koff, TK), pr * LANES:(pr + 1) * LANES]
        s2 = lax.dot_general(qm2_sc[pr], kc, (((1,), (1,)), ((), ())), preferred_element_type=f32)
        return s2[0:QB] + bias, s2[QB:2 * QB] + bias

    def max_chunk(c, carry):
        bias = jnp.where(key_sc[c] >= thr, 0.0, -jnp.inf)
        for pr in range(N_HEADS // 2):
            sa, sb = pair_logits(c, pr, bias)
            mrun_sc[pr, 0:QB, :] = jnp.maximum(mrun_sc[pr, 0:QB, :], _lane_fold(sa, jnp.maximum))
            mrun_sc[pr, QB:2 * QB, :] = jnp.maximum(mrun_sc[pr, QB:2 * QB, :], _lane_fold(sb, jnp.maximum))
        return carry

    def acc_chunk(c, carry):
        koff = pl.multiple_of(c * TK, TK)
        bias = jnp.where(key_sc[c] >= thr, 0.0, -jnp.inf)
        for pr in range(N_HEADS // 2):
            sa, sb = pair_logits(c, pr, bias)
            m = m_sc[pr]
            pa = jnp.exp2(sa - m[0:QB]).astype(bf16)
            pb = jnp.exp2(sb - m[QB:2 * QB]).astype(bf16)
            va = vx_ref[pl.ds(koff, TK), (2 * pr) * LANES:(2 * pr + 1) * LANES]
            vb = vx_ref[pl.ds(koff, TK), (2 * pr + 1) * LANES:(2 * pr + 2) * LANES]
            acc_sc[2 * pr] += jnp.dot(pa, va, preferred_element_type=f32)
            acc_sc[2 * pr + 1] += jnp.dot(pb, vb, preferred_element_type=f32)
        return carry

    _grouped_loop(nk, acc_chunk, 0)

    lmin = jnp.float32(jnp.inf)
    for pr in range(N_HEADS // 2):
        lmin = jnp.minimum(lmin, jnp.min(acc_sc[2 * pr][:, HEAD_DIM:HEAD_DIM + 1]))
        lmin = jnp.minimum(lmin, jnp.min(acc_sc[2 * pr + 1][:, 0:1]))

    @pl.when(jnp.logical_not(lmin >= SOFTMAX_DENOM_FLOOR))
    def _():
        mrun_sc[...] = jnp.full(mrun_sc.shape, -jnp.inf, f32)
        lax.fori_loop(0, nk, max_chunk, 0)
        for pr in range(N_HEADS // 2):
            m = jnp.max(mrun_sc[pr], axis=1, keepdims=True)
            m_sc[pr] = jnp.where(m == -jnp.inf, 0.0, m)
        acc_sc[...] = jnp.zeros(acc_sc.shape, f32)
        lax.fori_loop(0, nk, acc_chunk, 0)

    for pr in range(N_HEADS // 2):
        a0 = acc_sc[2 * pr]
        a1 = acc_sc[2 * pr + 1]
        o0 = a0 / a0[:, HEAD_DIM:HEAD_DIM + 1]
        o1 = a1 / a1[:, 0:1]
        o_ref[:, pr * LANES:(pr + 1) * LANES] = jnp.where(lane < HEAD_DIM, o0, o1).astype(bf16)


def _dsa_attention(q, qc, kw, k, vx, batch, seq):
    t = q.shape[0]
    nq = seq // QB
    topk = min(TOPK_MAX, seq // 4)
    qrow = lambda b, i: (b * nq + i, 0)
    per_batch = lambda b, i: (b, 0)
    return pl.pallas_call(
        functools.partial(_dsa_kernel, topk),
        grid=(batch, nq),
        in_specs=[
            pl.BlockSpec((QB, ATTN_WIDTH), qrow),
            pl.BlockSpec((QB, IDX_HEADS * LANES), qrow),
            pl.BlockSpec((QB, LANES), qrow),
            pl.BlockSpec((seq, ATTN_WIDTH), per_batch, pipeline_mode=pl.Buffered(1)),
            pl.BlockSpec((seq, N_HEADS * LANES), per_batch, pipeline_mode=pl.Buffered(1)),
            pl.BlockSpec((seq, LANES), per_batch, pipeline_mode=pl.Buffered(1)),
        ],
        out_specs=pl.BlockSpec((QB, ATTN_WIDTH), qrow),
        out_shape=jax.ShapeDtypeStruct((t, ATTN_WIDTH), bf16),
        scratch_shapes=[
            pltpu.VMEM((seq // TK, QB, TK), i32),
            pltpu.VMEM((seq, LANES), bf16),
            pltpu.VMEM((N_HEADS // 2, 2 * QB, LANES), bf16),
            pltpu.VMEM((N_HEADS // 2, 2 * QB, LANES), f32),
            pltpu.VMEM((N_HEADS // 2, 2 * QB, 1), f32),
            pltpu.VMEM((N_HEADS, QB, LANES), f32),
            pltpu.SMEM((N_HEADS,), f32),
        ],
        compiler_params=pltpu.CompilerParams(
            dimension_semantics=("arbitrary", "arbitrary"), vmem_limit_bytes=VMEM_LIMIT),
        name="dsa_attention",
    )(q, qc, kw, k, vx, kw)


def _merge_kernel(moe, seq, *refs):
    if moe:
        (x_ref, a_ref, u_ref, uh_ref, ga_ref, gp_ref, wp_ref, ps_ref, pa_ref, pb_ref, wo_ref, g2_ref,
         wrh_ref, wrl_ref, x2_ref, h2_ref, gate_ref, e_sc) = refs
    else:
        (x_ref, a_ref, u_ref, uh_ref, ga_ref, gp_ref, wp_ref, ps_ref, pa_ref, pb_ref, wo_ref, g2_ref,
         x2_ref, h2_ref, e_sc) = refs
    tm = TM_MERGE
    i = pl.program_id(0)
    ti = i % (seq // tm)
    e_sc[0:POOL_HALO, :] = jnp.where(ti == 0, 0.0, uh_ref[...])
    e_sc[POOL_HALO:POOL_HALO + tm, :] = u_ref[...]
    npos = (ti * tm + 1 + lax.broadcasted_iota(i32, (tm, 1), 0)).astype(f32)

    parts = []
    for g, w in enumerate(POOL_WINDOWS):
        sl = slice(g * POOL_GROUP_DIM, (g + 1) * POOL_GROUP_DIM)
        tot = e_sc[POOL_HALO:POOL_HALO + tm, sl]
        for j in range(1, w):
            tot = tot + e_sc[POOL_HALO - j:POOL_HALO - j + tm, sl]
        diff = tot / jnp.minimum(npos, float(w)) - u_ref[:, sl]
        parts.append(jnp.dot(diff.astype(bf16), wp_ref[g], preferred_element_type=f32))
    p = jnp.concatenate(parts, axis=1) * ps_ref[...]

    ab = jnp.dot(a_ref[...], pa_ref[...], preferred_element_type=f32)
    pb = jnp.dot(p.astype(bf16), pb_ref[...], preferred_element_type=f32)
    merged = _sigmoid(ga_ref[...]) * ab + _sigmoid(gp_ref[...]) * pb
    x2 = x_ref[...] + jnp.dot(merged.astype(bf16), wo_ref[...], preferred_element_type=f32)
    x2_ref[...] = x2
    ms = jnp.mean(x2 * x2, axis=-1, keepdims=True)
    hf = x2 * lax.rsqrt(ms + EPS) * g2_ref[...]
    h2_ref[...] = hf.astype(h2_ref.dtype)

    if moe:
        hi = hf.astype(bf16)
        lo = (hf - hi.astype(f32)).astype(bf16)
        logits = (jnp.dot(hi, wrh_ref[...], preferred_element_type=f32)
                  + jnp.dot(lo, wrh_ref[...], preferred_element_type=f32)
                  + jnp.dot(hi, wrl_ref[...], preferred_element_type=f32))
        lanef = lax.broadcasted_iota(i32, (1, LANES), 1).astype(f32)
        lg = jnp.where(lanef < N_EXPERTS, logits, -jnp.inf)
        v1 = jnp.max(lg, axis=1, keepdims=True)
        i1 = jnp.min(jnp.where(lg == v1, lanef, float(LANES)), axis=1, keepdims=True)
        lg2 = jnp.where(lanef == i1, -jnp.inf, lg)
        v2 = jnp.max(lg2, axis=1, keepdims=True)
        i2 = jnp.min(jnp.where(lg2 == v2, lanef, float(LANES)), axis=1, keepdims=True)
        tt = jnp.exp(v2 - v1)
        w1 = 1.0 / (1.0 + tt)
        w2 = tt / (1.0 + tt)
        route = jnp.where((lanef == i1) | (lanef == i2), 1.0, 0.0)
        for ln, val in ((ROUTE_W1, w1), (ROUTE_W2, w2), (ROUTE_I1, i1), (ROUTE_I2, i2)):
            route = jnp.where(lanef == float(ln), val, route)
        gate_ref[...] = route


def _merge(moe, seq, x2d, a, u, ga, gp, wp, ps, pa, pb, wo, g2, wrh=None, wrl=None):
    t = x2d.shape[0]
    tm = TM_MERGE
    row = lambda i: (i, 0)
    const2 = lambda i: (0, 0)
    const3 = lambda i: (0, 0, 0)
    halo = lambda i: (jnp.maximum(i * (tm // POOL_HALO) - 1, 0), 0)
    in_specs = [
        pl.BlockSpec((tm, D_MODEL), row),
        pl.BlockSpec((tm, ATTN_WIDTH), row),
        pl.BlockSpec((tm, POOL_WIDTH), row),
        pl.BlockSpec((POOL_HALO, POOL_WIDTH), halo),
        pl.BlockSpec((tm, D_MODEL), row),
        pl.BlockSpec((tm, D_MODEL), row),
        pl.BlockSpec((POOL_GROUPS, POOL_GROUP_DIM, POOL_GROUP_DIM), const3),
        pl.BlockSpec((1, POOL_WIDTH), const2),
        pl.BlockSpec((ATTN_WIDTH, D_MODEL), const2),
        pl.BlockSpec((POOL_WIDTH, D_MODEL), const2),
        pl.BlockSpec((D_MODEL, D_MODEL), const2),
        pl.BlockSpec((1, D_MODEL), const2),
    ]
    args = [x2d, a, u, u, ga, gp, wp, ps, pa, pb, wo, g2]
    out_shapes = [jax.ShapeDtypeStruct((t, D_MODEL), f32), jax.ShapeDtypeStruct((t, D_MODEL), f32 if moe else bf16)]
    out_specs = [pl.BlockSpec((tm, D_MODEL), row), pl.BlockSpec((tm, D_MODEL), row)]
    if moe:
        in_specs += [pl.BlockSpec((D_MODEL, LANES), const2), pl.BlockSpec((D_MODEL, LANES), const2)]
        args += [wrh, wrl]
        out_shapes.append(jax.ShapeDtypeStruct((t, LANES), f32))
        out_specs.append(pl.BlockSpec((tm, LANES), row))
    return pl.pallas_call(
        functools.partial(_merge_kernel, moe, seq),
        grid=(t // tm,),
        in_specs=in_specs,
        out_specs=out_specs,
        out_shape=out_shapes,
        scratch_shapes=[pltpu.VMEM((tm + POOL_HALO, POOL_WIDTH), f32)],
        compiler_params=pltpu.CompilerParams(
            dimension_semantics=("arbitrary",), vmem_limit_bytes=VMEM_LIMIT),
        name="merge_moe" if moe else "merge_dense",
    )(*args)


def _swiglu_partial(h, w1, w3, w2):
    a = jnp.dot(h, w1, preferred_element_type=f32)
    b = jnp.dot(h, w3, preferred_element_type=f32)
    act = (a * _sigmoid(a) * b).astype(bf16)
    return jnp.dot(act, w2, preferred_element_type=f32)


def _ffn_kernel(h_ref, x_ref, w1_ref, w3_ref, w2_ref, o_ref):
    @pl.when(pl.program_id(1) == 0)
    def _():
        o_ref[...] = x_ref[...]

    o_ref[...] += _swiglu_partial(h_ref[...], w1_ref[...], w3_ref[...], w2_ref[...])


def _ffn(h2, x2, w1, w3, w2, tm, tf):
    t = h2.shape[0]
    ff = w1.shape[1]
    row = lambda i, j: (i, 0)
    return pl.pallas_call(
        _ffn_kernel,
        grid=(t // tm, ff // tf),
        in_specs=[
            pl.BlockSpec((tm, D_MODEL), row),
            pl.BlockSpec((tm, D_MODEL), row),
            pl.BlockSpec((D_MODEL, tf), lambda i, j: (0, j)),
            pl.BlockSpec((D_MODEL, tf), lambda i, j: (0, j)),
            pl.BlockSpec((tf, D_MODEL), lambda i, j: (j, 0)),
        ],
        out_specs=pl.BlockSpec((tm, D_MODEL), row),
        out_shape=jax.ShapeDtypeStruct((t, D_MODEL), f32),
        compiler_params=pltpu.CompilerParams(
            dimension_semantics=("arbitrary", "arbitrary"), vmem_limit_bytes=VMEM_LIMIT),
        name="ffn_dense",
    )(h2, x2, w1, w3, w2)


def _rank_kernel(route_ref, tri_ref, rk_ref, cnt_ref, carry_sc):
    @pl.when(pl.program_id(0) == 0)
    def _():
        carry_sc[...] = jnp.zeros(carry_sc.shape, f32)

    lane = lax.broadcasted_iota(i32, (1, LANES), 1)
    lanef = lane.astype(f32)
    r = route_ref[...]
    sel = jnp.where(lane < N_EXPERTS, r, 0.0)
    ranks = jnp.dot(tri_ref[...], sel.astype(bf16), preferred_element_type=f32) + carry_sc[...]
    rk1 = jnp.sum(jnp.where(lanef == r[:, ROUTE_I1:ROUTE_I1 + 1], ranks, 0.0), axis=1, keepdims=True)
    rk2 = jnp.sum(jnp.where(lanef == r[:, ROUTE_I2:ROUTE_I2 + 1], ranks, 0.0), axis=1, keepdims=True)
    rk_ref[...] = jnp.where(lane == 0, rk1, jnp.where(lane == 1, rk2, 0.0))
    carry_sc[...] += jnp.sum(sel, axis=0, keepdims=True)
    cnt_ref[...] = carry_sc[...]


def _moe_rank(route):
    t = route.shape[0]
    tm = TM_RANK
    idx = jnp.arange(tm)
    tri = (idx[None, :] < idx[:, None]).astype(bf16)
    return pl.pallas_call(
        _rank_kernel,
        grid=(t // tm,),
        in_specs=[pl.BlockSpec((tm, LANES), lambda i: (i, 0)), pl.BlockSpec((tm, tm), lambda i: (0, 0))],
        out_specs=[pl.BlockSpec((tm, LANES), lambda i: (i, 0)), pl.BlockSpec((1, LANES), lambda i: (0, 0))],
        out_shape=[jax.ShapeDtypeStruct((t, LANES), f32), jax.ShapeDtypeStruct((1, LANES), f32)],
        scratch_shapes=[pltpu.VMEM((1, LANES), f32)],
        compiler_params=pltpu.CompilerParams(dimension_semantics=("arbitrary",)),
        name="moe_rank",
    )(route, tri)


def _row_copy(src, dst, sem):
    return pltpu.make_async_copy(src, dst, sem)


def _moe_ffn_kernel(pos_ref, te_ref, meta_ref, h_hbm, w1_ref, w3_ref, w2_ref, ys_ref,
                    inv_sm, gbuf, xb, sem):
    del te_ref
    i = pl.program_id(0)
    j = pl.program_id(1)
    n_used = meta_ref[0]
    n_slots = pos_ref.shape[0]

    def start_gather(tile, slot):
        def issue(g, carry):
            for u in range(ROW_UNROLL):
                r = g * ROW_UNROLL + u
                tok = inv_sm[tile * TM_EXPERT + r]
                _row_copy(h_hbm.at[pl.ds(tok, 1)], gbuf.at[slot, pl.ds(r, 1)],
                          sem.at[slot]).start(priority=u % 2)
            return carry
        lax.fori_loop(0, TM_EXPERT // ROW_UNROLL, issue, 0)

    def wait_gather(slot):
        _row_copy(h_hbm.at[pl.ds(0, TM_EXPERT)], gbuf.at[slot], sem.at[slot]).wait()

    @pl.when((i == 0) & (j == 0))
    def _():
        for e in range(N_EXPERTS):
            def pad(r, carry):
                inv_sm[r] = 0
                return carry
            lax.fori_loop(meta_ref[1 + e], meta_ref[1 + N_EXPERTS + e], pad, 0)

        def scatter(g, carry):
            for u in range(ROW_UNROLL):
                n = g * ROW_UNROLL + u
                inv_sm[pos_ref[n]] = lax.shift_right_logical(n, 1)
            return carry
        lax.fori_loop(0, n_slots // ROW_UNROLL, scatter, 0)

        @pl.when(n_used > 0)
        def _():
            start_gather(0, 0)

    used = i < n_used

    @pl.when(used & (j == 0))
    def _():
        slot = i % 2
        wait_gather(slot)

        @pl.when(i + 1 < n_used)
        def _():
            start_gather(i + 1, 1 - slot)

        xb[...] = gbuf[slot].astype(bf16)

    @pl.when(used)
    def _():
        y = _swiglu_partial(xb[...], w1_ref[0], w3_ref[0], w2_ref[0])

        @pl.when(j == 0)
        def _():
            ys_ref[...] = y

        @pl.when(j > 0)
        def _():
            ys_ref[...] += y

    @pl.when(jnp.logical_not(used) & (j == 0))
    def _():
        ys_ref[...] = jnp.zeros(ys_ref.shape, f32)


def _moe_ffn(pos, tile_expert, meta, hf, w1, w3, w2, n_rows):
    ff = w1.shape[2]
    nj = ff // TF_EXPERT
    jj = lambda i, j, meta: jnp.where(i < meta[0], j, nj - 1)
    return pl.pallas_call(
        _moe_ffn_kernel,
        grid_spec=pltpu.PrefetchScalarGridSpec(
            num_scalar_prefetch=3,
            grid=(n_rows // TM_EXPERT, nj),
            in_specs=[
                pl.BlockSpec(memory_space=pl.ANY),
                pl.BlockSpec((1, D_MODEL, TF_EXPERT), lambda i, j, pos, te, meta: (te[i], 0, jj(i, j, meta))),
                pl.BlockSpec((1, D_MODEL, TF_EXPERT), lambda i, j, pos, te, meta: (te[i], 0, jj(i, j, meta))),
                pl.BlockSpec((1, TF_EXPERT, D_MODEL), lambda i, j, pos, te, meta: (te[i], jj(i, j, meta), 0)),
            ],
            out_specs=pl.BlockSpec((TM_EXPERT, D_MODEL), lambda i, j, pos, te, meta: (i, 0)),
            scratch_shapes=[
                pltpu.SMEM((n_rows,), i32),
                pltpu.VMEM((2, TM_EXPERT, D_MODEL), f32),
                pltpu.VMEM((TM_EXPERT, D_MODEL), bf16),
                pltpu.SemaphoreType.DMA((2,)),
            ],
        ),
        out_shape=jax.ShapeDtypeStruct((n_rows, D_MODEL), f32),
        compiler_params=pltpu.CompilerParams(
            dimension_semantics=("arbitrary", "arbitrary"), vmem_limit_bytes=VMEM_LIMIT),
        name="moe_ffn",
    )(pos, tile_expert, meta, hf, w1, w3, w2)


def _combine_kernel(pos_ref, ys_hbm, x2_ref, route_ref, o_ref, buf, sem):
    base = pl.program_id(0) * TM_ROWS

    def issue(g, carry):
        for u in range(ROW_UNROLL // 2):
            r = g * (ROW_UNROLL // 2) + u
            for s in range(2):
                _row_copy(ys_hbm.at[pl.ds(pos_ref[2 * (base + r) + s], 1)], buf.at[s, pl.ds(r, 1)],
                          sem).start(priority=s)
        return carry

    lax.fori_loop(0, TM_ROWS // (ROW_UNROLL // 2), issue, 0)
    for s in range(2):
        _row_copy(ys_hbm.at[pl.ds(0, TM_ROWS)], buf.at[s], sem).wait()
    route = route_ref[...]
    o_ref[...] = (x2_ref[...] + route[:, ROUTE_W1:ROUTE_W1 + 1] * buf[0]
                  + route[:, ROUTE_W2:ROUTE_W2 + 1] * buf[1])


def _moe_combine(pos, ys, x2, route):
    t = x2.shape[0]
    row = lambda i, pos: (i, 0)
    return pl.pallas_call(
        _combine_kernel,
        grid_spec=pltpu.PrefetchScalarGridSpec(
            num_scalar_prefetch=1,
            grid=(t // TM_ROWS,),
            in_specs=[
                pl.BlockSpec(memory_space=pl.ANY),
                pl.BlockSpec((TM_ROWS, D_MODEL), row),
                pl.BlockSpec((TM_ROWS, LANES), row),
            ],
            out_specs=pl.BlockSpec((TM_ROWS, D_MODEL), row),
            scratch_shapes=[pltpu.VMEM((2, TM_ROWS, D_MODEL), f32), pltpu.SemaphoreType.DMA(())],
        ),
        out_shape=jax.ShapeDtypeStruct((t, D_MODEL), f32),
        compiler_params=pltpu.CompilerParams(dimension_semantics=("arbitrary",)),
        name="moe_combine",
    )(pos, ys, x2, route)


def _moe_layer(route, hf, x2, w1, w3, w2):
    t = hf.shape[0]
    n_tiles = 2 * t // TM_EXPERT + N_EXPERTS
    rk, cnt = _moe_rank(route)
    counts = cnt[0, :N_EXPERTS].astype(i32)
    padded = (counts + TM_EXPERT - 1) // TM_EXPERT * TM_EXPERT
    ends = jnp.cumsum(padded)
    starts = ends - padded
    n_used = (ends[-1] // TM_EXPERT).astype(i32)
    tile_id = jnp.arange(n_tiles, dtype=i32)
    tile_expert = jnp.minimum(jnp.sum((tile_id[:, None] * TM_EXPERT >= ends[None, :]).astype(i32), axis=1),
                              N_EXPERTS - 1)
    tile_expert = jnp.where(tile_id < n_used, tile_expert, tile_expert[jnp.maximum(n_used - 1, 0)])
    experts = jnp.arange(N_EXPERTS, dtype=f32)[None, :]
    start_of = lambda ids: jnp.sum(jnp.where(ids[:, None] == experts, starts[None, :], 0), axis=1)
    pos1 = start_of(route[:, ROUTE_I1]) + rk[:, 0].astype(i32)
    pos2 = start_of(route[:, ROUTE_I2]) + rk[:, 1].astype(i32)
    pos = jnp.stack([pos1, pos2], axis=1).reshape(2 * t).astype(i32)

    meta = jnp.concatenate([n_used.reshape(1), starts + counts, ends]).astype(i32)
    ys = _moe_ffn(pos, tile_expert, meta, hf, w1, w3, w2, n_tiles * TM_EXPERT)
    return _moe_combine(pos, ys, x2, route)


def _rope_tables(seq, rot_dim, period, active_lanes):
    half = rot_dim // 2
    inv = jnp.power(ROPE_THETA, -jnp.arange(0, rot_dim, 2, dtype=f32) / rot_dim)
    ang = jnp.arange(seq, dtype=f32)[:, None] * inv[None, :]
    cos, sin = jnp.cos(ang), jnp.sin(ang)
    ones = jnp.ones((seq, period - rot_dim), f32)
    zeros = lambda n: jnp.zeros((seq, n), f32)
    cos_p = jnp.concatenate([cos, cos, ones], axis=1)
    sin_a = jnp.concatenate([-sin, zeros(period - half)], axis=1)
    sin_b = jnp.concatenate([zeros(half), sin, zeros(period - rot_dim)], axis=1)
    reps = LANES // period
    tabs = [jnp.tile(tb, (1, reps)) for tb in (cos_p, sin_a, sin_b)]
    live = (jnp.arange(LANES) < active_lanes)[None, :]
    tabs = [jnp.where(live, tabs[0], 1.0), jnp.where(live, tabs[1], 0.0), jnp.where(live, tabs[2], 0.0)]
    return jnp.stack(tabs, axis=0)


def _relayout_w_in(w):
    d = w.shape[0]
    o = 0
    segs = {}
    for name, n in (("q", 512), ("k", 512), ("v", 512), ("qi", 256), ("ki", 32), ("wi", 8),
                    ("u", 512), ("ga", 1024), ("gp", 1024)):
        segs[name] = w[:, o:o + n]
        o += n
    qi = segs["qi"].reshape(d, IDX_HEADS, IDX_DIM)
    qcat = jnp.concatenate([qi, qi, qi, jnp.zeros_like(qi)], axis=-1).reshape(d, IDX_HEADS * LANES)
    kwc = jnp.concatenate([segs["ki"], segs["ki"], segs["ki"], segs["wi"],
                           jnp.zeros((d, LANES - 3 * IDX_DIM - IDX_HEADS), w.dtype)], axis=1)
    vh = segs["v"].reshape(d, N_HEADS // 2, 2, HEAD_DIM)
    zh = jnp.zeros_like(vh[:, :, 0])
    vx = jnp.stack([jnp.concatenate([vh[:, :, 0], zh], axis=-1), jnp.concatenate([zh, vh[:, :, 1]], axis=-1)],
                   axis=2).reshape(d, N_HEADS * LANES)
    out = jnp.concatenate([segs["q"], segs["k"], vx, qcat, kwc, segs["u"], segs["ga"], segs["gp"]], axis=1)
    return out.astype(bf16)


def kernel(x, mix_norm, w_in, q_norm, k_norm, w_pool, pool_scale, w_attn_proj, w_pool_proj, w_out, ffn_norm,
           dense_w1, dense_w3, dense_w2, moe_router, moe_w1, moe_w3, moe_w2):
    batch, seq, d = x.shape
    depth = w_in.shape[0]
    t = batch * seq
    assert d == D_MODEL and seq % TK == 0 and seq % TM_PROJ == 0 and seq % TM_MERGE == 0

    rqk = _rope_tables(seq, ROPE_DIM, HEAD_DIM, LANES)
    rqi = _rope_tables(seq, IDX_ROPE_DIM, IDX_DIM, LANES)
    rki = _rope_tables(seq, IDX_ROPE_DIM, IDX_DIM, 3 * IDX_DIM)
    head_of = jnp.arange(ATTN_WIDTH) // HEAD_DIM
    bd = jnp.where(head_of[:, None] == head_of[None, :], 1.0 / HEAD_DIM, 0.0).astype(bf16)
    lane_in_tile = jnp.arange(N_HEADS * LANES) % LANES
    odd_head = (jnp.arange(N_HEADS * LANES) // LANES) % 2 == 1
    ones_lane = (lane_in_tile == jnp.where(odd_head, 0, HEAD_DIM)).astype(f32)[None, :]

    xc = x.reshape(t, d)
    for layer in range(depth):
        w = _relayout_w_in(w_in[layer])
        qn = jnp.tile(q_norm[layer], N_HEADS)[None, :]
        kn = jnp.tile(k_norm[layer], N_HEADS)[None, :]
        q, k, vx, qc, kw, u, ga, gp = _in_proj(xc, mix_norm[layer][None, :], w, qn, kn, bd, ones_lane,
                                               rqk, rqi, rki, seq)
        a = _dsa_attention(q, qc, kw, k, vx, batch, seq)
        moe = layer % 2 == 1
        common = (xc, a, u, ga, gp, w_pool[layer].astype(bf16), pool_scale[layer][None, :],
                  w_attn_proj[layer].astype(bf16), w_pool_proj[layer].astype(bf16), w_out[layer].astype(bf16),
                  ffn_norm[layer][None, :])
        idx = layer // 2
        if moe:
            wr = jnp.pad(moe_router[idx], ((0, 0), (0, LANES - N_EXPERTS)))
            wrh = wr.astype(bf16)
            wrl = (wr - wrh.astype(f32)).astype(bf16)
            x2, hf, route = _merge(True, seq, *common, wrh, wrl)
            xc = _moe_layer(route, hf, x2, moe_w1[idx].astype(bf16), moe_w3[idx].astype(bf16),
                            moe_w2[idx].astype(bf16))
        else:
            x2, h2 = _merge(False, seq, *common)
            xc = _ffn(h2, x2, dense_w1[idx].astype(bf16), dense_w3[idx].astype(bf16),
                      dense_w2[idx].astype(bf16), tm=1024, tf=1408)
    return xc.reshape(batch, seq, d)
```

```python
import functools

import jax
import jax.numpy as jnp
from jax import lax
from jax.experimental import pallas as pl
from jax.experimental.pallas import tpu as pltpu

bf16 = jnp.bfloat16
f32 = jnp.float32
i32 = jnp.int32

D_MODEL = 1024
N_HEADS = 8
HEAD_DIM = 64
ATTN_WIDTH = 512
ROPE_DIM = 16
ROPE_THETA = 500000.0
IDX_HEADS = 8
IDX_DIM = 32
IDX_ROPE_DIM = 8
TOPK_MAX = 256
POOL_GROUPS = 4
POOL_GROUP_DIM = 128
POOL_WIDTH = 512
POOL_WINDOWS = (2, 4, 8, 16)
POOL_HALO = 16
N_EXPERTS = 8
EPS = 1e-6

LANES = 128
INT_MIN = -(2 ** 31)
VMEM_LIMIT = 56 * 1024 * 1024
LOG2E = 1.4426950408889634
SOFTMAX_DENOM_FLOOR = 2.0 ** -90

OFF_Q = 0
OFF_K = 512
OFF_VX = 1024
OFF_QCAT = 2048
OFF_KW = 3072
OFF_U = 3200
OFF_GA = 3712
OFF_GP = 4736
W_COLS = 5760

ROUTE_W1, ROUTE_W2, ROUTE_I1, ROUTE_I2 = 8, 9, 10, 11

TM_PROJ = 512
TM_MERGE = 512
TM_RANK = 512
TM_EXPERT = 512
TF_EXPERT = 1792
TM_ROWS = 256
ROW_UNROLL = 8
QB = 256
COUNT_ROWS = 128
CHUNKS_PER_TRIP = 4
TK = 512


def _sigmoid(x):
    return 1.0 / (1.0 + jnp.exp(-x))


def _in_proj_kernel(x_ref, g_ref, w_ref, qn_ref, kn_ref, bd_ref, one_ref, rqk_ref, rqi_ref, rki_ref,
                    q_ref, k_ref, vx_ref, qc_ref, kw_ref, u_ref, ga_ref, gp_ref):
    x = x_ref[...]
    ms = jnp.mean(x * x, axis=-1, keepdims=True)
    h = (x * lax.rsqrt(ms + EPS) * g_ref[...]).astype(bf16)

    def proj(lo, n):
        return jnp.dot(h, w_ref[:, lo:lo + n], preferred_element_type=f32)

    def rope(xc, tab_ref, sh):
        return (xc * tab_ref[0] + pltpu.roll(xc, LANES - sh, 1) * tab_ref[1]
                + pltpu.roll(xc, sh, 1) * tab_ref[2])

    def headnorm(z, gain_ref):
        msh = jnp.dot((z * z).astype(bf16), bd_ref[...], preferred_element_type=f32)
        return z * lax.rsqrt(msh + EPS) * gain_ref[...]

    zq = headnorm(proj(OFF_Q, ATTN_WIDTH), qn_ref) * (HEAD_DIM ** -0.5 * LOG2E)
    zk = headnorm(proj(OFF_K, ATTN_WIDTH), kn_ref)
    for c in range(ATTN_WIDTH // LANES):
        sl = slice(c * LANES, (c + 1) * LANES)
        q_ref[:, sl] = rope(zq[:, sl], rqk_ref, ROPE_DIM // 2).astype(bf16)
        k_ref[:, sl] = rope(zk[:, sl], rqk_ref, ROPE_DIM // 2).astype(bf16)
    vx_ref[...] = jnp.where(one_ref[...] > 0.0, 1.0, proj(OFF_VX, N_HEADS * LANES)).astype(bf16)

    lane = lax.broadcasted_iota(i32, (1, LANES), 1)
    zc = proj(OFF_QCAT, IDX_HEADS * LANES)
    for c in range(IDX_HEADS):
        sl = slice(c * LANES, (c + 1) * LANES)
        r = rope(zc[:, sl], rqi_ref, IDX_ROPE_DIM // 2)
        lo = r - r.astype(bf16).astype(f32)
        qc_ref[:, sl] = jnp.where((lane >= 32) & (lane < 64), lo, r).astype(bf16)

    zkw = rope(proj(OFF_KW, LANES), rki_ref, IDX_ROPE_DIM // 2)
    lo = zkw - zkw.astype(bf16).astype(f32)
    zkw = jnp.where((lane >= 64) & (lane < 96), lo, zkw)
    kw_ref[...] = jnp.where(lane >= 96, zkw * ((IDX_HEADS * IDX_DIM) ** -0.5), zkw)

    u_ref[...] = proj(OFF_U, POOL_WIDTH)
    ga_ref[...] = proj(OFF_GA, D_MODEL)
    gp_ref[...] = proj(OFF_GP, D_MODEL)


def _in_proj(x2d, g, w, qn, kn, bd, ones_lane, rqk, rqi, rki, seq):
    t = x2d.shape[0]
    tm = TM_PROJ
    ns = seq // tm
    row = lambda i: (i, 0)
    const = lambda i: (0, 0)
    tab = lambda i: (0, i % ns, 0)
    out_shapes = (
        jax.ShapeDtypeStruct((t, ATTN_WIDTH), bf16),
        jax.ShapeDtypeStruct((t, ATTN_WIDTH), bf16),
        jax.ShapeDtypeStruct((t, N_HEADS * LANES), bf16),
        jax.ShapeDtypeStruct((t, IDX_HEADS * LANES), bf16),
        jax.ShapeDtypeStruct((t, LANES), f32),
        jax.ShapeDtypeStruct((t, POOL_WIDTH), f32),
        jax.ShapeDtypeStruct((t, D_MODEL), f32),
        jax.ShapeDtypeStruct((t, D_MODEL), f32),
    )
    return pl.pallas_call(
        _in_proj_kernel,
        grid=(t // tm,),
        in_specs=[
            pl.BlockSpec((tm, D_MODEL), row),
            pl.BlockSpec((1, D_MODEL), const),
            pl.BlockSpec((D_MODEL, W_COLS), const, pipeline_mode=pl.Buffered(1)),
            pl.BlockSpec((1, ATTN_WIDTH), const),
            pl.BlockSpec((1, ATTN_WIDTH), const),
            pl.BlockSpec((ATTN_WIDTH, ATTN_WIDTH), const),
            pl.BlockSpec((1, N_HEADS * LANES), const),
            pl.BlockSpec((3, tm, LANES), tab),
            pl.BlockSpec((3, tm, LANES), tab),
            pl.BlockSpec((3, tm, LANES), tab),
        ],
        out_specs=[pl.BlockSpec((tm, s.shape[1]), row) for s in out_shapes],
        out_shape=out_shapes,
        compiler_params=pltpu.CompilerParams(
            dimension_semantics=("arbitrary",), vmem_limit_bytes=VMEM_LIMIT),
        name="in_proj",
    )(x2d, g, w, qn, kn, bd, ones_lane, rqk, rqi, rki)


def _lane_fold(x, op):
    out = x[:, 0:LANES]
    for j in range(1, x.shape[1] // LANES):
        out = op(out, x[:, j * LANES:(j + 1) * LANES])
    return out


def _grouped_loop(n, body, init):
    def group(g, carry):
        for u in range(CHUNKS_PER_TRIP):
            carry = body(CHUNKS_PER_TRIP * g + u, carry)
        return carry
    carry = lax.fori_loop(0, n // CHUNKS_PER_TRIP, group, init)
    return lax.fori_loop(n // CHUNKS_PER_TRIP * CHUNKS_PER_TRIP, n, body, carry)


def _dsa_kernel(topk, q_ref, qc_ref, kwq_ref, k_ref, vx_ref, kw_ref, o_ref,
                key_sc, kcat_sc, qm2_sc, mrun_sc, m_sc, acc_sc, kmax_sm):
    i = pl.program_id(1)
    nk = (i * QB + QB + TK - 1) // TK
    kf = float(topk)

    lane = lax.broadcasted_iota(i32, (1, LANES), 1)

    @pl.when(i == 0)
    def _():
        kcat_sc[...] = kw_ref[...].astype(bf16)
        for pr in range(N_HEADS // 2):
            def norms(c, carry):
                kk = k_ref[pl.ds(pl.multiple_of(c * TK, TK), TK), pr * LANES:(pr + 1) * LANES].astype(f32)
                sq = kk * kk
                n0 = jnp.sqrt(jnp.sum(jnp.where(lane < HEAD_DIM, sq, 0.0), axis=1, keepdims=True))
                n1 = jnp.sqrt(jnp.sum(jnp.where(lane >= HEAD_DIM, sq, 0.0), axis=1, keepdims=True))
                return jnp.maximum(carry[0], jnp.max(n0)), jnp.maximum(carry[1], jnp.max(n1))
            k0, k1 = lax.fori_loop(0, k_ref.shape[0] // TK, norms, (jnp.float32(0.0), jnp.float32(0.0)))
            kmax_sm[2 * pr] = k0
            kmax_sm[2 * pr + 1] = k1

    qpos = i * QB + lax.broadcasted_iota(i32, (QB, 1), 0)
    lane_tk = lax.broadcasted_iota(i32, (1, TK), 1)
    wq = kwq_ref[...]

    def to_key(score):
        bits = pltpu.bitcast(score, i32)
        return bits ^ ((bits >> 31) & 0x7FFFFFFF)

    def score_chunk(c, carry):
        koff = pl.multiple_of(c * TK, TK)
        kc = kcat_sc[pl.ds(koff, TK), :]
        acc = jnp.zeros((QB, TK), f32)
        for h in range(IDX_HEADS):
            d = lax.dot_general(qc_ref[:, h * LANES:(h + 1) * LANES], kc,
                                (((1,), (1,)), ((), ())), preferred_element_type=f32)
            acc = acc + jnp.maximum(d, 0.0) * wq[:, 96 + h:97 + h]
        causal = (koff + lane_tk) <= qpos
        key_sc[c] = jnp.where(causal, to_key(acc), INT_MIN)
        return carry

    _grouped_loop(nk, score_chunk, 0)

    def count_rows(r0, cand_g):
        def body(c, part):
            for j in range(TK // LANES):
                kk = key_sc[c, r0:r0 + COUNT_ROWS, j * LANES:(j + 1) * LANES]
                part = part + jnp.where(kk >= cand_g, 1, 0)
            return part
        return _grouped_loop(nk, body, jnp.zeros((COUNT_ROWS, LANES), i32))

    def count_ge(cand):
        counts = [count_rows(r0, cand[r0:r0 + COUNT_ROWS]) for r0 in range(0, QB, COUNT_ROWS)]
        return jnp.sum(jnp.concatenate(counts, axis=0).astype(f32), axis=1, keepdims=True)

    zero = jnp.zeros((QB, 1), i32)
    prefix = jnp.where(count_ge(zero) >= kf, zero, INT_MIN)

    def bit_body(b, prefix):
        cand = prefix | jnp.left_shift(1, 30 - b)
        return jnp.where(count_ge(cand) >= kf, cand, prefix)

    thr_raw = lax.fori_loop(0, 31, bit_body, prefix)
    thr = jnp.maximum(thr_raw, INT_MIN + 1)

    is_tie = (count_ge(thr) > kf) & (thr_raw > INT_MIN)

    for r0 in range(0, QB, COUNT_ROWS):
        rows = slice(r0, r0 + COUNT_ROWS)

        @pl.when(jnp.max(jnp.where(is_tie[rows], 1.0, 0.0)) > 0.0)
        def _(r0=r0, rows=rows):
            tie_g = is_tie[rows]
            thr_g = thr[rows]
            need = kf - jnp.sum(count_rows(r0, thr_g + 1).astype(f32), axis=1, keepdims=True)
            upper = jnp.where(
                lax.broadcasted_iota(i32, (TK, TK), 0) <= lax.broadcasted_iota(i32, (TK, TK), 1),
                1.0, 0.0).astype(bf16)

            def drop_ties(c, seen):
                key = key_sc[c, rows, :]
                tied = jnp.where(key == thr_g, 1.0, 0.0)
                rank = jnp.dot(tied.astype(bf16), upper, preferred_element_type=f32) + seen
                drop = tie_g & (key == thr_g) & (rank > need)
                key_sc[c, rows, :] = jnp.where(drop, thr_g - 1, key)
                return seen + jnp.sum(tied, axis=1, keepdims=True)

            lax.fori_loop(0, nk, drop_ties, jnp.zeros((COUNT_ROWS, 1), f32))

    row2 = lax.broadcasted_iota(i32, (2 * QB, 1), 0)
    for pr in range(N_HEADS // 2):
        qp = q_ref[:, pr * LANES:(pr + 1) * LANES]
        qm2_sc[pr, 0:QB, :] = jnp.where(lane < HEAD_DIM, qp, jnp.zeros((), bf16))
        qm2_sc[pr, QB:2 * QB, :] = jnp.where(lane >= HEAD_DIM, qp, jnp.zeros((), bf16))
        q2 = qm2_sc[pr].astype(f32)
        qn = jnp.sqrt(jnp.sum(q2 * q2, axis=1, keepdims=True))
        m_sc[pr] = qn * jnp.where(row2 < QB, kmax_sm[2 * pr], kmax_sm[2 * pr + 1])
    acc_sc[...] = jnp.zeros(acc_sc.shape, f32)

    def pair_logits(c, pr, bias):
        koff = pl.multiple_of(c * TK, TK)
        kc = k_ref[pl.ds(koff, TK), pr * LANES:(pr + 1) * LANES]
        s2 = lax.dot_general(qm2_sc[pr], kc, (((1,), (1,)), ((), ())), preferred_element_type=f32)
        return s2[0:QB] + bias, s2[QB:2 * QB] + bias

    def max_chunk(c, carry):
        bias = jnp.where(key_sc[c] >= thr, 0.0, -jnp.inf)
        for pr in range(N_HEADS // 2):
            sa, sb = pair_logits(c, pr, bias)
            mrun_sc[pr, 0:QB, :] = jnp.maximum(mrun_sc[pr, 0:QB, :], _lane_fold(sa, jnp.maximum))
            mrun_sc[pr, QB:2 * QB, :] = jnp.maximum(mrun_sc[pr, QB:2 * QB, :], _lane_fold(sb, jnp.maximum))
        return carry

    def acc_chunk(c, carry):
        koff = pl.multiple_of(c * TK, TK)
        bias = jnp.where(key_sc[c] >= thr, 0.0, -jnp.inf)
        for pr in range(N_HEADS // 2):
            sa, sb = pair_logits(c, pr, bias)
            m = m_sc[pr]
            pa = jnp.exp2(sa - m[0:QB]).astype(bf16)
            pb = jnp.exp2(sb - m[QB:2 * QB]).astype(bf16)
            va = vx_ref[pl.ds(koff, TK), (2 * pr) * LANES:(2 * pr + 1) * LANES]
            vb = vx_ref[pl.ds(koff, TK), (2 * pr + 1) * LANES:(2 * pr + 2) * LANES]
            acc_sc[2 * pr] += jnp.dot(pa, va, preferred_element_type=f32)
            acc_sc[2 * pr + 1] += jnp.dot(pb, vb, preferred_element_type=f32)
        return carry

    _grouped_loop(nk, acc_chunk, 0)

    lmin = jnp.float32(jnp.inf)
    for pr in range(N_HEADS // 2):
        lmin = jnp.minimum(lmin, jnp.min(acc_sc[2 * pr][:, HEAD_DIM:HEAD_DIM + 1]))
        lmin = jnp.minimum(lmin, jnp.min(acc_sc[2 * pr + 1][:, 0:1]))

    @pl.when(jnp.logical_not(lmin >= SOFTMAX_DENOM_FLOOR))
    def _():
        mrun_sc[...] = jnp.full(mrun_sc.shape, -jnp.inf, f32)
        lax.fori_loop(0, nk, max_chunk, 0)
        for pr in range(N_HEADS // 2):
            m = jnp.max(mrun_sc[pr], axis=1, keepdims=True)
            m_sc[pr] = jnp.where(m == -jnp.inf, 0.0, m)
        acc_sc[...] = jnp.zeros(acc_sc.shape, f32)
        lax.fori_loop(0, nk, acc_chunk, 0)

    for pr in range(N_HEADS // 2):
        a0 = acc_sc[2 * pr]
        a1 = acc_sc[2 * pr + 1]
        o0 = a0 / a0[:, HEAD_DIM:HEAD_DIM + 1]
        o1 = a1 / a1[:, 0:1]
        o_ref[:, pr * LANES:(pr + 1) * LANES] = jnp.where(lane < HEAD_DIM, o0, o1).astype(bf16)


def _dsa_attention(q, qc, kw, k, vx, batch, seq):
    t = q.shape[0]
    nq = seq // QB
    topk = min(TOPK_MAX, seq // 4)
    qrow = lambda b, i: (b * nq + i, 0)
    per_batch = lambda b, i: (b, 0)
    return pl.pallas_call(
        functools.partial(_dsa_kernel, topk),
        grid=(batch, nq),
        in_specs=[
            pl.BlockSpec((QB, ATTN_WIDTH), qrow),
            pl.BlockSpec((QB, IDX_HEADS * LANES), qrow),
            pl.BlockSpec((QB, LANES), qrow),
            pl.BlockSpec((seq, ATTN_WIDTH), per_batch, pipeline_mode=pl.Buffered(1)),
            pl.BlockSpec((seq, N_HEADS * LANES), per_batch, pipeline_mode=pl.Buffered(1)),
            pl.BlockSpec((seq, LANES), per_batch, pipeline_mode=pl.Buffered(1)),
        ],
        out_specs=pl.BlockSpec((QB, ATTN_WIDTH), qrow),
        out_shape=jax.ShapeDtypeStruct((t, ATTN_WIDTH), bf16),
        scratch_shapes=[
            pltpu.VMEM((seq // TK, QB, TK), i32),
            pltpu.VMEM((seq, LANES), bf16),
            pltpu.VMEM((N_HEADS // 2, 2 * QB, LANES), bf16),
            pltpu.VMEM((N_HEADS // 2, 2 * QB, LANES), f32),
            pltpu.VMEM((N_HEADS // 2, 2 * QB, 1), f32),
            pltpu.VMEM((N_HEADS, QB, LANES), f32),
            pltpu.SMEM((N_HEADS,), f32),
        ],
        compiler_params=pltpu.CompilerParams(
            dimension_semantics=("arbitrary", "arbitrary"), vmem_limit_bytes=VMEM_LIMIT),
        name="dsa_attention",
    )(q, qc, kw, k, vx, kw)


def _merge_kernel(moe, seq, *refs):
    if moe:
        (x_ref, a_ref, u_ref, uh_ref, ga_ref, gp_ref, wp_ref, ps_ref, pa_ref, pb_ref, wo_ref, g2_ref,
         wrh_ref, wrl_ref, x2_ref, h2_ref, gate_ref, e_sc) = refs
    else:
        (x_ref, a_ref, u_ref, uh_ref, ga_ref, gp_ref, wp_ref, ps_ref, pa_ref, pb_ref, wo_ref, g2_ref,
         x2_ref, h2_ref, e_sc) = refs
    tm = TM_MERGE
    i = pl.program_id(0)
    ti = i % (seq // tm)
    e_sc[0:POOL_HALO, :] = jnp.where(ti == 0, 0.0, uh_ref[...])
    e_sc[POOL_HALO:POOL_HALO + tm, :] = u_ref[...]
    npos = (ti * tm + 1 + lax.broadcasted_iota(i32, (tm, 1), 0)).astype(f32)

    parts = []
    for g, w in enumerate(POOL_WINDOWS):
        sl = slice(g * POOL_GROUP_DIM, (g + 1) * POOL_GROUP_DIM)
        tot = e_sc[POOL_HALO:POOL_HALO + tm, sl]
        for j in range(1, w):
            tot = tot + e_sc[POOL_HALO - j:POOL_HALO - j + tm, sl]
        diff = tot / jnp.minimum(npos, float(w)) - u_ref[:, sl]
        parts.append(jnp.dot(diff.astype(bf16), wp_ref[g], preferred_element_type=f32))
    p = jnp.concatenate(parts, axis=1) * ps_ref[...]

    ab = jnp.dot(a_ref[...], pa_ref[...], preferred_element_type=f32)
    pb = jnp.dot(p.astype(bf16), pb_ref[...], preferred_element_type=f32)
    merged = _sigmoid(ga_ref[...]) * ab + _sigmoid(gp_ref[...]) * pb
    x2 = x_ref[...] + jnp.dot(merged.astype(bf16), wo_ref[...], preferred_element_type=f32)
    x2_ref[...] = x2
    ms = jnp.mean(x2 * x2, axis=-1, keepdims=True)
    hf = x2 * lax.rsqrt(ms + EPS) * g2_ref[...]
    h2_ref[...] = hf.astype(h2_ref.dtype)

    if moe:
        hi = hf.astype(bf16)
        lo = (hf - hi.astype(f32)).astype(bf16)
        logits = (jnp.dot(hi, wrh_ref[...], preferred_element_type=f32)
                  + jnp.dot(lo, wrh_ref[...], preferred_element_type=f32)
                  + jnp.dot(hi, wrl_ref[...], preferred_element_type=f32))
        lanef = lax.broadcasted_iota(i32, (1, LANES), 1).astype(f32)
        lg = jnp.where(lanef < N_EXPERTS, logits, -jnp.inf)
        v1 = jnp.max(lg, axis=1, keepdims=True)
        i1 = jnp.min(jnp.where(lg == v1, lanef, float(LANES)), axis=1, keepdims=True)
        lg2 = jnp.where(lanef == i1, -jnp.inf, lg)
        v2 = jnp.max(lg2, axis=1, keepdims=True)
        i2 = jnp.min(jnp.where(lg2 == v2, lanef, float(LANES)), axis=1, keepdims=True)
        tt = jnp.exp(v2 - v1)
        w1 = 1.0 / (1.0 + tt)
        w2 = tt / (1.0 + tt)
        route = jnp.where((lanef == i1) | (lanef == i2), 1.0, 0.0)
        for ln, val in ((ROUTE_W1, w1), (ROUTE_W2, w2), (ROUTE_I1, i1), (ROUTE_I2, i2)):
            route = jnp.where(lanef == float(ln), val, route)
        gate_ref[...] = route


def _merge(moe, seq, x2d, a, u, ga, gp, wp, ps, pa, pb, wo, g2, wrh=None, wrl=None):
    t = x2d.shape[0]
    tm = TM_MERGE
    row = lambda i: (i, 0)
    const2 = lambda i: (0, 0)
    const3 = lambda i: (0, 0, 0)
    halo = lambda i: (jnp.maximum(i * (tm // POOL_HALO) - 1, 0), 0)
    in_specs = [
        pl.BlockSpec((tm, D_MODEL), row),
        pl.BlockSpec((tm, ATTN_WIDTH), row),
        pl.BlockSpec((tm, POOL_WIDTH), row),
        pl.BlockSpec((POOL_HALO, POOL_WIDTH), halo),
        pl.BlockSpec((tm, D_MODEL), row),
        pl.BlockSpec((tm, D_MODEL), row),
        pl.BlockSpec((POOL_GROUPS, POOL_GROUP_DIM, POOL_GROUP_DIM), const3),
        pl.BlockSpec((1, POOL_WIDTH), const2),
        pl.BlockSpec((ATTN_WIDTH, D_MODEL), const2),
        pl.BlockSpec((POOL_WIDTH, D_MODEL), const2),
        pl.BlockSpec((D_MODEL, D_MODEL), const2),
        pl.BlockSpec((1, D_MODEL), const2),
    ]
    args = [x2d, a, u, u, ga, gp, wp, ps, pa, pb, wo, g2]
    out_shapes = [jax.ShapeDtypeStruct((t, D_MODEL), f32), jax.ShapeDtypeStruct((t, D_MODEL), f32 if moe else bf16)]
    out_specs = [pl.BlockSpec((tm, D_MODEL), row), pl.BlockSpec((tm, D_MODEL), row)]
    if moe:
        in_specs += [pl.BlockSpec((D_MODEL, LANES), const2), pl.BlockSpec((D_MODEL, LANES), const2)]
        args += [wrh, wrl]
        out_shapes.append(jax.ShapeDtypeStruct((t, LANES), f32))
        out_specs.append(pl.BlockSpec((tm, LANES), row))
    return pl.pallas_call(
        functools.partial(_merge_kernel, moe, seq),
        grid=(t // tm,),
        in_specs=in_specs,
        out_specs=out_specs,
        out_shape=out_shapes,
        scratch_shapes=[pltpu.VMEM((tm + POOL_HALO, POOL_WIDTH), f32)],
        compiler_params=pltpu.CompilerParams(
            dimension_semantics=("arbitrary",), vmem_limit_bytes=VMEM_LIMIT),
        name="merge_moe" if moe else "merge_dense",
    )(*args)


def _swiglu_partial(h, w1, w3, w2):
    a = jnp.dot(h, w1, preferred_element_type=f32)
    b = jnp.dot(h, w3, preferred_element_type=f32)
    act = (a * _sigmoid(a) * b).astype(bf16)
    return jnp.dot(act, w2, preferred_element_type=f32)


def _ffn_kernel(h_ref, x_ref, w1_ref, w3_ref, w2_ref, o_ref):
    @pl.when(pl.program_id(1) == 0)
    def _():
        o_ref[...] = x_ref[...]

    o_ref[...] += _swiglu_partial(h_ref[...], w1_ref[...], w3_ref[...], w2_ref[...])


def _ffn(h2, x2, w1, w3, w2, tm, tf):
    t = h2.shape[0]
    ff = w1.shape[1]
    row = lambda i, j: (i, 0)
    return pl.pallas_call(
        _ffn_kernel,
        grid=(t // tm, ff // tf),
        in_specs=[
            pl.BlockSpec((tm, D_MODEL), row),
            pl.BlockSpec((tm, D_MODEL), row),
            pl.BlockSpec((D_MODEL, tf), lambda i, j: (0, j)),
            pl.BlockSpec((D_MODEL, tf), lambda i, j: (0, j)),
            pl.BlockSpec((tf, D_MODEL), lambda i, j: (j, 0)),
        ],
        out_specs=pl.BlockSpec((tm, D_MODEL), row),
        out_shape=jax.ShapeDtypeStruct((t, D_MODEL), f32),
        compiler_params=pltpu.CompilerParams(
            dimension_semantics=("arbitrary", "arbitrary"), vmem_limit_bytes=VMEM_LIMIT),
        name="ffn_dense",
    )(h2, x2, w1, w3, w2)


def _rank_kernel(route_ref, tri_ref, rk_ref, cnt_ref, carry_sc):
    @pl.when(pl.program_id(0) == 0)
    def _():
        carry_sc[...] = jnp.zeros(carry_sc.shape, f32)

    lane = lax.broadcasted_iota(i32, (1, LANES), 1)
    lanef = lane.astype(f32)
    r = route_ref[...]
    sel = jnp.where(lane < N_EXPERTS, r, 0.0)
    ranks = jnp.dot(tri_ref[...], sel.astype(bf16), preferred_element_type=f32) + carry_sc[...]
    rk1 = jnp.sum(jnp.where(lanef == r[:, ROUTE_I1:ROUTE_I1 + 1], ranks, 0.0), axis=1, keepdims=True)
    rk2 = jnp.sum(jnp.where(lanef == r[:, ROUTE_I2:ROUTE_I2 + 1], ranks, 0.0), axis=1, keepdims=True)
    rk_ref[...] = jnp.where(lane == 0, rk1, jnp.where(lane == 1, rk2, 0.0))
    carry_sc[...] += jnp.sum(sel, axis=0, keepdims=True)
    cnt_ref[...] = carry_sc[...]


def _moe_rank(route):
    t = route.shape[0]
    tm = TM_RANK
    idx = jnp.arange(tm)
    tri = (idx[None, :] < idx[:, None]).astype(bf16)
    return pl.pallas_call(
        _rank_kernel,
        grid=(t // tm,),
        in_specs=[pl.BlockSpec((tm, LANES), lambda i: (i, 0)), pl.BlockSpec((tm, tm), lambda i: (0, 0))],
        out_specs=[pl.BlockSpec((tm, LANES), lambda i: (i, 0)), pl.BlockSpec((1, LANES), lambda i: (0, 0))],
        out_shape=[jax.ShapeDtypeStruct((t, LANES), f32), jax.ShapeDtypeStruct((1, LANES), f32)],
        scratch_shapes=[pltpu.VMEM((1, LANES), f32)],
        compiler_params=pltpu.CompilerParams(dimension_semantics=("arbitrary",)),
        name="moe_rank",
    )(route, tri)


def _row_copy(src, dst, sem):
    return pltpu.make_async_copy(src, dst, sem)


def _moe_ffn_kernel(pos_ref, te_ref, meta_ref, h_hbm, w1_ref, w3_ref, w2_ref, ys_ref,
                    inv_sm, gbuf, xb, sem):
    del te_ref
    i = pl.program_id(0)
    j = pl.program_id(1)
    n_used = meta_ref[0]
    n_slots = pos_ref.shape[0]

    def start_gather(tile, slot):
        def issue(g, carry):
            for u in range(ROW_UNROLL):
                r = g * ROW_UNROLL + u
                tok = inv_sm[tile * TM_EXPERT + r]
                _row_copy(h_hbm.at[pl.ds(tok, 1)], gbuf.at[slot, pl.ds(r, 1)],
                          sem.at[slot]).start(priority=u % 2)
            return carry
        lax.fori_loop(0, TM_EXPERT // ROW_UNROLL, issue, 0)

    def wait_gather(slot):
        _row_copy(h_hbm.at[pl.ds(0, TM_EXPERT)], gbuf.at[slot], sem.at[slot]).wait()

    @pl.when((i == 0) & (j == 0))
    def _():
        for e in range(N_EXPERTS):
            def pad(r, carry):
                inv_sm[r] = 0
                return carry
            lax.fori_loop(meta_ref[1 + e], meta_ref[1 + N_EXPERTS + e], pad, 0)

        def scatter(g, carry):
            for u in range(ROW_UNROLL):
                n = g * ROW_UNROLL + u
                inv_sm[pos_ref[n]] = lax.shift_right_logical(n, 1)
            return carry
        lax.fori_loop(0, n_slots // ROW_UNROLL, scatter, 0)

        @pl.when(n_used > 0)
        def _():
            start_gather(0, 0)

    used = i < n_used

    @pl.when(used & (j == 0))
    def _():
        slot = i % 2
        wait_gather(slot)

        @pl.when(i + 1 < n_used)
        def _():
            start_gather(i + 1, 1 - slot)

        xb[...] = gbuf[slot].astype(bf16)

    @pl.when(used)
    def _():
        y = _swiglu_partial(xb[...], w1_ref[0], w3_ref[0], w2_ref[0])

        @pl.when(j == 0)
        def _():
            ys_ref[...] = y

        @pl.when(j > 0)
        def _():
            ys_ref[...] += y

    @pl.when(jnp.logical_not(used) & (j == 0))
    def _():
        ys_ref[...] = jnp.zeros(ys_ref.shape, f32)


def _moe_ffn(pos, tile_expert, meta, hf, w1, w3, w2, n_rows):
    ff = w1.shape[2]
    nj = ff // TF_EXPERT
    jj = lambda i, j, meta: jnp.where(i < meta[0], j, nj - 1)
    return pl.pallas_call(
        _moe_ffn_kernel,
        grid_spec=pltpu.PrefetchScalarGridSpec(
            num_scalar_prefetch=3,
            grid=(n_rows // TM_EXPERT, nj),
            in_specs=[
                pl.BlockSpec(memory_space=pl.ANY),
                pl.BlockSpec((1, D_MODEL, TF_EXPERT), lambda i, j, pos, te, meta: (te[i], 0, jj(i, j, meta))),
                pl.BlockSpec((1, D_MODEL, TF_EXPERT), lambda i, j, pos, te, meta: (te[i], 0, jj(i, j, meta))),
                pl.BlockSpec((1, TF_EXPERT, D_MODEL), lambda i, j, pos, te, meta: (te[i], jj(i, j, meta), 0)),
            ],
            out_specs=pl.BlockSpec((TM_EXPERT, D_MODEL), lambda i, j, pos, te, meta: (i, 0)),
            scratch_shapes=[
                pltpu.SMEM((n_rows,), i32),
                pltpu.VMEM((2, TM_EXPERT, D_MODEL), f32),
                pltpu.VMEM((TM_EXPERT, D_MODEL), bf16),
                pltpu.SemaphoreType.DMA((2,)),
            ],
        ),
        out_shape=jax.ShapeDtypeStruct((n_rows, D_MODEL), f32),
        compiler_params=pltpu.CompilerParams(
            dimension_semantics=("arbitrary", "arbitrary"), vmem_limit_bytes=VMEM_LIMIT),
        name="moe_ffn",
    )(pos, tile_expert, meta, hf, w1, w3, w2)


def _combine_kernel(pos_ref, ys_hbm, x2_ref, route_ref, o_ref, buf, sem):
    base = pl.program_id(0) * TM_ROWS

    def issue(g, carry):
        for u in range(ROW_UNROLL // 2):
            r = g * (ROW_UNROLL // 2) + u
            for s in range(2):
                _row_copy(ys_hbm.at[pl.ds(pos_ref[2 * (base + r) + s], 1)], buf.at[s, pl.ds(r, 1)],
                          sem).start(priority=s)
        return carry

    lax.fori_loop(0, TM_ROWS // (ROW_UNROLL // 2), issue, 0)
    for s in range(2):
        _row_copy(ys_hbm.at[pl.ds(0, TM_ROWS)], buf.at[s], sem).wait()
    route = route_ref[...]
    o_ref[...] = (x2_ref[...] + route[:, ROUTE_W1:ROUTE_W1 + 1] * buf[0]
                  + route[:, ROUTE_W2:ROUTE_W2 + 1] * buf[1])


def _moe_combine(pos, ys, x2, route):
    t = x2.shape[0]
    row = lambda i, pos: (i, 0)
    return pl.pallas_call(
        _combine_kernel,
        grid_spec=pltpu.PrefetchScalarGridSpec(
            num_scalar_prefetch=1,
            grid=(t // TM_ROWS,),
            in_specs=[
                pl.BlockSpec(memory_space=pl.ANY),
                pl.BlockSpec((TM_ROWS, D_MODEL), row),
                pl.BlockSpec((TM_ROWS, LANES), row),
            ],
            out_specs=pl.BlockSpec((TM_ROWS, D_MODEL), row),
            scratch_shapes=[pltpu.VMEM((2, TM_ROWS, D_MODEL), f32), pltpu.SemaphoreType.DMA(())],
        ),
        out_shape=jax.ShapeDtypeStruct((t, D_MODEL), f32),
        compiler_params=pltpu.CompilerParams(dimension_semantics=("arbitrary",)),
        name="moe_combine",
    )(pos, ys, x2, route)


def _moe_layer(route, hf, x2, w1, w3, w2):
    t = hf.shape[0]
    n_tiles = 2 * t // TM_EXPERT + N_EXPERTS
    rk, cnt = _moe_rank(route)
    counts = cnt[0, :N_EXPERTS].astype(i32)
    padded = (counts + TM_EXPERT - 1) // TM_EXPERT * TM_EXPERT
    ends = jnp.cumsum(padded)
    starts = ends - padded
    n_used = (ends[-1] // TM_EXPERT).astype(i32)
    tile_id = jnp.arange(n_tiles, dtype=i32)
    tile_expert = jnp.minimum(jnp.sum((tile_id[:, None] * TM_EXPERT >= ends[None, :]).astype(i32), axis=1),
                              N_EXPERTS - 1)
    tile_expert = jnp.where(tile_id < n_used, tile_expert, tile_expert[jnp.maximum(n_used - 1, 0)])
    experts = jnp.arange(N_EXPERTS, dtype=f32)[None, :]
    start_of = lambda ids: jnp.sum(jnp.where(ids[:, None] == experts, starts[None, :], 0), axis=1)
    pos1 = start_of(route[:, ROUTE_I1]) + rk[:, 0].astype(i32)
    pos2 = start_of(route[:, ROUTE_I2]) + rk[:, 1].astype(i32)
    pos = jnp.stack([pos1, pos2], axis=1).reshape(2 * t).astype(i32)

    meta = jnp.concatenate([n_used.reshape(1), starts + counts, ends]).astype(i32)
    ys = _moe_ffn(pos, tile_expert, meta, hf, w1, w3, w2, n_tiles * TM_EXPERT)
    return _moe_combine(pos, ys, x2, route)


def _rope_tables(seq, rot_dim, period, active_lanes):
    half = rot_dim // 2
    inv = jnp.power(ROPE_THETA, -jnp.arange(0, rot_dim, 2, dtype=f32) / rot_dim)
    ang = jnp.arange(seq, dtype=f32)[:, None] * inv[None, :]
    cos, sin = jnp.cos(ang), jnp.sin(ang)
    ones = jnp.ones((seq, period - rot_dim), f32)
    zeros = lambda n: jnp.zeros((seq, n), f32)
    cos_p = jnp.concatenate([cos, cos, ones], axis=1)
    sin_a = jnp.concatenate([-sin, zeros(period - half)], axis=1)
    sin_b = jnp.concatenate([zeros(half), sin, zeros(period - rot_dim)], axis=1)
    reps = LANES // period
    tabs = [jnp.tile(tb, (1, reps)) for tb in (cos_p, sin_a, sin_b)]
    live = (jnp.arange(LANES) < active_lanes)[None, :]
    tabs = [jnp.where(live, tabs[0], 1.0), jnp.where(live, tabs[1], 0.0), jnp.where(live, tabs[2], 0.0)]
    return jnp.stack(tabs, axis=0)


def _relayout_w_in(w):
    d = w.shape[0]
    o = 0
    segs = {}
    for name, n in (("q", 512), ("k", 512), ("v", 512), ("qi", 256), ("ki", 32), ("wi", 8),
                    ("u", 512), ("ga", 1024), ("gp", 1024)):
        segs[name] = w[:, o:o + n]
        o += n
    qi = segs["qi"].reshape(d, IDX_HEADS, IDX_DIM)
    qcat = jnp.concatenate([qi, qi, qi, jnp.zeros_like(qi)], axis=-1).reshape(d, IDX_HEADS * LANES)
    kwc = jnp.concatenate([segs["ki"], segs["ki"], segs["ki"], segs["wi"],
                           jnp.zeros((d, LANES - 3 * IDX_DIM - IDX_HEADS), w.dtype)], axis=1)
    vh = segs["v"].reshape(d, N_HEADS // 2, 2, HEAD_DIM)
    zh = jnp.zeros_like(vh[:, :, 0])
    vx = jnp.stack([jnp.concatenate([vh[:, :, 0], zh], axis=-1), jnp.concatenate([zh, vh[:, :, 1]], axis=-1)],
                   axis=2).reshape(d, N_HEADS * LANES)
    out = jnp.concatenate([segs["q"], segs["k"], vx, qcat, kwc, segs["u"], segs["ga"], segs["gp"]], axis=1)
    return out.astype(bf16)


def kernel(x, mix_norm, w_in, q_norm, k_norm, w_pool, pool_scale, w_attn_proj, w_pool_proj, w_out, ffn_norm,
           dense_w1, dense_w3, dense_w2, moe_router, moe_w1, moe_w3, moe_w2):
    batch, seq, d = x.shape
    depth = w_in.shape[0]
    t = batch * seq
    assert d == D_MODEL and seq % TK == 0 and seq % TM_PROJ == 0 and seq % TM_MERGE == 0

    rqk = _rope_tables(seq, ROPE_DIM, HEAD_DIM, LANES)
    rqi = _rope_tables(seq, IDX_ROPE_DIM, IDX_DIM, LANES)
    rki = _rope_tables(seq, IDX_ROPE_DIM, IDX_DIM, 3 * IDX_DIM)
    head_of = jnp.arange(ATTN_WIDTH) // HEAD_DIM
    bd = jnp.where(head_of[:, None] == head_of[None, :], 1.0 / HEAD_DIM, 0.0).astype(bf16)
    lane_in_tile = jnp.arange(N_HEADS * LANES) % LANES
    odd_head = (jnp.arange(N_HEADS * LANES) // LANES) % 2 == 1
    ones_lane = (lane_in_tile == jnp.where(odd_head, 0, HEAD_DIM)).astype(f32)[None, :]

    xc = x.reshape(t, d)
    for layer in range(depth):
        w = _relayout_w_in(w_in[layer])
        qn = jnp.tile(q_norm[layer], N_HEADS)[None, :]
        kn = jnp.tile(k_norm[layer], N_HEADS)[None, :]
        q, k, vx, qc, kw, u, ga, gp = _in_proj(xc, mix_norm[layer][None, :], w, qn, kn, bd, ones_lane,
                                               rqk, rqi, rki, seq)
        a = _dsa_attention(q, qc, kw, k, vx, batch, seq)
        moe = layer % 2 == 1
        common = (xc, a, u, ga, gp, w_pool[layer].astype(bf16), pool_scale[layer][None, :],
                  w_attn_proj[layer].astype(bf16), w_pool_proj[layer].astype(bf16), w_out[layer].astype(bf16),
                  ffn_norm[layer][None, :])
        idx = layer // 2
        if moe:
            wr = jnp.pad(moe_router[idx], ((0, 0), (0, LANES - N_EXPERTS)))
            wrh = wr.astype(bf16)
            wrl = (wr - wrh.astype(f32)).astype(bf16)
            x2, hf, route = _merge(True, seq, *common, wrh, wrl)
            xc = _moe_layer(route, hf, x2, moe_w1[idx].astype(bf16), moe_w3[idx].astype(bf16),
                            moe_w2[idx].astype(bf16))
        else:
            x2, h2 = _merge(False, seq, *common)
            xc = _ffn(h2, x2, dense_w1[idx].astype(bf16), dense_w3[idx].astype(bf16),
                      dense_w2[idx].astype(bf16), tm=1024, tf=1408)
    return xc.reshape(batch, seq, d)
```

```python
import functools

import jax
import jax.numpy as jnp
from jax import lax
from jax.experimental import pallas as pl
from jax.experimental.pallas import tpu as pltpu

bf16 = jnp.bfloat16
f32 = jnp.float32
i32 = jnp.int32

D_MODEL = 1024
N_HEADS = 8
HEAD_DIM = 64
ATTN_WIDTH = 512
ROPE_DIM = 16
ROPE_THETA = 500000.0
IDX_HEADS = 8
IDX_DIM = 32
IDX_ROPE_DIM = 8
TOPK_MAX = 256
POOL_GROUPS = 4
POOL_GROUP_DIM = 128
POOL_WIDTH = 512
POOL_WINDOWS = (2, 4, 8, 16)
POOL_HALO = 16
N_EXPERTS = 8
EPS = 1e-6

LANES = 128
INT_MIN = -(2 ** 31)
VMEM_LIMIT = 56 * 1024 * 1024
LOG2E = 1.4426950408889634
SOFTMAX_DENOM_FLOOR = 2.0 ** -90

OFF_Q = 0
OFF_K = 512
OFF_VX = 1024
OFF_QCAT = 2048
OFF_KW = 3072
OFF_U = 3200
OFF_GA = 3712
OFF_GP = 4736
W_COLS = 5760

ROUTE_W1, ROUTE_W2, ROUTE_I1, ROUTE_I2 = 8, 9, 10, 11

TM_PROJ = 512
TM_MERGE = 512
TM_RANK = 512
TM_EXPERT = 512
TF_EXPERT = 1792
TM_ROWS = 256
ROW_UNROLL = 8
QB = 256
COUNT_ROWS = 128
CHUNKS_PER_TRIP = 4
TK = 512


def _sigmoid(x):
    return 1.0 / (1.0 + jnp.exp(-x))


def _in_proj_kernel(x_ref, g_ref, w_ref, qn_ref, kn_ref, bd_ref, one_ref, rqk_ref, rqi_ref, rki_ref,
                    q_ref, k_ref, vx_ref, qc_ref, kw_ref, u_ref, ga_ref, gp_ref):
    x = x_ref[...]
    ms = jnp.mean(x * x, axis=-1, keepdims=True)
    h = (x * lax.rsqrt(ms + EPS) * g_ref[...]).astype(bf16)

    def proj(lo, n):
        return jnp.dot(h, w_ref[:, lo:lo + n], preferred_element_type=f32)

    def rope(xc, tab_ref, sh):
        return (xc * tab_ref[0] + pltpu.roll(xc, LANES - sh, 1) * tab_ref[1]
                + pltpu.roll(xc, sh, 1) * tab_ref[2])

    def headnorm(z, gain_ref):
        msh = jnp.dot((z * z).astype(bf16), bd_ref[...], preferred_element_type=f32)
        return z * lax.rsqrt(msh + EPS) * gain_ref[...]

    zq = headnorm(proj(OFF_Q, ATTN_WIDTH), qn_ref) * (HEAD_DIM ** -0.5 * LOG2E)
    zk = headnorm(proj(OFF_K, ATTN_WIDTH), kn_ref)
    for c in range(ATTN_WIDTH // LANES):
        sl = slice(c * LANES, (c + 1) * LANES)
        q_ref[:, sl] = rope(zq[:, sl], rqk_ref, ROPE_DIM // 2).astype(bf16)
        k_ref[:, sl] = rope(zk[:, sl], rqk_ref, ROPE_DIM // 2).astype(bf16)
    vx_ref[...] = jnp.where(one_ref[...] > 0.0, 1.0, proj(OFF_VX, N_HEADS * LANES)).astype(bf16)

    lane = lax.broadcasted_iota(i32, (1, LANES), 1)
    zc = proj(OFF_QCAT, IDX_HEADS * LANES)
    for c in range(IDX_HEADS):
        sl = slice(c * LANES, (c + 1) * LANES)
        r = rope(zc[:, sl], rqi_ref, IDX_ROPE_DIM // 2)
        lo = r - r.astype(bf16).astype(f32)
        qc_ref[:, sl] = jnp.where((lane >= 32) & (lane < 64), lo, r).astype(bf16)

    zkw = rope(proj(OFF_KW, LANES), rki_ref, IDX_ROPE_DIM // 2)
    lo = zkw - zkw.astype(bf16).astype(f32)
    zkw = jnp.where((lane >= 64) & (lane < 96), lo, zkw)
    kw_ref[...] = jnp.where(lane >= 96, zkw * ((IDX_HEADS * IDX_DIM) ** -0.5), zkw)

    u_ref[...] = proj(OFF_U, POOL_WIDTH)
    ga_ref[...] = _sigmoid(proj(OFF_GA, D_MODEL)).astype(bf16)
    gp_ref[...] = _sigmoid(proj(OFF_GP, D_MODEL)).astype(bf16)


def _in_proj(x2d, g, w, qn, kn, bd, ones_lane, rqk, rqi, rki, seq):
    t = x2d.shape[0]
    tm = TM_PROJ
    ns = seq // tm
    row = lambda i: (i, 0)
    const = lambda i: (0, 0)
    tab = lambda i: (0, i % ns, 0)
    out_shapes = (
        jax.ShapeDtypeStruct((t, ATTN_WIDTH), bf16),
        jax.ShapeDtypeStruct((t, ATTN_WIDTH), bf16),
        jax.ShapeDtypeStruct((t, N_HEADS * LANES), bf16),
        jax.ShapeDtypeStruct((t, IDX_HEADS * LANES), bf16),
        jax.ShapeDtypeStruct((t, LANES), f32),
        jax.ShapeDtypeStruct((t, POOL_WIDTH), f32),
        jax.ShapeDtypeStruct((t, D_MODEL), bf16),
        jax.ShapeDtypeStruct((t, D_MODEL), bf16),
    )
    return pl.pallas_call(
        _in_proj_kernel,
        grid=(t // tm,),
        in_specs=[
            pl.BlockSpec((tm, D_MODEL), row),
            pl.BlockSpec((1, D_MODEL), const),
            pl.BlockSpec((D_MODEL, W_COLS), const, pipeline_mode=pl.Buffered(1)),
            pl.BlockSpec((1, ATTN_WIDTH), const),
            pl.BlockSpec((1, ATTN_WIDTH), const),
            pl.BlockSpec((ATTN_WIDTH, ATTN_WIDTH), const),
            pl.BlockSpec((1, N_HEADS * LANES), const),
            pl.BlockSpec((3, tm, LANES), tab),
            pl.BlockSpec((3, tm, LANES), tab),
            pl.BlockSpec((3, tm, LANES), tab),
        ],
        out_specs=[pl.BlockSpec((tm, s.shape[1]), row) for s in out_shapes],
        out_shape=out_shapes,
        compiler_params=pltpu.CompilerParams(
            dimension_semantics=("arbitrary",), vmem_limit_bytes=VMEM_LIMIT),
        name="in_proj",
    )(x2d, g, w, qn, kn, bd, ones_lane, rqk, rqi, rki)


def _lane_fold(x, op):
    out = x[:, 0:LANES]
    for j in range(1, x.shape[1] // LANES):
        out = op(out, x[:, j * LANES:(j + 1) * LANES])
    return out


def _grouped_loop(n, body, init):
    def group(g, carry):
        for u in range(CHUNKS_PER_TRIP):
            carry = body(CHUNKS_PER_TRIP * g + u, carry)
        return carry
    carry = lax.fori_loop(0, n // CHUNKS_PER_TRIP, group, init)
    return lax.fori_loop(n // CHUNKS_PER_TRIP * CHUNKS_PER_TRIP, n, body, carry)


def _dsa_kernel(topk, q_ref, qc_ref, kwq_ref, k_ref, vx_ref, kw_ref, o_ref,
                key_sc, kcat_sc, qm2_sc, mrun_sc, m_sc, acc_sc, kmax_sm):
    i = pl.program_id(1)
    nk = (i * QB + QB + TK - 1) // TK
    kf = float(topk)

    lane = lax.broadcasted_iota(i32, (1, LANES), 1)

    @pl.when(i == 0)
    def _():
        kcat_sc[...] = kw_ref[...].astype(bf16)
        for pr in range(N_HEADS // 2):
            def norms(c, carry):
                kk = k_ref[pl.ds(pl.multiple_of(c * TK, TK), TK), pr * LANES:(pr + 1) * LANES].astype(f32)
                sq = kk * kk
                n0 = jnp.sqrt(jnp.sum(jnp.where(lane < HEAD_DIM, sq, 0.0), axis=1, keepdims=True))
                n1 = jnp.sqrt(jnp.sum(jnp.where(lane >= HEAD_DIM, sq, 0.0), axis=1, keepdims=True))
                return jnp.maximum(carry[0], jnp.max(n0)), jnp.maximum(carry[1], jnp.max(n1))
            k0, k1 = lax.fori_loop(0, k_ref.shape[0] // TK, norms, (jnp.float32(0.0), jnp.float32(0.0)))
            kmax_sm[2 * pr] = k0
            kmax_sm[2 * pr + 1] = k1

    qpos = i * QB + lax.broadcasted_iota(i32, (QB, 1), 0)
    lane_tk = lax.broadcasted_iota(i32, (1, TK), 1)
    wq = kwq_ref[...]

    def to_key(score):
        bits = pltpu.bitcast(score, i32)
        return bits ^ ((bits >> 31) & 0x7FFFFFFF)

    def score_chunk(c, carry):
        koff = pl.multiple_of(c * TK, TK)
        kc = kcat_sc[pl.ds(koff, TK), :]
        acc = jnp.zeros((QB, TK), f32)
        for h in range(IDX_HEADS):
            d = lax.dot_general(qc_ref[:, h * LANES:(h + 1) * LANES], kc,
                                (((1,), (1,)), ((), ())), preferred_element_type=f32)
            acc = acc + jnp.maximum(d, 0.0) * wq[:, 96 + h:97 + h]
        causal = (koff + lane_tk) <= qpos
        key_sc[c] = jnp.where(causal, to_key(acc), INT_MIN)
        return carry

    _grouped_loop(nk, score_chunk, 0)

    def count_rows(r0, cand_g):
        def body(c, part):
            for j in range(TK // LANES):
                kk = key_sc[c, r0:r0 + COUNT_ROWS, j * LANES:(j + 1) * LANES]
                part = part + jnp.where(kk >= cand_g, 1, 0)
            return part
        return _grouped_loop(nk, body, jnp.zeros((COUNT_ROWS, LANES), i32))

    def count_ge(cand):
        counts = [count_rows(r0, cand[r0:r0 + COUNT_ROWS]) for r0 in range(0, QB, COUNT_ROWS)]
        return jnp.sum(jnp.concatenate(counts, axis=0).astype(f32), axis=1, keepdims=True)

    zero = jnp.zeros((QB, 1), i32)
    prefix = jnp.where(count_ge(zero) >= kf, zero, INT_MIN)

    def bit_body(b, prefix):
        cand = prefix | jnp.left_shift(1, 30 - b)
        return jnp.where(count_ge(cand) >= kf, cand, prefix)

    thr_raw = lax.fori_loop(0, 31, bit_body, prefix)
    thr = jnp.maximum(thr_raw, INT_MIN + 1)

    is_tie = (count_ge(thr) > kf) & (thr_raw > INT_MIN)

    for r0 in range(0, QB, COUNT_ROWS):
        rows = slice(r0, r0 + COUNT_ROWS)

        @pl.when(jnp.max(jnp.where(is_tie[rows], 1.0, 0.0)) > 0.0)
        def _(r0=r0, rows=rows):
            tie_g = is_tie[rows]
            thr_g = thr[rows]
            need = kf - jnp.sum(count_rows(r0, thr_g + 1).astype(f32), axis=1, keepdims=True)
            upper = jnp.where(
                lax.broadcasted_iota(i32, (TK, TK), 0) <= lax.broadcasted_iota(i32, (TK, TK), 1),
                1.0, 0.0).astype(bf16)

            def drop_ties(c, seen):
                key = key_sc[c, rows, :]
                tied = jnp.where(key == thr_g, 1.0, 0.0)
                rank = jnp.dot(tied.astype(bf16), upper, preferred_element_type=f32) + seen
                drop = tie_g & (key == thr_g) & (rank > need)
                key_sc[c, rows, :] = jnp.where(drop, thr_g - 1, key)
                return seen + jnp.sum(tied, axis=1, keepdims=True)

            lax.fori_loop(0, nk, drop_ties, jnp.zeros((COUNT_ROWS, 1), f32))

    row2 = lax.broadcasted_iota(i32, (2 * QB, 1), 0)
    for pr in range(N_HEADS // 2):
        qp = q_ref[:, pr * LANES:(pr + 1) * LANES]
        qm2_sc[pr, 0:QB, :] = jnp.where(lane < HEAD_DIM, qp, jnp.zeros((), bf16))
        qm2_sc[pr, QB:2 * QB, :] = jnp.where(lane >= HEAD_DIM, qp, jnp.zeros((), bf16))
        q2 = qm2_sc[pr].astype(f32)
        qn = jnp.sqrt(jnp.sum(q2 * q2, axis=1, keepdims=True))
        m_sc[pr] = qn * jnp.where(row2 < QB, kmax_sm[2 * pr], kmax_sm[2 * pr + 1])
    acc_sc[...] = jnp.zeros(acc_sc.shape, f32)

    def pair_logits(c, pr, bias):
        koff = pl.multiple_of(c * TK, TK)
        kc = k_ref[pl.ds(koff, TK), pr * LANES:(pr + 1) * LANES]
        s2 = lax.dot_general(qm2_sc[pr], kc, (((1,), (1,)), ((), ())), preferred_element_type=f32)
        return s2[0:QB] + bias, s2[QB:2 * QB] + bias

    def max_chunk(c, carry):
        bias = jnp.where(key_sc[c] >= thr, 0.0, -jnp.inf)
        for pr in range(N_HEADS // 2):
            sa, sb = pair_logits(c, pr, bias)
            mrun_sc[pr, 0:QB, :] = jnp.maximum(mrun_sc[pr, 0:QB, :], _lane_fold(sa, jnp.maximum))
            mrun_sc[pr, QB:2 * QB, :] = jnp.maximum(mrun_sc[pr, QB:2 * QB, :], _lane_fold(sb, jnp.maximum))
        return carry

    def acc_chunk(c, carry):
        koff = pl.multiple_of(c * TK, TK)
        bias = jnp.where(key_sc[c] >= thr, 0.0, -jnp.inf)
        for pr in range(N_HEADS // 2):
            sa, sb = pair_logits(c, pr, bias)
            m = m_sc[pr]
            pa = jnp.exp2(sa - m[0:QB]).astype(bf16)
            pb = jnp.exp2(sb - m[QB:2 * QB]).astype(bf16)
            va = vx_ref[pl.ds(koff, TK), (2 * pr) * LANES:(2 * pr + 1) * LANES]
            vb = vx_ref[pl.ds(koff, TK), (2 * pr + 1) * LANES:(2 * pr + 2) * LANES]
            acc_sc[2 * pr] += jnp.dot(pa, va, preferred_element_type=f32)
            acc_sc[2 * pr + 1] += jnp.dot(pb, vb, preferred_element_type=f32)
        return carry

    _grouped_loop(nk, acc_chunk, 0)

    lmin = jnp.float32(jnp.inf)
    for pr in range(N_HEADS // 2):
        lmin = jnp.minimum(lmin, jnp.min(acc_sc[2 * pr][:, HEAD_DIM:HEAD_DIM + 1]))
        lmin = jnp.minimum(lmin, jnp.min(acc_sc[2 * pr + 1][:, 0:1]))

    @pl.when(jnp.logical_not(lmin >= SOFTMAX_DENOM_FLOOR))
    def _():
        mrun_sc[...] = jnp.full(mrun_sc.shape, -jnp.inf, f32)
        lax.fori_loop(0, nk, max_chunk, 0)
        for pr in range(N_HEADS // 2):
            m = jnp.max(mrun_sc[pr], axis=1, keepdims=True)
            m_sc[pr] = jnp.where(m == -jnp.inf, 0.0, m)
        acc_sc[...] = jnp.zeros(acc_sc.shape, f32)
        lax.fori_loop(0, nk, acc_chunk, 0)

    for pr in range(N_HEADS // 2):
        a0 = acc_sc[2 * pr]
        a1 = acc_sc[2 * pr + 1]
        o0 = a0 / a0[:, HEAD_DIM:HEAD_DIM + 1]
        o1 = a1 / a1[:, 0:1]
        o_ref[:, pr * LANES:(pr + 1) * LANES] = jnp.where(lane < HEAD_DIM, o0, o1).astype(bf16)


def _dsa_attention(q, qc, kw, k, vx, batch, seq):
    t = q.shape[0]
    nq = seq // QB
    topk = min(TOPK_MAX, seq // 4)
    qrow = lambda b, i: (b * nq + i, 0)
    per_batch = lambda b, i: (b, 0)
    return pl.pallas_call(
        functools.partial(_dsa_kernel, topk),
        grid=(batch, nq),
        in_specs=[
            pl.BlockSpec((QB, ATTN_WIDTH), qrow),
            pl.BlockSpec((QB, IDX_HEADS * LANES), qrow),
            pl.BlockSpec((QB, LANES), qrow),
            pl.BlockSpec((seq, ATTN_WIDTH), per_batch, pipeline_mode=pl.Buffered(1)),
            pl.BlockSpec((seq, N_HEADS * LANES), per_batch, pipeline_mode=pl.Buffered(1)),
            pl.BlockSpec((seq, LANES), per_batch, pipeline_mode=pl.Buffered(1)),
        ],
        out_specs=pl.BlockSpec((QB, ATTN_WIDTH), qrow),
        out_shape=jax.ShapeDtypeStruct((t, ATTN_WIDTH), bf16),
        scratch_shapes=[
            pltpu.VMEM((seq // TK, QB, TK), i32),
            pltpu.VMEM((seq, LANES), bf16),
            pltpu.VMEM((N_HEADS // 2, 2 * QB, LANES), bf16),
            pltpu.VMEM((N_HEADS // 2, 2 * QB, LANES), f32),
            pltpu.VMEM((N_HEADS // 2, 2 * QB, 1), f32),
            pltpu.VMEM((N_HEADS, QB, LANES), f32),
            pltpu.SMEM((N_HEADS,), f32),
        ],
        compiler_params=pltpu.CompilerParams(
            dimension_semantics=("arbitrary", "arbitrary"), vmem_limit_bytes=VMEM_LIMIT),
        name="dsa_attention",
    )(q, qc, kw, k, vx, kw)


def _merge_kernel(moe, seq, *refs):
    if moe:
        (x_ref, a_ref, u_ref, uh_ref, ga_ref, gp_ref, wp_ref, ps_ref, pa_ref, pb_ref, wo_ref, g2_ref,
         wrh_ref, wrl_ref, x2_ref, h2_ref, gate_ref, e_sc) = refs
    else:
        (x_ref, a_ref, u_ref, uh_ref, ga_ref, gp_ref, wp_ref, ps_ref, pa_ref, pb_ref, wo_ref, g2_ref,
         x2_ref, h2_ref, e_sc) = refs
    tm = TM_MERGE
    i = pl.program_id(0)
    ti = i % (seq // tm)
    e_sc[0:POOL_HALO, :] = jnp.where(ti == 0, 0.0, uh_ref[...])
    e_sc[POOL_HALO:POOL_HALO + tm, :] = u_ref[...]
    npos = (ti * tm + 1 + lax.broadcasted_iota(i32, (tm, 1), 0)).astype(f32)

    parts = []
    for g, w in enumerate(POOL_WINDOWS):
        sl = slice(g * POOL_GROUP_DIM, (g + 1) * POOL_GROUP_DIM)
        tot = e_sc[POOL_HALO:POOL_HALO + tm, sl]
        for j in range(1, w):
            tot = tot + e_sc[POOL_HALO - j:POOL_HALO - j + tm, sl]
        diff = tot / jnp.minimum(npos, float(w)) - u_ref[:, sl]
        parts.append(jnp.dot(diff.astype(bf16), wp_ref[g], preferred_element_type=f32))
    p = jnp.concatenate(parts, axis=1) * ps_ref[...]

    ab = jnp.dot(a_ref[...], pa_ref[...], preferred_element_type=f32)
    pb = jnp.dot(p.astype(bf16), pb_ref[...], preferred_element_type=f32)
    merged = ga_ref[...].astype(f32) * ab + gp_ref[...].astype(f32) * pb
    x2 = x_ref[...] + jnp.dot(merged.astype(bf16), wo_ref[...], preferred_element_type=f32)
    x2_ref[...] = x2
    ms = jnp.mean(x2 * x2, axis=-1, keepdims=True)
    hf = x2 * lax.rsqrt(ms + EPS) * g2_ref[...]
    h2_ref[...] = hf.astype(h2_ref.dtype)

    if moe:
        hi = hf.astype(bf16)
        lo = (hf - hi.astype(f32)).astype(bf16)
        logits = (jnp.dot(hi, wrh_ref[...], preferred_element_type=f32)
                  + jnp.dot(lo, wrh_ref[...], preferred_element_type=f32)
                  + jnp.dot(hi, wrl_ref[...], preferred_element_type=f32))
        lanef = lax.broadcasted_iota(i32, (1, LANES), 1).astype(f32)
        lg = jnp.where(lanef < N_EXPERTS, logits, -jnp.inf)
        v1 = jnp.max(lg, axis=1, keepdims=True)
        i1 = jnp.min(jnp.where(lg == v1, lanef, float(LANES)), axis=1, keepdims=True)
        lg2 = jnp.where(lanef == i1, -jnp.inf, lg)
        v2 = jnp.max(lg2, axis=1, keepdims=True)
        i2 = jnp.min(jnp.where(lg2 == v2, lanef, float(LANES)), axis=1, keepdims=True)
        tt = jnp.exp(v2 - v1)
        w1 = 1.0 / (1.0 + tt)
        w2 = tt / (1.0 + tt)
        route = jnp.where((lanef == i1) | (lanef == i2), 1.0, 0.0)
        for ln, val in ((ROUTE_W1, w1), (ROUTE_W2, w2), (ROUTE_I1, i1), (ROUTE_I2, i2)):
            route = jnp.where(lanef == float(ln), val, route)
        gate_ref[...] = route


def _merge(moe, seq, x2d, a, u, ga, gp, wp, ps, pa, pb, wo, g2, wrh=None, wrl=None):
    t = x2d.shape[0]
    tm = TM_MERGE
    row = lambda i: (i, 0)
    const2 = lambda i: (0, 0)
    const3 = lambda i: (0, 0, 0)
    halo = lambda i: (jnp.maximum(i * (tm // POOL_HALO) - 1, 0), 0)
    in_specs = [
        pl.BlockSpec((tm, D_MODEL), row),
        pl.BlockSpec((tm, ATTN_WIDTH), row),
        pl.BlockSpec((tm, POOL_WIDTH), row),
        pl.BlockSpec((POOL_HALO, POOL_WIDTH), halo),
        pl.BlockSpec((tm, D_MODEL), row),
        pl.BlockSpec((tm, D_MODEL), row),
        pl.BlockSpec((POOL_GROUPS, POOL_GROUP_DIM, POOL_GROUP_DIM), const3),
        pl.BlockSpec((1, POOL_WIDTH), const2),
        pl.BlockSpec((ATTN_WIDTH, D_MODEL), const2),
        pl.BlockSpec((POOL_WIDTH, D_MODEL), const2),
        pl.BlockSpec((D_MODEL, D_MODEL), const2),
        pl.BlockSpec((1, D_MODEL), const2),
    ]
    args = [x2d, a, u, u, ga, gp, wp, ps, pa, pb, wo, g2]
    out_shapes = [jax.ShapeDtypeStruct((t, D_MODEL), f32), jax.ShapeDtypeStruct((t, D_MODEL), f32 if moe else bf16)]
    out_specs = [pl.BlockSpec((tm, D_MODEL), row), pl.BlockSpec((tm, D_MODEL), row)]
    if moe:
        in_specs += [pl.BlockSpec((D_MODEL, LANES), const2), pl.BlockSpec((D_MODEL, LANES), const2)]
        args += [wrh, wrl]
        out_shapes.append(jax.ShapeDtypeStruct((t, LANES), f32))
        out_specs.append(pl.BlockSpec((tm, LANES), row))
    return pl.pallas_call(
        functools.partial(_merge_kernel, moe, seq),
        grid=(t // tm,),
        in_specs=in_specs,
        out_specs=out_specs,
        out_shape=out_shapes,
        scratch_shapes=[pltpu.VMEM((tm + POOL_HALO, POOL_WIDTH), f32)],
        compiler_params=pltpu.CompilerParams(
            dimension_semantics=("arbitrary",), vmem_limit_bytes=VMEM_LIMIT),
        name="merge_moe" if moe else "merge_dense",
    )(*args)


def _swiglu_partial(h, w1, w3, w2):
    a = jnp.dot(h, w1, preferred_element_type=f32)
    b = jnp.dot(h, w3, preferred_element_type=f32)
    act = (a * _sigmoid(a) * b).astype(bf16)
    return jnp.dot(act, w2, preferred_element_type=f32)


def _ffn_kernel(h_ref, x_ref, w1_ref, w3_ref, w2_ref, o_ref):
    @pl.when(pl.program_id(1) == 0)
    def _():
        o_ref[...] = x_ref[...]

    o_ref[...] += _swiglu_partial(h_ref[...], w1_ref[...], w3_ref[...], w2_ref[...])


def _ffn(h2, x2, w1, w3, w2, tm, tf):
    t = h2.shape[0]
    ff = w1.shape[1]
    row = lambda i, j: (i, 0)
    return pl.pallas_call(
        _ffn_kernel,
        grid=(t // tm, ff // tf),
        in_specs=[
            pl.BlockSpec((tm, D_MODEL), row),
            pl.BlockSpec((tm, D_MODEL), row),
            pl.BlockSpec((D_MODEL, tf), lambda i, j: (0, j)),
            pl.BlockSpec((D_MODEL, tf), lambda i, j: (0, j)),
            pl.BlockSpec((tf, D_MODEL), lambda i, j: (j, 0)),
        ],
        out_specs=pl.BlockSpec((tm, D_MODEL), row),
        out_shape=jax.ShapeDtypeStruct((t, D_MODEL), f32),
        compiler_params=pltpu.CompilerParams(
            dimension_semantics=("arbitrary", "arbitrary"), vmem_limit_bytes=VMEM_LIMIT),
        name="ffn_dense",
    )(h2, x2, w1, w3, w2)


def _rank_kernel(route_ref, tri_ref, rk_ref, cnt_ref, carry_sc):
    @pl.when(pl.program_id(0) == 0)
    def _():
        carry_sc[...] = jnp.zeros(carry_sc.shape, f32)

    lane = lax.broadcasted_iota(i32, (1, LANES), 1)
    lanef = lane.astype(f32)
    r = route_ref[...]
    sel = jnp.where(lane < N_EXPERTS, r, 0.0)
    ranks = jnp.dot(tri_ref[...], sel.astype(bf16), preferred_element_type=f32) + carry_sc[...]
    rk1 = jnp.sum(jnp.where(lanef == r[:, ROUTE_I1:ROUTE_I1 + 1], ranks, 0.0), axis=1, keepdims=True)
    rk2 = jnp.sum(jnp.where(lanef == r[:, ROUTE_I2:ROUTE_I2 + 1], ranks, 0.0), axis=1, keepdims=True)
    rk_ref[...] = jnp.where(lane == 0, rk1, jnp.where(lane == 1, rk2, 0.0))
    carry_sc[...] += jnp.sum(sel, axis=0, keepdims=True)
    cnt_ref[...] = carry_sc[...]


def _moe_rank(route):
    t = route.shape[0]
    tm = TM_RANK
    idx = jnp.arange(tm)
    tri = (idx[None, :] < idx[:, None]).astype(bf16)
    return pl.pallas_call(
        _rank_kernel,
        grid=(t // tm,),
        in_specs=[pl.BlockSpec((tm, LANES), lambda i: (i, 0)), pl.BlockSpec((tm, tm), lambda i: (0, 0))],
        out_specs=[pl.BlockSpec((tm, LANES), lambda i: (i, 0)), pl.BlockSpec((1, LANES), lambda i: (0, 0))],
        out_shape=[jax.ShapeDtypeStruct((t, LANES), f32), jax.ShapeDtypeStruct((1, LANES), f32)],
        scratch_shapes=[pltpu.VMEM((1, LANES), f32)],
        compiler_params=pltpu.CompilerParams(dimension_semantics=("arbitrary",)),
        name="moe_rank",
    )(route, tri)


def _row_copy(src, dst, sem):
    return pltpu.make_async_copy(src, dst, sem)


def _moe_ffn_kernel(pos_ref, te_ref, meta_ref, h_hbm, w1_ref, w3_ref, w2_ref, ys_ref,
                    inv_sm, gbuf, xb, sem):
    del te_ref
    i = pl.program_id(0)
    j = pl.program_id(1)
    n_used = meta_ref[0]
    n_slots = pos_ref.shape[0]

    def start_gather(tile, slot):
        def issue(g, carry):
            for u in range(ROW_UNROLL):
                r = g * ROW_UNROLL + u
                tok = inv_sm[tile * TM_EXPERT + r]
                _row_copy(h_hbm.at[pl.ds(tok, 1)], gbuf.at[slot, pl.ds(r, 1)],
                          sem.at[slot]).start(priority=u % 2)
            return carry
        lax.fori_loop(0, TM_EXPERT // ROW_UNROLL, issue, 0)

    def wait_gather(slot):
        _row_copy(h_hbm.at[pl.ds(0, TM_EXPERT)], gbuf.at[slot], sem.at[slot]).wait()

    @pl.when((i == 0) & (j == 0))
    def _():
        for e in range(N_EXPERTS):
            def pad(r, carry):
                inv_sm[r] = 0
                return carry
            lax.fori_loop(meta_ref[1 + e], meta_ref[1 + N_EXPERTS + e], pad, 0)

        def scatter(g, carry):
            for u in range(ROW_UNROLL):
                n = g * ROW_UNROLL + u
                inv_sm[pos_ref[n]] = lax.shift_right_logical(n, 1)
            return carry
        lax.fori_loop(0, n_slots // ROW_UNROLL, scatter, 0)

        @pl.when(n_used > 0)
        def _():
            start_gather(0, 0)

    used = i < n_used

    @pl.when(used & (j == 0))
    def _():
        slot = i % 2
        wait_gather(slot)

        @pl.when(i + 1 < n_used)
        def _():
            start_gather(i + 1, 1 - slot)

        xb[...] = gbuf[slot].astype(bf16)

    @pl.when(used)
    def _():
        y = _swiglu_partial(xb[...], w1_ref[0], w3_ref[0], w2_ref[0])

        @pl.when(j == 0)
        def _():
            ys_ref[...] = y

        @pl.when(j > 0)
        def _():
            ys_ref[...] += y

    @pl.when(jnp.logical_not(used) & (j == 0))
    def _():
        ys_ref[...] = jnp.zeros(ys_ref.shape, f32)


def _moe_ffn(pos, tile_expert, meta, hf, w1, w3, w2, n_rows):
    ff = w1.shape[2]
    nj = ff // TF_EXPERT
    jj = lambda i, j, meta: jnp.where(i < meta[0], j, nj - 1)
    return pl.pallas_call(
        _moe_ffn_kernel,
        grid_spec=pltpu.PrefetchScalarGridSpec(
            num_scalar_prefetch=3,
            grid=(n_rows // TM_EXPERT, nj),
            in_specs=[
                pl.BlockSpec(memory_space=pl.ANY),
                pl.BlockSpec((1, D_MODEL, TF_EXPERT), lambda i, j, pos, te, meta: (te[i], 0, jj(i, j, meta))),
                pl.BlockSpec((1, D_MODEL, TF_EXPERT), lambda i, j, pos, te, meta: (te[i], 0, jj(i, j, meta))),
                pl.BlockSpec((1, TF_EXPERT, D_MODEL), lambda i, j, pos, te, meta: (te[i], jj(i, j, meta), 0)),
            ],
            out_specs=pl.BlockSpec((TM_EXPERT, D_MODEL), lambda i, j, pos, te, meta: (i, 0)),
            scratch_shapes=[
                pltpu.SMEM((n_rows,), i32),
                pltpu.VMEM((2, TM_EXPERT, D_MODEL), f32),
                pltpu.VMEM((TM_EXPERT, D_MODEL), bf16),
                pltpu.SemaphoreType.DMA((2,)),
            ],
        ),
        out_shape=jax.ShapeDtypeStruct((n_rows, D_MODEL), f32),
        compiler_params=pltpu.CompilerParams(
            dimension_semantics=("arbitrary", "arbitrary"), vmem_limit_bytes=VMEM_LIMIT),
        name="moe_ffn",
    )(pos, tile_expert, meta, hf, w1, w3, w2)


def _combine_kernel(pos_ref, ys_hbm, x2_ref, route_ref, o_ref, buf, sem):
    base = pl.program_id(0) * TM_ROWS

    def issue(g, carry):
        for u in range(ROW_UNROLL // 2):
            r = g * (ROW_UNROLL // 2) + u
            for s in range(2):
                _row_copy(ys_hbm.at[pl.ds(pos_ref[2 * (base + r) + s], 1)], buf.at[s, pl.ds(r, 1)],
                          sem).start(priority=s)
        return carry

    lax.fori_loop(0, TM_ROWS // (ROW_UNROLL // 2), issue, 0)
    for s in range(2):
        _row_copy(ys_hbm.at[pl.ds(0, TM_ROWS)], buf.at[s], sem).wait()
    route = route_ref[...]
    o_ref[...] = (x2_ref[...] + route[:, ROUTE_W1:ROUTE_W1 + 1] * buf[0]
                  + route[:, ROUTE_W2:ROUTE_W2 + 1] * buf[1])


def _moe_combine(pos, ys, x2, route):
    t = x2.shape[0]
    row = lambda i, pos: (i, 0)
    return pl.pallas_call(
        _combine_kernel,
        grid_spec=pltpu.PrefetchScalarGridSpec(
            num_scalar_prefetch=1,
            grid=(t // TM_ROWS,),
            in_specs=[
                pl.BlockSpec(memory_space=pl.ANY),
                pl.BlockSpec((TM_ROWS, D_MODEL), row),
                pl.BlockSpec((TM_ROWS, LANES), row),
            ],
            out_specs=pl.BlockSpec((TM_ROWS, D_MODEL), row),
            scratch_shapes=[pltpu.VMEM((2, TM_ROWS, D_MODEL), f32), pltpu.SemaphoreType.DMA(())],
        ),
        out_shape=jax.ShapeDtypeStruct((t, D_MODEL), f32),
        compiler_params=pltpu.CompilerParams(dimension_semantics=("arbitrary",)),
        name="moe_combine",
    )(pos, ys, x2, route)


def _moe_layer(route, hf, x2, w1, w3, w2):
    t = hf.shape[0]
    n_tiles = 2 * t // TM_EXPERT + N_EXPERTS
    rk, cnt = _moe_rank(route)
    counts = cnt[0, :N_EXPERTS].astype(i32)
    padded = (counts + TM_EXPERT - 1) // TM_EXPERT * TM_EXPERT
    ends = jnp.cumsum(padded)
    starts = ends - padded
    n_used = (ends[-1] // TM_EXPERT).astype(i32)
    tile_id = jnp.arange(n_tiles, dtype=i32)
    tile_expert = jnp.minimum(jnp.sum((tile_id[:, None] * TM_EXPERT >= ends[None, :]).astype(i32), axis=1),
                              N_EXPERTS - 1)
    tile_expert = jnp.where(tile_id < n_used, tile_expert, tile_expert[jnp.maximum(n_used - 1, 0)])
    experts = jnp.arange(N_EXPERTS, dtype=f32)[None, :]
    start_of = lambda ids: jnp.sum(jnp.where(ids[:, None] == experts, starts[None, :], 0), axis=1)
    pos1 = start_of(route[:, ROUTE_I1]) + rk[:, 0].astype(i32)
    pos2 = start_of(route[:, ROUTE_I2]) + rk[:, 1].astype(i32)
    pos = jnp.stack([pos1, pos2], axis=1).reshape(2 * t).astype(i32)

    meta = jnp.concatenate([n_used.reshape(1), starts + counts, ends]).astype(i32)
    ys = _moe_ffn(pos, tile_expert, meta, hf, w1, w3, w2, n_tiles * TM_EXPERT)
    return _moe_combine(pos, ys, x2, route)


def _rope_tables(seq, rot_dim, period, active_lanes):
    half = rot_dim // 2
    inv = jnp.power(ROPE_THETA, -jnp.arange(0, rot_dim, 2, dtype=f32) / rot_dim)
    ang = jnp.arange(seq, dtype=f32)[:, None] * inv[None, :]
    cos, sin = jnp.cos(ang), jnp.sin(ang)
    ones = jnp.ones((seq, period - rot_dim), f32)
    zeros = lambda n: jnp.zeros((seq, n), f32)
    cos_p = jnp.concatenate([cos, cos, ones], axis=1)
    sin_a = jnp.concatenate([-sin, zeros(period - half)], axis=1)
    sin_b = jnp.concatenate([zeros(half), sin, zeros(period - rot_dim)], axis=1)
    reps = LANES // period
    tabs = [jnp.tile(tb, (1, reps)) for tb in (cos_p, sin_a, sin_b)]
    live = (jnp.arange(LANES) < active_lanes)[None, :]
    tabs = [jnp.where(live, tabs[0], 1.0), jnp.where(live, tabs[1], 0.0), jnp.where(live, tabs[2], 0.0)]
    return jnp.stack(tabs, axis=0)


def _relayout_w_in(w):
    d = w.shape[0]
    o = 0
    segs = {}
    for name, n in (("q", 512), ("k", 512), ("v", 512), ("qi", 256), ("ki", 32), ("wi", 8),
                    ("u", 512), ("ga", 1024), ("gp", 1024)):
        segs[name] = w[:, o:o + n]
        o += n
    qi = segs["qi"].reshape(d, IDX_HEADS, IDX_DIM)
    qcat = jnp.concatenate([qi, qi, qi, jnp.zeros_like(qi)], axis=-1).reshape(d, IDX_HEADS * LANES)
    kwc = jnp.concatenate([segs["ki"], segs["ki"], segs["ki"], segs["wi"],
                           jnp.zeros((d, LANES - 3 * IDX_DIM - IDX_HEADS), w.dtype)], axis=1)
    vh = segs["v"].reshape(d, N_HEADS // 2, 2, HEAD_DIM)
    zh = jnp.zeros_like(vh[:, :, 0])
    vx = jnp.stack([jnp.concatenate([vh[:, :, 0], zh], axis=-1), jnp.concatenate([zh, vh[:, :, 1]], axis=-1)],
                   axis=2).reshape(d, N_HEADS * LANES)
    out = jnp.concatenate([segs["q"], segs["k"], vx, qcat, kwc, segs["u"], segs["ga"], segs["gp"]], axis=1)
    return out.astype(bf16)


def kernel(x, mix_norm, w_in, q_norm, k_norm, w_pool, pool_scale, w_attn_proj, w_pool_proj, w_out, ffn_norm,
           dense_w1, dense_w3, dense_w2, moe_router, moe_w1, moe_w3, moe_w2):
    batch, seq, d = x.shape
    depth = w_in.shape[0]
    t = batch * seq
    assert d == D_MODEL and seq % TK == 0 and seq % TM_PROJ == 0 and seq % TM_MERGE == 0

    rqk = _rope_tables(seq, ROPE_DIM, HEAD_DIM, LANES)
    rqi = _rope_tables(seq, IDX_ROPE_DIM, IDX_DIM, LANES)
    rki = _rope_tables(seq, IDX_ROPE_DIM, IDX_DIM, 3 * IDX_DIM)
    head_of = jnp.arange(ATTN_WIDTH) // HEAD_DIM
    bd = jnp.where(head_of[:, None] == head_of[None, :], 1.0 / HEAD_DIM, 0.0).astype(bf16)
    lane_in_tile = jnp.arange(N_HEADS * LANES) % LANES
    odd_head = (jnp.arange(N_HEADS * LANES) // LANES) % 2 == 1
    ones_lane = (lane_in_tile == jnp.where(odd_head, 0, HEAD_DIM)).astype(f32)[None, :]

    xc = x.reshape(t, d)
    for layer in range(depth):
        w = _relayout_w_in(w_in[layer])
        qn = jnp.tile(q_norm[layer], N_HEADS)[None, :]
        kn = jnp.tile(k_norm[layer], N_HEADS)[None, :]
        q, k, vx, qc, kw, u, ga, gp = _in_proj(xc, mix_norm[layer][None, :], w, qn, kn, bd, ones_lane,
                                               rqk, rqi, rki, seq)
        a = _dsa_attention(q, qc, kw, k, vx, batch, seq)
        moe = layer % 2 == 1
        common = (xc, a, u, ga, gp, w_pool[layer].astype(bf16), pool_scale[layer][None, :],
                  w_attn_proj[layer].astype(bf16), w_pool_proj[layer].astype(bf16), w_out[layer].astype(bf16),
                  ffn_norm[layer][None, :])
        idx = layer // 2
        if moe:
            wr = jnp.pad(moe_router[idx], ((0, 0), (0, LANES - N_EXPERTS)))
            wrh = wr.astype(bf16)
            wrl = (wr - wrh.astype(f32)).astype(bf16)
            x2, hf, route = _merge(True, seq, *common, wrh, wrl)
            xc = _moe_layer(route, hf, x2, moe_w1[idx].astype(bf16), moe_w3[idx].astype(bf16),
                            moe_w2[idx].astype(bf16))
        else:
            x2, h2 = _merge(False, seq, *common)
            xc = _ffn(h2, x2, dense_w1[idx].astype(bf16), dense_w3[idx].astype(bf16),
                      dense_w2[idx].astype(bf16), tm=1024, tf=1408)
    return xc.reshape(batch, seq, d)
```
